```python
import math
import jax
import jax.numpy as jnp
from jax import lax
import numpy as np

D_MODEL = 2048
BATCH = 8
SEQ = 8192
DEPTH = 2

CTX_LEN = 256
GRID_W = 64
N_BRANCH = 3
BRANCH_WIDTH = 1024
N_HEADS = 8
N_KV_HEADS = 2
HEAD_DIM = 128
GROUP = N_HEADS // N_KV_HEADS
WINDOW = 128
Q_BLOCK = 128
ROPE_BASE = 10000.0
RNN_WIDTH = BRANCH_WIDTH
RNN_BLOCKS = 8
RNN_BLOCK = RNN_WIDTH // RNN_BLOCKS
RNN_CONV = 4
RNN_CONV_LEFT = 2
LRU_C = 8.0
SC_WIDTH = BRANCH_WIDTH
SC_CONV = 3
SC_CONV_LEFT = 1
D_FF = 5504
N_MOD = 9
EPS = 1e-6
NEG_INF = -1e30
IN_SIZES = (RNN_WIDTH, RNN_WIDTH, SC_WIDTH, SC_WIDTH, SC_WIDTH,
            N_HEADS * HEAD_DIM, N_KV_HEADS * HEAD_DIM, N_KV_HEADS * HEAD_DIM,
            N_BRANCH * D_MODEL)
IN_COLS = sum(IN_SIZES)

kernel_name = "hybrid_rglru_shortconv_swa_diffusion_block"


def rmsnorm(x, g):
    xf = x.astype(jnp.float32)
    y = xf * lax.rsqrt(jnp.mean(xf * xf, axis=-1, keepdims=True) + EPS)
    return (y * g.astype(jnp.float32)).astype(x.dtype)


def modulate(x, shift, scale):
    return x * (1 + scale) + shift


def swiglu(u, w13, w2):
    gu = u @ w13
    gate, up = gu[..., :D_FF], gu[..., D_FF:]
    return (jax.nn.silu(gate) * up) @ w2


def split_in(z):
    out, start = [], 0
    for n in IN_SIZES:
        out.append(z[..., start:start + n])
        start += n
    return out


def dwconv(x, w, left):
    k_w, ch = w.shape
    return lax.conv_general_dilated(
        x, w[:, None, :], window_strides=(1,), padding=[(left, k_w - 1 - left)],
        dimension_numbers=("NWC", "WIO", "NWC"), feature_group_count=ch)


def axial_rope(n_tok):
    rows = n_tok // GRID_W
    row = jnp.repeat(jnp.arange(rows), GRID_W).astype(jnp.float32)
    col = jnp.tile(jnp.arange(GRID_W), rows).astype(jnp.float32)
    half = HEAD_DIM // 2
    inv = ROPE_BASE ** (-jnp.arange(0, half, 2, dtype=jnp.float32) / half)
    ang = jnp.concatenate([row[:, None] * inv, col[:, None] * inv], axis=-1)
    ang = ang.reshape(n_tok, 2, half // 2)
    return jnp.cos(ang), jnp.sin(ang)


def apply_rope(x, cos, sin):
    b, l, h, d = x.shape
    xr = x.astype(jnp.float32).reshape(b, l, h, 2, 2, d // 4)
    x1, x2 = xr[..., 0, :], xr[..., 1, :]
    cs, sn = cos[None, :, None], sin[None, :, None]
    out = jnp.stack([x1 * cs - x2 * sn, x2 * cs + x1 * sn], axis=-2)
    return out.reshape(b, l, h, d).astype(x.dtype)


def linear_scan(a, b, h0):
    def combine(e1, e2):
        a1, b1 = e1
        a2, b2 = e2
        return a1 * a2, a2 * b1 + b2
    a_cum, b_cum = lax.associative_scan(combine, (a, b), axis=1)
    return b_cum + a_cum * h0[:, None, :]


def rglru(x, w_a, b_a, w_x, b_x, lam, h0, reverse):
    b, l, r = x.shape
    xb = x.reshape(b, l, RNN_BLOCKS, RNN_BLOCK)
    rg = jax.nn.sigmoid((jnp.einsum("blnd,nde->blne", xb, w_a).reshape(b, l, r) + b_a).astype(jnp.float32))
    ig = jax.nn.sigmoid((jnp.einsum("blnd,nde->blne", xb, w_x).reshape(b, l, r) + b_x).astype(jnp.float32))
    log_a = -LRU_C * rg * jax.nn.softplus(-lam.astype(jnp.float32))
    a = jnp.exp(log_a)
    u = jnp.sqrt(-jnp.expm1(2.0 * log_a)) * (ig * x.astype(jnp.float32))
    if reverse:
        a, u = jnp.flip(a, axis=1), jnp.flip(u, axis=1)
    h = linear_scan(a, u, h0)
    h_last = h[:, -1]
    if reverse:
        h = jnp.flip(h, axis=1)
    return h.astype(x.dtype), h_last


def sink_softmax(logits, sink):
    s = jnp.broadcast_to(sink.astype(jnp.float32)[None, :, :, None, None], logits.shape[:-1] + (1,))
    p = jax.nn.softmax(jnp.concatenate([s, logits], axis=-1), axis=-1)
    return p[..., 1:]


def banded_attention(q, k, v, kc, vc, sink):
    b, l = q.shape[0], q.shape[1]
    nblk = l // Q_BLOCK
    scale = HEAD_DIM ** -0.5
    qb = q.reshape(b, nblk, Q_BLOCK, N_KV_HEADS, GROUP, HEAD_DIM)
    pad = ((0, 0), (Q_BLOCK, Q_BLOCK), (0, 0), (0, 0))
    kp, vp = jnp.pad(k, pad), jnp.pad(v, pad)
    span = 3 * Q_BLOCK
    offs_q = jnp.arange(Q_BLOCK)
    offs_k = jnp.arange(span) - Q_BLOCK

    def block(n):
        qn = lax.dynamic_index_in_dim(qb, n, axis=1, keepdims=False)
        kn = lax.dynamic_slice_in_dim(kp, n * Q_BLOCK, span, axis=1)
        vn = lax.dynamic_slice_in_dim(vp, n * Q_BLOCK, span, axis=1)
        qpos = n * Q_BLOCK + offs_q
        kpos = n * Q_BLOCK + offs_k
        valid = (jnp.abs(qpos[:, None] - kpos[None, :]) <= WINDOW) & (kpos >= 0)[None, :] & (kpos < l)[None, :]
        s_loc = jnp.einsum("bqkgd,bskd->bkgqs", qn, kn).astype(jnp.float32) * scale
        s_loc = jnp.where(valid, s_loc, NEG_INF)
        s_ctx = jnp.einsum("bqkgd,bckd->bkgqc", qn, kc).astype(jnp.float32) * scale
        p = sink_softmax(jnp.concatenate([s_loc, s_ctx], axis=-1), sink).astype(v.dtype)
        return (jnp.einsum("bkgqs,bskd->bqkgd", p[..., :span], vn)
                + jnp.einsum("bkgqc,bckd->bqkgd", p[..., span:], vc))

    out = lax.map(block, jnp.arange(nblk))
    return jnp.moveaxis(out, 0, 1).reshape(b, l, N_HEADS * HEAD_DIM)


def context_attention(qc, kc, vc, sink):
    b, n = qc.shape[0], qc.shape[1]
    s = jnp.einsum("bqkgd,bckd->bkgqc", qc, kc).astype(jnp.float32) * (HEAD_DIM ** -0.5)
    p = sink_softmax(s, sink).astype(vc.dtype)
    return jnp.einsum("bkgqc,bckd->bqkgd", p, vc).reshape(b, n, N_HEADS * HEAD_DIM)


def merge_branches(ys, g, b_merge, w_branch, w_out):
    gates = jax.nn.sigmoid(g.reshape(g.shape[:-1] + (N_BRANCH, D_MODEL)) + b_merge)
    merged = gates[..., 0, :] * (ys[0] @ w_branch[0])
    for i in range(1, N_BRANCH):
        merged = merged + gates[..., i, :] * (ys[i] @ w_branch[i])
    return merged @ w_out


def token_mixer(u, uc, cos, sin, w_in, b_merge, rnn_conv_w, rnn_conv_b, lru_w_a, lru_b_a,
                lru_w_x, lru_b_x, lru_lambda, sc_conv_w, attn_sink, w_branch, w_out, with_ctx_out):
    b, l, _ = u.shape
    n_ctx = uc.shape[1]
    rx, rg, sb, scg, sx, q, k, v, g = split_in(u @ w_in)
    rxc, rgc, sbc, scgc, sxc, qc, kc, vc, gc = split_in(uc @ w_in)

    xa = dwconv(rx, rnn_conv_w, RNN_CONV_LEFT) + rnn_conv_b
    xac = dwconv(rxc, rnn_conv_w, RNN_CONV_LEFT) + rnn_conv_b
    h0 = jnp.zeros((b, RNN_WIDTH), jnp.float32)
    hc_f, last_f = rglru(xac, lru_w_a[0], lru_b_a[0], lru_w_x[0], lru_b_x[0], lru_lambda[0], h0, False)
    hc_b, last_b = rglru(xac, lru_w_a[1], lru_b_a[1], lru_w_x[1], lru_b_x[1], lru_lambda[1], h0, True)
    hl_f, _ = rglru(xa, lru_w_a[0], lru_b_a[0], lru_w_x[0], lru_b_x[0], lru_lambda[0], last_f, False)
    hl_b, _ = rglru(xa, lru_w_a[1], lru_b_a[1], lru_w_x[1], lru_b_x[1], lru_lambda[1], last_b, True)
    ya = (hl_f + hl_b) * jax.nn.gelu(rg)

    yb = sb * dwconv(scg * sx, sc_conv_w, SC_CONV_LEFT)

    sink = attn_sink.reshape(N_KV_HEADS, GROUP)
    q = apply_rope(q.reshape(b, l, N_HEADS, HEAD_DIM), cos, sin).reshape(b, l, N_KV_HEADS, GROUP, HEAD_DIM)
    k = apply_rope(k.reshape(b, l, N_KV_HEADS, HEAD_DIM), cos, sin)
    v = v.reshape(b, l, N_KV_HEADS, HEAD_DIM)
    kc = kc.reshape(b, n_ctx, N_KV_HEADS, HEAD_DIM)
    vc = vc.reshape(b, n_ctx, N_KV_HEADS, HEAD_DIM)
    yatt = banded_attention(q, k, v, kc, vc, sink)

    y = merge_branches((ya, yb, yatt), g, b_merge, w_branch, w_out)
    if not with_ctx_out:
        return y, None

    yac = (hc_f + hc_b) * jax.nn.gelu(rgc)
    ybc = sbc * dwconv(scgc * sxc, sc_conv_w, SC_CONV_LEFT)
    yattc = context_attention(qc.reshape(b, n_ctx, N_KV_HEADS, GROUP, HEAD_DIM), kc, vc, sink)
    yc = merge_branches((yac, ybc, yattc), gc, b_merge, w_branch, w_out)
    return y, yc


def _fwd_setup_inputs(seed: int = 0) -> dict:
    key = jax.random.key(seed)
    ks = jax.random.split(key, 32)
    f32 = jnp.float32

    def nrm(k, shape, scale):
        return jax.random.normal(k, shape, f32) * scale

    a_c = jax.random.uniform(ks[15], (DEPTH, 2, RNN_WIDTH), f32, minval=0.9, maxval=0.999)
    s_l = a_c ** (1.0 / LRU_C)
    lam = jnp.log(s_l) - jnp.log1p(-s_l)
    return {
        "x": nrm(ks[0], (BATCH, SEQ, D_MODEL), 1.0),
        "c": nrm(ks[1], (BATCH, D_MODEL), 1.0),
        "ctx": nrm(ks[2], (BATCH, CTX_LEN, D_MODEL), 1.0),
        "c_ctx": nrm(ks[3], (D_MODEL,), 1.0),
        "ada_w": nrm(ks[4], (DEPTH, D_MODEL, N_MOD * D_MODEL), 0.5 * D_MODEL ** -0.5),
        "ada_b": nrm(ks[5], (DEPTH, N_MOD * D_MODEL), 0.02),
        "norm_g": 1.0 + nrm(ks[6], (DEPTH, 3, D_MODEL), 0.02),
        "ffn1_w13": nrm(ks[7], (DEPTH, D_MODEL, 2 * D_FF), D_MODEL ** -0.5),
        "ffn1_w2": nrm(ks[8], (DEPTH, D_FF, D_MODEL), D_FF ** -0.5),
        "w_in": nrm(ks[9], (DEPTH, D_MODEL, IN_COLS), D_MODEL ** -0.5),
        "b_merge": nrm(ks[10], (DEPTH, N_BRANCH, D_MODEL), 0.02),
        "rnn_conv_w": nrm(ks[11], (DEPTH, RNN_CONV, RNN_WIDTH), RNN_CONV ** -0.5),
        "rnn_conv_b": nrm(ks[12], (DEPTH, RNN_WIDTH), 0.02),
        "lru_w_a": nrm(ks[13], (DEPTH, 2, RNN_BLOCKS, RNN_BLOCK, RNN_BLOCK), RNN_BLOCK ** -0.5),
        "lru_b_a": nrm(ks[14], (DEPTH, 2, RNN_WIDTH), 0.02),
        "lru_w_x": nrm(ks[16], (DEPTH, 2, RNN_BLOCKS, RNN_BLOCK, RNN_BLOCK), RNN_BLOCK ** -0.5),
        "lru_b_x": nrm(ks[17], (DEPTH, 2, RNN_WIDTH), 0.02),
        "lru_lambda": lam,
        "sc_conv_w": nrm(ks[18], (DEPTH, SC_CONV, SC_WIDTH), SC_CONV ** -0.5),
        "attn_sink": nrm(ks[19], (DEPTH, N_HEADS), 0.5),
        "w_branch": nrm(ks[20], (DEPTH, N_BRANCH, BRANCH_WIDTH, D_MODEL), BRANCH_WIDTH ** -0.5),
        "w_out": nrm(ks[21], (DEPTH, D_MODEL, D_MODEL), D_MODEL ** -0.5),
        "ffn2_w13": nrm(ks[22], (DEPTH, D_MODEL, 2 * D_FF), D_MODEL ** -0.5),
        "ffn2_w2": nrm(ks[23], (DEPTH, D_FF, D_MODEL), D_FF ** -0.5),
        "final_norm_g": 1.0 + nrm(ks[24], (D_MODEL,), 0.02),
    }


def _fwd_reference(x, c, ctx, c_ctx, ada_w, ada_b, norm_g, ffn1_w13, ffn1_w2, w_in, b_merge,
              rnn_conv_w, rnn_conv_b, lru_w_a, lru_b_a, lru_w_x, lru_b_x, lru_lambda,
              sc_conv_w, attn_sink, w_branch, w_out, ffn2_w13, ffn2_w2, final_norm_g):
    b, l, _ = x.shape
    cos, sin = axial_rope(l)
    silu_c = jax.nn.silu(c)
    silu_cc = jax.nn.silu(c_ctx)
    h, hc = x, ctx
    for layer in range(DEPTH):
        last = layer == DEPTH - 1
        mod = (silu_c @ ada_w[layer] + ada_b[layer]).reshape(b, N_MOD, 1, D_MODEL)
        modc = (silu_cc @ ada_w[layer] + ada_b[layer]).reshape(N_MOD, D_MODEL)

        u = modulate(rmsnorm(h, norm_g[layer, 0]), mod[:, 0], mod[:, 1])
        h = h + 0.5 * mod[:, 2] * swiglu(u, ffn1_w13[layer], ffn1_w2[layer])
        uc = modulate(rmsnorm(hc, norm_g[layer, 0]), modc[0], modc[1])
        hc = hc + 0.5 * modc[2] * swiglu(uc, ffn1_w13[layer], ffn1_w2[layer])

        u = modulate(rmsnorm(h, norm_g[layer, 1]), mod[:, 3], mod[:, 4])
        uc = modulate(rmsnorm(hc, norm_g[layer, 1]), modc[3], modc[4])
        y, yc = token_mixer(u, uc, cos, sin, w_in[layer], b_merge[layer], rnn_conv_w[layer],
                            rnn_conv_b[layer], lru_w_a[layer], lru_b_a[layer], lru_w_x[layer],
                            lru_b_x[layer], lru_lambda[layer], sc_conv_w[layer], attn_sink[layer],
                            w_branch[layer], w_out[layer], not last)
        h = h + mod[:, 5] * y

        u = modulate(rmsnorm(h, norm_g[layer, 2]), mod[:, 6], mod[:, 7])
        h = h + 0.5 * mod[:, 8] * swiglu(u, ffn2_w13[layer], ffn2_w2[layer])
        if not last:
            hc = hc + modc[5] * yc
            uc = modulate(rmsnorm(hc, norm_g[layer, 2]), modc[6], modc[7])
            hc = hc + 0.5 * modc[8] * swiglu(uc, ffn2_w13[layer], ffn2_w2[layer])
    return rmsnorm(h, final_norm_g)


import jax as _jax
import jax.numpy as _jnp

TWIN_FORMAT = 'train_step'
FWD_PARAMS = ['x', 'c', 'ctx', 'c_ctx', 'ada_w', 'ada_b', 'norm_g', 'ffn1_w13', 'ffn1_w2', 'w_in', 'b_merge', 'rnn_conv_w', 'rnn_conv_b', 'lru_w_a', 'lru_b_a', 'lru_w_x', 'lru_b_x', 'lru_lambda', 'sc_conv_w', 'attn_sink', 'w_branch', 'w_out', 'ffn2_w13', 'ffn2_w2', 'final_norm_g']
TWIN_WEIGHTS = ['c_ctx', 'ada_w', 'ada_b', 'norm_g', 'ffn1_w13', 'ffn1_w2', 'w_in', 'b_merge', 'rnn_conv_w', 'rnn_conv_b', 'lru_w_a', 'lru_b_a', 'lru_w_x', 'lru_b_x', 'lru_lambda', 'sc_conv_w', 'attn_sink', 'w_branch', 'w_out', 'ffn2_w13', 'ffn2_w2', 'final_norm_g']
TWIN_DIFF_INPUT = 'x'
TWIN_INPUTS = ['x', 'c', 'ctx', 'c_ctx', 'ada_w', 'ada_b', 'norm_g', 'ffn1_w13', 'ffn1_w2', 'w_in', 'b_merge', 'rnn_conv_w', 'rnn_conv_b', 'lru_w_a', 'lru_b_a', 'lru_w_x', 'lru_b_x', 'lru_lambda', 'sc_conv_w', 'attn_sink', 'w_branch', 'w_out', 'ffn2_w13', 'ffn2_w2', 'final_norm_g', 'loss_target', 'm_c_ctx', 'm_ada_w', 'm_ada_b', 'm_norm_g', 'm_ffn1_w13', 'm_ffn1_w2', 'm_w_in', 'm_b_merge', 'm_rnn_conv_w', 'm_rnn_conv_b', 'm_lru_w_a', 'm_lru_b_a', 'm_lru_w_x', 'm_lru_b_x', 'm_lru_lambda', 'm_sc_conv_w', 'm_attn_sink', 'm_w_branch', 'm_w_out', 'm_ffn2_w13', 'm_ffn2_w2', 'm_final_norm_g', 'v_c_ctx', 'v_ada_w', 'v_ada_b', 'v_norm_g', 'v_ffn1_w13', 'v_ffn1_w2', 'v_w_in', 'v_b_merge', 'v_rnn_conv_w', 'v_rnn_conv_b', 'v_lru_w_a', 'v_lru_b_a', 'v_lru_w_x', 'v_lru_b_x', 'v_lru_lambda', 'v_sc_conv_w', 'v_attn_sink', 'v_w_branch', 'v_w_out', 'v_ffn2_w13', 'v_ffn2_w2', 'v_final_norm_g']
TWIN_OUTPUTS = ['loss', 'grad_x', 'grad_c_ctx', 'grad_ada_w', 'grad_ada_b', 'grad_norm_g', 'grad_ffn1_w13', 'grad_ffn1_w2', 'grad_w_in', 'grad_b_merge', 'grad_rnn_conv_w', 'grad_rnn_conv_b', 'grad_lru_w_a', 'grad_lru_b_a', 'grad_lru_w_x', 'grad_lru_b_x', 'grad_lru_lambda', 'grad_sc_conv_w', 'grad_attn_sink', 'grad_w_branch', 'grad_w_out', 'grad_ffn2_w13', 'grad_ffn2_w2', 'grad_final_norm_g', 'delta_c_ctx', 'delta_ada_w', 'delta_ada_b', 'delta_norm_g', 'delta_ffn1_w13', 'delta_ffn1_w2', 'delta_w_in', 'delta_b_merge', 'delta_rnn_conv_w', 'delta_rnn_conv_b', 'delta_lru_w_a', 'delta_lru_b_a', 'delta_lru_w_x', 'delta_lru_b_x', 'delta_lru_lambda', 'delta_sc_conv_w', 'delta_attn_sink', 'delta_w_branch', 'delta_w_out', 'delta_ffn2_w13', 'delta_ffn2_w2', 'delta_final_norm_g', 'new_m_c_ctx', 'new_m_ada_w', 'new_m_ada_b', 'new_m_norm_g', 'new_m_ffn1_w13', 'new_m_ffn1_w2', 'new_m_w_in', 'new_m_b_merge', 'new_m_rnn_conv_w', 'new_m_rnn_conv_b', 'new_m_lru_w_a', 'new_m_lru_b_a', 'new_m_lru_w_x', 'new_m_lru_b_x', 'new_m_lru_lambda', 'new_m_sc_conv_w', 'new_m_attn_sink', 'new_m_w_branch', 'new_m_w_out', 'new_m_ffn2_w13', 'new_m_ffn2_w2', 'new_m_final_norm_g', 'new_v_c_ctx', 'new_v_ada_w', 'new_v_ada_b', 'new_v_norm_g', 'new_v_ffn1_w13', 'new_v_ffn1_w2', 'new_v_w_in', 'new_v_b_merge', 'new_v_rnn_conv_w', 'new_v_rnn_conv_b', 'new_v_lru_w_a', 'new_v_lru_b_a', 'new_v_lru_w_x', 'new_v_lru_b_x', 'new_v_lru_lambda', 'new_v_sc_conv_w', 'new_v_attn_sink', 'new_v_w_branch', 'new_v_w_out', 'new_v_ffn2_w13', 'new_v_ffn2_w2', 'new_v_final_norm_g']
TWIN_LEAF_KINDS = {'loss': 'loss', 'grad_x': 'grad_x', 'grad_c_ctx': 'grad_w', 'grad_ada_w': 'grad_w', 'grad_ada_b': 'grad_w', 'grad_norm_g': 'grad_w', 'grad_ffn1_w13': 'grad_w', 'grad_ffn1_w2': 'grad_w', 'grad_w_in': 'grad_w', 'grad_b_merge': 'grad_w', 'grad_rnn_conv_w': 'grad_w', 'grad_rnn_conv_b': 'grad_w', 'grad_lru_w_a': 'grad_w', 'grad_lru_b_a': 'grad_w', 'grad_lru_w_x': 'grad_w', 'grad_lru_b_x': 'grad_w', 'grad_lru_lambda': 'grad_w', 'grad_sc_conv_w': 'grad_w', 'grad_attn_sink': 'grad_w', 'grad_w_branch': 'grad_w', 'grad_w_out': 'grad_w', 'grad_ffn2_w13': 'grad_w', 'grad_ffn2_w2': 'grad_w', 'grad_final_norm_g': 'grad_w', 'delta_c_ctx': 'delta_w', 'delta_ada_w': 'delta_w', 'delta_ada_b': 'delta_w', 'delta_norm_g': 'delta_w', 'delta_ffn1_w13': 'delta_w', 'delta_ffn1_w2': 'delta_w', 'delta_w_in': 'delta_w', 'delta_b_merge': 'delta_w', 'delta_rnn_conv_w': 'delta_w', 'delta_rnn_conv_b': 'delta_w', 'delta_lru_w_a': 'delta_w', 'delta_lru_b_a': 'delta_w', 'delta_lru_w_x': 'delta_w', 'delta_lru_b_x': 'delta_w', 'delta_lru_lambda': 'delta_w', 'delta_sc_conv_w': 'delta_w', 'delta_attn_sink': 'delta_w', 'delta_w_branch': 'delta_w', 'delta_w_out': 'delta_w', 'delta_ffn2_w13': 'delta_w', 'delta_ffn2_w2': 'delta_w', 'delta_final_norm_g': 'delta_w', 'new_m_c_ctx': 'new_m', 'new_m_ada_w': 'new_m', 'new_m_ada_b': 'new_m', 'new_m_norm_g': 'new_m', 'new_m_ffn1_w13': 'new_m', 'new_m_ffn1_w2': 'new_m', 'new_m_w_in': 'new_m', 'new_m_b_merge': 'new_m', 'new_m_rnn_conv_w': 'new_m', 'new_m_rnn_conv_b': 'new_m', 'new_m_lru_w_a': 'new_m', 'new_m_lru_b_a': 'new_m', 'new_m_lru_w_x': 'new_m', 'new_m_lru_b_x': 'new_m', 'new_m_lru_lambda': 'new_m', 'new_m_sc_conv_w': 'new_m', 'new_m_attn_sink': 'new_m', 'new_m_w_branch': 'new_m', 'new_m_w_out': 'new_m', 'new_m_ffn2_w13': 'new_m', 'new_m_ffn2_w2': 'new_m', 'new_m_final_norm_g': 'new_m', 'new_v_c_ctx': 'new_v', 'new_v_ada_w': 'new_v', 'new_v_ada_b': 'new_v', 'new_v_norm_g': 'new_v', 'new_v_ffn1_w13': 'new_v', 'new_v_ffn1_w2': 'new_v', 'new_v_w_in': 'new_v', 'new_v_b_merge': 'new_v', 'new_v_rnn_conv_w': 'new_v', 'new_v_rnn_conv_b': 'new_v', 'new_v_lru_w_a': 'new_v', 'new_v_lru_b_a': 'new_v', 'new_v_lru_w_x': 'new_v', 'new_v_lru_b_x': 'new_v', 'new_v_lru_lambda': 'new_v', 'new_v_sc_conv_w': 'new_v', 'new_v_attn_sink': 'new_v', 'new_v_w_branch': 'new_v', 'new_v_w_out': 'new_v', 'new_v_ffn2_w13': 'new_v', 'new_v_ffn2_w2': 'new_v', 'new_v_final_norm_g': 'new_v'}


def _forward(args):
    return _fwd_reference(*[args[k] for k in FWD_PARAMS])


def _output_shape():
    def fwd():
        inp = _fwd_setup_inputs(0)
        return _fwd_reference(*[inp[k] for k in FWD_PARAMS])
    out = _jax.eval_shape(fwd)
    return out.shape, out.dtype

N_MICROBATCH = 1
ADAM_LR = 0.001
ADAM_B1 = 0.9
ADAM_B2 = 0.999
ADAM_EPS = 1e-08
ADAM_WD = 0.01
ADAM_STEP = 10
PER_EXAMPLE_BATCH_AXIS = {'x': 0, 'c': 0, 'ctx': 0, 'loss_target': 0}
SHARED_INPUTS = []
_WEIGHT_DTYPES = {'c_ctx': _jnp.float32, 'ada_w': _jnp.float32, 'ada_b': _jnp.float32, 'norm_g': _jnp.float32, 'ffn1_w13': _jnp.float32, 'ffn1_w2': _jnp.float32, 'w_in': _jnp.float32, 'b_merge': _jnp.float32, 'rnn_conv_w': _jnp.float32, 'rnn_conv_b': _jnp.float32, 'lru_w_a': _jnp.float32, 'lru_b_a': _jnp.float32, 'lru_w_x': _jnp.float32, 'lru_b_x': _jnp.float32, 'lru_lambda': _jnp.float32, 'sc_conv_w': _jnp.float32, 'attn_sink': _jnp.float32, 'w_branch': _jnp.float32, 'w_out': _jnp.float32, 'ffn2_w13': _jnp.float32, 'ffn2_w2': _jnp.float32, 'final_norm_g': _jnp.float32}
MOMENT_SCALE = {'c_ctx': 2.119206e-02, 'ada_w': 7.043434e-02, 'ada_b': 1.212648e-01, 'norm_g': 4.545663e-02, 'ffn1_w13': 8.606361e-03, 'ffn1_w2': 1.395139e-02, 'w_in': 3.902648e-02, 'b_merge': 1.464783e-02, 'rnn_conv_w': 8.998158e-02, 'rnn_conv_b': 2.396757e-01, 'lru_w_a': 4.856197e-03, 'lru_b_a': 7.996807e-03, 'lru_w_x': 1.006059e-02, 'lru_b_x': 2.082818e-02, 'lru_lambda': 2.038121e-02, 'sc_conv_w': 3.444445e-02, 'attn_sink': 1.444001e-04, 'w_branch': 3.778085e-02, 'w_out': 6.593378e-02, 'ffn2_w13': 7.629832e-03, 'ffn2_w2': 1.232248e-02, 'final_norm_g': 3.220353e+01}


def _to_microbatches(a, axis):
    t = _jnp.moveaxis(a, axis, 0)
    t = t.reshape((N_MICROBATCH, t.shape[0] // N_MICROBATCH) + t.shape[1:])
    return _jnp.moveaxis(t, 1, axis + 1)


def setup_inputs(seed: int = 0) -> dict:
    inp = _fwd_setup_inputs(seed)
    key = _jax.random.fold_in(_jax.random.key(seed), 7919)
    shape, _ = _output_shape()
    out = dict(inp)
    out["loss_target"] = _jax.random.normal(_jax.random.fold_in(key, 0), shape, _jnp.float32)
    for i, name in enumerate(TWIN_WEIGHTS):
        w = inp[name].astype(_jnp.float32)
        if MOMENT_SCALE is None:
            s = _jnp.sqrt(_jnp.mean(_jnp.square(w)) + 1e-30)
        else:
            s = MOMENT_SCALE[name]
        km, kv = _jax.random.split(_jax.random.fold_in(key, i + 1))
        out[name] = w
        out["m_" + name] = s * _jax.random.normal(km, w.shape, _jnp.float32)
        out["v_" + name] = (s * s) * _jax.random.uniform(kv, w.shape, _jnp.float32, 0.5, 1.5)
    if N_MICROBATCH > 1:
        for name, axis in PER_EXAMPLE_BATCH_AXIS.items():
            out[name] = _to_microbatches(out[name], axis)
    return {'x': out['x'], 'c': out['c'], 'ctx': out['ctx'], 'c_ctx': out['c_ctx'], 'ada_w': out['ada_w'], 'ada_b': out['ada_b'], 'norm_g': out['norm_g'], 'ffn1_w13': out['ffn1_w13'], 'ffn1_w2': out['ffn1_w2'], 'w_in': out['w_in'], 'b_merge': out['b_merge'], 'rnn_conv_w': out['rnn_conv_w'], 'rnn_conv_b': out['rnn_conv_b'], 'lru_w_a': out['lru_w_a'], 'lru_b_a': out['lru_b_a'], 'lru_w_x': out['lru_w_x'], 'lru_b_x': out['lru_b_x'], 'lru_lambda': out['lru_lambda'], 'sc_conv_w': out['sc_conv_w'], 'attn_sink': out['attn_sink'], 'w_branch': out['w_branch'], 'w_out': out['w_out'], 'ffn2_w13': out['ffn2_w13'], 'ffn2_w2': out['ffn2_w2'], 'final_norm_g': out['final_norm_g'], 'loss_target': out['loss_target'], 'm_c_ctx': out['m_c_ctx'], 'm_ada_w': out['m_ada_w'], 'm_ada_b': out['m_ada_b'], 'm_norm_g': out['m_norm_g'], 'm_ffn1_w13': out['m_ffn1_w13'], 'm_ffn1_w2': out['m_ffn1_w2'], 'm_w_in': out['m_w_in'], 'm_b_merge': out['m_b_merge'], 'm_rnn_conv_w': out['m_rnn_conv_w'], 'm_rnn_conv_b': out['m_rnn_conv_b'], 'm_lru_w_a': out['m_lru_w_a'], 'm_lru_b_a': out['m_lru_b_a'], 'm_lru_w_x': out['m_lru_w_x'], 'm_lru_b_x': out['m_lru_b_x'], 'm_lru_lambda': out['m_lru_lambda'], 'm_sc_conv_w': out['m_sc_conv_w'], 'm_attn_sink': out['m_attn_sink'], 'm_w_branch': out['m_w_branch'], 'm_w_out': out['m_w_out'], 'm_ffn2_w13': out['m_ffn2_w13'], 'm_ffn2_w2': out['m_ffn2_w2'], 'm_final_norm_g': out['m_final_norm_g'], 'v_c_ctx': out['v_c_ctx'], 'v_ada_w': out['v_ada_w'], 'v_ada_b': out['v_ada_b'], 'v_norm_g': out['v_norm_g'], 'v_ffn1_w13': out['v_ffn1_w13'], 'v_ffn1_w2': out['v_ffn1_w2'], 'v_w_in': out['v_w_in'], 'v_b_merge': out['v_b_merge'], 'v_rnn_conv_w': out['v_rnn_conv_w'], 'v_rnn_conv_b': out['v_rnn_conv_b'], 'v_lru_w_a': out['v_lru_w_a'], 'v_lru_b_a': out['v_lru_b_a'], 'v_lru_w_x': out['v_lru_w_x'], 'v_lru_b_x': out['v_lru_b_x'], 'v_lru_lambda': out['v_lru_lambda'], 'v_sc_conv_w': out['v_sc_conv_w'], 'v_attn_sink': out['v_attn_sink'], 'v_w_branch': out['v_w_branch'], 'v_w_out': out['v_w_out'], 'v_ffn2_w13': out['v_ffn2_w13'], 'v_ffn2_w2': out['v_ffn2_w2'], 'v_final_norm_g': out['v_final_norm_g']}


def _loss(weights, diff, rest, loss_target):
    with _jax.named_scope("forward"):
        args = {**rest, TWIN_DIFF_INPUT: diff, **{k: w.astype(_WEIGHT_DTYPES[k]) for k, w in weights.items()}}
        y = _forward(args)
    with _jax.named_scope("loss_head"):
        err = _jnp.square(y.astype(_jnp.float32) - loss_target)
        return 0.5 * _jnp.sum(_jnp.mean(err, axis=-1)) if err.ndim else 0.5 * err


def _adamw(w, g, m, v):
    m = ADAM_B1 * m + (1.0 - ADAM_B1) * g
    v = ADAM_B2 * v + (1.0 - ADAM_B2) * _jnp.square(g)
    m_hat = m / (1.0 - ADAM_B1 ** ADAM_STEP)
    v_hat = v / (1.0 - ADAM_B2 ** ADAM_STEP)
    delta = -ADAM_LR * (m_hat / (_jnp.sqrt(v_hat) + ADAM_EPS) + ADAM_WD * w)
    return delta, m, v


def reference(x, c, ctx, c_ctx, ada_w, ada_b, norm_g, ffn1_w13, ffn1_w2, w_in, b_merge, rnn_conv_w, rnn_conv_b, lru_w_a, lru_b_a, lru_w_x, lru_b_x, lru_lambda, sc_conv_w, attn_sink, w_branch, w_out, ffn2_w13, ffn2_w2, final_norm_g, loss_target, m_c_ctx, m_ada_w, m_ada_b, m_norm_g, m_ffn1_w13, m_ffn1_w2, m_w_in, m_b_merge, m_rnn_conv_w, m_rnn_conv_b, m_lru_w_a, m_lru_b_a, m_lru_w_x, m_lru_b_x, m_lru_lambda, m_sc_conv_w, m_attn_sink, m_w_branch, m_w_out, m_ffn2_w13, m_ffn2_w2, m_final_norm_g, v_c_ctx, v_ada_w, v_ada_b, v_norm_g, v_ffn1_w13, v_ffn1_w2, v_w_in, v_b_merge, v_rnn_conv_w, v_rnn_conv_b, v_lru_w_a, v_lru_b_a, v_lru_w_x, v_lru_b_x, v_lru_lambda, v_sc_conv_w, v_attn_sink, v_w_branch, v_w_out, v_ffn2_w13, v_ffn2_w2, v_final_norm_g):
    given = dict(x=x, c=c, ctx=ctx, c_ctx=c_ctx, ada_w=ada_w, ada_b=ada_b, norm_g=norm_g, ffn1_w13=ffn1_w13, ffn1_w2=ffn1_w2, w_in=w_in, b_merge=b_merge, rnn_conv_w=rnn_conv_w, rnn_conv_b=rnn_conv_b, lru_w_a=lru_w_a, lru_b_a=lru_b_a, lru_w_x=lru_w_x, lru_b_x=lru_b_x, lru_lambda=lru_lambda, sc_conv_w=sc_conv_w, attn_sink=attn_sink, w_branch=w_branch, w_out=w_out, ffn2_w13=ffn2_w13, ffn2_w2=ffn2_w2, final_norm_g=final_norm_g, loss_target=loss_target, m_c_ctx=m_c_ctx, m_ada_w=m_ada_w, m_ada_b=m_ada_b, m_norm_g=m_norm_g, m_ffn1_w13=m_ffn1_w13, m_ffn1_w2=m_ffn1_w2, m_w_in=m_w_in, m_b_merge=m_b_merge, m_rnn_conv_w=m_rnn_conv_w, m_rnn_conv_b=m_rnn_conv_b, m_lru_w_a=m_lru_w_a, m_lru_b_a=m_lru_b_a, m_lru_w_x=m_lru_w_x, m_lru_b_x=m_lru_b_x, m_lru_lambda=m_lru_lambda, m_sc_conv_w=m_sc_conv_w, m_attn_sink=m_attn_sink, m_w_branch=m_w_branch, m_w_out=m_w_out, m_ffn2_w13=m_ffn2_w13, m_ffn2_w2=m_ffn2_w2, m_final_norm_g=m_final_norm_g, v_c_ctx=v_c_ctx, v_ada_w=v_ada_w, v_ada_b=v_ada_b, v_norm_g=v_norm_g, v_ffn1_w13=v_ffn1_w13, v_ffn1_w2=v_ffn1_w2, v_w_in=v_w_in, v_b_merge=v_b_merge, v_rnn_conv_w=v_rnn_conv_w, v_rnn_conv_b=v_rnn_conv_b, v_lru_w_a=v_lru_w_a, v_lru_b_a=v_lru_b_a, v_lru_w_x=v_lru_w_x, v_lru_b_x=v_lru_b_x, v_lru_lambda=v_lru_lambda, v_sc_conv_w=v_sc_conv_w, v_attn_sink=v_attn_sink, v_w_branch=v_w_branch, v_w_out=v_w_out, v_ffn2_w13=v_ffn2_w13, v_ffn2_w2=v_ffn2_w2, v_final_norm_g=v_final_norm_g)
    weights = {n: given[n] for n in TWIN_WEIGHTS}
    shared = {n: given[n] for n in SHARED_INPUTS}
    per_example = {n: given[n] for n in ['x', 'c', 'ctx']}
    grad_fn = _jax.value_and_grad(_loss, argnums=(0, 1))

    def one_microbatch(ex, loss_target):
        ex = dict(ex)
        diff = ex.pop(TWIN_DIFF_INPUT)
        return grad_fn(weights, diff, {**shared, **ex}, loss_target)

    if N_MICROBATCH == 1:
        loss, (grad_w, grad_x) = one_microbatch(per_example, given["loss_target"])
    else:
        def body(carry, xs):
            loss_sum, grad_sum = carry
            l_k, (gw_k, gx_k) = one_microbatch(xs[0], xs[1])
            with _jax.named_scope("update"):
                return (loss_sum + l_k, _jax.tree.map(_jnp.add, grad_sum, gw_k)), gx_k

        init = (_jnp.zeros((), _jnp.float32), _jax.tree.map(_jnp.zeros_like, weights))
        (loss, grad_w), grad_x = _jax.lax.scan(body, init, (per_example, given["loss_target"]))
    with _jax.named_scope("update"):
        delta_w, new_m, new_v = {}, {}, {}
        for n in TWIN_WEIGHTS:
            delta_w[n], new_m[n], new_v[n] = _adamw(weights[n], grad_w[n], given["m_" + n], given["v_" + n])
    return (loss, grad_x, *[grad_w[n] for n in TWIN_WEIGHTS], *[delta_w[n] for n in TWIN_WEIGHTS],
            *[new_m[n] for n in TWIN_WEIGHTS], *[new_v[n] for n in TWIN_WEIGHTS])
```

```python
import functools
import math

import jax
import jax.numpy as jnp
from jax import lax
from jax.experimental import pallas as pl
from jax.experimental.pallas import tpu as pltpu

f32 = jnp.float32
bf16 = jnp.bfloat16
SDS = jax.ShapeDtypeStruct

N_DEV = 8
AXES = ("x", "y", "c")
HEAD_DIM = 128
GRID_W = 64
ATT_BLOCK = 128
ROPE_BASE = 10000.0
LRU_C = 8.0
EPS = 1e-6
NEG_INF = -1e30
N_MOD = 9
N_BRANCH = 3
ADAM_LR, ADAM_B1, ADAM_B2, ADAM_EPS, ADAM_WD, ADAM_STEP = 0.001, 0.9, 0.999, 1e-08, 0.01, 10
VMEM_LIMIT = 56 * 1024 * 1024
ANY = pl.BlockSpec(memory_space=pl.ANY)


def _round_up(n, m):
    return (n + m - 1) // m * m


def _tile(n, target, align):
    best = None
    t = align
    while t <= min(n, target):
        if n % t == 0:
            best = t
        t += align
    return best if best is not None else n


def _params(sem):
    return pltpu.CompilerParams(dimension_semantics=sem, vmem_limit_bytes=VMEM_LIMIT)


_DIMS = {"nn": (((1,), (0,)), ((), ())), "nt": (((1,), (1,)), ((), ())), "tn": (((0,), (0,)), ((), ()))}


def _mm(a, b, mode, out_dtype, name, *, M=None, N=None, K=None, a_off=0, b_off=0, b_lead=None,
        tm=768, tn=512, tk=2048):
    b2 = b.shape[1:] if b_lead is not None else b.shape
    if mode == "tn":
        K = a.shape[0] if K is None else K
        M = a.shape[1] if M is None else M
        N = b2[1] if N is None else N
    else:
        M = a.shape[0] if M is None else M
        K = a.shape[1] if K is None else K
        N = (b2[1] if mode == "nn" else b2[0]) if N is None else N
    g = math.gcd
    if mode == "tn":
        tm = _tile(g(M, a_off) if a_off else M, tm, 128)
        tk = _tile(K, tk, 16)
    else:
        tm = _tile(M, tm, 16)
        tk = _tile(g(K, a_off) if a_off else K, tk, 128)
    tn = _tile(g(N, b_off) if b_off else N, tn, 128)
    nk = K // tk
    ao, bo = (a_off // (tm if mode == "tn" else tk)), b_off // tn
    lead = () if b_lead is None else (b_lead,)
    ld = () if b_lead is None else (None,)
    if mode == "nn":
        a_spec = pl.BlockSpec((tm, tk), lambda i, j, k: (i, ao + k))
        b_spec = pl.BlockSpec(ld + (tk, tn), lambda i, j, k: lead + (k, bo + j))
    elif mode == "nt":
        a_spec = pl.BlockSpec((tm, tk), lambda i, j, k: (i, ao + k))
        b_spec = pl.BlockSpec(ld + (tn, tk), lambda i, j, k: lead + (bo + j, k))
    else:
        a_spec = pl.BlockSpec((tk, tm), lambda i, j, k: (k, ao + i))
        b_spec = pl.BlockSpec(ld + (tk, tn), lambda i, j, k: lead + (k, bo + j))
    dims = _DIMS[mode]

    if nk == 1:
        def body(a_ref, b_ref, o_ref):
            o_ref[...] = lax.dot_general(a_ref[...], b_ref[...], dims, preferred_element_type=f32).astype(out_dtype)
        scratch = []
    else:
        def body(a_ref, b_ref, o_ref, acc):
            k = pl.program_id(2)

            @pl.when(k == 0)
            def _():
                acc[...] = jnp.zeros_like(acc)

            acc[...] += lax.dot_general(a_ref[...], b_ref[...], dims, preferred_element_type=f32)

            @pl.when(k == nk - 1)
            def _():
                o_ref[...] = acc[...].astype(out_dtype)
        scratch = [pltpu.VMEM((tm, tn), f32)]

    return pl.pallas_call(
        body, name=name, grid=(M // tm, N // tn, nk), in_specs=[a_spec, b_spec],
        out_specs=pl.BlockSpec((tm, tn), lambda i, j, k: (i, j)), out_shape=SDS((M, N), out_dtype),
        scratch_shapes=scratch, compiler_params=_params(("parallel", "parallel", "arbitrary")),
    )(a, b)


class _Tile:
    pass


def _rowwise(name, fn, T, C, tr, rows=(), halos=(), consts=(), mods=(), outs=(), accs=(), maccs=()):
    nT, nC = T // tr, C // tr
    assert T % tr == 0 and C % tr == 0 and nC >= 1 and tr % 8 == 0
    r8, n8 = tr // 8, T // 8
    seg_of = lambda i: jnp.where(i >= nC, 1, 0)
    in_specs, args = [], []
    for arr, w, cb, ro in rows:
        nt = arr.shape[0] // tr
        in_specs.append(pl.BlockSpec((tr, w), (lambda i, cb=cb, ro=ro, nt=nt: (jnp.clip(i + ro, 0, nt - 1), cb))))
        args.append(arr)
    for arr, w, cb in halos:
        in_specs.append(pl.BlockSpec((8, w), lambda i, cb=cb: (jnp.maximum(i * r8 - 1, 0), cb)))
        in_specs.append(pl.BlockSpec((tr, w), lambda i, cb=cb: (i, cb)))
        in_specs.append(pl.BlockSpec((8, w), lambda i, cb=cb: (jnp.minimum((i + 1) * r8, n8 - 1), cb)))
        args += [arr, arr, arr]
    for arr in consts:
        in_specs.append(pl.BlockSpec(arr.shape, lambda i, nd=arr.ndim: (0,) * nd))
        args.append(arr)
    for arr in mods:
        in_specs.append(pl.BlockSpec((None,) + arr.shape[1:], lambda i, nd=arr.ndim: (seg_of(i),) + (0,) * (nd - 1)))
        args.append(arr)
    out_specs, out_shape = [], []
    for w, dt in outs:
        out_specs.append(pl.BlockSpec((tr, w), lambda i: (i, 0)))
        out_shape.append(SDS((T, w), dt))
    for shp in accs:
        out_specs.append(pl.BlockSpec(shp, lambda i, nd=len(shp): (0,) * nd))
        out_shape.append(SDS(shp, f32))
    for shp in maccs:
        out_specs.append(pl.BlockSpec((None,) + shp, lambda i, nd=len(shp): (seg_of(i),) + (0,) * nd))
        out_shape.append(SDS((2,) + shp, f32))
    nr, nh, nk, nm, no, na, nma = len(rows), len(halos), len(consts), len(mods), len(outs), len(accs), len(maccs)

    def body(*refs):
        i = pl.program_id(0)
        t = _Tile()
        t.i, t.seg = i, seg_of(i)
        p = 0
        t.R = [refs[p + k][...] for k in range(nr)]
        p += nr
        pvalid = jnp.logical_and(i != 0, i != nC)
        nvalid = jnp.logical_and(i != nC - 1, i != nT - 1)
        t.H = []
        for k in range(nh):
            pr, cu, nx = refs[p][...], refs[p + 1][...], refs[p + 2][...]
            p += 3
            pr = jnp.where(pvalid, pr, jnp.zeros_like(pr))
            nx = jnp.where(nvalid, nx, jnp.zeros_like(nx))
            t.H.append(jnp.concatenate([pr, cu, nx], axis=0))
        t.K = [refs[p + k][...] for k in range(nk)]
        p += nk
        t.M = [refs[p + k][...] for k in range(nm)]
        p += nm
        o, a, ma = fn(t)
        for k in range(no):
            ref = refs[p + k]
            pieces = o[k] if isinstance(o[k], (list, tuple)) else [o[k]]
            c0 = 0
            for pc in pieces:
                ref[:, c0:c0 + pc.shape[1]] = pc.astype(ref.dtype)
                c0 += pc.shape[1]
        p += no
        for k in range(na):
            ref = refs[p + k]

            @pl.when(i == 0)
            def _(ref=ref):
                ref[...] = jnp.zeros_like(ref)

            ref[...] += a[k]
        p += na
        for k in range(nma):
            ref = refs[p + k]

            @pl.when(jnp.logical_or(i == 0, i == nC))
            def _(ref=ref):
                ref[...] = jnp.zeros_like(ref)

            ref[...] += ma[k]

    res = pl.pallas_call(
        body, name=name, grid=(nT,), in_specs=in_specs, out_specs=out_specs, out_shape=out_shape,
        compiler_params=_params(("arbitrary",)),
    )(*args)
    return res


@jax.custom_vjp
def _bdot(a, b):
    return jnp.dot(a.astype(bf16), b.astype(bf16), preferred_element_type=f32)


def _bdot_fwd(a, b):
    return _bdot(a, b), (a, b)


def _bdot_bwd(res, ct):
    a, b = res
    ctb = ct.astype(bf16)
    da = lax.dot_general(ctb, b.astype(bf16), _DIMS["nt"], preferred_element_type=f32)
    db = lax.dot_general(a.astype(bf16), ctb, _DIMS["tn"], preferred_element_type=f32)
    return da, db


_bdot.defvjp(_bdot_fwd, _bdot_bwd)


def _rms_mod(h, g, shift, scale):
    y = h * lax.rsqrt(jnp.mean(h * h, axis=-1, keepdims=True) + EPS) * g
    return y * (1.0 + scale) + shift


def _norm_mod_fwd(h, g, modv, s, d):
    def fn(t):
        m = t.M[0]
        return [_rms_mod(t.R[0], t.K[0], m[s:s + 1], m[s + 1:s + 2])], [], []
    return _rowwise("norm_mod", fn, d.T, d.C, d.tr, rows=[(h, d.D, 0, 0)], consts=[g], mods=[modv],
                    outs=[(d.D, bf16)])[0]


def _norm_mod_bwd(h, du, dh, g, modv, s, d):
    def fn(t):
        m = t.M[0]
        _, vjp = jax.vjp(_rms_mod, t.R[0], t.K[0], m[s:s + 1], m[s + 1:s + 2])
        dx, dg, dsh, dsc = vjp(t.R[1])
        return [t.R[2] + dx], [dg], [jnp.concatenate([dsh, dsc], axis=0)]
    return _rowwise("norm_mod_bwd", fn, d.T, d.C, d.tr, rows=[(h, d.D, 0, 0), (du, d.D, 0, 0), (dh, d.D, 0, 0)],
                    consts=[g], mods=[modv], outs=[(d.D, f32)], accs=[(1, d.D)], maccs=[(2, d.D)])


def _swiglu(g, u):
    return jax.nn.silu(g) * u


def _resid_fwd(h, o, modv, gi, coef, d):
    def fn(t):
        return [t.R[0] + coef * t.M[0][gi:gi + 1] * t.R[1]], [], []
    return _rowwise("resid", fn, d.T, d.C, d.tr, rows=[(h, d.D, 0, 0), (o, d.D, 0, 0)], mods=[modv], outs=[(d.D, f32)])[0]


def _resid_bwd(dh, o, modv, gi, coef, d):
    def fn(t):
        dhv = t.R[0]
        return [coef * t.M[0][gi:gi + 1] * dhv], [], [jnp.sum(coef * dhv * t.R[1], axis=0, keepdims=True)]
    return _rowwise("resid_bwd", fn, d.T, d.C, d.tr, rows=[(dh, d.D, 0, 0), (o, d.D, 0, 0)], mods=[modv],
                    outs=[(d.D, bf16)], maccs=[(1, d.D)])


def _ffn_fwd(h, modv, g, w13t, w2, s, d):
    u = _norm_mod_fwd(h, g, modv, s, d)
    gu = _mm(u, w13t, "nt", f32, "ffn_up")
    hmid = _rowwise("swiglu", lambda t: ([_swiglu(t.R[0], t.R[1])], [], []), d.T, d.C, d.trw,
                    rows=[(gu, d.Fp, 0, 0), (gu, d.Fp, 1, 0)], outs=[(d.Fp, bf16)])[0]
    o = _mm(hmid, w2, "nn", f32, "ffn_down", tk=512)
    hn = _resid_fwd(h, o, modv, s + 2, 0.5, d)
    return hn, dict(h=h, u=u, gu=gu, hmid=hmid, o=o)


def _ffn_bwd(dh, sv, modv, g, w13t, w2, s, d):
    do, dgate = _resid_bwd(dh, sv["o"], modv, s + 2, 0.5, d)
    dhmid = _mm(do, w2, "nt", f32, "ffn_down_dx")

    def fn(t):
        _, vjp = jax.vjp(_swiglu, t.R[0], t.R[1])
        dg, du = vjp(t.R[2])
        return [[dg, du]], [], []
    dgu = _rowwise("swiglu_bwd", fn, d.T, d.C, d.trw,
                   rows=[(sv["gu"], d.Fp, 0, 0), (sv["gu"], d.Fp, 1, 0), (dhmid, d.Fp, 0, 0)], outs=[(2 * d.Fp, bf16)])[0]
    dw2 = _mm(sv["hmid"], do, "tn", bf16, "ffn_down_dw", tm=512, tn=2048, tk=768)
    du = _mm(dgu, w13t, "nn", f32, "ffn_up_dx", tk=512)
    dw13 = _mm(dgu, sv["u"], "tn", bf16, "ffn_up_dw", tm=512, tn=2048, tk=768)
    dhn, dg, dss = _norm_mod_bwd(sv["h"], du, dh, g, modv, s, d)
    return dhn, dw13, dw2, dg, dss, dgate


def _taps(ext, w, left, tr, sign):
    acc = None
    for k in range(w.shape[0]):
        o = 8 + sign * (k - left)
        term = w[k:k + 1] * ext[o:o + tr]
        acc = term if acc is None else acc + term
    return acc


def _gates(xa, wa, ba, wx, bx, lam, nb):
    bs = xa.shape[1] // nb
    out = []
    for dr in range(2):
        pa = jnp.concatenate([_bdot(xa[:, n * bs:(n + 1) * bs], wa[dr, n]) for n in range(nb)], axis=1)
        px = jnp.concatenate([_bdot(xa[:, n * bs:(n + 1) * bs], wx[dr, n]) for n in range(nb)], axis=1)
        rg = jax.nn.sigmoid(pa + ba[dr:dr + 1])
        ig = jax.nn.sigmoid(px + bx[dr:dr + 1])
        log_a = -LRU_C * rg * jax.nn.softplus(-lam[dr:dr + 1])
        a = jnp.exp(log_a)
        u = jnp.sqrt(1.0 - jnp.exp(2.0 * log_a)) * (ig * xa)
        out += [a, u]
    return out


def _combine_a(hf, hb, rg):
    return (hf + hb) * jax.nn.gelu(rg)


def _scan_order(kind, nT, nC):
    nL = nT - nC
    if kind == "F":
        return (lambda s: s), False
    if kind == "revF":
        return (lambda s: nT - 1 - s), True
    if kind == "B":
        return (lambda s: jnp.where(s < nC, nC - 1 - s, nT - 1 - (s - nC))), True
    return (lambda s: jnp.where(s < nL, nC + s, s - nL)), False


def _scan_fwd(au, dr, d):
    R, tc = d.BW, d.tr
    tile_of, down = _scan_order("F" if dr == 0 else "B", d.T // tc, d.C // tc)

    def body(a_ref, u_ref, h_ref, hp_ref, st):
        @pl.when(pl.program_id(0) == 0)
        def _():
            st[...] = jnp.zeros_like(st)

        def grp(gi, h):
            r = pl.multiple_of((tc // 8 - 1 - gi if down else gi) * 8, 8)
            at, ut = a_ref[pl.ds(r, 8), :], u_ref[pl.ds(r, 8), :]
            hs, hps = [None] * 8, [None] * 8
            for k in (range(7, -1, -1) if down else range(8)):
                hps[k] = h
                h = at[k:k + 1] * h + ut[k:k + 1]
                hs[k] = h
            h_ref[pl.ds(r, 8), :] = jnp.concatenate(hs, axis=0)
            hp_ref[pl.ds(r, 8), :] = jnp.concatenate(hps, axis=0)
            return h
        st[...] = lax.fori_loop(0, tc // 8, grp, st[...])

    return pl.pallas_call(
        body, name="lru_scan", grid=(d.T // tc,),
        in_specs=[pl.BlockSpec((tc, R), lambda s: (tile_of(s), 2 * dr)), pl.BlockSpec((tc, R), lambda s: (tile_of(s), 2 * dr + 1))],
        out_specs=[pl.BlockSpec((tc, R), lambda s: (tile_of(s), 0))] * 2, out_shape=[SDS((d.T, R), f32)] * 2,
        scratch_shapes=[pltpu.VMEM((1, R), f32)], compiler_params=_params(("arbitrary",)),
    )(au, au)


def _scan_bwd(dh, au, hp, dr, d):
    R, tc = d.BW, d.tr
    tile_of, down = _scan_order("revF" if dr == 0 else "revB", d.T // tc, d.C // tc)

    def body(dh_ref, a_ref, hp_ref, o_ref, st):
        @pl.when(pl.program_id(0) == 0)
        def _():
            st[...] = jnp.zeros_like(st)

        def grp(gi, carry):
            lam, an = carry
            r = pl.multiple_of((tc // 8 - 1 - gi if down else gi) * 8, 8)
            at, dt, ht = a_ref[pl.ds(r, 8), :], dh_ref[pl.ds(r, 8), :], hp_ref[pl.ds(r, 8), :]
            ls = [None] * 8
            for k in (range(7, -1, -1) if down else range(8)):
                lam = dt[k:k + 1] + an * lam
                an = at[k:k + 1]
                ls[k] = lam
            lt = jnp.concatenate(ls, axis=0)
            o_ref[pl.ds(r, 8), 0:R] = lt * ht
            o_ref[pl.ds(r, 8), R:2 * R] = lt
            return lam, an
        lam, an = lax.fori_loop(0, tc // 8, grp, (st[0:1], st[1:2]))
        st[0:1] = lam
        st[1:2] = an

    return pl.pallas_call(
        body, name="lru_scan_bwd", grid=(d.T // tc,),
        in_specs=[pl.BlockSpec((tc, R), lambda s: (tile_of(s), 0)), pl.BlockSpec((tc, R), lambda s: (tile_of(s), 2 * dr)),
                  pl.BlockSpec((tc, R), lambda s: (tile_of(s), 0))],
        out_specs=pl.BlockSpec((tc, 2 * R), lambda s: (tile_of(s), 0)), out_shape=SDS((d.T, 2 * R), f32),
        scratch_shapes=[pltpu.VMEM((2, R), f32)], compiler_params=_params(("arbitrary",)),
    )(dh, au, hp)


def _swap_pairs(x):
    w = x.shape[1]
    lane = lax.broadcasted_iota(jnp.int32, x.shape, 1)
    return jnp.where(lane % 64 < 32, pltpu.roll(x, w - 32, 1), pltpu.roll(x, 32, 1))


def _att_masks(blk, nB, nCb, G):
    rows, cols = G * ATT_BLOCK, 3 * ATT_BLOCK
    qi = lax.broadcasted_iota(jnp.int32, (rows, cols), 0) % ATT_BLOCK
    kj = lax.broadcasted_iota(jnp.int32, (rows, cols), 1)
    rel = kj - ATT_BLOCK - qi
    kb = kj // ATT_BLOCK
    one = jnp.int32(1)
    latent = jnp.where(blk >= nCb, one, 0)
    prev_ok = jnp.where(blk - 1 >= nCb, latent, 0)
    next_ok = jnp.where(blk + 1 <= nB - 1, latent, 0)
    bv = jnp.where(kb == 0, prev_ok, jnp.where(kb == 1, latent, next_ok))
    return jnp.logical_and(jnp.abs(rel) <= ATT_BLOCK, bv > 0)


def _att_specs(d):
    G, nB = d.G, d.T // ATT_BLOCK
    blk = lambda f: pl.BlockSpec((ATT_BLOCK, HEAD_DIM), lambda kh, b: (f(b), kh))
    three = [blk(lambda b: jnp.maximum(b - 1, 0)), blk(lambda b: b), blk(lambda b: jnp.minimum(b + 1, nB - 1))]
    ctxs = pl.BlockSpec((d.C, HEAD_DIM), lambda kh, b: (0, kh))
    qs = pl.BlockSpec((ATT_BLOCK, G * HEAD_DIM), lambda kh, b: (b, kh))
    sk = pl.BlockSpec((None, G * ATT_BLOCK, 1), lambda kh, b: (kh, 0, 0))
    return qs, three, ctxs, sk


def _stack_heads(x, G):
    return jnp.concatenate([x[:, g * HEAD_DIM:(g + 1) * HEAD_DIM] for g in range(G)], axis=0)


def _unstack_heads(x, G):
    return jnp.concatenate([x[g * ATT_BLOCK:(g + 1) * ATT_BLOCK] for g in range(G)], axis=1)


def _att_probs(qg, kcat, kctx, sink, valid):
    scale = HEAD_DIM ** -0.5
    s = lax.dot_general(qg, kcat, _DIMS["nt"], preferred_element_type=f32) * scale
    s = jnp.where(valid, s, NEG_INF)
    sc = lax.dot_general(qg, kctx, _DIMS["nt"], preferred_element_type=f32) * scale
    m = jnp.maximum(jnp.maximum(jnp.max(s, axis=1, keepdims=True), jnp.max(sc, axis=1, keepdims=True)), sink)
    e, ec, es = jnp.exp(s - m), jnp.exp(sc - m), jnp.exp(sink - m)
    inv = 1.0 / (jnp.sum(e, axis=1, keepdims=True) + jnp.sum(ec, axis=1, keepdims=True) + es)
    return e * inv, ec * inv, es * inv


def _attn_fwd(qr, kr, vb, sinkrows, d):
    G, nB, nCb = d.G, d.T // ATT_BLOCK, d.C // ATT_BLOCK
    qs, three, ctxs, sk = _att_specs(d)

    def body(q_ref, k0, k1, k2, v0, v1, v2, kc_ref, vc_ref, s_ref, o_ref):
        b = pl.program_id(1)
        qg = _stack_heads(q_ref[...], G)
        kcat = jnp.concatenate([k0[...], k1[...], k2[...]], axis=0)
        vcat = jnp.concatenate([v0[...], v1[...], v2[...]], axis=0)
        p, pc, _ = _att_probs(qg, kcat, kc_ref[...], s_ref[...], _att_masks(b, nB, nCb, G))
        o = jnp.dot(p.astype(bf16), vcat, preferred_element_type=f32) + jnp.dot(pc.astype(bf16), vc_ref[...], preferred_element_type=f32)
        o_ref[...] = _unstack_heads(o, G).astype(bf16)

    return pl.pallas_call(
        body, name="attn", grid=(d.NKV, nB), in_specs=[qs] + three + three + [ctxs, ctxs, sk],
        out_specs=qs, out_shape=SDS((d.T, d.NH * HEAD_DIM), bf16), compiler_params=_params(("parallel", "arbitrary")),
    )(qr, kr, kr, kr, vb, vb, vb, kr, vb, sinkrows)


def _attn_bwd(qr, kr, vb, sinkrows, dy, d):
    G, nB, nCb = d.G, d.T // ATT_BLOCK, d.C // ATT_BLOCK
    qs, three, ctxs, sk = _att_specs(d)
    KW = d.NKV * HEAD_DIM
    part = pl.BlockSpec((ATT_BLOCK, HEAD_DIM), lambda kh, b: (b, kh))

    def body(q_ref, k0, k1, k2, v0, v1, v2, kc_ref, vc_ref, s_ref, dy_ref,
             dq_ref, dk0, dk1, dk2, dv0, dv1, dv2, dkc_ref, dvc_ref, ds_ref):
        b = pl.program_id(1)
        scale = HEAD_DIM ** -0.5
        qg = _stack_heads(q_ref[...], G)
        kcat = jnp.concatenate([k0[...], k1[...], k2[...]], axis=0)
        vcat = jnp.concatenate([v0[...], v1[...], v2[...]], axis=0)
        kctx, vctx = kc_ref[...], vc_ref[...]
        p, pc, ps = _att_probs(qg, kcat, kctx, s_ref[...], _att_masks(b, nB, nCb, G))
        dog = _stack_heads(dy_ref[...], G).astype(bf16)
        dp = lax.dot_general(dog, vcat, _DIMS["nt"], preferred_element_type=f32)
        dpc = lax.dot_general(dog, vctx, _DIMS["nt"], preferred_element_type=f32)
        delta = jnp.sum(p * dp, axis=1, keepdims=True) + jnp.sum(pc * dpc, axis=1, keepdims=True)
        ds = (p * (dp - delta) * scale).astype(bf16)
        dsc = (pc * (dpc - delta) * scale).astype(bf16)
        dq = jnp.dot(ds, kcat, preferred_element_type=f32) + jnp.dot(dsc, kctx, preferred_element_type=f32)
        dq_ref[...] = _unstack_heads(dq, G)
        dk = lax.dot_general(ds, qg, _DIMS["tn"], preferred_element_type=f32)
        dv = lax.dot_general(p.astype(bf16), dog, _DIMS["tn"], preferred_element_type=f32)
        for j, (rk, rv) in enumerate(((dk0, dv0), (dk1, dv1), (dk2, dv2))):
            rk[...] = dk[j * ATT_BLOCK:(j + 1) * ATT_BLOCK]
            rv[...] = dv[j * ATT_BLOCK:(j + 1) * ATT_BLOCK]

        @pl.when(b == 0)
        def _():
            dkc_ref[...] = jnp.zeros_like(dkc_ref)
            dvc_ref[...] = jnp.zeros_like(dvc_ref)
            ds_ref[...] = jnp.zeros_like(ds_ref)

        dkc_ref[...] += lax.dot_general(dsc, qg, _DIMS["tn"], preferred_element_type=f32)
        dvc_ref[...] += lax.dot_general(pc.astype(bf16), dog, _DIMS["tn"], preferred_element_type=f32)
        ds_ref[...] += -ps * delta

    kv = SDS((d.T, KW), f32)
    return pl.pallas_call(
        body, name="attn_bwd", grid=(d.NKV, nB), in_specs=[qs] + three + three + [ctxs, ctxs, sk, qs],
        out_specs=[qs] + [part] * 6 + [ctxs, ctxs, sk],
        out_shape=[SDS((d.T, d.NH * HEAD_DIM), f32)] + [kv] * 6 + [SDS((d.C, KW), f32)] * 2 + [SDS((d.NKV, G * ATT_BLOCK, 1), f32)],
        compiler_params=_params(("parallel", "arbitrary")),
    )(qr, kr, kr, kr, vb, vb, vb, kr, vb, sinkrows, dy)


def _mixer_fwd(h, modv, P, W, tabs, d):
    D, BW, T, C = d.D, d.BW, d.T, d.C
    u = _norm_mod_fwd(h, P["norm_g"][1:2], modv, 3, d)
    za = _mm(u, W["wint"], "nt", f32, "in_proj_a", N=d.NA)
    gz = _mm(u, W["wint"], "nt", f32, "in_proj_g", N=3 * D, b_off=d.NA)
    xa = _rowwise("lru_conv", lambda t: ([_taps(t.H[0], t.K[0], 2, d.tr, 1) + t.K[1]], [], []), T, C, d.tr,
                  halos=[(za, BW, 0)], consts=[P["rnn_conv_w"], P["rnn_conv_b"]], outs=[(BW, f32)])[0]
    lru = [P["lru_w_a"], P["lru_b_a"], P["lru_w_x"], P["lru_b_x"], P["lru_lambda"]]
    au = _rowwise("lru_gates", lambda t: ([_gates(t.R[0], *t.K, d.NB)], [], []), T, C, d.tr,
                  rows=[(xa, BW, 0, 0)], consts=lru, outs=[(4 * BW, f32)])[0]
    hf, hpf = _scan_fwd(au, 0, d)
    hb, hpb = _scan_fwd(au, 1, d)
    ya = _rowwise("lru_out", lambda t: ([_combine_a(*t.R)], [], []), T, C, d.tr,
                  rows=[(hf, BW, 0, 0), (hb, BW, 0, 0), (za, BW, 1, 0)], outs=[(BW, bf16)])[0]
    yb = _rowwise("sconv", lambda t: ([t.R[0] * _taps(t.H[0] * t.H[1], t.K[0], 1, d.tr, 1)], [], []), T, C, d.tr,
                  rows=[(za, BW, 2, 0)], halos=[(za, BW, 3), (za, BW, 4)], consts=[P["sc_conv_w"]], outs=[(BW, bf16)])[0]
    QW, KW = d.NH * HEAD_DIM, d.NKV * HEAD_DIM

    def rope(t):
        q, k, v, cq, sq, ck, skn = t.R
        return [q * cq + _swap_pairs(q) * sq, k * ck + _swap_pairs(k) * skn, v], [], []
    kcb = (5 * BW + QW) // KW
    qr, kr, vb = _rowwise("rope", rope, T, C, d.tr,
                          rows=[(za, QW, (5 * BW) // QW, 0), (za, KW, kcb, 0), (za, KW, kcb + 1, 0),
                                (tabs["cos"], QW, 0, 0), (tabs["sin"], QW, 0, 0), (tabs["cos"], KW, 0, 0), (tabs["sin"], KW, 0, 0)],
                          outs=[(QW, bf16), (KW, bf16), (KW, bf16)])
    yatt = _attn_fwd(qr, kr, vb, P["sinkrows"], d)
    ys = (ya, yb, yatt)
    ps = [_mm(ys[i], W["wbt"], "nt", f32, "lift", b_lead=i, tn=1024) for i in range(N_BRANCH)]

    def merge(t):
        gzv, bm = t.R[0], t.K[0]
        acc = None
        for i in range(N_BRANCH):
            term = jax.nn.sigmoid(gzv[:, i * D:(i + 1) * D] + bm[i:i + 1]) * t.R[1 + i]
            acc = term if acc is None else acc + term
        return [acc], [], []
    merged = _rowwise("merge", merge, T, C, d.trw, rows=[(gz, 3 * D, 0, 0)] + [(p, D, 0, 0) for p in ps],
                      consts=[P["b_merge"]], outs=[(D, bf16)])[0]
    y = _mm(merged, W["wout"], "nn", f32, "out_proj", tn=1024)
    hn = _resid_fwd(h, y, modv, 5, 1.0, d)
    sv = dict(h=h, u=u, za=za, gz=gz, xa=xa, au=au, hf=hf, hpf=hpf, hb=hb, hpb=hpb, ys=ys, qr=qr, kr=kr, vb=vb,
              ps=ps, merged=merged, y=y)
    return hn, sv


def _mixer_bwd(dh, sv, modv, P, W, tabs, d):
    D, BW, T, C = d.D, d.BW, d.T, d.C
    QW, KW = d.NH * HEAD_DIM, d.NKV * HEAD_DIM
    za, gz = sv["za"], sv["gz"]
    dy, dgate = _resid_bwd(dh, sv["y"], modv, 5, 1.0, d)
    dmerged = _mm(dy, W["wout"], "nt", f32, "out_proj_dx", tn=1024)
    dwout = _mm(sv["merged"], dy, "tn", bf16, "out_proj_dw", tm=512, tn=2048, tk=768)

    def merge_bwd(t):
        gzv, bm, dm = t.R[0], t.K[0], t.R[4]
        dps, dgs, dbs = [], [], []
        for i in range(N_BRANCH):
            gate = jax.nn.sigmoid(gzv[:, i * D:(i + 1) * D] + bm[i:i + 1])
            dps.append(dm * gate)
            dgi = dm * t.R[1 + i] * gate * (1.0 - gate)
            dgs.append(dgi)
            dbs.append(jnp.sum(dgi, axis=0, keepdims=True))
        return [dps, dgs], [jnp.concatenate(dbs, axis=0)], []
    dp, dgz, dbm = _rowwise("merge_bwd", merge_bwd, T, C, d.trw,
                            rows=[(gz, 3 * D, 0, 0)] + [(p, D, 0, 0) for p in sv["ps"]] + [(dmerged, D, 0, 0)],
                            consts=[P["b_merge"]], outs=[(3 * D, bf16), (3 * D, bf16)], accs=[(N_BRANCH, D)])
    dys = [_mm(dp, W["wbt"], "nn", f32, "lift_dx", K=D, a_off=i * D, b_lead=i, tn=1024) for i in range(N_BRANCH)]
    dwbt = jnp.stack([_mm(dp, sv["ys"][i], "tn", bf16, "lift_dw", M=D, a_off=i * D, tm=512, tn=1024, tk=768)
                      for i in range(N_BRANCH)])
    def out_bwd(t):
        _, vjp = jax.vjp(_combine_a, t.R[1], t.R[2], t.R[3])
        dhf, _, drg = vjp(t.R[0])
        return [dhf, drg], [], []
    dhs, drg = _rowwise("lru_out_bwd", out_bwd, T, C, d.tr,
                        rows=[(dys[0], BW, 0, 0), (sv["hf"], BW, 0, 0), (sv["hb"], BW, 0, 0), (za, BW, 1, 0)],
                        outs=[(BW, f32), (BW, bf16)])
    dau0 = _scan_bwd(dhs, sv["au"], sv["hpf"], 0, d)
    dau1 = _scan_bwd(dhs, sv["au"], sv["hpb"], 1, d)
    lru = [P["lru_w_a"], P["lru_b_a"], P["lru_w_x"], P["lru_b_x"], P["lru_lambda"]]

    def gates_bwd(t):
        _, vjp = jax.vjp(lambda xa, *k: _gates(xa, *k, d.NB), t.R[0], *t.K)
        d0, d1 = t.R[1], t.R[2]
        g = vjp([d0[:, :BW], d0[:, BW:], d1[:, :BW], d1[:, BW:]])
        return [g[0]], list(g[1:]), []
    dxa, dwa, dba, dwx, dbx, dlam = _rowwise(
        "lru_gates_bwd", gates_bwd, T, C, d.tr, rows=[(sv["xa"], BW, 0, 0), (dau0, 2 * BW, 0, 0), (dau1, 2 * BW, 0, 0)],
        consts=lru, outs=[(BW, f32)], accs=[p.shape for p in lru])

    def conv_bwd(t):
        dxe, xe, w = t.H[0], t.H[1], t.K[0]
        cur = dxe[8:8 + d.tr]
        dw = jnp.concatenate([jnp.sum(cur * xe[8 + k - 2:8 + k - 2 + d.tr], axis=0, keepdims=True) for k in range(w.shape[0])], axis=0)
        return [_taps(dxe, w, 2, d.tr, -1)], [dw, jnp.sum(cur, axis=0, keepdims=True)], []
    drx, dcw, dcb = _rowwise("lru_conv_bwd", conv_bwd, T, C, d.tr, halos=[(dxa, BW, 0), (za, BW, 0)],
                             consts=[P["rnn_conv_w"]], outs=[(BW, bf16)], accs=[P["rnn_conv_w"].shape, (1, BW)])
    def sconv_bwd(t):
        scg, sx, sb, dyb = t.H
        w, tr = t.K[0], d.tr
        me = scg * sx
        dsb = dyb[8:8 + tr] * _taps(me, w, 1, tr, 1)
        dce = dyb * sb
        dm = _taps(dce, w, 1, tr, -1)
        cur = dce[8:8 + tr]
        dw = jnp.concatenate([jnp.sum(cur * me[8 + k - 1:8 + k - 1 + tr], axis=0, keepdims=True) for k in range(w.shape[0])], axis=0)
        return [[dsb, dm * sx[8:8 + tr], dm * scg[8:8 + tr]]], [dw], []
    dsc, dscw = _rowwise("sconv_bwd", sconv_bwd, T, C, d.tr, halos=[(za, BW, 3), (za, BW, 4), (za, BW, 2), (dys[1], BW, 0)],
                         consts=[P["sc_conv_w"]], outs=[(3 * BW, bf16)], accs=[P["sc_conv_w"].shape])
    dqr, dk0, dk1, dk2, dv0, dv1, dv2, dkc, dvc, dsink = _attn_bwd(sv["qr"], sv["kr"], sv["vb"], P["sinkrows"], dys[2], d)
    nB, nCb = T // ATT_BLOCK, C // ATT_BLOCK

    def att_join(t):
        dq, a1, a0, a2, b1, b0, b2, kc, vc, cq, sq, ck, skn = t.R
        up = t.i + 1 <= nB - 1
        dn = t.i >= 1
        isc = t.seg == 0
        dk = a1 + jnp.where(up, a0, 0.0) + jnp.where(dn, a2, 0.0) + jnp.where(isc, kc, 0.0)
        dv = b1 + jnp.where(up, b0, 0.0) + jnp.where(dn, b2, 0.0) + jnp.where(isc, vc, 0.0)
        return [[dq * cq + _swap_pairs(dq * sq), dk * ck + _swap_pairs(dk * skn), dv]], [], []
    dqkv = _rowwise("attn_join", att_join, T, C, ATT_BLOCK,
                    rows=[(dqr, QW, 0, 0), (dk1, KW, 0, 0), (dk0, KW, 0, 1), (dk2, KW, 0, -1), (dv1, KW, 0, 0), (dv0, KW, 0, 1),
                          (dv2, KW, 0, -1), (dkc, KW, 0, 0), (dvc, KW, 0, 0), (tabs["cos"], QW, 0, 0), (tabs["sin"], QW, 0, 0),
                          (tabs["cos"], KW, 0, 0), (tabs["sin"], KW, 0, 0)], outs=[(QW + 2 * KW, bf16)])[0]
    dz = jnp.concatenate([drx, drg, dsc, dqkv, dgz], axis=1)
    du = _mm(dz, W["wint"], "nn", f32, "in_proj_dx", tk=512)
    dwint = _mm(dz, sv["u"], "tn", bf16, "in_proj_dw", tm=512, tn=2048, tk=768)
    dhn, dg, dss = _norm_mod_bwd(sv["h"], du, dh, P["norm_g"][1:2], modv, 3, d)
    grads = dict(wint=dwint, wbt=dwbt, wout=dwout, norm_g1=dg, dss=dss, dgate=dgate, b_merge=dbm, rnn_conv_w=dcw, rnn_conv_b=dcb,
                 lru_w_a=dwa, lru_b_a=dba, lru_w_x=dwx, lru_b_x=dbx, lru_lambda=dlam, sc_conv_w=dscw, sinkrows=dsink)
    return dhn, grads


def _coords():
    return lax.axis_index("x"), lax.axis_index("y"), lax.axis_index("c")


def _peer(xyc, r):
    x, y, c = xyc
    return (1 - x if r & 4 else x, 1 - y if r & 2 else y, 1 - c if r & 1 else c)


def _index(xyc):
    return xyc[0] * 4 + xyc[1] * 2 + xyc[2]


def _exchange(name, items):
    n = len(items)

    def body(*refs):
        srcs, dsts = refs[:n], refs[2 * n:3 * n]
        send, recv, lsem = refs[3 * n:]
        me3 = _coords()
        me = _index(me3)
        local = [pltpu.make_async_copy(items[k][2](srcs[k], me, me, 0), items[k][3](dsts[k], me, 0), lsem.at[k]) for k in range(n)]
        for cp in local:
            cp.start()
        sends = []
        for k in range(n):
            for r in range(1, N_DEV):
                p3 = _peer(me3, r)
                cp = pltpu.make_async_remote_copy(
                    src_ref=items[k][2](srcs[k], me, _index(p3), r), dst_ref=items[k][3](dsts[k], me, r),
                    send_sem=send.at[k * 7 + r - 1], recv_sem=recv.at[k * 7 + r - 1], device_id=p3, device_id_type=pl.DeviceIdType.MESH)
                cp.start()
                sends.append(cp)
        for k in range(n):
            for r in range(1, N_DEV):
                p3 = _peer(me3, r)
                pi = _index(p3)
                pltpu.make_async_remote_copy(
                    src_ref=items[k][2](srcs[k], me, pi, r), dst_ref=items[k][3](dsts[k], pi, r),
                    send_sem=send.at[k * 7 + r - 1], recv_sem=recv.at[k * 7 + r - 1], device_id=p3,
                    device_id_type=pl.DeviceIdType.MESH).wait_recv()
        for cp in sends:
            cp.wait_send()
        for cp in local:
            cp.wait()

    srcs = [it[0] for it in items]
    inits = [it[1] for it in items]
    return pl.pallas_call(
        body, name=name, in_specs=[ANY] * (2 * n), out_specs=[ANY] * n,
        out_shape=[SDS(a.shape, a.dtype) for a in inits], input_output_aliases={n + k: k for k in range(n)},
        scratch_shapes=[pltpu.SemaphoreType.DMA((7 * n,)), pltpu.SemaphoreType.DMA((7 * n,)), pltpu.SemaphoreType.DMA((n,))],
    )(*srcs, *inits)


def _allgather(name, arrays):
    items = [(a, jnp.zeros((N_DEV,) + a.shape, a.dtype), lambda ref, me, pi, r: ref, lambda ref, sender, r: ref.at[sender])
             for a in arrays]
    return _exchange(name, items)


def _sum8(x):
    n = x.shape[1]
    tn = _tile(n, 2048, 8)

    def body(x_ref, o_ref):
        acc = x_ref[0]
        for j in range(1, N_DEV):
            acc = acc + x_ref[j]
        o_ref[...] = acc
    return pl.pallas_call(body, name="sum8", grid=(n // tn,), in_specs=[pl.BlockSpec((N_DEV, tn, 128), lambda i: (0, i, 0))],
                          out_specs=pl.BlockSpec((tn, 128), lambda i: (i, 0)), out_shape=SDS((n, 128), f32),
                          compiler_params=_params(("parallel",)))(x)


def _cast_direct(w):
    ly, R, Cn = w.shape
    tc = _tile(Cn, 512, 128)
    spec = pl.BlockSpec((None, R, tc), lambda l, j: (l, 0, j))

    def body(w_ref, o_ref):
        o_ref[...] = w_ref[...].astype(bf16)
    return pl.pallas_call(body, name="cast", grid=(ly, Cn // tc), in_specs=[spec], out_specs=spec, out_shape=SDS(w.shape, bf16),
                          compiler_params=_params(("parallel", "parallel")))(w)


def _cast_transposed(w):
    ly, I, K, Nl = w.shape
    tk = _tile(K, 256, 128)

    def body(w_ref, o_ref):
        o_ref[...] = w_ref[...].T.astype(bf16)
    return pl.pallas_call(
        body, name="cast_t", grid=(ly, I, K // tk), in_specs=[pl.BlockSpec((None, None, tk, Nl), lambda l, i, k: (l, i, k, 0))],
        out_specs=pl.BlockSpec((None, None, Nl, tk), lambda l, i, k: (l, i, 0, k)), out_shape=SDS((ly, I, Nl, K), bf16),
        compiler_params=_params(("parallel", "parallel", "parallel")))(w)


def _adam(g, w, m, v):
    m = ADAM_B1 * m + (1.0 - ADAM_B1) * g
    v = ADAM_B2 * v + (1.0 - ADAM_B2) * (g * g)
    m_hat = m / (1.0 - ADAM_B1 ** ADAM_STEP)
    v_hat = v / (1.0 - ADAM_B2 ** ADAM_STEP)
    delta = -ADAM_LR * (m_hat / (jnp.sqrt(v_hat) + ADAM_EPS) + ADAM_WD * w)
    return delta, m, v


def _slot_sum(x):
    acc = x[0].astype(f32)
    for r in range(1, x.shape[0]):
        acc = acc + x[r].astype(f32)
    return acc


def _adamw_direct(g, w, m, v, slots):
    ly, R, Cn = w.shape
    tr = R if R <= 1024 else _tile(R, 512, 16)
    tc = _tile(Cn, max(128, (256 * 1024) // tr // 128 * 128), 128)
    spec = pl.BlockSpec((None, tr, tc), lambda l, i, j: (l, i, j))
    gspec = pl.BlockSpec((None, N_DEV, tr, tc), lambda l, i, j: (l, 0, i, j)) if slots else spec

    def body(g_ref, w_ref, m_ref, v_ref, go, do, mo, vo):
        gv = _slot_sum(g_ref[...]) if slots else g_ref[...]
        go[...] = gv
        do[...], mo[...], vo[...] = _adam(gv, w_ref[...], m_ref[...], v_ref[...])
    return pl.pallas_call(body, name="adamw", grid=(ly, R // tr, Cn // tc), in_specs=[gspec, spec, spec, spec], out_specs=[spec] * 4,
                          out_shape=[SDS(w.shape, f32)] * 4, compiler_params=_params(("parallel",) * 3))(g, w, m, v)


def _adamw_transposed(g, w, m, v):
    ly, I, K, Nl = w.shape
    tk = _tile(K, 128, 128)
    spec = pl.BlockSpec((None, None, tk, Nl), lambda l, i, k: (l, i, k, 0))
    gspec = pl.BlockSpec((None, N_DEV, None, Nl, tk), lambda l, i, k: (l, 0, i, 0, k))

    def body(g_ref, w_ref, m_ref, v_ref, go, do, mo, vo):
        gv = _slot_sum(g_ref[...]).T
        go[...] = gv
        do[...], mo[...], vo[...] = _adam(gv, w_ref[...], m_ref[...], v_ref[...])
    return pl.pallas_call(body, name="adamw_t", grid=(ly, I, K // tk), in_specs=[gspec, spec, spec, spec], out_specs=[spec] * 4,
                          out_shape=[SDS(w.shape, f32)] * 4, compiler_params=_params(("parallel",) * 3))(g, w, m, v)


def _ada_fwd(cvec, ada_w, ada_b_cols):
    ly, D, cols = ada_w.shape
    tn = _tile(cols, 768, 128)

    def body(c_ref, w_ref, b_ref, o_ref):
        o_ref[...] = _bdot(jax.nn.silu(c_ref[...]), w_ref[...]) + b_ref[...]
    return pl.pallas_call(
        body, name="ada", grid=(ly, cols // tn),
        in_specs=[pl.BlockSpec((16, D), lambda l, j: (0, 0)), pl.BlockSpec((None, D, tn), lambda l, j: (l, 0, j)),
                  pl.BlockSpec((None, 1, tn), lambda l, j: (l, 0, j))],
        out_specs=pl.BlockSpec((None, 16, tn), lambda l, j: (l, 0, j)), out_shape=SDS((ly, 16, cols), f32),
        compiler_params=_params(("parallel", "parallel")))(cvec, ada_w, ada_b_cols)


def _ada_bwd(cvec, ada_w, dm):
    ly, D, cols = ada_w.shape
    tn = _tile(cols, 768, 128)

    def body(c_ref, w_ref, d_ref, gw_ref, gc_ref):
        first = jnp.logical_and(pl.program_id(0) == 0, pl.program_id(1) == 0)

        @pl.when(first)
        def _():
            gc_ref[...] = jnp.zeros_like(gc_ref)

        def f(cv, w):
            return _bdot(jax.nn.silu(cv), w)
        _, vjp = jax.vjp(f, c_ref[...], w_ref[...])
        dc, dw = vjp(d_ref[...])
        gw_ref[...] = dw
        gc_ref[...] += dc
    return pl.pallas_call(
        body, name="ada_bwd", grid=(ly, cols // tn),
        in_specs=[pl.BlockSpec((16, D), lambda l, j: (0, 0)), pl.BlockSpec((None, D, tn), lambda l, j: (l, 0, j)),
                  pl.BlockSpec((None, 16, tn), lambda l, j: (l, 0, j))],
        out_specs=[pl.BlockSpec((None, D, tn), lambda l, j: (l, 0, j)), pl.BlockSpec((16, D), lambda l, j: (0, 0))],
        out_shape=[SDS(ada_w.shape, f32), SDS((16, D), f32)], compiler_params=_params(("arbitrary", "arbitrary")))(cvec, ada_w, dm)


def _flat_adamw(g, w, m, v):
    n = w.shape[0]
    tn = _tile(n, 1024, 8)
    spec = pl.BlockSpec((tn, 128), lambda i: (i, 0))

    def body(g_ref, w_ref, m_ref, v_ref, do, mo, vo):
        do[...], mo[...], vo[...] = _adam(g_ref[...], w_ref[...], m_ref[...], v_ref[...])
    return pl.pallas_call(body, name="adamw_small", grid=(n // tn,), in_specs=[spec] * 4, out_specs=[spec] * 3,
                          out_shape=[SDS(w.shape, f32)] * 3, compiler_params=_params(("parallel",)))(g, w, m, v)


def _pack(arrs):
    flat = jnp.concatenate([a.reshape(-1).astype(f32) for a in arrs])
    n = _round_up(flat.shape[0], 1024)
    return jnp.pad(flat, (0, n - flat.shape[0])).reshape(n // 128, 128)


def _unpack(flat, shapes, lead=()):
    flat = flat.reshape(lead + (-1,))
    out, off = [], 0
    for s in shapes:
        sz = math.prod(s)
        out.append(flat[..., off:off + sz].reshape(lead + tuple(s)))
        off += sz
    return out


def _unshard_last(g):
    g = jnp.moveaxis(g, 0, -2)
    return g.reshape(g.shape[:-2] + (g.shape[-2] * g.shape[-1],))


class _Dims:
    pass


def _rope_tables(L, C, NH):
    rows = L // GRID_W
    row = jnp.repeat(jnp.arange(rows), GRID_W).astype(f32)
    col = jnp.tile(jnp.arange(GRID_W), rows).astype(f32)
    half = HEAD_DIM // 2
    inv = ROPE_BASE ** (-jnp.arange(0, half, 2, dtype=f32) / half)
    ar, ac = row[:, None] * inv, col[:, None] * inv
    cos = jnp.concatenate([jnp.cos(ar), jnp.cos(ar), jnp.cos(ac), jnp.cos(ac)], axis=-1)
    sin = jnp.concatenate([-jnp.sin(ar), jnp.sin(ar), -jnp.sin(ac), jnp.sin(ac)], axis=-1)
    cos = jnp.concatenate([jnp.ones((C, HEAD_DIM), f32), cos], axis=0)
    sin = jnp.concatenate([jnp.zeros((C, HEAD_DIM), f32), sin], axis=0)
    return dict(cos=jnp.tile(cos, (1, NH)), sin=jnp.tile(sin, (1, NH)))


def kernel(x, c, ctx, c_ctx, ada_w, ada_b, norm_g, ffn1_w13, ffn1_w2, w_in, b_merge, rnn_conv_w, rnn_conv_b, lru_w_a, lru_b_a, lru_w_x, lru_b_x, lru_lambda, sc_conv_w, attn_sink, w_branch, w_out, ffn2_w13, ffn2_w2, final_norm_g, loss_target, m_c_ctx, m_ada_w, m_ada_b, m_norm_g, m_ffn1_w13, m_ffn1_w2, m_w_in, m_b_merge, m_rnn_conv_w, m_rnn_conv_b, m_lru_w_a, m_lru_b_a, m_lru_w_x, m_lru_b_x, m_lru_lambda, m_sc_conv_w, m_attn_sink, m_w_branch, m_w_out, m_ffn2_w13, m_ffn2_w2, m_final_norm_g, v_c_ctx, v_ada_w, v_ada_b, v_norm_g, v_ffn1_w13, v_ffn1_w2, v_w_in, v_b_merge, v_rnn_conv_w, v_rnn_conv_b, v_lru_w_a, v_lru_b_a, v_lru_w_x, v_lru_b_x, v_lru_lambda, v_sc_conv_w, v_attn_sink, v_w_branch, v_w_out, v_ffn2_w13, v_ffn2_w2, v_final_norm_g):
    d = _Dims()
    L, D = x.shape[1], x.shape[2]
    C = ctx.shape[1]
    LY = ada_w.shape[0]
    d.D, d.C, d.T = D, C, C + L
    d.F = ffn1_w2.shape[1] * N_DEV
    d.Fp = _round_up(d.F, 512)
    d.IN = w_in.shape[2] * N_DEV
    d.BW = w_branch.shape[2]
    d.NH = attn_sink.shape[1]
    d.NA = d.IN - N_BRANCH * D
    d.NKV = (d.NA - 5 * d.BW - d.NH * HEAD_DIM) // (2 * HEAD_DIM)
    d.G = d.NH // d.NKV
    d.NB = lru_w_a.shape[2]
    d.tr = _tile(C, 256, 8)
    d.trw = _tile(C, 128, 8)
    assert d.NH * HEAD_DIM == d.BW and L % GRID_W == 0 and C % ATT_BLOCK == 0 and L % ATT_BLOCK == 0
    sh13, sh2, shin, shd = 2 * d.F // N_DEV, d.F // N_DEV, d.IN // N_DEV, D // N_DEV
    cols9 = N_MOD * D // N_DEV
    me3 = _coords()
    me = _index(me3)

    sharded = [norm_g, b_merge, rnn_conv_w, lru_b_a, lru_b_x, lru_lambda, sc_conv_w]
    shapes1 = [a.shape for a in sharded] + [(D,)]
    g1 = _allgather("gather_small", [_pack(sharded + [c.reshape(-1)])])[0]
    parts = _unpack(g1, shapes1, (N_DEV,))
    norm_g_f, b_merge_f, conv_w_f, lru_b_a_f, lru_b_x_f, lru_lam_f, sc_w_f = [_unshard_last(p) for p in parts[:-1]]
    cvec = jnp.concatenate([parts[-1], c_ctx[None], jnp.zeros((7, D), f32)], axis=0)
    ada_b_cols = lax.dynamic_slice_in_dim(ada_b, me * cols9, cols9, axis=1)[:, None, :]
    modcols = _ada_fwd(cvec, ada_w, ada_b_cols)
    g2 = _allgather("gather_mod", [modcols])[0]
    modall = jnp.moveaxis(g2, 0, 2).reshape(LY, 16, N_MOD, D)
    mod_lat = lax.dynamic_index_in_dim(modall, me, axis=1, keepdims=False)
    modv = jnp.stack([modall[:, N_DEV], mod_lat], axis=1)

    def gather_rows(src, total_rows, off_fn, nrows):
        return (src, jnp.zeros((LY, total_rows) + src.shape[2:], bf16), lambda ref, me_, pi, r: ref,
                lambda ref, sender, r: ref.at[:, pl.ds(pl.multiple_of(off_fn(sender), 16), nrows)])
    off13 = lambda j: (j // 4) * d.Fp + (j % 4) * sh13
    c13a = _cast_transposed(ffn1_w13[:, None])[:, 0]
    c13b = _cast_transposed(ffn2_w13[:, None])[:, 0]
    cin = _cast_transposed(w_in[:, None])[:, 0]
    cwb = _cast_transposed(w_branch)
    c2a, c2b, cwo = _cast_direct(ffn1_w2), _cast_direct(ffn2_w2), _cast_direct(w_out)
    wb_item = (cwb, jnp.zeros((LY, N_BRANCH, D, d.BW), bf16), lambda ref, me_, pi, r: ref,
               lambda ref, sender, r: ref.at[:, :, pl.ds(pl.multiple_of(sender * shd, 16), shd)])
    w13a, w2a, wint, wbt, wout, w13b, w2b = _exchange("gather_weights", [
        gather_rows(c13a, 2 * d.Fp, off13, sh13), gather_rows(c2a, d.Fp, lambda j: j * sh2, sh2),
        gather_rows(cin, d.IN, lambda j: j * shin, shin), wb_item, gather_rows(cwo, D, lambda j: j * shd, shd),
        gather_rows(c13b, 2 * d.Fp, off13, sh13), gather_rows(c2b, d.Fp, lambda j: j * sh2, sh2)])

    tabs = _rope_tables(L, C, d.NH)
    h = jnp.concatenate([ctx[0], x[0]], axis=0)
    Ws, Ps, saves = [], [], []
    for l in range(LY):
        Ws.append(dict(w13a=w13a[l], w2a=w2a[l], wint=wint[l], wbt=wbt[l], wout=wout[l], w13b=w13b[l], w2b=w2b[l]))
        sinkrows = jnp.repeat(attn_sink[l].reshape(d.NKV, d.G), ATT_BLOCK, axis=1)[:, :, None]
        Ps.append(dict(norm_g=norm_g_f[l], b_merge=b_merge_f[l], rnn_conv_w=conv_w_f[l], rnn_conv_b=rnn_conv_b[l][None],
                       lru_w_a=lru_w_a[l], lru_b_a=lru_b_a_f[l], lru_w_x=lru_w_x[l], lru_b_x=lru_b_x_f[l], lru_lambda=lru_lam_f[l],
                       sc_conv_w=sc_w_f[l], sinkrows=sinkrows))
    for l in range(LY):
        W, P = Ws[l], Ps[l]
        h, s1 = _ffn_fwd(h, modv[l], P["norm_g"][0:1], W["w13a"], W["w2a"], 0, d)
        h, s2 = _mixer_fwd(h, modv[l], P, W, tabs, d)
        h, s3 = _ffn_fwd(h, modv[l], P["norm_g"][2:3], W["w13b"], W["w2b"], 6, d)
        saves.append((s1, s2, s3))

    fng = final_norm_g[None]

    def loss_fn(t):
        def f(hv, g):
            y = hv * lax.rsqrt(jnp.mean(hv * hv, axis=-1, keepdims=True) + EPS) * g
            e = y - t.R[1]
            return 0.5 * jnp.sum(jnp.mean(e * e, axis=-1))
        lat = (t.seg == 1).astype(f32)
        val, (dhv, dg) = jax.value_and_grad(f, argnums=(0, 1))(t.R[0], t.K[0])
        return [dhv * lat], [dg * lat, jnp.full((1, 128), val * lat, f32)], []
    dh, dfng, lossv = _rowwise("loss", loss_fn, d.T, C, d.tr, rows=[(h, D, 0, 0), (loss_target[0], D, 0, -(C // d.tr))],
                               consts=[fng], outs=[(D, f32)], accs=[(1, D), (1, 128)])
    loss = lax.psum(lossv[0, 0], AXES)

    gl = [None] * LY
    for l in reversed(range(LY)):
        W, P = Ws[l], Ps[l]
        s1, s2, s3 = saves[l]
        dh, dw13b, dw2b, dg2, dss2, dgt2 = _ffn_bwd(dh, s3, modv[l], P["norm_g"][2:3], W["w13b"], W["w2b"], 6, d)
        dh, gm = _mixer_bwd(dh, s2, modv[l], P, W, tabs, d)
        dh, dw13a, dw2a, dg0, dss0, dgt0 = _ffn_bwd(dh, s1, modv[l], P["norm_g"][0:1], W["w13a"], W["w2a"], 0, d)
        gm.update(w13a=dw13a, w2a=dw2a, w13b=dw13b, w2b=dw2b, norm_g=jnp.concatenate([dg0, gm["norm_g1"], dg2], axis=0),
                  dmod=jnp.concatenate([dss0, dgt0, gm["dss"], gm["dgate"], dss2, dgt2], axis=1))
        gl[l] = gm
    grad_x = dh[C:][None]
    st = lambda k: jnp.stack([g[k] for g in gl])

    def rs_rows(src, off_fn, nrows):
        return (src, jnp.zeros((LY, N_DEV, nrows) + src.shape[2:], bf16),
                lambda ref, me_, pi, r: ref.at[:, pl.ds(pl.multiple_of(off_fn(pi), 16), nrows)],
                lambda ref, sender, r: ref.at[:, r])
    wb_rs = (st("wbt"), jnp.zeros((LY, N_DEV, N_BRANCH, shd, d.BW), bf16),
             lambda ref, me_, pi, r: ref.at[:, :, pl.ds(pl.multiple_of(pi * shd, 16), shd)], lambda ref, sender, r: ref.at[:, r])
    r13a, r2a, rin, rwb, rwo, r13b, r2b = _exchange("exchange_grads", [
        rs_rows(st("w13a"), off13, sh13), rs_rows(st("w2a"), lambda j: j * sh2, sh2), rs_rows(st("wint"), lambda j: j * shin, shin),
        wb_rs, rs_rows(st("wout"), lambda j: j * shd, shd), rs_rows(st("w13b"), off13, sh13), rs_rows(st("w2b"), lambda j: j * sh2, sh2)])
    big = {}
    big["ffn1_w13"] = [o[:, 0] for o in _adamw_transposed(r13a[:, :, None], ffn1_w13[:, None], m_ffn1_w13[:, None], v_ffn1_w13[:, None])]
    big["ffn2_w13"] = [o[:, 0] for o in _adamw_transposed(r13b[:, :, None], ffn2_w13[:, None], m_ffn2_w13[:, None], v_ffn2_w13[:, None])]
    big["w_in"] = [o[:, 0] for o in _adamw_transposed(rin[:, :, None], w_in[:, None], m_w_in[:, None], v_w_in[:, None])]
    big["w_branch"] = _adamw_transposed(rwb, w_branch, m_w_branch, v_w_branch)
    big["ffn1_w2"] = _adamw_direct(r2a, ffn1_w2, m_ffn1_w2, v_ffn1_w2, True)
    big["ffn2_w2"] = _adamw_direct(r2b, ffn2_w2, m_ffn2_w2, v_ffn2_w2, True)
    big["w_out"] = _adamw_direct(rwo, w_out, m_w_out, v_w_out, True)

    sink_g = jnp.stack([jnp.sum(g["sinkrows"].reshape(d.NKV, d.G, ATT_BLOCK), axis=-1).reshape(d.NH) for g in gl])
    small_full = dict(norm_g=st("norm_g"), b_merge=st("b_merge"), rnn_conv_w=st("rnn_conv_w"), rnn_conv_b=st("rnn_conv_b")[:, 0],
                      lru_w_a=st("lru_w_a"), lru_b_a=st("lru_b_a"), lru_w_x=st("lru_w_x"), lru_b_x=st("lru_b_x"),
                      lru_lambda=st("lru_lambda"), sc_conv_w=st("sc_conv_w"), attn_sink=sink_g, final_norm_g=dfng[0])
    names_s = list(small_full)
    dmod = st("dmod")
    pk = _pack([small_full[k] for k in names_s] + [dmod])
    g4 = _allgather("gather_small_grads", [pk])[0]
    tot = _unpack(_sum8(g4), [small_full[k].shape for k in names_s] + [dmod.shape])
    sums = dict(zip(names_s, tot[:-1]))
    dmod_sum = tot[-1].reshape(LY, 2, N_MOD * D)
    dmod_all = _unpack(g4, [small_full[k].shape for k in names_s] + [dmod.shape], (N_DEV,))[-1].reshape(N_DEV, LY, 2, N_MOD * D)
    dm_rows = jnp.concatenate([jnp.moveaxis(dmod_all[:, :, 1], 0, 1), dmod_sum[:, 0:1], jnp.zeros((LY, 7, N_MOD * D), f32)], axis=1)
    dm_cols = lax.dynamic_slice_in_dim(dm_rows, me * cols9, cols9, axis=2)
    g_ada_w, dcv = _ada_bwd(cvec, ada_w, dm_cols)
    g5 = _allgather("gather_cctx", [_pack([dcv[N_DEV]])])[0]
    g_c_ctx = _sum8(g5).reshape(-1)[:D]
    g_ada_b = dmod_sum[:, 0] + dmod_sum[:, 1]
    ada_out = _adamw_direct(g_ada_w, ada_w, m_ada_w, v_ada_w, False)

    def shard_last(a, n):
        return lax.dynamic_slice_in_dim(a, me * n, n, axis=a.ndim - 1)
    local_g = dict(c_ctx=g_c_ctx, ada_b=g_ada_b, norm_g=shard_last(sums["norm_g"], shd), b_merge=shard_last(sums["b_merge"], shd),
                   rnn_conv_w=shard_last(sums["rnn_conv_w"], d.BW // N_DEV), rnn_conv_b=sums["rnn_conv_b"], lru_w_a=sums["lru_w_a"],
                   lru_b_a=shard_last(sums["lru_b_a"], d.BW // N_DEV), lru_w_x=sums["lru_w_x"], lru_b_x=shard_last(sums["lru_b_x"], d.BW // N_DEV),
                   lru_lambda=shard_last(sums["lru_lambda"], d.BW // N_DEV), sc_conv_w=shard_last(sums["sc_conv_w"], d.BW // N_DEV),
                   attn_sink=sums["attn_sink"], final_norm_g=sums["final_norm_g"])
    wmv = dict(c_ctx=(c_ctx, m_c_ctx, v_c_ctx), ada_b=(ada_b, m_ada_b, v_ada_b), norm_g=(norm_g, m_norm_g, v_norm_g),
               b_merge=(b_merge, m_b_merge, v_b_merge), rnn_conv_w=(rnn_conv_w, m_rnn_conv_w, v_rnn_conv_w),
               rnn_conv_b=(rnn_conv_b, m_rnn_conv_b, v_rnn_conv_b), lru_w_a=(lru_w_a, m_lru_w_a, v_lru_w_a),
               lru_b_a=(lru_b_a, m_lru_b_a, v_lru_b_a), lru_w_x=(lru_w_x, m_lru_w_x, v_lru_w_x), lru_b_x=(lru_b_x, m_lru_b_x, v_lru_b_x),
               lru_lambda=(lru_lambda, m_lru_lambda, v_lru_lambda), sc_conv_w=(sc_conv_w, m_sc_conv_w, v_sc_conv_w),
               attn_sink=(attn_sink, m_attn_sink, v_attn_sink), final_norm_g=(final_norm_g, m_final_norm_g, v_final_norm_g))
    names_l = list(local_g)
    shapes_l = [wmv[k][0].shape for k in names_l]
    gp = _pack([local_g[k].reshape(wmv[k][0].shape) for k in names_l])
    outs_s = _flat_adamw(gp, _pack([wmv[k][0] for k in names_l]), _pack([wmv[k][1] for k in names_l]), _pack([wmv[k][2] for k in names_l]))
    small = {k: [local_g[k].reshape(wmv[k][0].shape)] for k in names_l}
    for o in outs_s:
        for k, a in zip(names_l, _unpack(o, shapes_l)):
            small[k].append(a)

    order = ["c_ctx", "ada_w", "ada_b", "norm_g", "ffn1_w13", "ffn1_w2", "w_in", "b_merge", "rnn_conv_w", "rnn_conv_b", "lru_w_a",
             "lru_b_a", "lru_w_x", "lru_b_x", "lru_lambda", "sc_conv_w", "attn_sink", "w_branch", "w_out", "ffn2_w13", "ffn2_w2",
             "final_norm_g"]
    allo = dict(small)
    allo.update(big)
    allo["ada_w"] = ada_out
    res = [loss, grad_x]
    for q in range(4):
        res += [allo[k][q] for k in order]
    return tuple(res)
```

```python
import functools
import math

import jax
import jax.numpy as jnp
from jax import lax
from jax.experimental import pallas as pl
from jax.experimental.pallas import tpu as pltpu

f32 = jnp.float32
bf16 = jnp.bfloat16
SDS = jax.ShapeDtypeStruct

N_DEV = 8
AXES = ("x", "y", "c")
HEAD_DIM = 128
GRID_W = 64
ATT_BLOCK = 128
ROPE_BASE = 10000.0
LRU_C = 8.0
EPS = 1e-6
NEG_INF = -1e30
N_MOD = 9
N_BRANCH = 3
ADAM_LR, ADAM_B1, ADAM_B2, ADAM_EPS, ADAM_WD, ADAM_STEP = 0.001, 0.9, 0.999, 1e-08, 0.01, 10
VMEM_LIMIT = 56 * 1024 * 1024
ANY = pl.BlockSpec(memory_space=pl.ANY)


def _round_up(n, m):
    return (n + m - 1) // m * m


def _tile(n, target, align):
    best = None
    t = align
    while t <= min(n, target):
        if n % t == 0:
            best = t
        t += align
    return best if best is not None else n


def _params(sem):
    return pltpu.CompilerParams(dimension_semantics=sem, vmem_limit_bytes=VMEM_LIMIT)


_DIMS = {"nn": (((1,), (0,)), ((), ())), "nt": (((1,), (1,)), ((), ())), "tn": (((0,), (0,)), ((), ()))}


def _coords():
    return lax.axis_index("x"), lax.axis_index("y"), lax.axis_index("c")


def _peer(xyc, r):
    x, y, c = xyc
    return (1 - x if r & 4 else x, 1 - y if r & 2 else y, 1 - c if r & 1 else c)


def _index(xyc):
    return xyc[0] * 4 + xyc[1] * 2 + xyc[2]


class _Comm:
    def __init__(self, env):
        self.env, self.srcs, self.dsts, self.keys, self.items, self.fills = env, [], [], [], [], []

    def _src(self, a):
        for q, b in enumerate(self.srcs):
            if b is a:
                return q
        self.srcs.append(a)
        return len(self.srcs) - 1

    def _dst(self, key):
        if key not in self.keys:
            self.keys.append(key)
            self.dsts.append(self.env[key])
        return self.keys.index(key)

    def add(self, src, key, src_view, dst_view):
        self.items.append((self._src(src), self._dst(key), src_view, dst_view))

    def fill(self, src, key, dst_view):
        self.fills.append((self._src(src), self._dst(key), dst_view))

    def scratch(self):
        n = len(self.items)
        return [pltpu.SemaphoreType.DMA((7 * n,)), pltpu.SemaphoreType.DMA((7 * n,)), pltpu.SemaphoreType.DMA((n + len(self.fills),))]

    def store(self, arrays):
        for key, a in zip(self.keys, arrays):
            self.env[key] = a

    def descriptors(self, srcs, dsts, sems, me3):
        send, recv, lsem = sems
        me = _index(me3)
        local, sends, recvs = [], [], []
        for q, (si, di, sv, dv) in enumerate(self.items):
            local.append(pltpu.make_async_copy(sv(srcs[si], me, me, 0), dv(dsts[di], me, 0), lsem.at[q]))
            for r in range(1, N_DEV):
                p3 = _peer(me3, r)
                pi = _index(p3)
                kw = dict(send_sem=send.at[q * 7 + r - 1], recv_sem=recv.at[q * 7 + r - 1], device_id=p3,
                          device_id_type=pl.DeviceIdType.MESH)
                sends.append(pltpu.make_async_remote_copy(src_ref=sv(srcs[si], me, pi, r), dst_ref=dv(dsts[di], me, r), **kw))
                recvs.append(pltpu.make_async_remote_copy(src_ref=sv(srcs[si], me, pi, r), dst_ref=dv(dsts[di], pi, r), **kw))
        for q, (si, di, dv) in enumerate(self.fills):
            local.append(pltpu.make_async_copy(srcs[si], dv(dsts[di]), lsem.at[len(self.items) + q]))
        return local, sends, recvs


def _comm_start(descs):
    local, sends, _ = descs
    for cp in local + sends:
        cp.start()


def _comm_wait(descs):
    local, sends, recvs = descs
    for cp in recvs:
        cp.wait_recv()
    for cp in sends:
        cp.wait_send()
    for cp in local:
        cp.wait()


def _mm(a, b, mode, out_dtype, name, *, M=None, N=None, K=None, a_off=0, b_off=0, b_lead=None,
        tm=768, tn=512, tk=2048, comm=None):
    b2 = b.shape[1:] if b_lead is not None else b.shape
    if mode == "tn":
        K = a.shape[0] if K is None else K
        M = a.shape[1] if M is None else M
        N = b2[1] if N is None else N
    else:
        M = a.shape[0] if M is None else M
        K = a.shape[1] if K is None else K
        N = (b2[1] if mode == "nn" else b2[0]) if N is None else N
    g = math.gcd
    if mode == "tn":
        tm = _tile(g(M, a_off) if a_off else M, tm, 128)
        tk = _tile(K, tk, 16)
    else:
        tm = _tile(M, tm, 16)
        tk = _tile(g(K, a_off) if a_off else K, tk, 128)
    tn = _tile(g(N, b_off) if b_off else N, tn, 128)
    nk = K // tk
    ao, bo = (a_off // (tm if mode == "tn" else tk)), b_off // tn
    lead = () if b_lead is None else (b_lead,)
    ld = () if b_lead is None else (None,)
    if mode == "nn":
        a_spec = pl.BlockSpec((tm, tk), lambda i, j, k: (i, ao + k))
        b_spec = pl.BlockSpec(ld + (tk, tn), lambda i, j, k: lead + (k, bo + j))
    elif mode == "nt":
        a_spec = pl.BlockSpec((tm, tk), lambda i, j, k: (i, ao + k))
        b_spec = pl.BlockSpec(ld + (tn, tk), lambda i, j, k: lead + (bo + j, k))
    else:
        a_spec = pl.BlockSpec((tk, tm), lambda i, j, k: (k, ao + i))
        b_spec = pl.BlockSpec(ld + (tk, tn), lambda i, j, k: lead + (k, bo + j))
    dims = _DIMS[mode]
    grid = (M // tm, N // tn, nk)
    ns, nd = (len(comm.srcs), len(comm.dsts)) if comm is not None else (0, 0)

    def body(*refs):
        a_ref, b_ref = refs[0], refs[1]
        o_ref = refs[2 + ns + nd]
        rest = refs[3 + ns + 2 * nd:]
        k = pl.program_id(2)
        if comm is not None:
            first = jnp.logical_and(jnp.logical_and(pl.program_id(0) == 0, pl.program_id(1) == 0), k == 0)
            last = jnp.logical_and(jnp.logical_and(pl.program_id(0) == grid[0] - 1, pl.program_id(1) == grid[1] - 1), k == nk - 1)
            sems = rest[-3:]
            me3 = _coords()
            mk = lambda: comm.descriptors(refs[2:2 + ns], refs[3 + ns + nd:3 + ns + 2 * nd], sems, me3)

            @pl.when(first)
            def _():
                _comm_start(mk())
        if nk == 1:
            o_ref[...] = lax.dot_general(a_ref[...], b_ref[...], dims, preferred_element_type=f32).astype(out_dtype)
        else:
            acc = rest[0]

            @pl.when(k == 0)
            def _():
                acc[...] = jnp.zeros_like(acc)

            acc[...] += lax.dot_general(a_ref[...], b_ref[...], dims, preferred_element_type=f32)

            @pl.when(k == nk - 1)
            def _():
                o_ref[...] = acc[...].astype(out_dtype)
        if comm is not None:
            @pl.when(last)
            def _():
                _comm_wait(mk())

    scratch = [] if nk == 1 else [pltpu.VMEM((tm, tn), f32)]
    o_spec, o_shape = pl.BlockSpec((tm, tn), lambda i, j, k: (i, j)), SDS((M, N), out_dtype)
    if comm is None:
        return pl.pallas_call(
            body, name=name, grid=grid, in_specs=[a_spec, b_spec], out_specs=o_spec, out_shape=o_shape,
            scratch_shapes=scratch, compiler_params=_params(("parallel", "parallel", "arbitrary")),
        )(a, b)
    res = pl.pallas_call(
        body, name=name, grid=grid, in_specs=[a_spec, b_spec] + [ANY] * (ns + nd), out_specs=[o_spec] + [ANY] * nd,
        out_shape=[o_shape] + [SDS(x.shape, x.dtype) for x in comm.dsts], scratch_shapes=scratch + comm.scratch(),
        input_output_aliases={2 + ns + q: 1 + q for q in range(nd)}, compiler_params=_params(("arbitrary",) * 3),
    )(a, b, *comm.srcs, *comm.dsts)
    comm.store(res[1:])
    return res[0]


class _Tile:
    pass


def _rowwise(name, fn, T, C, tr, rows=(), halos=(), consts=(), mods=(), outs=(), accs=(), maccs=()):
    nT, nC = T // tr, C // tr
    assert T % tr == 0 and C % tr == 0 and nC >= 1 and tr % 8 == 0
    r8, n8 = tr // 8, T // 8
    seg_of = lambda i: jnp.where(i >= nC, 1, 0)
    in_specs, args = [], []
    for arr, w, cb, ro in rows:
        nt = arr.shape[0] // tr
        in_specs.append(pl.BlockSpec((tr, w), (lambda i, cb=cb, ro=ro, nt=nt: (jnp.clip(i + ro, 0, nt - 1), cb))))
        args.append(arr)
    for arr, w, cb in halos:
        in_specs.append(pl.BlockSpec((8, w), lambda i, cb=cb: (jnp.maximum(i * r8 - 1, 0), cb)))
        in_specs.append(pl.BlockSpec((tr, w), lambda i, cb=cb: (i, cb)))
        in_specs.append(pl.BlockSpec((8, w), lambda i, cb=cb: (jnp.minimum((i + 1) * r8, n8 - 1), cb)))
        args += [arr, arr, arr]
    for arr in consts:
        in_specs.append(pl.BlockSpec(arr.shape, lambda i, nd=arr.ndim: (0,) * nd))
        args.append(arr)
    for arr in mods:
        in_specs.append(pl.BlockSpec((None,) + arr.shape[1:], lambda i, nd=arr.ndim: (seg_of(i),) + (0,) * (nd - 1)))
        args.append(arr)
    out_specs, out_shape = [], []
    for w, dt in outs:
        out_specs.append(pl.BlockSpec((tr, w), lambda i: (i, 0)))
        out_shape.append(SDS((T, w), dt))
    for shp in accs:
        out_specs.append(pl.BlockSpec(shp, lambda i, nd=len(shp): (0,) * nd))
        out_shape.append(SDS(shp, f32))
    for shp in maccs:
        out_specs.append(pl.BlockSpec((None,) + shp, lambda i, nd=len(shp): (seg_of(i),) + (0,) * nd))
        out_shape.append(SDS((2,) + shp, f32))
    nr, nh, nk, nm, no, na, nma = len(rows), len(halos), len(consts), len(mods), len(outs), len(accs), len(maccs)

    def body(*refs):
        i = pl.program_id(0)
        t = _Tile()
        t.i, t.seg = i, seg_of(i)
        p = 0
        t.R = [refs[p + k][...] for k in range(nr)]
        p += nr
        pvalid = jnp.logical_and(i != 0, i != nC)
        nvalid = jnp.logical_and(i != nC - 1, i != nT - 1)
        t.H = []
        for k in range(nh):
            pr, cu, nx = refs[p][...], refs[p + 1][...], refs[p + 2][...]
            p += 3
            pr = jnp.where(pvalid, pr, jnp.zeros_like(pr))
            nx = jnp.where(nvalid, nx, jnp.zeros_like(nx))
            t.H.append(jnp.concatenate([pr, cu, nx], axis=0))
        t.K = [refs[p + k][...] for k in range(nk)]
        p += nk
        t.M = [refs[p + k][...] for k in range(nm)]
        p += nm
        o, a, ma = fn(t)
        for k in range(no):
            ref = refs[p + k]
            pieces = o[k] if isinstance(o[k], (list, tuple)) else [o[k]]
            c0 = 0
            for pc in pieces:
                ref[:, c0:c0 + pc.shape[1]] = pc.astype(ref.dtype)
                c0 += pc.shape[1]
        p += no
        for k in range(na):
            ref = refs[p + k]

            @pl.when(i == 0)
            def _(ref=ref):
                ref[...] = jnp.zeros_like(ref)

            ref[...] += a[k]
        p += na
        for k in range(nma):
            ref = refs[p + k]

            @pl.when(jnp.logical_or(i == 0, i == nC))
            def _(ref=ref):
                ref[...] = jnp.zeros_like(ref)

            ref[...] += ma[k]

    res = pl.pallas_call(
        body, name=name, grid=(nT,), in_specs=in_specs, out_specs=out_specs, out_shape=out_shape,
        compiler_params=_params(("arbitrary",)),
    )(*args)
    return res


@jax.custom_vjp
def _bdot(a, b):
    return jnp.dot(a.astype(bf16), b.astype(bf16), preferred_element_type=f32)


def _bdot_fwd(a, b):
    return _bdot(a, b), (a, b)


def _bdot_bwd(res, ct):
    a, b = res
    ctb = ct.astype(bf16)
    da = lax.dot_general(ctb, b.astype(bf16), _DIMS["nt"], preferred_element_type=f32)
    db = lax.dot_general(a.astype(bf16), ctb, _DIMS["tn"], preferred_element_type=f32)
    return da, db


_bdot.defvjp(_bdot_fwd, _bdot_bwd)


def _rms_mod(h, g, shift, scale):
    y = h * lax.rsqrt(jnp.mean(h * h, axis=-1, keepdims=True) + EPS) * g
    return y * (1.0 + scale) + shift


def _norm_mod_fwd(h, g, modv, s, d):
    def fn(t):
        m = t.M[0]
        return [_rms_mod(t.R[0], t.K[0], m[s:s + 1], m[s + 1:s + 2])], [], []
    return _rowwise("norm_mod", fn, d.T, d.C, d.tr, rows=[(h, d.D, 0, 0)], consts=[g], mods=[modv],
                    outs=[(d.D, bf16)])[0]


def _norm_mod_bwd(h, du, dh, g, modv, s, d):
    def fn(t):
        m = t.M[0]
        _, vjp = jax.vjp(_rms_mod, t.R[0], t.K[0], m[s:s + 1], m[s + 1:s + 2])
        dx, dg, dsh, dsc = vjp(t.R[1])
        return [t.R[2] + dx], [dg], [jnp.concatenate([dsh, dsc], axis=0)]
    return _rowwise("norm_mod_bwd", fn, d.T, d.C, d.tr, rows=[(h, d.D, 0, 0), (du, d.D, 0, 0), (dh, d.D, 0, 0)],
                    consts=[g], mods=[modv], outs=[(d.D, f32)], accs=[(1, d.D)], maccs=[(2, d.D)])


def _swiglu(g, u):
    return jax.nn.silu(g) * u


def _resid_fwd(h, o, modv, gi, coef, d):
    def fn(t):
        return [t.R[0] + coef * t.M[0][gi:gi + 1] * t.R[1]], [], []
    return _rowwise("resid", fn, d.T, d.C, d.tr, rows=[(h, d.D, 0, 0), (o, d.D, 0, 0)], mods=[modv], outs=[(d.D, f32)])[0]


def _resid_bwd(dh, o, modv, gi, coef, d):
    def fn(t):
        dhv = t.R[0]
        return [coef * t.M[0][gi:gi + 1] * dhv], [], [jnp.sum(coef * dhv * t.R[1], axis=0, keepdims=True)]
    return _rowwise("resid_bwd", fn, d.T, d.C, d.tr, rows=[(dh, d.D, 0, 0), (o, d.D, 0, 0)], mods=[modv],
                    outs=[(d.D, bf16)], maccs=[(1, d.D)])


class _Sched:
    def __init__(self, plan, env, builders):
        self.plan, self.env, self.builders = plan, env, builders

    def at(self, site):
        specs = self.plan.get(site)
        if not specs:
            return None
        cm = _Comm(self.env)
        for kind, *args in specs:
            self.builders[kind](cm, *args)
        return cm


def _ffn_fwd(h, modv, g, k13, k2, s, d, sc, site):
    u = _norm_mod_fwd(h, g, modv, s, d)
    gu = _mm(u, sc.env[k13], "nt", f32, "ffn_up", comm=sc.at(site + "up"))
    hmid = _rowwise("swiglu", lambda t: ([_swiglu(t.R[0], t.R[1])], [], []), d.T, d.C, d.trw,
                    rows=[(gu, d.Fp, 0, 0), (gu, d.Fp, 1, 0)], outs=[(d.Fp, bf16)])[0]
    o = _mm(hmid, sc.env[k2], "nn", f32, "ffn_down", tn=1024, tk=2816, comm=sc.at(site + "down"))
    hn = _resid_fwd(h, o, modv, s + 2, 0.5, d)
    return hn, dict(h=h, u=u, gu=gu, hmid=hmid, o=o)


def _ffn_bwd(dh, sv, modv, g, k13, k2, s, d, sc, site):
    w13t, w2 = sc.env[k13], sc.env[k2]
    do, dgate = _resid_bwd(dh, sv["o"], modv, s + 2, 0.5, d)
    dhmid = _mm(do, w2, "nt", f32, "ffn_down_dx", comm=sc.at(site + "dx"))

    def fn(t):
        _, vjp = jax.vjp(_swiglu, t.R[0], t.R[1])
        dg, du = vjp(t.R[2])
        return [[dg, du]], [], []
    dgu = _rowwise("swiglu_bwd", fn, d.T, d.C, d.trw,
                   rows=[(sv["gu"], d.Fp, 0, 0), (sv["gu"], d.Fp, 1, 0), (dhmid, d.Fp, 0, 0)], outs=[(2 * d.Fp, bf16)])[0]
    sc.env[("g",) + k2] = _mm(sv["hmid"], do, "tn", bf16, "ffn_down_dw", tm=512, tn=2048, tk=768, comm=sc.at(site + "dw"))
    du = _mm(dgu, w13t, "nn", f32, "ffn_up_dx", tn=1024, tk=2816, comm=sc.at(site + "ux"))
    sc.env[("g",) + k13] = _mm(dgu, sv["u"], "tn", bf16, "ffn_up_dw", tm=512, tn=2048, tk=768, comm=sc.at(site + "uw"))
    dhn, dg, dss = _norm_mod_bwd(sv["h"], du, dh, g, modv, s, d)
    return dhn, dg, dss, dgate


def _taps(ext, w, left, tr, sign):
    acc = None
    for k in range(w.shape[0]):
        o = 8 + sign * (k - left)
        term = w[k:k + 1] * ext[o:o + tr]
        acc = term if acc is None else acc + term
    return acc


def _gates(xa, wa, ba, wx, bx, lam, nb):
    bs = xa.shape[1] // nb
    out = []
    for dr in range(2):
        pa = jnp.concatenate([_bdot(xa[:, n * bs:(n + 1) * bs], wa[dr, n]) for n in range(nb)], axis=1)
        px = jnp.concatenate([_bdot(xa[:, n * bs:(n + 1) * bs], wx[dr, n]) for n in range(nb)], axis=1)
        rg = jax.nn.sigmoid(pa + ba[dr:dr + 1])
        ig = jax.nn.sigmoid(px + bx[dr:dr + 1])
        log_a = -LRU_C * rg * jax.nn.softplus(-lam[dr:dr + 1])
        a = jnp.exp(log_a)
        u = jnp.sqrt(1.0 - jnp.exp(2.0 * log_a)) * (ig * xa)
        out += [a, u]
    return out


def _combine_a(hf, hb, rg):
    return (hf + hb) * jax.nn.gelu(rg)


def _scan_order(kind, nT, nC):
    nL = nT - nC
    if kind == "F":
        return (lambda s: s), False
    if kind == "revF":
        return (lambda s: nT - 1 - s), True
    if kind == "B":
        return (lambda s: jnp.where(s < nC, nC - 1 - s, nT - 1 - (s - nC))), True
    return (lambda s: jnp.where(s < nL, nC + s, s - nL)), False


def _scan_fwd(au, dr, d):
    R, tc = d.BW, d.tr
    tile_of, down = _scan_order("F" if dr == 0 else "B", d.T // tc, d.C // tc)

    def body(a_ref, u_ref, h_ref, hp_ref, st):
        @pl.when(pl.program_id(0) == 0)
        def _():
            st[...] = jnp.zeros_like(st)

        def grp(gi, h):
            r = pl.multiple_of((tc // 8 - 1 - gi if down else gi) * 8, 8)
            at, ut = a_ref[pl.ds(r, 8), :], u_ref[pl.ds(r, 8), :]
            hs, hps = [None] * 8, [None] * 8
            for k in (range(7, -1, -1) if down else range(8)):
                hps[k] = h
                h = at[k:k + 1] * h + ut[k:k + 1]
                hs[k] = h
            h_ref[pl.ds(r, 8), :] = jnp.concatenate(hs, axis=0)
            hp_ref[pl.ds(r, 8), :] = jnp.concatenate(hps, axis=0)
            return h
        st[...] = lax.fori_loop(0, tc // 8, grp, st[...])

    return pl.pallas_call(
        body, name="lru_scan", grid=(d.T // tc,),
        in_specs=[pl.BlockSpec((tc, R), lambda s: (tile_of(s), 2 * dr)), pl.BlockSpec((tc, R), lambda s: (tile_of(s), 2 * dr + 1))],
        out_specs=[pl.BlockSpec((tc, R), lambda s: (tile_of(s), 0))] * 2, out_shape=[SDS((d.T, R), f32)] * 2,
        scratch_shapes=[pltpu.VMEM((1, R), f32)], compiler_params=_params(("arbitrary",)),
    )(au, au)


def _scan_bwd(dh, au, hp, dr, d):
    R, tc = d.BW, d.tr
    tile_of, down = _scan_order("revF" if dr == 0 else "revB", d.T // tc, d.C // tc)

    def body(dh_ref, a_ref, hp_ref, o_ref, st):
        @pl.when(pl.program_id(0) == 0)
        def _():
            st[...] = jnp.zeros_like(st)

        def grp(gi, carry):
            lam, an = carry
            r = pl.multiple_of((tc // 8 - 1 - gi if down else gi) * 8, 8)
            at, dt, ht = a_ref[pl.ds(r, 8), :], dh_ref[pl.ds(r, 8), :], hp_ref[pl.ds(r, 8), :]
            ls = [None] * 8
            for k in (range(7, -1, -1) if down else range(8)):
                lam = dt[k:k + 1] + an * lam
                an = at[k:k + 1]
                ls[k] = lam
            lt = jnp.concatenate(ls, axis=0)
            o_ref[pl.ds(r, 8), 0:R] = lt * ht
            o_ref[pl.ds(r, 8), R:2 * R] = lt
            return lam, an
        lam, an = lax.fori_loop(0, tc // 8, grp, (st[0:1], st[1:2]))
        st[0:1] = lam
        st[1:2] = an

    return pl.pallas_call(
        body, name="lru_scan_bwd", grid=(d.T // tc,),
        in_specs=[pl.BlockSpec((tc, R), lambda s: (tile_of(s), 0)), pl.BlockSpec((tc, R), lambda s: (tile_of(s), 2 * dr)),
                  pl.BlockSpec((tc, R), lambda s: (tile_of(s), 0))],
        out_specs=pl.BlockSpec((tc, 2 * R), lambda s: (tile_of(s), 0)), out_shape=SDS((d.T, 2 * R), f32),
        scratch_shapes=[pltpu.VMEM((2, R), f32)], compiler_params=_params(("arbitrary",)),
    )(dh, au, hp)


def _swap_pairs(x):
    w = x.shape[1]
    lane = lax.broadcasted_iota(jnp.int32, x.shape, 1)
    return jnp.where(lane % 64 < 32, pltpu.roll(x, w - 32, 1), pltpu.roll(x, 32, 1))


def _att_masks(blk, nB, nCb, G):
    rows, cols = G * ATT_BLOCK, 3 * ATT_BLOCK
    qi = lax.broadcasted_iota(jnp.int32, (rows, cols), 0) % ATT_BLOCK
    kj = lax.broadcasted_iota(jnp.int32, (rows, cols), 1)
    rel = kj - ATT_BLOCK - qi
    kb = kj // ATT_BLOCK
    one = jnp.int32(1)
    latent = jnp.where(blk >= nCb, one, 0)
    prev_ok = jnp.where(blk - 1 >= nCb, latent, 0)
    next_ok = jnp.where(blk + 1 <= nB - 1, latent, 0)
    bv = jnp.where(kb == 0, prev_ok, jnp.where(kb == 1, latent, next_ok))
    return jnp.logical_and(jnp.abs(rel) <= ATT_BLOCK, bv > 0)


def _att_specs(d):
    G, nB = d.G, d.T // ATT_BLOCK
    blk = lambda f: pl.BlockSpec((ATT_BLOCK, HEAD_DIM), lambda kh, b: (f(b), kh))
    three = [blk(lambda b: jnp.maximum(b - 1, 0)), blk(lambda b: b), blk(lambda b: jnp.minimum(b + 1, nB - 1))]
    ctxs = pl.BlockSpec((d.C, HEAD_DIM), lambda kh, b: (0, kh))
    qs = pl.BlockSpec((ATT_BLOCK, G * HEAD_DIM), lambda kh, b: (b, kh))
    sk = pl.BlockSpec((None, G * ATT_BLOCK, 1), lambda kh, b: (kh, 0, 0))
    return qs, three, ctxs, sk


def _stack_heads(x, G):
    return jnp.concatenate([x[:, g * HEAD_DIM:(g + 1) * HEAD_DIM] for g in range(G)], axis=0)


def _unstack_heads(x, G):
    return jnp.concatenate([x[g * ATT_BLOCK:(g + 1) * ATT_BLOCK] for g in range(G)], axis=1)


def _att_probs(qg, kcat, kctx, sink, valid):
    scale = HEAD_DIM ** -0.5
    s = lax.dot_general(qg, kcat, _DIMS["nt"], preferred_element_type=f32) * scale
    s = jnp.where(valid, s, NEG_INF)
    sc = lax.dot_general(qg, kctx, _DIMS["nt"], preferred_element_type=f32) * scale
    m = jnp.maximum(jnp.maximum(jnp.max(s, axis=1, keepdims=True), jnp.max(sc, axis=1, keepdims=True)), sink)
    e, ec, es = jnp.exp(s - m), jnp.exp(sc - m), jnp.exp(sink - m)
    inv = 1.0 / (jnp.sum(e, axis=1, keepdims=True) + jnp.sum(ec, axis=1, keepdims=True) + es)
    return e * inv, ec * inv, es * inv


def _attn_fwd(qr, kr, vb, sinkrows, d):
    G, nB, nCb = d.G, d.T // ATT_BLOCK, d.C // ATT_BLOCK
    qs, three, ctxs, sk = _att_specs(d)

    def body(q_ref, k0, k1, k2, v0, v1, v2, kc_ref, vc_ref, s_ref, o_ref):
        b = pl.program_id(1)
        qg = _stack_heads(q_ref[...], G)
        kcat = jnp.concatenate([k0[...], k1[...], k2[...]], axis=0)
        vcat = jnp.concatenate([v0[...], v1[...], v2[...]], axis=0)
        p, pc, _ = _att_probs(qg, kcat, kc_ref[...], s_ref[...], _att_masks(b, nB, nCb, G))
        o = jnp.dot(p.astype(bf16), vcat, preferred_element_type=f32) + jnp.dot(pc.astype(bf16), vc_ref[...], preferred_element_type=f32)
        o_ref[...] = _unstack_heads(o, G).astype(bf16)

    return pl.pallas_call(
        body, name="attn", grid=(d.NKV, nB), in_specs=[qs] + three + three + [ctxs, ctxs, sk],
        out_specs=qs, out_shape=SDS((d.T, d.NH * HEAD_DIM), bf16), compiler_params=_params(("parallel", "arbitrary")),
    )(qr, kr, kr, kr, vb, vb, vb, kr, vb, sinkrows)


def _attn_bwd(qr, kr, vb, sinkrows, dy, d):
    G, nB, nCb = d.G, d.T // ATT_BLOCK, d.C // ATT_BLOCK
    qs, three, ctxs, sk = _att_specs(d)
    KW = d.NKV * HEAD_DIM
    part = pl.BlockSpec((ATT_BLOCK, HEAD_DIM), lambda kh, b: (b, kh))

    def body(q_ref, k0, k1, k2, v0, v1, v2, kc_ref, vc_ref, s_ref, dy_ref,
             dq_ref, dk0, dk1, dk2, dv0, dv1, dv2, dkc_ref, dvc_ref, ds_ref):
        b = pl.program_id(1)
        scale = HEAD_DIM ** -0.5
        qg = _stack_heads(q_ref[...], G)
        kcat = jnp.concatenate([k0[...], k1[...], k2[...]], axis=0)
        vcat = jnp.concatenate([v0[...], v1[...], v2[...]], axis=0)
        kctx, vctx = kc_ref[...], vc_ref[...]
        p, pc, ps = _att_probs(qg, kcat, kctx, s_ref[...], _att_masks(b, nB, nCb, G))
        dog = _stack_heads(dy_ref[...], G).astype(bf16)
        dp = lax.dot_general(dog, vcat, _DIMS["nt"], preferred_element_type=f32)
        dpc = lax.dot_general(dog, vctx, _DIMS["nt"], preferred_element_type=f32)
        delta = jnp.sum(p * dp, axis=1, keepdims=True) + jnp.sum(pc * dpc, axis=1, keepdims=True)
        ds = (p * (dp - delta) * scale).astype(bf16)
        dsc = (pc * (dpc - delta) * scale).astype(bf16)
        dq = jnp.dot(ds, kcat, preferred_element_type=f32) + jnp.dot(dsc, kctx, preferred_element_type=f32)
        dq_ref[...] = _unstack_heads(dq, G)
        dk = lax.dot_general(ds, qg, _DIMS["tn"], preferred_element_type=f32)
        dv = lax.dot_general(p.astype(bf16), dog, _DIMS["tn"], preferred_element_type=f32)
        for j, (rk, rv) in enumerate(((dk0, dv0), (dk1, dv1), (dk2, dv2))):
            rk[...] = dk[j * ATT_BLOCK:(j + 1) * ATT_BLOCK]
            rv[...] = dv[j * ATT_BLOCK:(j + 1) * ATT_BLOCK]

        @pl.when(b == 0)
        def _():
            dkc_ref[...] = jnp.zeros_like(dkc_ref)
            dvc_ref[...] = jnp.zeros_like(dvc_ref)
            ds_ref[...] = jnp.zeros_like(ds_ref)

        dkc_ref[...] += lax.dot_general(dsc, qg, _DIMS["tn"], preferred_element_type=f32)
        dvc_ref[...] += lax.dot_general(pc.astype(bf16), dog, _DIMS["tn"], preferred_element_type=f32)
        ds_ref[...] += -ps * delta

    kv = SDS((d.T, KW), f32)
    return pl.pallas_call(
        body, name="attn_bwd", grid=(d.NKV, nB), in_specs=[qs] + three + three + [ctxs, ctxs, sk, qs],
        out_specs=[qs] + [part] * 6 + [ctxs, ctxs, sk],
        out_shape=[SDS((d.T, d.NH * HEAD_DIM), f32)] + [kv] * 6 + [SDS((d.C, KW), f32)] * 2 + [SDS((d.NKV, G * ATT_BLOCK, 1), f32)],
        compiler_params=_params(("parallel", "arbitrary")),
    )(qr, kr, kr, kr, vb, vb, vb, kr, vb, sinkrows, dy)


def _mixer_fwd(h, modv, P, l, tabs, d, sc, site):
    D, BW, T, C = d.D, d.BW, d.T, d.C
    u = _norm_mod_fwd(h, P["norm_g"][1:2], modv, 3, d)
    za = _mm(u, sc.env[("wint", l)], "nt", f32, "in_proj_a", N=d.NA, comm=sc.at(site + "in_a"))
    gz = _mm(u, sc.env[("wint", l)], "nt", f32, "in_proj_g", N=3 * D, b_off=d.NA, comm=sc.at(site + "in_g"))
    xa = _rowwise("lru_conv", lambda t: ([_taps(t.H[0], t.K[0], 2, d.tr, 1) + t.K[1]], [], []), T, C, d.tr,
                  halos=[(za, BW, 0)], consts=[P["rnn_conv_w"], P["rnn_conv_b"]], outs=[(BW, f32)])[0]
    lru = [P["lru_w_a"], P["lru_b_a"], P["lru_w_x"], P["lru_b_x"], P["lru_lambda"]]
    au = _rowwise("lru_gates", lambda t: ([_gates(t.R[0], *t.K, d.NB)], [], []), T, C, d.tr,
                  rows=[(xa, BW, 0, 0)], consts=lru, outs=[(4 * BW, f32)])[0]
    hf, hpf = _scan_fwd(au, 0, d)
    hb, hpb = _scan_fwd(au, 1, d)
    ya = _rowwise("lru_out", lambda t: ([_combine_a(*t.R)], [], []), T, C, d.tr,
                  rows=[(hf, BW, 0, 0), (hb, BW, 0, 0), (za, BW, 1, 0)], outs=[(BW, bf16)])[0]
    yb = _rowwise("sconv", lambda t: ([t.R[0] * _taps(t.H[0] * t.H[1], t.K[0], 1, d.tr, 1)], [], []), T, C, d.tr,
                  rows=[(za, BW, 2, 0)], halos=[(za, BW, 3), (za, BW, 4)], consts=[P["sc_conv_w"]], outs=[(BW, bf16)])[0]
    QW, KW = d.NH * HEAD_DIM, d.NKV * HEAD_DIM

    def rope(t):
        q, k, v, cq, sq, ck, skn = t.R
        return [q * cq + _swap_pairs(q) * sq, k * ck + _swap_pairs(k) * skn, v], [], []
    kcb = (5 * BW + QW) // KW
    qr, kr, vb = _rowwise("rope", rope, T, C, d.tr,
                          rows=[(za, QW, (5 * BW) // QW, 0), (za, KW, kcb, 0), (za, KW, kcb + 1, 0),
                                (tabs["cos"], QW, 0, 0), (tabs["sin"], QW, 0, 0), (tabs["cos"], KW, 0, 0), (tabs["sin"], KW, 0, 0)],
                          outs=[(QW, bf16), (KW, bf16), (KW, bf16)])
    yatt = _attn_fwd(qr, kr, vb, P["sinkrows"], d)
    ys = (ya, yb, yatt)
    ps = [_mm(ys[i], sc.env[("wbt", l)], "nt", f32, "lift", b_lead=i, tn=1024) for i in range(N_BRANCH)]

    def merge(t):
        gzv, bm = t.R[0], t.K[0]
        acc = None
        for i in range(N_BRANCH):
            term = jax.nn.sigmoid(gzv[:, i * D:(i + 1) * D] + bm[i:i + 1]) * t.R[1 + i]
            acc = term if acc is None else acc + term
        return [acc], [], []
    merged = _rowwise("merge", merge, T, C, d.trw, rows=[(gz, 3 * D, 0, 0)] + [(p, D, 0, 0) for p in ps],
                      consts=[P["b_merge"]], outs=[(D, bf16)])[0]
    y = _mm(merged, sc.env[("wout", l)], "nn", f32, "out_proj", tn=1024, comm=sc.at(site + "out"))
    hn = _resid_fwd(h, y, modv, 5, 1.0, d)
    sv = dict(h=h, u=u, za=za, gz=gz, xa=xa, au=au, hf=hf, hpf=hpf, hb=hb, hpb=hpb, ys=ys, qr=qr, kr=kr, vb=vb,
              ps=ps, merged=merged, y=y)
    return hn, sv


def _mixer_bwd(dh, sv, modv, P, l, tabs, d, sc, site):
    W = dict(wint=sc.env[("wint", l)], wbt=sc.env[("wbt", l)], wout=sc.env[("wout", l)])
    D, BW, T, C = d.D, d.BW, d.T, d.C
    QW, KW = d.NH * HEAD_DIM, d.NKV * HEAD_DIM
    za, gz = sv["za"], sv["gz"]
    dy, dgate = _resid_bwd(dh, sv["y"], modv, 5, 1.0, d)
    dmerged = _mm(dy, W["wout"], "nt", f32, "out_proj_dx", tn=1024)
    sc.env[("g", "wout", l)] = _mm(sv["merged"], dy, "tn", bf16, "out_proj_dw", tm=512, tn=2048, tk=768)

    def merge_bwd(t):
        gzv, bm, dm = t.R[0], t.K[0], t.R[4]
        dps, dgs, dbs = [], [], []
        for i in range(N_BRANCH):
            gate = jax.nn.sigmoid(gzv[:, i * D:(i + 1) * D] + bm[i:i + 1])
            dps.append(dm * gate)
            dgi = dm * t.R[1 + i] * gate * (1.0 - gate)
            dgs.append(dgi)
            dbs.append(jnp.sum(dgi, axis=0, keepdims=True))
        return [dps, dgs], [jnp.concatenate(dbs, axis=0)], []
    dp, dgz, dbm = _rowwise("merge_bwd", merge_bwd, T, C, d.trw,
                            rows=[(gz, 3 * D, 0, 0)] + [(p, D, 0, 0) for p in sv["ps"]] + [(dmerged, D, 0, 0)],
                            consts=[P["b_merge"]], outs=[(3 * D, bf16), (3 * D, bf16)], accs=[(N_BRANCH, D)])
    dys = [_mm(dp, W["wbt"], "nn", f32, "lift_dx", K=D, a_off=i * D, b_lead=i, tn=1024) for i in range(N_BRANCH)]
    sc.env[("g", "wbt", l)] = jnp.stack([_mm(dp, sv["ys"][i], "tn", bf16, "lift_dw", M=D, a_off=i * D, tm=512, tn=1024, tk=768)
                                         for i in range(N_BRANCH)])
    def out_bwd(t):
        _, vjp = jax.vjp(_combine_a, t.R[1], t.R[2], t.R[3])
        dhf, _, drg = vjp(t.R[0])
        return [dhf, drg], [], []
    dhs, drg = _rowwise("lru_out_bwd", out_bwd, T, C, d.tr,
                        rows=[(dys[0], BW, 0, 0), (sv["hf"], BW, 0, 0), (sv["hb"], BW, 0, 0), (za, BW, 1, 0)],
                        outs=[(BW, f32), (BW, bf16)])
    dau0 = _scan_bwd(dhs, sv["au"], sv["hpf"], 0, d)
    dau1 = _scan_bwd(dhs, sv["au"], sv["hpb"], 1, d)
    lru = [P["lru_w_a"], P["lru_b_a"], P["lru_w_x"], P["lru_b_x"], P["lru_lambda"]]

    def gates_bwd(t):
        _, vjp = jax.vjp(lambda xa, *k: _gates(xa, *k, d.NB), t.R[0], *t.K)
        d0, d1 = t.R[1], t.R[2]
        g = vjp([d0[:, :BW], d0[:, BW:], d1[:, :BW], d1[:, BW:]])
        return [g[0]], list(g[1:]), []
    dxa, dwa, dba, dwx, dbx, dlam = _rowwise(
        "lru_gates_bwd", gates_bwd, T, C, d.tr, rows=[(sv["xa"], BW, 0, 0), (dau0, 2 * BW, 0, 0), (dau1, 2 * BW, 0, 0)],
        consts=lru, outs=[(BW, f32)], accs=[p.shape for p in lru])

    def conv_bwd(t):
        dxe, xe, w = t.H[0], t.H[1], t.K[0]
        cur = dxe[8:8 + d.tr]
        dw = jnp.concatenate([jnp.sum(cur * xe[8 + k - 2:8 + k - 2 + d.tr], axis=0, keepdims=True) for k in range(w.shape[0])], axis=0)
        return [_taps(dxe, w, 2, d.tr, -1)], [dw, jnp.sum(cur, axis=0, keepdims=True)], []
    drx, dcw, dcb = _rowwise("lru_conv_bwd", conv_bwd, T, C, d.tr, halos=[(dxa, BW, 0), (za, BW, 0)],
                             consts=[P["rnn_conv_w"]], outs=[(BW, bf16)], accs=[P["rnn_conv_w"].shape, (1, BW)])
    def sconv_bwd(t):
        scg, sx, sb, dyb = t.H
        w, tr = t.K[0], d.tr
        me = scg * sx
        dsb = dyb[8:8 + tr] * _taps(me, w, 1, tr, 1)
        dce = dyb * sb
        dm = _taps(dce, w, 1, tr, -1)
        cur = dce[8:8 + tr]
        dw = jnp.concatenate([jnp.sum(cur * me[8 + k - 1:8 + k - 1 + tr], axis=0, keepdims=True) for k in range(w.shape[0])], axis=0)
        return [[dsb, dm * sx[8:8 + tr], dm * scg[8:8 + tr]]], [dw], []
    dsc, dscw = _rowwise("sconv_bwd", sconv_bwd, T, C, d.tr, halos=[(za, BW, 3), (za, BW, 4), (za, BW, 2), (dys[1], BW, 0)],
                         consts=[P["sc_conv_w"]], outs=[(3 * BW, bf16)], accs=[P["sc_conv_w"].shape])
    dqr, dk0, dk1, dk2, dv0, dv1, dv2, dkc, dvc, dsink = _attn_bwd(sv["qr"], sv["kr"], sv["vb"], P["sinkrows"], dys[2], d)
    nB, nCb = T // ATT_BLOCK, C // ATT_BLOCK

    def att_join(t):
        dq, a1, a0, a2, b1, b0, b2, kc, vc, cq, sq, ck, skn = t.R
        up = t.i + 1 <= nB - 1
        dn = t.i >= 1
        isc = t.seg == 0
        dk = a1 + jnp.where(up, a0, 0.0) + jnp.where(dn, a2, 0.0) + jnp.where(isc, kc, 0.0)
        dv = b1 + jnp.where(up, b0, 0.0) + jnp.where(dn, b2, 0.0) + jnp.where(isc, vc, 0.0)
        return [[dq * cq + _swap_pairs(dq * sq), dk * ck + _swap_pairs(dk * skn), dv]], [], []
    dqkv = _rowwise("attn_join", att_join, T, C, ATT_BLOCK,
                    rows=[(dqr, QW, 0, 0), (dk1, KW, 0, 0), (dk0, KW, 0, 1), (dk2, KW, 0, -1), (dv1, KW, 0, 0), (dv0, KW, 0, 1),
                          (dv2, KW, 0, -1), (dkc, KW, 0, 0), (dvc, KW, 0, 0), (tabs["cos"], QW, 0, 0), (tabs["sin"], QW, 0, 0),
                          (tabs["cos"], KW, 0, 0), (tabs["sin"], KW, 0, 0)], outs=[(QW + 2 * KW, bf16)])[0]
    dz = jnp.concatenate([drx, drg, dsc, dqkv, dgz], axis=1)
    du = _mm(dz, W["wint"], "nn", f32, "in_proj_dx", tn=1024, tk=3200, comm=sc.at(site + "in_x"))
    sc.env[("g", "wint", l)] = _mm(dz, sv["u"], "tn", bf16, "in_proj_dw", tm=512, tn=2048, tk=768, comm=sc.at(site + "in_w"))
    dhn, dg, dss = _norm_mod_bwd(sv["h"], du, dh, P["norm_g"][1:2], modv, 3, d)
    grads = dict(norm_g1=dg, dss=dss, dgate=dgate, b_merge=dbm, rnn_conv_w=dcw, rnn_conv_b=dcb,
                 lru_w_a=dwa, lru_b_a=dba, lru_w_x=dwx, lru_b_x=dbx, lru_lambda=dlam, sc_conv_w=dscw, sinkrows=dsink)
    return dhn, grads


def _exchange(name, comm):
    ns, nd = len(comm.srcs), len(comm.dsts)

    def body(*refs):
        descs = comm.descriptors(refs[:ns], refs[ns + nd:ns + 2 * nd], refs[ns + 2 * nd:], _coords())
        _comm_start(descs)
        _comm_wait(descs)

    res = pl.pallas_call(
        body, name=name, in_specs=[ANY] * (ns + nd), out_specs=[ANY] * nd, out_shape=[SDS(a.shape, a.dtype) for a in comm.dsts],
        input_output_aliases={ns + q: q for q in range(nd)}, scratch_shapes=comm.scratch(),
    )(*comm.srcs, *comm.dsts)
    comm.store(res)


def _allgather(name, a):
    env = {"g": lax.empty((N_DEV,) + a.shape, a.dtype)}
    cm = _Comm(env)
    cm.add(a, "g", lambda ref, me, pi, r: ref, lambda ref, sender, r: ref.at[sender])
    _exchange(name, cm)
    return env["g"]


def _sum8(x):
    n = x.shape[1]
    tn = _tile(n, 2048, 8)

    def body(x_ref, o_ref):
        acc = x_ref[0]
        for j in range(1, N_DEV):
            acc = acc + x_ref[j]
        o_ref[...] = acc
    return pl.pallas_call(body, name="sum8", grid=(n // tn,), in_specs=[pl.BlockSpec((N_DEV, tn, 128), lambda i: (0, i, 0))],
                          out_specs=pl.BlockSpec((tn, 128), lambda i: (i, 0)), out_shape=SDS((n, 128), f32),
                          compiler_params=_params(("parallel",)))(x)


def _cast_direct(w):
    ly, R, Cn = w.shape
    tc = _tile(Cn, 512, 128)
    spec = pl.BlockSpec((None, R, tc), lambda l, j: (l, 0, j))

    def body(w_ref, o_ref):
        o_ref[...] = w_ref[...].astype(bf16)
    return pl.pallas_call(body, name="cast", grid=(ly, Cn // tc), in_specs=[spec], out_specs=spec, out_shape=SDS(w.shape, bf16),
                          compiler_params=_params(("parallel", "parallel")))(w)


def _cast_transposed(w):
    ly, I, K, Nl = w.shape
    tk = _tile(K, 256, 128)

    def body(w_ref, o_ref):
        o_ref[...] = w_ref[...].T.astype(bf16)
    return pl.pallas_call(
        body, name="cast_t", grid=(ly, I, K // tk), in_specs=[pl.BlockSpec((None, None, tk, Nl), lambda l, i, k: (l, i, k, 0))],
        out_specs=pl.BlockSpec((None, None, Nl, tk), lambda l, i, k: (l, i, 0, k)), out_shape=SDS((ly, I, Nl, K), bf16),
        compiler_params=_params(("parallel", "parallel", "parallel")))(w)


def _adam(g, w, m, v):
    m = ADAM_B1 * m + (1.0 - ADAM_B1) * g
    v = ADAM_B2 * v + (1.0 - ADAM_B2) * (g * g)
    m_hat = m / (1.0 - ADAM_B1 ** ADAM_STEP)
    v_hat = v / (1.0 - ADAM_B2 ** ADAM_STEP)
    delta = -ADAM_LR * (m_hat / (jnp.sqrt(v_hat) + ADAM_EPS) + ADAM_WD * w)
    return delta, m, v


def _slot_sum(x):
    acc = x[0].astype(f32)
    for r in range(1, x.shape[0]):
        acc = acc + x[r].astype(f32)
    return acc


def _adamw_direct(g, w, m, v, slots):
    ly, R, Cn = w.shape
    tr = R if R <= 1024 else _tile(R, 512, 16)
    tc = _tile(Cn, max(128, (256 * 1024) // tr // 128 * 128), 128)
    spec = pl.BlockSpec((None, tr, tc), lambda l, i, j: (l, i, j))
    gspec = pl.BlockSpec((None, N_DEV, tr, tc), lambda l, i, j: (l, 0, i, j)) if slots else spec

    def body(g_ref, w_ref, m_ref, v_ref, go, do, mo, vo):
        gv = _slot_sum(g_ref[...]) if slots else g_ref[...]
        go[...] = gv
        do[...], mo[...], vo[...] = _adam(gv, w_ref[...], m_ref[...], v_ref[...])
    return pl.pallas_call(body, name="adamw", grid=(ly, R // tr, Cn // tc), in_specs=[gspec, spec, spec, spec], out_specs=[spec] * 4,
                          out_shape=[SDS(w.shape, f32)] * 4, compiler_params=_params(("parallel",) * 3))(g, w, m, v)


def _adamw_transposed(g, w, m, v):
    ly, I, K, Nl = w.shape
    tk = _tile(K, 128, 128)
    spec = pl.BlockSpec((None, None, tk, Nl), lambda l, i, k: (l, i, k, 0))
    gspec = pl.BlockSpec((None, N_DEV, None, Nl, tk), lambda l, i, k: (l, 0, i, 0, k))

    def body(g_ref, w_ref, m_ref, v_ref, go, do, mo, vo):
        gv = _slot_sum(g_ref[...]).T
        go[...] = gv
        do[...], mo[...], vo[...] = _adam(gv, w_ref[...], m_ref[...], v_ref[...])
    return pl.pallas_call(body, name="adamw_t", grid=(ly, I, K // tk), in_specs=[gspec, spec, spec, spec], out_specs=[spec] * 4,
                          out_shape=[SDS(w.shape, f32)] * 4, compiler_params=_params(("parallel",) * 3))(g, w, m, v)


def _ada_fwd(cvec, ada_w, ada_b_cols):
    ly, D, cols = ada_w.shape
    tn = _tile(cols, 768, 128)

    def body(c_ref, w_ref, b_ref, o_ref):
        o_ref[...] = _bdot(jax.nn.silu(c_ref[...]), w_ref[...]) + b_ref[...]
    return pl.pallas_call(
        body, name="ada", grid=(ly, cols // tn),
        in_specs=[pl.BlockSpec((16, D), lambda l, j: (0, 0)), pl.BlockSpec((None, D, tn), lambda l, j: (l, 0, j)),
                  pl.BlockSpec((None, 1, tn), lambda l, j: (l, 0, j))],
        out_specs=pl.BlockSpec((None, 16, tn), lambda l, j: (l, 0, j)), out_shape=SDS((ly, 16, cols), f32),
        compiler_params=_params(("parallel", "parallel")))(cvec, ada_w, ada_b_cols)


def _ada_bwd(cvec, ada_w, dm):
    ly, D, cols = ada_w.shape
    tn = _tile(cols, 768, 128)

    def body(c_ref, w_ref, d_ref, gw_ref, gc_ref):
        first = jnp.logical_and(pl.program_id(0) == 0, pl.program_id(1) == 0)

        @pl.when(first)
        def _():
            gc_ref[...] = jnp.zeros_like(gc_ref)

        def f(cv, w):
            return _bdot(jax.nn.silu(cv), w)
        _, vjp = jax.vjp(f, c_ref[...], w_ref[...])
        dc, dw = vjp(d_ref[...])
        gw_ref[...] = dw
        gc_ref[...] += dc
    return pl.pallas_call(
        body, name="ada_bwd", grid=(ly, cols // tn),
        in_specs=[pl.BlockSpec((16, D), lambda l, j: (0, 0)), pl.BlockSpec((None, D, tn), lambda l, j: (l, 0, j)),
                  pl.BlockSpec((None, 16, tn), lambda l, j: (l, 0, j))],
        out_specs=[pl.BlockSpec((None, D, tn), lambda l, j: (l, 0, j)), pl.BlockSpec((16, D), lambda l, j: (0, 0))],
        out_shape=[SDS(ada_w.shape, f32), SDS((16, D), f32)], compiler_params=_params(("arbitrary", "arbitrary")))(cvec, ada_w, dm)


def _flat_adamw(g, w, m, v):
    n = w.shape[0]
    tn = _tile(n, 1024, 8)
    spec = pl.BlockSpec((tn, 128), lambda i: (i, 0))

    def body(g_ref, w_ref, m_ref, v_ref, do, mo, vo):
        do[...], mo[...], vo[...] = _adam(g_ref[...], w_ref[...], m_ref[...], v_ref[...])
    return pl.pallas_call(body, name="adamw_small", grid=(n // tn,), in_specs=[spec] * 4, out_specs=[spec] * 3,
                          out_shape=[SDS(w.shape, f32)] * 3, compiler_params=_params(("parallel",)))(g, w, m, v)


def _pack(arrs):
    flat = jnp.concatenate([a.reshape(-1).astype(f32) for a in arrs])
    n = _round_up(flat.shape[0], 512 * 128)
    return jnp.pad(flat, (0, n - flat.shape[0])).reshape(n // 128, 128)


def _unpack(flat, shapes, lead=()):
    flat = flat.reshape(lead + (-1,))
    out, off = [], 0
    for s in shapes:
        sz = math.prod(s)
        out.append(flat[..., off:off + sz].reshape(lead + tuple(s)))
        off += sz
    return out


def _unshard_last(g):
    g = jnp.moveaxis(g, 0, -2)
    return g.reshape(g.shape[:-2] + (g.shape[-2] * g.shape[-1],))


class _Dims:
    pass


def _rope_tables(L, C, NH):
    rows = L // GRID_W
    row = jnp.repeat(jnp.arange(rows), GRID_W).astype(f32)
    col = jnp.tile(jnp.arange(GRID_W), rows).astype(f32)
    half = HEAD_DIM // 2
    inv = ROPE_BASE ** (-jnp.arange(0, half, 2, dtype=f32) / half)
    ar, ac = row[:, None] * inv, col[:, None] * inv
    cos = jnp.concatenate([jnp.cos(ar), jnp.cos(ar), jnp.cos(ac), jnp.cos(ac)], axis=-1)
    sin = jnp.concatenate([-jnp.sin(ar), jnp.sin(ar), -jnp.sin(ac), jnp.sin(ac)], axis=-1)
    cos = jnp.concatenate([jnp.ones((C, HEAD_DIM), f32), cos], axis=0)
    sin = jnp.concatenate([jnp.zeros((C, HEAD_DIM), f32), sin], axis=0)
    return dict(cos=jnp.tile(cos, (1, NH)), sin=jnp.tile(sin, (1, NH)))


def kernel(x, c, ctx, c_ctx, ada_w, ada_b, norm_g, ffn1_w13, ffn1_w2, w_in, b_merge, rnn_conv_w, rnn_conv_b, lru_w_a, lru_b_a, lru_w_x, lru_b_x, lru_lambda, sc_conv_w, attn_sink, w_branch, w_out, ffn2_w13, ffn2_w2, final_norm_g, loss_target, m_c_ctx, m_ada_w, m_ada_b, m_norm_g, m_ffn1_w13, m_ffn1_w2, m_w_in, m_b_merge, m_rnn_conv_w, m_rnn_conv_b, m_lru_w_a, m_lru_b_a, m_lru_w_x, m_lru_b_x, m_lru_lambda, m_sc_conv_w, m_attn_sink, m_w_branch, m_w_out, m_ffn2_w13, m_ffn2_w2, m_final_norm_g, v_c_ctx, v_ada_w, v_ada_b, v_norm_g, v_ffn1_w13, v_ffn1_w2, v_w_in, v_b_merge, v_rnn_conv_w, v_rnn_conv_b, v_lru_w_a, v_lru_b_a, v_lru_w_x, v_lru_b_x, v_lru_lambda, v_sc_conv_w, v_attn_sink, v_w_branch, v_w_out, v_ffn2_w13, v_ffn2_w2, v_final_norm_g):
    d = _Dims()
    L, D = x.shape[1], x.shape[2]
    C = ctx.shape[1]
    LY = ada_w.shape[0]
    d.D, d.C, d.T = D, C, C + L
    d.F = ffn1_w2.shape[1] * N_DEV
    d.Fp = _round_up(d.F, 512)
    d.IN = w_in.shape[2] * N_DEV
    d.BW = w_branch.shape[2]
    d.NH = attn_sink.shape[1]
    d.NA = d.IN - N_BRANCH * D
    d.NKV = (d.NA - 5 * d.BW - d.NH * HEAD_DIM) // (2 * HEAD_DIM)
    d.G = d.NH // d.NKV
    d.NB = lru_w_a.shape[2]
    d.tr = _tile(C, 256, 8)
    d.trw = _tile(C, 128, 8)
    assert d.NH * HEAD_DIM == d.BW and L % GRID_W == 0 and C % ATT_BLOCK == 0 and L % ATT_BLOCK == 0
    sh13, sh2, shin, shd = 2 * d.F // N_DEV, d.F // N_DEV, d.IN // N_DEV, D // N_DEV
    cols9 = N_MOD * D // N_DEV
    me3 = _coords()
    me = _index(me3)

    sharded = [norm_g, b_merge, rnn_conv_w, lru_b_a, lru_b_x, lru_lambda, sc_conv_w]
    shapes1 = [a.shape for a in sharded] + [(D,)]
    g1 = _allgather("gather_small", _pack(sharded + [c.reshape(-1)]))
    parts = _unpack(g1, shapes1, (N_DEV,))
    norm_g_f, b_merge_f, conv_w_f, lru_b_a_f, lru_b_x_f, lru_lam_f, sc_w_f = [_unshard_last(p) for p in parts[:-1]]
    cvec = jnp.concatenate([parts[-1], c_ctx[None], jnp.zeros((7, D), f32)], axis=0)
    ada_b_cols = lax.dynamic_slice_in_dim(ada_b, me * cols9, cols9, axis=1)[:, None, :]
    modcols = _ada_fwd(cvec, ada_w, ada_b_cols)
    g2 = _allgather("gather_mod", modcols)
    modall = jnp.moveaxis(g2, 0, 2).reshape(LY, 16, N_MOD, D)
    mod_lat = lax.dynamic_index_in_dim(modall, me, axis=1, keepdims=False)
    modv = jnp.stack([modall[:, N_DEV], mod_lat], axis=1)

    assert LY == 2
    off13 = lambda j: (j // 4) * d.Fp + (j % 4) * sh13
    c13a = _cast_transposed(ffn1_w13[:, None])[:, 0]
    c13b = _cast_transposed(ffn2_w13[:, None])[:, 0]
    wt = {"w13a": (c13a, off13, sh13, 2 * d.Fp, (D,), False), "w13b": (c13b, off13, sh13, 2 * d.Fp, (D,), False),
          "w2a": (_cast_direct(ffn1_w2), lambda j: j * sh2, sh2, d.Fp, (D,), False),
          "w2b": (_cast_direct(ffn2_w2), lambda j: j * sh2, sh2, d.Fp, (D,), False),
          "wint": (_cast_transposed(w_in[:, None])[:, 0], lambda j: j * shin, shin, d.IN, (D,), False),
          "wbt": (_cast_transposed(w_branch), lambda j: j * shd, shd, D, (d.BW,), True),
          "wout": (_cast_direct(w_out), lambda j: j * shd, shd, D, (D,), False)}
    env = {}
    for name, (cw, off, sh, tot, rest, three) in wt.items():
        for l in range(LY):
            env[(name, l)] = lax.empty(((N_BRANCH, tot) if three else (tot,)) + rest, bf16)
        env[("r", name)] = lax.empty((LY, N_DEV) + ((N_BRANCH, sh) if three else (sh,)) + rest, bf16)
    zpad = jnp.zeros((max(d.Fp - d.F, 16), D), bf16)

    def part(sh, h):
        return (0, sh) if h is None else (h * (sh // 2), sh // 2)

    def rows_of(ref, start, n, three):
        sl = pl.ds(start if isinstance(start, int) else pl.multiple_of(start, 16), n)
        return ref.at[:, sl] if three else ref.at[sl]

    def add_gather(cm, name, l, h=None):
        cw, off, sh, tot, rest, three = wt[name]
        r0, n = part(sh, h)
        cm.add(cw, (name, l), lambda ref, me_, pi, r: rows_of(ref.at[l], r0, n, three),
               lambda ref, sender, r: rows_of(ref, off(sender) + r0, n, three))
        if d.Fp > d.F and h in (None, 0) and name[:2] in ("w1", "w2"):
            for base in ((0, d.Fp) if name[:3] == "w13" else (0,)):
                cm.fill(zpad, (name, l), lambda ref, base=base: ref.at[pl.ds(base + d.F, d.Fp - d.F)])

    def add_rs(cm, name, l, h=None):
        cw, off, sh, tot, rest, three = wt[name]
        r0, n = part(sh, h)
        cm.add(env[("g", name, l)], ("r", name), lambda ref, me_, pi, r: rows_of(ref, off(pi) + r0, n, three),
               lambda ref, sender, r: rows_of(ref.at[l, r], r0, n, three))

    plan = {"F0a.up": [("ag", "wint", 0)], "F0a.down": [("ag", "wbt", 0), ("ag", "wout", 0)],
            "F0m.in_a": [("ag", "w13b", 0, 0)], "F0m.in_g": [("ag", "w13b", 0, 1)],
            "F0b.up": [("ag", "w2b", 0), ("ag", "w13a", 1, 0)], "F0b.down": [("ag", "w13a", 1, 1)],
            "F1a.up": [("ag", "w2a", 1), ("ag", "wint", 1, 0)], "F1a.down": [("ag", "wint", 1, 1)],
            "F1m.in_a": [("ag", "wbt", 1), ("ag", "wout", 1)], "F1m.in_g": [("ag", "w13b", 1, 0)],
            "F1m.out": [("ag", "w13b", 1, 1)], "F1b.up": [("ag", "w2b", 1)],
            "B0b.dx": [("rs", "w13a", 1, 0)], "B0b.dw": [("rs", "w13a", 1, 1)]}
    for l in range(LY):
        plan.update({f"B{l}b.ux": [("rs", "w2b", l)], f"B{l}m.in_x": [("rs", "w13b", l)],
                     f"B{l}m.in_w": [("rs", "wout", l), ("rs", "wbt", l)], f"B{l}a.dx": [("rs", "wint", l, 0)],
                     f"B{l}a.dw": [("rs", "wint", l, 1)], f"B{l}a.ux": [("rs", "w2a", l)]})
    sc = _Sched(plan, env, {"ag": add_gather, "rs": add_rs})
    first = _Comm(env)
    add_gather(first, "w13a", 0)
    add_gather(first, "w2a", 0)
    _exchange("gather_first", first)

    tabs = _rope_tables(L, C, d.NH)
    h = jnp.concatenate([ctx[0], x[0]], axis=0)
    Ps, saves = [], []
    for l in range(LY):
        sinkrows = jnp.repeat(attn_sink[l].reshape(d.NKV, d.G), ATT_BLOCK, axis=1)[:, :, None]
        Ps.append(dict(norm_g=norm_g_f[l], b_merge=b_merge_f[l], rnn_conv_w=conv_w_f[l], rnn_conv_b=rnn_conv_b[l][None],
                       lru_w_a=lru_w_a[l], lru_b_a=lru_b_a_f[l], lru_w_x=lru_w_x[l], lru_b_x=lru_b_x_f[l], lru_lambda=lru_lam_f[l],
                       sc_conv_w=sc_w_f[l], sinkrows=sinkrows))
    for l in range(LY):
        P = Ps[l]
        h, s1 = _ffn_fwd(h, modv[l], P["norm_g"][0:1], ("w13a", l), ("w2a", l), 0, d, sc, f"F{l}a.")
        h, s2 = _mixer_fwd(h, modv[l], P, l, tabs, d, sc, f"F{l}m.")
        h, s3 = _ffn_fwd(h, modv[l], P["norm_g"][2:3], ("w13b", l), ("w2b", l), 6, d, sc, f"F{l}b.")
        saves.append((s1, s2, s3))

    fng = final_norm_g[None]

    def loss_fn(t):
        def f(hv, g):
            y = hv * lax.rsqrt(jnp.mean(hv * hv, axis=-1, keepdims=True) + EPS) * g
            e = y - t.R[1]
            return 0.5 * jnp.sum(jnp.mean(e * e, axis=-1))
        lat = (t.seg == 1).astype(f32)
        val, (dhv, dg) = jax.value_and_grad(f, argnums=(0, 1))(t.R[0], t.K[0])
        return [dhv * lat], [dg * lat, jnp.full((1, 128), val * lat, f32)], []
    dh, dfng, lossv = _rowwise("loss", loss_fn, d.T, C, d.tr, rows=[(h, D, 0, 0), (loss_target[0], D, 0, -(C // d.tr))],
                               consts=[fng], outs=[(D, f32)], accs=[(1, D), (1, 128)])
    loss = lax.psum(lossv[0, 0], AXES)

    gl = [None] * LY
    for l in reversed(range(LY)):
        P = Ps[l]
        s1, s2, s3 = saves[l]
        dh, dg2, dss2, dgt2 = _ffn_bwd(dh, s3, modv[l], P["norm_g"][2:3], ("w13b", l), ("w2b", l), 6, d, sc, f"B{l}b.")
        dh, gm = _mixer_bwd(dh, s2, modv[l], P, l, tabs, d, sc, f"B{l}m.")
        dh, dg0, dss0, dgt0 = _ffn_bwd(dh, s1, modv[l], P["norm_g"][0:1], ("w13a", l), ("w2a", l), 0, d, sc, f"B{l}a.")
        gm.update(norm_g=jnp.concatenate([dg0, gm["norm_g1"], dg2], axis=0),
                  dmod=jnp.concatenate([dss0, dgt0, gm["dss"], gm["dgate"], dss2, dgt2], axis=1))
        gl[l] = gm
    grad_x = dh[C:][None]
    st = lambda k: jnp.stack([g[k] for g in gl])

    last = _Comm(env)
    add_rs(last, "w13a", 0)
    _exchange("exchange_last", last)
    r13a, r2a, rin, rwb, rwo, r13b, r2b = [env[("r", k)] for k in ("w13a", "w2a", "wint", "wbt", "wout", "w13b", "w2b")]
    big = {}
    big["ffn1_w13"] = [o[:, 0] for o in _adamw_transposed(r13a[:, :, None], ffn1_w13[:, None], m_ffn1_w13[:, None], v_ffn1_w13[:, None])]
    big["ffn2_w13"] = [o[:, 0] for o in _adamw_transposed(r13b[:, :, None], ffn2_w13[:, None], m_ffn2_w13[:, None], v_ffn2_w13[:, None])]
    big["w_in"] = [o[:, 0] for o in _adamw_transposed(rin[:, :, None], w_in[:, None], m_w_in[:, None], v_w_in[:, None])]
    big["w_branch"] = _adamw_transposed(rwb, w_branch, m_w_branch, v_w_branch)
    big["ffn1_w2"] = _adamw_direct(r2a, ffn1_w2, m_ffn1_w2, v_ffn1_w2, True)
    big["ffn2_w2"] = _adamw_direct(r2b, ffn2_w2, m_ffn2_w2, v_ffn2_w2, True)
    big["w_out"] = _adamw_direct(rwo, w_out, m_w_out, v_w_out, True)

    sink_g = jnp.stack([jnp.sum(g["sinkrows"].reshape(d.NKV, d.G, ATT_BLOCK), axis=-1).reshape(d.NH) for g in gl])
    small_full = dict(norm_g=st("norm_g"), b_merge=st("b_merge"), rnn_conv_w=st("rnn_conv_w"), rnn_conv_b=st("rnn_conv_b")[:, 0],
                      lru_w_a=st("lru_w_a"), lru_b_a=st("lru_b_a"), lru_w_x=st("lru_w_x"), lru_b_x=st("lru_b_x"),
                      lru_lambda=st("lru_lambda"), sc_conv_w=st("sc_conv_w"), attn_sink=sink_g, final_norm_g=dfng[0])
    names_s = list(small_full)
    dmod = st("dmod")
    pk = _pack([small_full[k] for k in names_s] + [dmod])
    g4 = _allgather("gather_small_grads", pk)
    tot = _unpack(_sum8(g4), [small_full[k].shape for k in names_s] + [dmod.shape])
    sums = dict(zip(names_s, tot[:-1]))
    dmod_sum = tot[-1].reshape(LY, 2, N_MOD * D)
    dmod_all = _unpack(g4, [small_full[k].shape for k in names_s] + [dmod.shape], (N_DEV,))[-1].reshape(N_DEV, LY, 2, N_MOD * D)
    dm_rows = jnp.concatenate([jnp.moveaxis(dmod_all[:, :, 1], 0, 1), dmod_sum[:, 0:1], jnp.zeros((LY, 7, N_MOD * D), f32)], axis=1)
    dm_cols = lax.dynamic_slice_in_dim(dm_rows, me * cols9, cols9, axis=2)
    g_ada_w, dcv = _ada_bwd(cvec, ada_w, dm_cols)
    g5 = _allgather("gather_cctx", _pack([dcv[N_DEV]]))
    g_c_ctx = _sum8(g5).reshape(-1)[:D]
    g_ada_b = dmod_sum[:, 0] + dmod_sum[:, 1]
    ada_out = _adamw_direct(g_ada_w, ada_w, m_ada_w, v_ada_w, False)

    def shard_last(a, n):
        return lax.dynamic_slice_in_dim(a, me * n, n, axis=a.ndim - 1)
    local_g = dict(c_ctx=g_c_ctx, ada_b=g_ada_b, norm_g=shard_last(sums["norm_g"], shd), b_merge=shard_last(sums["b_merge"], shd),
                   rnn_conv_w=shard_last(sums["rnn_conv_w"], d.BW // N_DEV), rnn_conv_b=sums["rnn_conv_b"], lru_w_a=sums["lru_w_a"],
                   lru_b_a=shard_last(sums["lru_b_a"], d.BW // N_DEV), lru_w_x=sums["lru_w_x"], lru_b_x=shard_last(sums["lru_b_x"], d.BW // N_DEV),
                   lru_lambda=shard_last(sums["lru_lambda"], d.BW // N_DEV), sc_conv_w=shard_last(sums["sc_conv_w"], d.BW // N_DEV),
                   attn_sink=sums["attn_sink"], final_norm_g=sums["final_norm_g"])
    wmv = dict(c_ctx=(c_ctx, m_c_ctx, v_c_ctx), ada_b=(ada_b, m_ada_b, v_ada_b), norm_g=(norm_g, m_norm_g, v_norm_g),
               b_merge=(b_merge, m_b_merge, v_b_merge), rnn_conv_w=(rnn_conv_w, m_rnn_conv_w, v_rnn_conv_w),
               rnn_conv_b=(rnn_conv_b, m_rnn_conv_b, v_rnn_conv_b), lru_w_a=(lru_w_a, m_lru_w_a, v_lru_w_a),
               lru_b_a=(lru_b_a, m_lru_b_a, v_lru_b_a), lru_w_x=(lru_w_x, m_lru_w_x, v_lru_w_x), lru_b_x=(lru_b_x, m_lru_b_x, v_lru_b_x),
               lru_lambda=(lru_lambda, m_lru_lambda, v_lru_lambda), sc_conv_w=(sc_conv_w, m_sc_conv_w, v_sc_conv_w),
               attn_sink=(attn_sink, m_attn_sink, v_attn_sink), final_norm_g=(final_norm_g, m_final_norm_g, v_final_norm_g))
    names_l = list(local_g)
    shapes_l = [wmv[k][0].shape for k in names_l]
    gp = _pack([local_g[k].reshape(wmv[k][0].shape) for k in names_l])
    outs_s = _flat_adamw(gp, _pack([wmv[k][0] for k in names_l]), _pack([wmv[k][1] for k in names_l]), _pack([wmv[k][2] for k in names_l]))
    small = {k: [local_g[k].reshape(wmv[k][0].shape)] for k in names_l}
    for o in outs_s:
        for k, a in zip(names_l, _unpack(o, shapes_l)):
            small[k].append(a)

    order = ["c_ctx", "ada_w", "ada_b", "norm_g", "ffn1_w13", "ffn1_w2", "w_in", "b_merge", "rnn_conv_w", "rnn_conv_b", "lru_w_a",
             "lru_b_a", "lru_w_x", "lru_b_x", "lru_lambda", "sc_conv_w", "attn_sink", "w_branch", "w_out", "ffn2_w13", "ffn2_w2",
             "final_norm_g"]
    allo = dict(small)
    allo.update(big)
    allo["ada_w"] = ada_out
    res = [loss, grad_x]
    for q in range(4):
        res += [allo[k][q] for k in order]
    return tuple(res)
```

```python
import functools
import math

import jax
import jax.numpy as jnp
from jax import lax
from jax.experimental import pallas as pl
from jax.experimental.pallas import tpu as pltpu

f32 = jnp.float32
bf16 = jnp.bfloat16
SDS = jax.ShapeDtypeStruct

N_DEV = 8
AXES = ("x", "y", "c")
HEAD_DIM = 128
GRID_W = 64
ATT_BLOCK = 128
HALO = 16
ROPE_BASE = 10000.0
LRU_C = 8.0
EPS = 1e-6
NEG_INF = -1e30
N_MOD = 9
N_BRANCH = 3
ADAM_LR, ADAM_B1, ADAM_B2, ADAM_EPS, ADAM_WD, ADAM_STEP = 0.001, 0.9, 0.999, 1e-08, 0.01, 10
VMEM_LIMIT = 56 * 1024 * 1024
ANY = pl.BlockSpec(memory_space=pl.ANY)


def _round_up(n, m):
    return (n + m - 1) // m * m


def _tile(n, target, align):
    best = None
    t = align
    while t <= min(n, target):
        if n % t == 0:
            best = t
        t += align
    return best if best is not None else n


def _params(sem):
    return pltpu.CompilerParams(dimension_semantics=sem, vmem_limit_bytes=VMEM_LIMIT)


_DIMS = {"nn": (((1,), (0,)), ((), ())), "nt": (((1,), (1,)), ((), ())), "tn": (((0,), (0,)), ((), ()))}


def _coords():
    return lax.axis_index("x"), lax.axis_index("y"), lax.axis_index("c")


def _peer(xyc, r):
    x, y, c = xyc
    return (1 - x if r & 4 else x, 1 - y if r & 2 else y, 1 - c if r & 1 else c)


def _index(xyc):
    return xyc[0] * 4 + xyc[1] * 2 + xyc[2]


class _Comm:
    def __init__(self, env):
        self.env, self.srcs, self.dsts, self.keys, self.items, self.fills = env, [], [], [], [], []

    def _src(self, a):
        for q, b in enumerate(self.srcs):
            if b is a:
                return q
        self.srcs.append(a)
        return len(self.srcs) - 1

    def _dst(self, key):
        if key not in self.keys:
            self.keys.append(key)
            self.dsts.append(self.env[key])
        return self.keys.index(key)

    def add(self, src, key, src_view, dst_view):
        self.items.append((self._src(src), self._dst(key), src_view, dst_view))

    def fill(self, src, key, dst_view):
        self.fills.append((self._src(src), self._dst(key), dst_view))

    def scratch(self):
        n = len(self.items)
        return [pltpu.SemaphoreType.DMA((7 * n,)), pltpu.SemaphoreType.DMA((7 * n,)), pltpu.SemaphoreType.DMA((n + len(self.fills),))]

    def store(self, arrays):
        for key, a in zip(self.keys, arrays):
            self.env[key] = a

    def descriptors(self, srcs, dsts, sems, me3):
        send, recv, lsem = sems
        me = _index(me3)
        local, sends, recvs = [], [], []
        for q, (si, di, sv, dv) in enumerate(self.items):
            local.append(pltpu.make_async_copy(sv(srcs[si], me, me, 0), dv(dsts[di], me, 0), lsem.at[q]))
            for r in range(1, N_DEV):
                p3 = _peer(me3, r)
                pi = _index(p3)
                kw = dict(send_sem=send.at[q * 7 + r - 1], recv_sem=recv.at[q * 7 + r - 1], device_id=p3,
                          device_id_type=pl.DeviceIdType.MESH)
                sends.append(pltpu.make_async_remote_copy(src_ref=sv(srcs[si], me, pi, r), dst_ref=dv(dsts[di], me, r), **kw))
                recvs.append(pltpu.make_async_remote_copy(src_ref=sv(srcs[si], me, pi, r), dst_ref=dv(dsts[di], pi, r), **kw))
        for q, (si, di, dv) in enumerate(self.fills):
            local.append(pltpu.make_async_copy(srcs[si], dv(dsts[di]), lsem.at[len(self.items) + q]))
        return local, sends, recvs


def _comm_start(descs):
    local, sends, _ = descs
    for cp in local + sends:
        cp.start()


def _comm_wait(descs):
    local, sends, recvs = descs
    for cp in recvs:
        cp.wait_recv()
    for cp in sends:
        cp.wait_send()
    for cp in local:
        cp.wait()


def _carry_call(core, name, grid, in_specs, out_specs, out_shape, scratch, args, comm):
    ni, no, nsc, ng = len(in_specs), len(out_specs), len(scratch), len(grid)
    if comm is None:
        def plain(*refs):
            core(refs[:ni], refs[ni:ni + no], refs[ni + no:])
        return pl.pallas_call(plain, name=name, grid=grid, in_specs=in_specs, out_specs=out_specs, out_shape=out_shape,
                              scratch_shapes=scratch, compiler_params=_params(("parallel",) * (ng - 1) + ("arbitrary",)))(*args)
    ns, nd = len(comm.srcs), len(comm.dsts)
    o0 = ni + ns + nd

    def body(*refs):
        pid = [pl.program_id(q) for q in range(ng)]
        first, last = pid[0] == 0, pid[0] == grid[0] - 1
        for q in range(1, ng):
            first = jnp.logical_and(first, pid[q] == 0)
            last = jnp.logical_and(last, pid[q] == grid[q] - 1)
        me3 = _coords()
        mk = lambda: comm.descriptors(refs[ni:ni + ns], refs[o0 + no:o0 + no + nd], refs[o0 + no + nd + nsc:], me3)

        @pl.when(first)
        def _():
            _comm_start(mk())

        core(refs[:ni], refs[o0:o0 + no], refs[o0 + no + nd:o0 + no + nd + nsc])

        @pl.when(last)
        def _():
            _comm_wait(mk())

    res = pl.pallas_call(
        body, name=name, grid=grid, in_specs=list(in_specs) + [ANY] * (ns + nd), out_specs=list(out_specs) + [ANY] * nd,
        out_shape=list(out_shape) + [SDS(x.shape, x.dtype) for x in comm.dsts], scratch_shapes=list(scratch) + comm.scratch(),
        input_output_aliases={ni + ns + q: no + q for q in range(nd)}, compiler_params=_params(("arbitrary",) * ng),
    )(*args, *comm.srcs, *comm.dsts)
    comm.store(res[no:])
    return res[:no]


def _mm(a, b, mode, out_dtype, name, *, M=None, N=None, K=None, a_off=0, b_off=0, b_lead=None,
        tm=768, tn=512, tk=2048, comm=None, a3=False):
    b2 = b.shape[1:] if b_lead is not None else b.shape
    if mode == "tn":
        K = a.shape[-2] if K is None else K
        M = (a.shape[0] * a.shape[2] if a3 else a.shape[1]) if M is None else M
        N = b2[1] if N is None else N
    else:
        M = a.shape[-2] if M is None else M
        K = (a.shape[0] * a.shape[2] if a3 else a.shape[1]) if K is None else K
        N = (b2[1] if mode == "nn" else b2[0]) if N is None else N
    g = math.gcd
    if mode == "tn":
        tm = _tile(g(M, a_off) if a_off else (a.shape[2] if a3 else M), tm, 128)
        tk = _tile(K, tk, 16)
    else:
        tm = _tile(M, tm, 16)
        tk = _tile(g(K, a_off) if a_off else (a.shape[2] if a3 else K), tk, 128)
    tn = _tile(g(N, b_off) if b_off else N, tn, 128)
    nk = K // tk
    ao, bo = (a_off // (tm if mode == "tn" else tk)), b_off // tn
    lead = () if b_lead is None else (b_lead,)
    ld = () if b_lead is None else (None,)
    if mode == "nn":
        a_spec = pl.BlockSpec((tm, tk), lambda i, j, k: (i, ao + k))
        b_spec = pl.BlockSpec(ld + (tk, tn), lambda i, j, k: lead + (k, bo + j))
    elif mode == "nt":
        a_spec = pl.BlockSpec((tm, tk), lambda i, j, k: (i, ao + k))
        b_spec = pl.BlockSpec(ld + (tn, tk), lambda i, j, k: lead + (bo + j, k))
    else:
        a_spec = pl.BlockSpec((tk, tm), lambda i, j, k: (k, ao + i))
        b_spec = pl.BlockSpec(ld + (tk, tn), lambda i, j, k: lead + (k, bo + j))
    if a3:
        if mode == "nn":
            nkc = a.shape[2] // tk
            a_spec = pl.BlockSpec((None, tm, tk), lambda i, j, k: (k // nkc, i, k % nkc))
        else:
            nmc = a.shape[2] // tm
            a_spec = pl.BlockSpec((None, tk, tm), lambda i, j, k: (i // nmc, k, i % nmc))
    dims = _DIMS[mode]

    def core(ins, outs, scr):
        a_ref, b_ref = ins
        o_ref = outs[0]
        k = pl.program_id(2)
        if nk == 1:
            o_ref[...] = lax.dot_general(a_ref[...], b_ref[...], dims, preferred_element_type=f32).astype(out_dtype)
        else:
            acc = scr[0]

            @pl.when(k == 0)
            def _():
                acc[...] = jnp.zeros_like(acc)

            acc[...] += lax.dot_general(a_ref[...], b_ref[...], dims, preferred_element_type=f32)

            @pl.when(k == nk - 1)
            def _():
                o_ref[...] = acc[...].astype(out_dtype)

    return _carry_call(core, name, (M // tm, N // tn, nk), [a_spec, b_spec], [pl.BlockSpec((tm, tn), lambda i, j, k: (i, j))],
                       [SDS((M, N), out_dtype)], [] if nk == 1 else [pltpu.VMEM((tm, tn), f32)], (a, b), comm)[0]


class _Tile:
    pass


def _rowwise(name, fn, T, C, tr, rows=(), halos=(), consts=(), mods=(), outs=(), accs=(), maccs=()):
    nT, nC = T // tr, C // tr
    assert T % tr == 0 and C % tr == 0 and nC >= 1 and tr % HALO == 0
    r8, n8 = tr // HALO, T // HALO
    seg_of = lambda i: jnp.where(i >= nC, 1, 0)
    in_specs, args = [], []
    for arr, w, cb, ro in rows:
        nt = arr.shape[0] // tr
        in_specs.append(pl.BlockSpec((tr, w), (lambda i, cb=cb, ro=ro, nt=nt: (jnp.clip(i + ro, 0, nt - 1), cb))))
        args.append(arr)
    for arr, w, cb in halos:
        in_specs.append(pl.BlockSpec((HALO, w), lambda i, cb=cb: (jnp.maximum(i * r8 - 1, 0), cb)))
        in_specs.append(pl.BlockSpec((tr, w), lambda i, cb=cb: (i, cb)))
        in_specs.append(pl.BlockSpec((HALO, w), lambda i, cb=cb: (jnp.minimum((i + 1) * r8, n8 - 1), cb)))
        args += [arr, arr, arr]
    for arr in consts:
        in_specs.append(pl.BlockSpec(arr.shape, lambda i, nd=arr.ndim: (0,) * nd))
        args.append(arr)
    for arr in mods:
        in_specs.append(pl.BlockSpec((None,) + arr.shape[1:], lambda i, nd=arr.ndim: (seg_of(i),) + (0,) * (nd - 1)))
        args.append(arr)
    out_specs, out_shape = [], []
    for w, dt in outs:
        out_specs.append(pl.BlockSpec((tr, w), lambda i: (i, 0)))
        out_shape.append(SDS((T, w), dt))
    for shp in accs:
        out_specs.append(pl.BlockSpec(shp, lambda i, nd=len(shp): (0,) * nd))
        out_shape.append(SDS(shp, f32))
    for shp in maccs:
        out_specs.append(pl.BlockSpec((None,) + shp, lambda i, nd=len(shp): (seg_of(i),) + (0,) * nd))
        out_shape.append(SDS((2,) + shp, f32))
    nr, nh, nk, nm, no, na, nma = len(rows), len(halos), len(consts), len(mods), len(outs), len(accs), len(maccs)

    def body(*refs):
        i = pl.program_id(0)
        t = _Tile()
        t.i, t.seg = i, seg_of(i)
        p = 0
        t.R = [refs[p + k][...].astype(f32) for k in range(nr)]
        p += nr
        pvalid = jnp.logical_and(i != 0, i != nC)
        nvalid = jnp.logical_and(i != nC - 1, i != nT - 1)
        t.H = []
        for k in range(nh):
            pr, cu, nx = refs[p][...].astype(f32), refs[p + 1][...].astype(f32), refs[p + 2][...].astype(f32)
            p += 3
            pr = jnp.where(pvalid, pr, jnp.zeros_like(pr))
            nx = jnp.where(nvalid, nx, jnp.zeros_like(nx))
            t.H.append(jnp.concatenate([pr, cu, nx], axis=0))
        t.K = [refs[p + k][...] for k in range(nk)]
        p += nk
        t.M = [refs[p + k][...] for k in range(nm)]
        p += nm
        o, a, ma = fn(t)
        for k in range(no):
            ref = refs[p + k]
            pieces = o[k] if isinstance(o[k], (list, tuple)) else [o[k]]
            c0 = 0
            for pc in pieces:
                ref[:, c0:c0 + pc.shape[1]] = pc.astype(ref.dtype)
                c0 += pc.shape[1]
        p += no
        for k in range(na):
            ref = refs[p + k]

            @pl.when(i == 0)
            def _(ref=ref):
                ref[...] = jnp.zeros_like(ref)

            ref[...] += a[k]
        p += na
        for k in range(nma):
            ref = refs[p + k]

            @pl.when(jnp.logical_or(i == 0, i == nC))
            def _(ref=ref):
                ref[...] = jnp.zeros_like(ref)

            ref[...] += ma[k]

    res = pl.pallas_call(
        body, name=name, grid=(nT,), in_specs=in_specs, out_specs=out_specs, out_shape=out_shape,
        compiler_params=_params(("arbitrary",)),
    )(*args)
    return res


@jax.custom_vjp
def _bdot(a, b):
    return jnp.dot(a.astype(bf16), b.astype(bf16), preferred_element_type=f32)


def _bdot_fwd(a, b):
    return _bdot(a, b), (a, b)


def _bdot_bwd(res, ct):
    a, b = res
    ctb = ct.astype(bf16)
    da = lax.dot_general(ctb, b.astype(bf16), _DIMS["nt"], preferred_element_type=f32)
    db = lax.dot_general(a.astype(bf16), ctb, _DIMS["tn"], preferred_element_type=f32)
    return da, db


_bdot.defvjp(_bdot_fwd, _bdot_bwd)


def _rms_mod(h, g, shift, scale):
    y = h * lax.rsqrt(jnp.mean(h * h, axis=-1, keepdims=True) + EPS) * g
    return y * (1.0 + scale) + shift


def _norm_mod_fwd(h, g, modv, s, d):
    def fn(t):
        m = t.M[0]
        return [_rms_mod(t.R[0], t.K[0], m[s:s + 1], m[s + 1:s + 2])], [], []
    return _rowwise("norm_mod", fn, d.T, d.C, d.tr, rows=[(h, d.D, 0, 0)], consts=[g], mods=[modv],
                    outs=[(d.D, bf16)])[0]


def _norm_mod_bwd(h, du, dh, g, modv, s, d):
    def fn(t):
        m = t.M[0]
        _, vjp = jax.vjp(_rms_mod, t.R[0], t.K[0], m[s:s + 1], m[s + 1:s + 2])
        dx, dg, dsh, dsc = vjp(t.R[1])
        return [t.R[2] + dx], [dg], [jnp.concatenate([dsh, dsc], axis=0)]
    return _rowwise("norm_mod_bwd", fn, d.T, d.C, d.tr, rows=[(h, d.D, 0, 0), (du, d.D, 0, 0), (dh, d.D, 0, 0)],
                    consts=[g], mods=[modv], outs=[(d.D, f32)], accs=[(1, d.D)], maccs=[(2, d.D)])


def _swiglu(g, u):
    return jax.nn.silu(g) * u


def _resid_fwd(h, o, modv, gi, coef, d):
    def fn(t):
        return [t.R[0] + coef * t.M[0][gi:gi + 1] * t.R[1]], [], []
    return _rowwise("resid", fn, d.T, d.C, d.tr, rows=[(h, d.D, 0, 0), (o, d.D, 0, 0)], mods=[modv], outs=[(d.D, f32)])[0]


def _resid_bwd(dh, o, modv, gi, coef, d):
    def fn(t):
        dhv = t.R[0]
        return [coef * t.M[0][gi:gi + 1] * dhv], [], [jnp.sum(coef * dhv * t.R[1], axis=0, keepdims=True)]
    return _rowwise("resid_bwd", fn, d.T, d.C, d.tr, rows=[(dh, d.D, 0, 0), (o, d.D, 0, 0)], mods=[modv],
                    outs=[(d.D, bf16)], maccs=[(1, d.D)])


class _Sched:
    def __init__(self, plan, env, builders):
        self.plan, self.env, self.builders = plan, env, builders

    def at(self, site):
        specs = self.plan.get(site)
        if not specs:
            return None
        cm = _Comm(self.env)
        for kind, *args in specs:
            self.builders[kind](cm, *args)
        return cm


def _ffn_up(u, w13t, d, comm):
    T, D, Fp = d.T, d.D, d.Fp
    tm, tn = _tile(T, 768, 16), _tile(Fp, 512, 128)

    def core(ins, outs, scr):
        uv, w_ref = ins[0][...], ins[1]
        gt = lax.dot_general(uv, w_ref[0], _DIMS["nt"], preferred_element_type=f32)
        up = lax.dot_general(uv, w_ref[1], _DIMS["nt"], preferred_element_type=f32)
        outs[0][0] = gt.astype(bf16)
        outs[0][1] = up.astype(bf16)
        outs[1][...] = _swiglu(gt, up).astype(bf16)

    return _carry_call(core, "ffn_up", (T // tm, Fp // tn),
                       [pl.BlockSpec((tm, D), lambda i, j: (i, 0)), pl.BlockSpec((2, tn, D), lambda i, j: (0, j, 0))],
                       [pl.BlockSpec((2, tm, tn), lambda i, j: (0, i, j)), pl.BlockSpec((tm, tn), lambda i, j: (i, j))],
                       [SDS((2, T, Fp), bf16), SDS((T, Fp), bf16)], [], (u, w13t.reshape(2, Fp, D)), comm)


def _ffn_down_dx(do, w2, gu, d, comm):
    T, D, Fp = d.T, d.D, d.Fp
    tm, tn = _tile(T, 768, 16), _tile(Fp, 512, 128)

    def core(ins, outs, scr):
        dhm = lax.dot_general(ins[0][...], ins[1][...], _DIMS["nt"], preferred_element_type=f32)
        _, vjp = jax.vjp(_swiglu, ins[2][0].astype(f32), ins[2][1].astype(f32))
        dg, du = vjp(dhm)
        outs[0][0] = dg.astype(bf16)
        outs[0][1] = du.astype(bf16)

    return _carry_call(core, "ffn_down_dx", (T // tm, Fp // tn),
                       [pl.BlockSpec((tm, D), lambda i, j: (i, 0)), pl.BlockSpec((tn, D), lambda i, j: (j, 0)),
                        pl.BlockSpec((2, tm, tn), lambda i, j: (0, i, j))],
                       [pl.BlockSpec((2, tm, tn), lambda i, j: (0, i, j))], [SDS((2, T, Fp), bf16)], [], (do, w2, gu), comm)[0]


def _ffn_fwd(h, modv, g, k13, k2, s, d, sc, site):
    u = _norm_mod_fwd(h, g, modv, s, d)
    gu, hmid = _ffn_up(u, sc.env[k13], d, sc.at(site + "up"))
    o = _mm(hmid, sc.env[k2], "nn", f32, "ffn_down", tn=1024, tk=2816, comm=sc.at(site + "down"))
    hn = _resid_fwd(h, o, modv, s + 2, 0.5, d)
    return hn, dict(h=h, u=u, gu=gu, hmid=hmid, o=o)


def _ffn_bwd(dh, sv, modv, g, k13, k2, s, d, sc, site):
    w13t, w2 = sc.env[k13], sc.env[k2]
    do, dgate = _resid_bwd(dh, sv["o"], modv, s + 2, 0.5, d)
    dgu = _ffn_down_dx(do, w2, sv["gu"], d, sc.at(site + "dx"))
    sc.env[("g",) + k2] = _mm(sv["hmid"], do, "tn", bf16, "ffn_down_dw", tm=2816, tn=1024, tk=768, comm=sc.at(site + "dw"))
    du = _mm(dgu, w13t, "nn", f32, "ffn_up_dx", tn=1024, tk=2816, comm=sc.at(site + "ux"), a3=True)
    sc.env[("g",) + k13] = _mm(dgu, sv["u"], "tn", bf16, "ffn_up_dw", tm=2816, tn=1024, tk=768, comm=sc.at(site + "uw"), a3=True)
    dhn, dg, dss = _norm_mod_bwd(sv["h"], du, dh, g, modv, s, d)
    return dhn, dg, dss, dgate


def _taps(ext, w, left, tr, sign):
    acc = None
    for k in range(w.shape[0]):
        o = HALO + sign * (k - left)
        term = w[k:k + 1] * ext[o:o + tr]
        acc = term if acc is None else acc + term
    return acc


def _gates(xa, wa, ba, wx, bx, lam, nb):
    bs = xa.shape[1] // nb
    out = []
    for dr in range(2):
        pa = jnp.concatenate([_bdot(xa[:, n * bs:(n + 1) * bs], wa[dr, n]) for n in range(nb)], axis=1)
        px = jnp.concatenate([_bdot(xa[:, n * bs:(n + 1) * bs], wx[dr, n]) for n in range(nb)], axis=1)
        rg = jax.nn.sigmoid(pa + ba[dr:dr + 1])
        ig = jax.nn.sigmoid(px + bx[dr:dr + 1])
        log_a = -LRU_C * rg * jax.nn.softplus(-lam[dr:dr + 1])
        a = jnp.exp(log_a)
        u = jnp.sqrt(1.0 - jnp.exp(2.0 * log_a)) * (ig * xa)
        out += [a, u]
    return out


def _combine_a(hf, hb, rg):
    return (hf + hb) * jax.nn.gelu(rg)


def _scan_order(kind, nT, nC):
    nL = nT - nC
    if kind == "F":
        return (lambda s: s), False
    if kind == "revF":
        return (lambda s: nT - 1 - s), True
    if kind == "B":
        return (lambda s: jnp.where(s < nC, nC - 1 - s, nT - 1 - (s - nC))), True
    return (lambda s: jnp.where(s < nL, nC + s, s - nL)), False


def _scan_fwd(au, dr, d):
    R, tc = d.BW, d.tr
    tile_of, down = _scan_order("F" if dr == 0 else "B", d.T // tc, d.C // tc)

    def body(a_ref, u_ref, h_ref, hp_ref, st):
        @pl.when(pl.program_id(0) == 0)
        def _():
            st[...] = jnp.zeros_like(st)

        def grp(gi, h):
            r = pl.multiple_of((tc // 8 - 1 - gi if down else gi) * 8, 8)
            at, ut = a_ref[pl.ds(r, 8), :], u_ref[pl.ds(r, 8), :]
            hs, hps = [None] * 8, [None] * 8
            for k in (range(7, -1, -1) if down else range(8)):
                hps[k] = h
                h = at[k:k + 1] * h + ut[k:k + 1]
                hs[k] = h
            h_ref[pl.ds(r, 8), :] = jnp.concatenate(hs, axis=0)
            hp_ref[pl.ds(r, 8), :] = jnp.concatenate(hps, axis=0)
            return h
        st[...] = lax.fori_loop(0, tc // 8, grp, st[...])

    return pl.pallas_call(
        body, name="lru_scan", grid=(d.T // tc,),
        in_specs=[pl.BlockSpec((tc, R), lambda s: (tile_of(s), 2 * dr)), pl.BlockSpec((tc, R), lambda s: (tile_of(s), 2 * dr + 1))],
        out_specs=[pl.BlockSpec((tc, R), lambda s: (tile_of(s), 0))] * 2, out_shape=[SDS((d.T, R), f32)] * 2,
        scratch_shapes=[pltpu.VMEM((1, R), f32)], compiler_params=_params(("arbitrary",)),
    )(au, au)


def _scan_bwd(dh, au, hp, dr, d):
    R, tc = d.BW, d.tr
    tile_of, down = _scan_order("revF" if dr == 0 else "revB", d.T // tc, d.C // tc)

    def body(dh_ref, a_ref, hp_ref, o_ref, st):
        @pl.when(pl.program_id(0) == 0)
        def _():
            st[...] = jnp.zeros_like(st)

        def grp(gi, carry):
            lam, an = carry
            r = pl.multiple_of((tc // 8 - 1 - gi if down else gi) * 8, 8)
            at, dt, ht = a_ref[pl.ds(r, 8), :], dh_ref[pl.ds(r, 8), :], hp_ref[pl.ds(r, 8), :]
            ls = [None] * 8
            for k in (range(7, -1, -1) if down else range(8)):
                lam = dt[k:k + 1] + an * lam
                an = at[k:k + 1]
                ls[k] = lam
            lt = jnp.concatenate(ls, axis=0)
            o_ref[pl.ds(r, 8), 0:R] = lt * ht
            o_ref[pl.ds(r, 8), R:2 * R] = lt
            return lam, an
        lam, an = lax.fori_loop(0, tc // 8, grp, (st[0:1], st[1:2]))
        st[0:1] = lam
        st[1:2] = an

    return pl.pallas_call(
        body, name="lru_scan_bwd", grid=(d.T // tc,),
        in_specs=[pl.BlockSpec((tc, R), lambda s: (tile_of(s), 0)), pl.BlockSpec((tc, R), lambda s: (tile_of(s), 2 * dr)),
                  pl.BlockSpec((tc, R), lambda s: (tile_of(s), 0))],
        out_specs=pl.BlockSpec((tc, 2 * R), lambda s: (tile_of(s), 0)), out_shape=SDS((d.T, 2 * R), f32),
        scratch_shapes=[pltpu.VMEM((2, R), f32)], compiler_params=_params(("arbitrary",)),
    )(dh, au, hp)


def _swap_pairs(x):
    w = x.shape[1]
    lane = lax.broadcasted_iota(jnp.int32, x.shape, 1)
    return jnp.where(lane % 64 < 32, pltpu.roll(x, w - 32, 1), pltpu.roll(x, 32, 1))


def _att_masks(blk, nB, nCb, G):
    rows, cols = G * ATT_BLOCK, 3 * ATT_BLOCK
    qi = lax.broadcasted_iota(jnp.int32, (rows, cols), 0) % ATT_BLOCK
    kj = lax.broadcasted_iota(jnp.int32, (rows, cols), 1)
    rel = kj - ATT_BLOCK - qi
    kb = kj // ATT_BLOCK
    one = jnp.int32(1)
    latent = jnp.where(blk >= nCb, one, 0)
    prev_ok = jnp.where(blk - 1 >= nCb, latent, 0)
    next_ok = jnp.where(blk + 1 <= nB - 1, latent, 0)
    bv = jnp.where(kb == 0, prev_ok, jnp.where(kb == 1, latent, next_ok))
    return jnp.logical_and(jnp.abs(rel) <= ATT_BLOCK, bv > 0)


def _att_specs(d):
    G, nB = d.G, d.T // ATT_BLOCK
    blk = lambda f: pl.BlockSpec((ATT_BLOCK, HEAD_DIM), lambda kh, b: (f(b), kh))
    three = [blk(lambda b: jnp.maximum(b - 1, 0)), blk(lambda b: b), blk(lambda b: jnp.minimum(b + 1, nB - 1))]
    ctxs = pl.BlockSpec((d.C, HEAD_DIM), lambda kh, b: (0, kh))
    qs = pl.BlockSpec((ATT_BLOCK, G * HEAD_DIM), lambda kh, b: (b, kh))
    sk = pl.BlockSpec((None, G * ATT_BLOCK, 1), lambda kh, b: (kh, 0, 0))
    return qs, three, ctxs, sk


def _stack_heads(x, G):
    return jnp.concatenate([x[:, g * HEAD_DIM:(g + 1) * HEAD_DIM] for g in range(G)], axis=0)


def _unstack_heads(x, G):
    return jnp.concatenate([x[g * ATT_BLOCK:(g + 1) * ATT_BLOCK] for g in range(G)], axis=1)


def _att_probs(qg, kcat, kctx, sink, valid):
    scale = HEAD_DIM ** -0.5
    s = lax.dot_general(qg, kcat, _DIMS["nt"], preferred_element_type=f32) * scale
    s = jnp.where(valid, s, NEG_INF)
    sc = lax.dot_general(qg, kctx, _DIMS["nt"], preferred_element_type=f32) * scale
    m = jnp.maximum(jnp.maximum(jnp.max(s, axis=1, keepdims=True), jnp.max(sc, axis=1, keepdims=True)), sink)
    e, ec, es = jnp.exp(s - m), jnp.exp(sc - m), jnp.exp(sink - m)
    inv = 1.0 / (jnp.sum(e, axis=1, keepdims=True) + jnp.sum(ec, axis=1, keepdims=True) + es)
    return e * inv, ec * inv, es * inv


def _attn_fwd(qr, kr, vb, sinkrows, d):
    G, nB, nCb = d.G, d.T // ATT_BLOCK, d.C // ATT_BLOCK
    qs, three, ctxs, sk = _att_specs(d)

    def body(q_ref, k0, k1, k2, v0, v1, v2, kc_ref, vc_ref, s_ref, o_ref):
        b = pl.program_id(1)
        qg = _stack_heads(q_ref[...], G)
        kcat = jnp.concatenate([k0[...], k1[...], k2[...]], axis=0)
        vcat = jnp.concatenate([v0[...], v1[...], v2[...]], axis=0)
        p, pc, _ = _att_probs(qg, kcat, kc_ref[...], s_ref[...], _att_masks(b, nB, nCb, G))
        o = jnp.dot(p.astype(bf16), vcat, preferred_element_type=f32) + jnp.dot(pc.astype(bf16), vc_ref[...], preferred_element_type=f32)
        o_ref[...] = _unstack_heads(o, G).astype(bf16)

    return pl.pallas_call(
        body, name="attn", grid=(d.NKV, nB), in_specs=[qs] + three + three + [ctxs, ctxs, sk],
        out_specs=qs, out_shape=SDS((d.T, d.NH * HEAD_DIM), bf16), compiler_params=_params(("parallel", "arbitrary")),
    )(qr, kr, kr, kr, vb, vb, vb, kr, vb, sinkrows)


def _attn_bwd(qr, kr, vb, sinkrows, dy, d):
    G, nB, nCb = d.G, d.T // ATT_BLOCK, d.C // ATT_BLOCK
    qs, three, ctxs, sk = _att_specs(d)
    KW = d.NKV * HEAD_DIM
    part = pl.BlockSpec((ATT_BLOCK, HEAD_DIM), lambda kh, b: (b, kh))

    def body(q_ref, k0, k1, k2, v0, v1, v2, kc_ref, vc_ref, s_ref, dy_ref,
             dq_ref, dk0, dk1, dk2, dv0, dv1, dv2, dkc_ref, dvc_ref, ds_ref):
        b = pl.program_id(1)
        scale = HEAD_DIM ** -0.5
        qg = _stack_heads(q_ref[...], G)
        kcat = jnp.concatenate([k0[...], k1[...], k2[...]], axis=0)
        vcat = jnp.concatenate([v0[...], v1[...], v2[...]], axis=0)
        kctx, vctx = kc_ref[...], vc_ref[...]
        p, pc, ps = _att_probs(qg, kcat, kctx, s_ref[...], _att_masks(b, nB, nCb, G))
        dog = _stack_heads(dy_ref[...], G).astype(bf16)
        dp = lax.dot_general(dog, vcat, _DIMS["nt"], preferred_element_type=f32)
        dpc = lax.dot_general(dog, vctx, _DIMS["nt"], preferred_element_type=f32)
        delta = jnp.sum(p * dp, axis=1, keepdims=True) + jnp.sum(pc * dpc, axis=1, keepdims=True)
        ds = (p * (dp - delta) * scale).astype(bf16)
        dsc = (pc * (dpc - delta) * scale).astype(bf16)
        dq = jnp.dot(ds, kcat, preferred_element_type=f32) + jnp.dot(dsc, kctx, preferred_element_type=f32)
        dq_ref[...] = _unstack_heads(dq, G)
        dk = lax.dot_general(ds, qg, _DIMS["tn"], preferred_element_type=f32)
        dv = lax.dot_general(p.astype(bf16), dog, _DIMS["tn"], preferred_element_type=f32)
        for j, (rk, rv) in enumerate(((dk0, dv0), (dk1, dv1), (dk2, dv2))):
            rk[...] = dk[j * ATT_BLOCK:(j + 1) * ATT_BLOCK]
            rv[...] = dv[j * ATT_BLOCK:(j + 1) * ATT_BLOCK]

        @pl.when(b == 0)
        def _():
            dkc_ref[...] = jnp.zeros_like(dkc_ref)
            dvc_ref[...] = jnp.zeros_like(dvc_ref)
            ds_ref[...] = jnp.zeros_like(ds_ref)

        dkc_ref[...] += lax.dot_general(dsc, qg, _DIMS["tn"], preferred_element_type=f32)
        dvc_ref[...] += lax.dot_general(pc.astype(bf16), dog, _DIMS["tn"], preferred_element_type=f32)
        ds_ref[...] += -ps * delta

    kv = SDS((d.T, KW), f32)
    return pl.pallas_call(
        body, name="attn_bwd", grid=(d.NKV, nB), in_specs=[qs] + three + three + [ctxs, ctxs, sk, qs],
        out_specs=[qs] + [part] * 6 + [ctxs, ctxs, sk],
        out_shape=[SDS((d.T, d.NH * HEAD_DIM), f32)] + [kv] * 6 + [SDS((d.C, KW), f32)] * 2 + [SDS((d.NKV, G * ATT_BLOCK, 1), f32)],
        compiler_params=_params(("parallel", "arbitrary")),
    )(qr, kr, kr, kr, vb, vb, vb, kr, vb, sinkrows, dy)


def _mixer_fwd(h, modv, P, l, tabs, d, sc, site):
    D, BW, T, C = d.D, d.BW, d.T, d.C
    u = _norm_mod_fwd(h, P["norm_g"][1:2], modv, 3, d)
    za = _mm(u, sc.env[("wint", l)], "nt", bf16, "in_proj_a", N=d.NA, comm=sc.at(site + "in_a"))
    gz = _mm(u, sc.env[("wint", l)], "nt", bf16, "in_proj_g", N=3 * D, b_off=d.NA, comm=sc.at(site + "in_g"))
    xa = _rowwise("lru_conv", lambda t: ([_taps(t.H[0], t.K[0], 2, d.tr, 1) + t.K[1]], [], []), T, C, d.tr,
                  halos=[(za, BW, 0)], consts=[P["rnn_conv_w"], P["rnn_conv_b"]], outs=[(BW, f32)])[0]
    lru = [P["lru_w_a"], P["lru_b_a"], P["lru_w_x"], P["lru_b_x"], P["lru_lambda"]]
    au = _rowwise("lru_gates", lambda t: ([_gates(t.R[0], *t.K, d.NB)], [], []), T, C, d.tr,
                  rows=[(xa, BW, 0, 0)], consts=lru, outs=[(4 * BW, f32)])[0]
    hf, hpf = _scan_fwd(au, 0, d)
    hb, hpb = _scan_fwd(au, 1, d)
    ya = _rowwise("lru_out", lambda t: ([_combine_a(*t.R)], [], []), T, C, d.tr,
                  rows=[(hf, BW, 0, 0), (hb, BW, 0, 0), (za, BW, 1, 0)], outs=[(BW, bf16)])[0]
    yb = _rowwise("sconv", lambda t: ([t.R[0] * _taps(t.H[0] * t.H[1], t.K[0], 1, d.tr, 1)], [], []), T, C, d.tr,
                  rows=[(za, BW, 2, 0)], halos=[(za, BW, 3), (za, BW, 4)], consts=[P["sc_conv_w"]], outs=[(BW, bf16)])[0]
    QW, KW = d.NH * HEAD_DIM, d.NKV * HEAD_DIM

    def rope(t):
        q, k, v, cq, sq, ck, skn = t.R
        return [q * cq + _swap_pairs(q) * sq, k * ck + _swap_pairs(k) * skn, v], [], []
    kcb = (5 * BW + QW) // KW
    qr, kr, vb = _rowwise("rope", rope, T, C, d.tr,
                          rows=[(za, QW, (5 * BW) // QW, 0), (za, KW, kcb, 0), (za, KW, kcb + 1, 0),
                                (tabs["cos"], QW, 0, 0), (tabs["sin"], QW, 0, 0), (tabs["cos"], KW, 0, 0), (tabs["sin"], KW, 0, 0)],
                          outs=[(QW, bf16), (KW, bf16), (KW, bf16)])
    yatt = _attn_fwd(qr, kr, vb, P["sinkrows"], d)
    ys = (ya, yb, yatt)
    ps = [_mm(ys[i], sc.env[("wbt", l)], "nt", bf16, "lift", b_lead=i, tn=1024) for i in range(N_BRANCH)]

    def merge(t):
        gzv, bm = t.R[0], t.K[0]
        acc = None
        for i in range(N_BRANCH):
            term = jax.nn.sigmoid(gzv[:, i * D:(i + 1) * D] + bm[i:i + 1]) * t.R[1 + i]
            acc = term if acc is None else acc + term
        return [acc], [], []
    merged = _rowwise("merge", merge, T, C, d.trw, rows=[(gz, 3 * D, 0, 0)] + [(p, D, 0, 0) for p in ps],
                      consts=[P["b_merge"]], outs=[(D, bf16)])[0]
    y = _mm(merged, sc.env[("wout", l)], "nn", f32, "out_proj", tn=1024, comm=sc.at(site + "out"))
    hn = _resid_fwd(h, y, modv, 5, 1.0, d)
    sv = dict(h=h, u=u, za=za, gz=gz, xa=xa, au=au, hf=hf, hpf=hpf, hb=hb, hpb=hpb, ys=ys, qr=qr, kr=kr, vb=vb,
              ps=ps, merged=merged, y=y)
    return hn, sv


def _mixer_bwd(dh, sv, modv, P, l, tabs, d, sc, site):
    W = dict(wint=sc.env[("wint", l)], wbt=sc.env[("wbt", l)], wout=sc.env[("wout", l)])
    D, BW, T, C = d.D, d.BW, d.T, d.C
    QW, KW = d.NH * HEAD_DIM, d.NKV * HEAD_DIM
    za, gz = sv["za"], sv["gz"]
    dy, dgate = _resid_bwd(dh, sv["y"], modv, 5, 1.0, d)
    dmerged = _mm(dy, W["wout"], "nt", f32, "out_proj_dx", tn=1024)
    sc.env[("g", "wout", l)] = _mm(sv["merged"], dy, "tn", bf16, "out_proj_dw", tm=1024, tn=2048, tk=768)

    def merge_bwd(t):
        gzv, bm, dm = t.R[0], t.K[0], t.R[4]
        dps, dgs, dbs = [], [], []
        for i in range(N_BRANCH):
            gate = jax.nn.sigmoid(gzv[:, i * D:(i + 1) * D] + bm[i:i + 1])
            dps.append(dm * gate)
            dgi = dm * t.R[1 + i] * gate * (1.0 - gate)
            dgs.append(dgi)
            dbs.append(jnp.sum(dgi, axis=0, keepdims=True))
        return [dps, dgs], [jnp.concatenate(dbs, axis=0)], []
    dp, dgz, dbm = _rowwise("merge_bwd", merge_bwd, T, C, d.trw,
                            rows=[(gz, 3 * D, 0, 0)] + [(p, D, 0, 0) for p in sv["ps"]] + [(dmerged, D, 0, 0)],
                            consts=[P["b_merge"]], outs=[(3 * D, bf16), (3 * D, bf16)], accs=[(N_BRANCH, D)])
    dys = [_mm(dp, W["wbt"], "nn", f32, "lift_dx", K=D, a_off=i * D, b_lead=i, tn=1024) for i in range(N_BRANCH)]
    sc.env[("g", "wbt", l)] = jnp.stack([_mm(dp, sv["ys"][i], "tn", bf16, "lift_dw", M=D, a_off=i * D, tm=1024, tn=1024, tk=768)
                                         for i in range(N_BRANCH)])
    def out_bwd(t):
        _, vjp = jax.vjp(_combine_a, t.R[1], t.R[2], t.R[3])
        dhf, _, drg = vjp(t.R[0])
        return [dhf, drg], [], []
    dhs, drg = _rowwise("lru_out_bwd", out_bwd, T, C, d.tr,
                        rows=[(dys[0], BW, 0, 0), (sv["hf"], BW, 0, 0), (sv["hb"], BW, 0, 0), (za, BW, 1, 0)],
                        outs=[(BW, f32), (BW, bf16)])
    dau0 = _scan_bwd(dhs, sv["au"], sv["hpf"], 0, d)
    dau1 = _scan_bwd(dhs, sv["au"], sv["hpb"], 1, d)
    lru = [P["lru_w_a"], P["lru_b_a"], P["lru_w_x"], P["lru_b_x"], P["lru_lambda"]]

    def gates_bwd(t):
        _, vjp = jax.vjp(lambda xa, *k: _gates(xa, *k, d.NB), t.R[0], *t.K)
        d0, d1 = t.R[1], t.R[2]
        g = vjp([d0[:, :BW], d0[:, BW:], d1[:, :BW], d1[:, BW:]])
        return [g[0]], list(g[1:]), []
    dxa, dwa, dba, dwx, dbx, dlam = _rowwise(
        "lru_gates_bwd", gates_bwd, T, C, d.tr, rows=[(sv["xa"], BW, 0, 0), (dau0, 2 * BW, 0, 0), (dau1, 2 * BW, 0, 0)],
        consts=lru, outs=[(BW, f32)], accs=[p.shape for p in lru])

    def conv_bwd(t):
        dxe, xe, w = t.H[0], t.H[1], t.K[0]
        cur = dxe[HALO:HALO + d.tr]
        dw = jnp.concatenate([jnp.sum(cur * xe[HALO + k - 2:HALO + k - 2 + d.tr], axis=0, keepdims=True) for k in range(w.shape[0])], axis=0)
        return [_taps(dxe, w, 2, d.tr, -1)], [dw, jnp.sum(cur, axis=0, keepdims=True)], []
    drx, dcw, dcb = _rowwise("lru_conv_bwd", conv_bwd, T, C, d.tr, halos=[(dxa, BW, 0), (za, BW, 0)],
                             consts=[P["rnn_conv_w"]], outs=[(BW, bf16)], accs=[P["rnn_conv_w"].shape, (1, BW)])
    def sconv_bwd(t):
        scg, sx, sb, dyb = t.H
        w, tr = t.K[0], d.tr
        me = scg * sx
        dsb = dyb[HALO:HALO + tr] * _taps(me, w, 1, tr, 1)
        dce = dyb * sb
        dm = _taps(dce, w, 1, tr, -1)
        cur = dce[HALO:HALO + tr]
        dw = jnp.concatenate([jnp.sum(cur * me[HALO + k - 1:HALO + k - 1 + tr], axis=0, keepdims=True) for k in range(w.shape[0])], axis=0)
        return [[dsb, dm * sx[HALO:HALO + tr], dm * scg[HALO:HALO + tr]]], [dw], []
    dsc, dscw = _rowwise("sconv_bwd", sconv_bwd, T, C, d.tr, halos=[(za, BW, 3), (za, BW, 4), (za, BW, 2), (dys[1], BW, 0)],
                         consts=[P["sc_conv_w"]], outs=[(3 * BW, bf16)], accs=[P["sc_conv_w"].shape])
    dqr, dk0, dk1, dk2, dv0, dv1, dv2, dkc, dvc, dsink = _attn_bwd(sv["qr"], sv["kr"], sv["vb"], P["sinkrows"], dys[2], d)
    nB, nCb = T // ATT_BLOCK, C // ATT_BLOCK

    def att_join(t):
        dq, a1, a0, a2, b1, b0, b2, kc, vc, cq, sq, ck, skn = t.R
        up = t.i + 1 <= nB - 1
        dn = t.i >= 1
        isc = t.seg == 0
        dk = a1 + jnp.where(up, a0, 0.0) + jnp.where(dn, a2, 0.0) + jnp.where(isc, kc, 0.0)
        dv = b1 + jnp.where(up, b0, 0.0) + jnp.where(dn, b2, 0.0) + jnp.where(isc, vc, 0.0)
        return [[dq * cq + _swap_pairs(dq * sq), dk * ck + _swap_pairs(dk * skn), dv]], [], []
    dqkv = _rowwise("attn_join", att_join, T, C, ATT_BLOCK,
                    rows=[(dqr, QW, 0, 0), (dk1, KW, 0, 0), (dk0, KW, 0, 1), (dk2, KW, 0, -1), (dv1, KW, 0, 0), (dv0, KW, 0, 1),
                          (dv2, KW, 0, -1), (dkc, KW, 0, 0), (dvc, KW, 0, 0), (tabs["cos"], QW, 0, 0), (tabs["sin"], QW, 0, 0),
                          (tabs["cos"], KW, 0, 0), (tabs["sin"], KW, 0, 0)], outs=[(QW + 2 * KW, bf16)])[0]
    dz = jnp.concatenate([drx, drg, dsc, dqkv, dgz], axis=1)
    du = _mm(dz, W["wint"], "nn", f32, "in_proj_dx", tn=1024, tk=3200, comm=sc.at(site + "in_x"))
    sc.env[("g", "wint", l)] = _mm(dz, sv["u"], "tn", bf16, "in_proj_dw", tm=2560, tn=1024, tk=768, comm=sc.at(site + "in_w"))
    dhn, dg, dss = _norm_mod_bwd(sv["h"], du, dh, P["norm_g"][1:2], modv, 3, d)
    grads = dict(norm_g1=dg, dss=dss, dgate=dgate, b_merge=dbm, rnn_conv_w=dcw, rnn_conv_b=dcb,
                 lru_w_a=dwa, lru_b_a=dba, lru_w_x=dwx, lru_b_x=dbx, lru_lambda=dlam, sc_conv_w=dscw, sinkrows=dsink)
    return dhn, grads


def _exchange(name, comm):
    ns, nd = len(comm.srcs), len(comm.dsts)

    def body(*refs):
        descs = comm.descriptors(refs[:ns], refs[ns + nd:ns + 2 * nd], refs[ns + 2 * nd:], _coords())
        _comm_start(descs)
        _comm_wait(descs)

    res = pl.pallas_call(
        body, name=name, in_specs=[ANY] * (ns + nd), out_specs=[ANY] * nd, out_shape=[SDS(a.shape, a.dtype) for a in comm.dsts],
        input_output_aliases={ns + q: q for q in range(nd)}, scratch_shapes=comm.scratch(),
    )(*comm.srcs, *comm.dsts)
    comm.store(res)


def _allgather(name, a):
    env = {"g": lax.empty((N_DEV,) + a.shape, a.dtype)}
    cm = _Comm(env)
    cm.add(a, "g", lambda ref, me, pi, r: ref, lambda ref, sender, r: ref.at[sender])
    _exchange(name, cm)
    return env["g"]


def _sum8(x):
    n = x.shape[1]
    tn = _tile(n, 2048, 8)

    def body(x_ref, o_ref):
        acc = x_ref[0]
        for j in range(1, N_DEV):
            acc = acc + x_ref[j]
        o_ref[...] = acc
    return pl.pallas_call(body, name="sum8", grid=(n // tn,), in_specs=[pl.BlockSpec((N_DEV, tn, 128), lambda i: (0, i, 0))],
                          out_specs=pl.BlockSpec((tn, 128), lambda i: (i, 0)), out_shape=SDS((n, 128), f32),
                          compiler_params=_params(("parallel",)))(x)


def _cast_direct(w):
    ly, R, Cn = w.shape
    tc = _tile(Cn, 512, 128)
    spec = pl.BlockSpec((None, R, tc), lambda l, j: (l, 0, j))

    def body(w_ref, o_ref):
        o_ref[...] = w_ref[...].astype(bf16)
    return pl.pallas_call(body, name="cast", grid=(ly, Cn // tc), in_specs=[spec], out_specs=spec, out_shape=SDS(w.shape, bf16),
                          compiler_params=_params(("parallel", "parallel")))(w)


def _cast_transposed(w):
    ly, I, K, Nl = w.shape
    tk = _tile(K, 256, 128)

    def body(w_ref, o_ref):
        o_ref[...] = w_ref[...].T.astype(bf16)
    return pl.pallas_call(
        body, name="cast_t", grid=(ly, I, K // tk), in_specs=[pl.BlockSpec((None, None, tk, Nl), lambda l, i, k: (l, i, k, 0))],
        out_specs=pl.BlockSpec((None, None, Nl, tk), lambda l, i, k: (l, i, 0, k)), out_shape=SDS((ly, I, Nl, K), bf16),
        compiler_params=_params(("parallel", "parallel", "parallel")))(w)


def _adam(g, w, m, v):
    m = ADAM_B1 * m + (1.0 - ADAM_B1) * g
    v = ADAM_B2 * v + (1.0 - ADAM_B2) * (g * g)
    m_hat = m / (1.0 - ADAM_B1 ** ADAM_STEP)
    v_hat = v / (1.0 - ADAM_B2 ** ADAM_STEP)
    delta = -ADAM_LR * (m_hat / (jnp.sqrt(v_hat) + ADAM_EPS) + ADAM_WD * w)
    return delta, m, v


def _slot_sum(x):
    acc = x[0].astype(f32)
    for r in range(1, x.shape[0]):
        acc = acc + x[r].astype(f32)
    return acc


def _adamw_direct(g, w, m, v, slots):
    ly, R, Cn = w.shape
    tr = R if R <= 1024 else _tile(R, 512, 16)
    tc = _tile(Cn, max(128, (256 * 1024) // tr // 128 * 128), 128)
    spec = pl.BlockSpec((None, tr, tc), lambda l, i, j: (l, i, j))
    gspec = pl.BlockSpec((None, N_DEV, tr, tc), lambda l, i, j: (l, 0, i, j)) if slots else spec

    def body(g_ref, w_ref, m_ref, v_ref, go, do, mo, vo):
        gv = _slot_sum(g_ref[...]) if slots else g_ref[...]
        go[...] = gv
        do[...], mo[...], vo[...] = _adam(gv, w_ref[...], m_ref[...], v_ref[...])
    return pl.pallas_call(body, name="adamw", grid=(ly, R // tr, Cn // tc), in_specs=[gspec, spec, spec, spec], out_specs=[spec] * 4,
                          out_shape=[SDS(w.shape, f32)] * 4, compiler_params=_params(("parallel",) * 3))(g, w, m, v)


def _adamw_transposed(g, w, m, v):
    ly, I, K, Nl = w.shape
    tk = _tile(K, 128, 128)
    spec = pl.BlockSpec((None, None, tk, Nl), lambda l, i, k: (l, i, k, 0))
    gspec = pl.BlockSpec((None, N_DEV, None, Nl, tk), lambda l, i, k: (l, 0, i, 0, k))

    def body(g_ref, w_ref, m_ref, v_ref, go, do, mo, vo):
        gv = _slot_sum(g_ref[...]).T
        go[...] = gv
        do[...], mo[...], vo[...] = _adam(gv, w_ref[...], m_ref[...], v_ref[...])
    return pl.pallas_call(body, name="adamw_t", grid=(ly, I, K // tk), in_specs=[gspec, spec, spec, spec], out_specs=[spec] * 4,
                          out_shape=[SDS(w.shape, f32)] * 4, compiler_params=_params(("parallel",) * 3))(g, w, m, v)


def _ada_fwd(cvec, ada_w, ada_b_cols):
    ly, D, cols = ada_w.shape
    tn = _tile(cols, 768, 128)

    def body(c_ref, w_ref, b_ref, o_ref):
        o_ref[...] = _bdot(jax.nn.silu(c_ref[...]), w_ref[...]) + b_ref[...]
    return pl.pallas_call(
        body, name="ada", grid=(ly, cols // tn),
        in_specs=[pl.BlockSpec((16, D), lambda l, j: (0, 0)), pl.BlockSpec((None, D, tn), lambda l, j: (l, 0, j)),
                  pl.BlockSpec((None, 1, tn), lambda l, j: (l, 0, j))],
        out_specs=pl.BlockSpec((None, 16, tn), lambda l, j: (l, 0, j)), out_shape=SDS((ly, 16, cols), f32),
        compiler_params=_params(("parallel", "parallel")))(cvec, ada_w, ada_b_cols)


def _ada_bwd(cvec, ada_w, dm):
    ly, D, cols = ada_w.shape
    tn = _tile(cols, 768, 128)

    def body(c_ref, w_ref, d_ref, gw_ref, gc_ref):
        first = jnp.logical_and(pl.program_id(0) == 0, pl.program_id(1) == 0)

        @pl.when(first)
        def _():
            gc_ref[...] = jnp.zeros_like(gc_ref)

        def f(cv, w):
            return _bdot(jax.nn.silu(cv), w)
        _, vjp = jax.vjp(f, c_ref[...], w_ref[...])
        dc, dw = vjp(d_ref[...])
        gw_ref[...] = dw
        gc_ref[...] += dc
    return pl.pallas_call(
        body, name="ada_bwd", grid=(ly, cols // tn),
        in_specs=[pl.BlockSpec((16, D), lambda l, j: (0, 0)), pl.BlockSpec((None, D, tn), lambda l, j: (l, 0, j)),
                  pl.BlockSpec((None, 16, tn), lambda l, j: (l, 0, j))],
        out_specs=[pl.BlockSpec((None, D, tn), lambda l, j: (l, 0, j)), pl.BlockSpec((16, D), lambda l, j: (0, 0))],
        out_shape=[SDS(ada_w.shape, f32), SDS((16, D), f32)], compiler_params=_params(("arbitrary", "arbitrary")))(cvec, ada_w, dm)


def _flat_adamw(g, w, m, v):
    n = w.shape[0]
    tn = _tile(n, 1024, 8)
    spec = pl.BlockSpec((tn, 128), lambda i: (i, 0))

    def body(g_ref, w_ref, m_ref, v_ref, do, mo, vo):
        do[...], mo[...], vo[...] = _adam(g_ref[...], w_ref[...], m_ref[...], v_ref[...])
    return pl.pallas_call(body, name="adamw_small", grid=(n // tn,), in_specs=[spec] * 4, out_specs=[spec] * 3,
                          out_shape=[SDS(w.shape, f32)] * 3, compiler_params=_params(("parallel",)))(g, w, m, v)


def _pack(arrs):
    flat = jnp.concatenate([a.reshape(-1).astype(f32) for a in arrs])
    n = _round_up(flat.shape[0], 512 * 128)
    return jnp.pad(flat, (0, n - flat.shape[0])).reshape(n // 128, 128)


def _unpack(flat, shapes, lead=()):
    flat = flat.reshape(lead + (-1,))
    out, off = [], 0
    for s in shapes:
        sz = math.prod(s)
        out.append(flat[..., off:off + sz].reshape(lead + tuple(s)))
        off += sz
    return out


def _unshard_last(g):
    g = jnp.moveaxis(g, 0, -2)
    return g.reshape(g.shape[:-2] + (g.shape[-2] * g.shape[-1],))


class _Dims:
    pass


def _rope_tables(L, C, NH):
    rows = L // GRID_W
    row = jnp.repeat(jnp.arange(rows), GRID_W).astype(f32)
    col = jnp.tile(jnp.arange(GRID_W), rows).astype(f32)
    half = HEAD_DIM // 2
    inv = ROPE_BASE ** (-jnp.arange(0, half, 2, dtype=f32) / half)
    ar, ac = row[:, None] * inv, col[:, None] * inv
    cos = jnp.concatenate([jnp.cos(ar), jnp.cos(ar), jnp.cos(ac), jnp.cos(ac)], axis=-1)
    sin = jnp.concatenate([-jnp.sin(ar), jnp.sin(ar), -jnp.sin(ac), jnp.sin(ac)], axis=-1)
    cos = jnp.concatenate([jnp.ones((C, HEAD_DIM), f32), cos], axis=0)
    sin = jnp.concatenate([jnp.zeros((C, HEAD_DIM), f32), sin], axis=0)
    return dict(cos=jnp.tile(cos, (1, NH)), sin=jnp.tile(sin, (1, NH)))


def kernel(x, c, ctx, c_ctx, ada_w, ada_b, norm_g, ffn1_w13, ffn1_w2, w_in, b_merge, rnn_conv_w, rnn_conv_b, lru_w_a, lru_b_a, lru_w_x, lru_b_x, lru_lambda, sc_conv_w, attn_sink, w_branch, w_out, ffn2_w13, ffn2_w2, final_norm_g, loss_target, m_c_ctx, m_ada_w, m_ada_b, m_norm_g, m_ffn1_w13, m_ffn1_w2, m_w_in, m_b_merge, m_rnn_conv_w, m_rnn_conv_b, m_lru_w_a, m_lru_b_a, m_lru_w_x, m_lru_b_x, m_lru_lambda, m_sc_conv_w, m_attn_sink, m_w_branch, m_w_out, m_ffn2_w13, m_ffn2_w2, m_final_norm_g, v_c_ctx, v_ada_w, v_ada_b, v_norm_g, v_ffn1_w13, v_ffn1_w2, v_w_in, v_b_merge, v_rnn_conv_w, v_rnn_conv_b, v_lru_w_a, v_lru_b_a, v_lru_w_x, v_lru_b_x, v_lru_lambda, v_sc_conv_w, v_attn_sink, v_w_branch, v_w_out, v_ffn2_w13, v_ffn2_w2, v_final_norm_g):
    d = _Dims()
    L, D = x.shape[1], x.shape[2]
    C = ctx.shape[1]
    LY = ada_w.shape[0]
    d.D, d.C, d.T = D, C, C + L
    d.F = ffn1_w2.shape[1] * N_DEV
    d.Fp = _round_up(d.F, 512)
    d.IN = w_in.shape[2] * N_DEV
    d.BW = w_branch.shape[2]
    d.NH = attn_sink.shape[1]
    d.NA = d.IN - N_BRANCH * D
    d.NKV = (d.NA - 5 * d.BW - d.NH * HEAD_DIM) // (2 * HEAD_DIM)
    d.G = d.NH // d.NKV
    d.NB = lru_w_a.shape[2]
    d.tr = _tile(C, 256, 8)
    d.trw = _tile(C, 128, 8)
    assert d.NH * HEAD_DIM == d.BW and L % GRID_W == 0 and C % ATT_BLOCK == 0 and L % ATT_BLOCK == 0
    sh13, sh2, shin, shd = 2 * d.F // N_DEV, d.F // N_DEV, d.IN // N_DEV, D // N_DEV
    cols9 = N_MOD * D // N_DEV
    me3 = _coords()
    me = _index(me3)

    sharded = [norm_g, b_merge, rnn_conv_w, lru_b_a, lru_b_x, lru_lambda, sc_conv_w]
    shapes1 = [a.shape for a in sharded] + [(D,)]
    g1 = _allgather("gather_small", _pack(sharded + [c.reshape(-1)]))
    parts = _unpack(g1, shapes1, (N_DEV,))
    norm_g_f, b_merge_f, conv_w_f, lru_b_a_f, lru_b_x_f, lru_lam_f, sc_w_f = [_unshard_last(p) for p in parts[:-1]]
    cvec = jnp.concatenate([parts[-1], c_ctx[None], jnp.zeros((7, D), f32)], axis=0)
    ada_b_cols = lax.dynamic_slice_in_dim(ada_b, me * cols9, cols9, axis=1)[:, None, :]
    modcols = _ada_fwd(cvec, ada_w, ada_b_cols)
    g2 = _allgather("gather_mod", modcols)
    modall = jnp.moveaxis(g2, 0, 2).reshape(LY, 16, N_MOD, D)
    mod_lat = lax.dynamic_index_in_dim(modall, me, axis=1, keepdims=False)
    modv = jnp.stack([modall[:, N_DEV], mod_lat], axis=1)

    assert LY == 2
    off13 = lambda j: (j // 4) * d.Fp + (j % 4) * sh13
    cast_t = lambda w: _cast_transposed(w.reshape(LY, 1, w.shape[1], w.shape[2])).reshape(LY, w.shape[2], w.shape[1])
    c13a, c13b = cast_t(ffn1_w13), cast_t(ffn2_w13)
    wt = {"w13a": (c13a, off13, sh13, 2 * d.Fp, (D,), False), "w13b": (c13b, off13, sh13, 2 * d.Fp, (D,), False),
          "w2a": (_cast_direct(ffn1_w2), lambda j: j * sh2, sh2, d.Fp, (D,), False),
          "w2b": (_cast_direct(ffn2_w2), lambda j: j * sh2, sh2, d.Fp, (D,), False),
          "wint": (cast_t(w_in), lambda j: j * shin, shin, d.IN, (D,), False),
          "wbt": (_cast_transposed(w_branch), lambda j: j * shd, shd, D, (d.BW,), True),
          "wout": (_cast_direct(w_out), lambda j: j * shd, shd, D, (D,), False)}
    env = {}
    for name, (cw, off, sh, tot, rest, three) in wt.items():
        for l in range(LY):
            env[(name, l)] = lax.empty(((N_BRANCH, tot) if three else (tot,)) + rest, bf16)
        env[("r", name)] = lax.empty((LY, N_DEV) + ((N_BRANCH, sh) if three else (sh,)) + rest, bf16)
    zpad = jnp.zeros((max(d.Fp - d.F, 16), D), bf16)

    def part(sh, h):
        return (0, sh) if h is None else (h * (sh // 2), sh // 2)

    def rows_of(ref, start, n, three):
        sl = pl.ds(start if isinstance(start, int) else pl.multiple_of(start, 16), n)
        return ref.at[:, sl] if three else ref.at[sl]

    def add_gather(cm, name, l, h=None):
        cw, off, sh, tot, rest, three = wt[name]
        r0, n = part(sh, h)
        cm.add(cw, (name, l), lambda ref, me_, pi, r: rows_of(ref.at[l], r0, n, three),
               lambda ref, sender, r: rows_of(ref, off(sender) + r0, n, three))
        if d.Fp > d.F and h in (None, 0) and name[:2] in ("w1", "w2"):
            for base in ((0, d.Fp) if name[:3] == "w13" else (0,)):
                cm.fill(zpad, (name, l), lambda ref, base=base: ref.at[pl.ds(base + d.F, d.Fp - d.F)])

    def add_rs(cm, name, l, h=None):
        cw, off, sh, tot, rest, three = wt[name]
        r0, n = part(sh, h)
        cm.add(env[("g", name, l)], ("r", name), lambda ref, me_, pi, r: rows_of(ref, off(pi) + r0, n, three),
               lambda ref, sender, r: rows_of(ref.at[l, r], r0, n, three))

    plan = {"F0a.up": [("ag", "wint", 0)], "F0a.down": [("ag", "wbt", 0), ("ag", "wout", 0)],
            "F0m.in_a": [("ag", "w13b", 0, 0)], "F0m.in_g": [("ag", "w13b", 0, 1)],
            "F0b.up": [("ag", "w2b", 0), ("ag", "w13a", 1, 0)], "F0b.down": [("ag", "w13a", 1, 1)],
            "F1a.up": [("ag", "w2a", 1), ("ag", "wint", 1, 0)], "F1a.down": [("ag", "wint", 1, 1)],
            "F1m.in_a": [("ag", "wbt", 1), ("ag", "wout", 1)], "F1m.in_g": [("ag", "w13b", 1, 0)],
            "F1m.out": [("ag", "w13b", 1, 1)], "F1b.up": [("ag", "w2b", 1)],
            "B0b.dx": [("rs", "w13a", 1, 0)], "B0b.dw": [("rs", "w13a", 1, 1)]}
    for l in range(LY):
        plan.update({f"B{l}b.ux": [("rs", "w2b", l)], f"B{l}m.in_x": [("rs", "w13b", l)],
                     f"B{l}m.in_w": [("rs", "wout", l), ("rs", "wbt", l)], f"B{l}a.dx": [("rs", "wint", l, 0)],
                     f"B{l}a.dw": [("rs", "wint", l, 1)], f"B{l}a.ux": [("rs", "w2a", l)]})
    sc = _Sched(plan, env, {"ag": add_gather, "rs": add_rs})
    first = _Comm(env)
    add_gather(first, "w13a", 0)
    add_gather(first, "w2a", 0)
    _exchange("gather_first", first)

    tabs = _rope_tables(L, C, d.NH)
    h = jnp.concatenate([ctx[0], x[0]], axis=0)
    Ps, saves = [], []
    for l in range(LY):
        sinkrows = jnp.repeat(attn_sink[l].reshape(d.NKV, d.G), ATT_BLOCK, axis=1)[:, :, None]
        Ps.append(dict(norm_g=norm_g_f[l], b_merge=b_merge_f[l], rnn_conv_w=conv_w_f[l], rnn_conv_b=rnn_conv_b[l][None],
                       lru_w_a=lru_w_a[l], lru_b_a=lru_b_a_f[l], lru_w_x=lru_w_x[l], lru_b_x=lru_b_x_f[l], lru_lambda=lru_lam_f[l],
                       sc_conv_w=sc_w_f[l], sinkrows=sinkrows))
    for l in range(LY):
        P = Ps[l]
        h, s1 = _ffn_fwd(h, modv[l], P["norm_g"][0:1], ("w13a", l), ("w2a", l), 0, d, sc, f"F{l}a.")
        h, s2 = _mixer_fwd(h, modv[l], P, l, tabs, d, sc, f"F{l}m.")
        h, s3 = _ffn_fwd(h, modv[l], P["norm_g"][2:3], ("w13b", l), ("w2b", l), 6, d, sc, f"F{l}b.")
        saves.append((s1, s2, s3))

    fng = final_norm_g[None]

    def loss_fn(t):
        def f(hv, g):
            y = hv * lax.rsqrt(jnp.mean(hv * hv, axis=-1, keepdims=True) + EPS) * g
            e = y - t.R[1]
            return 0.5 * jnp.sum(jnp.mean(e * e, axis=-1))
        lat = (t.seg == 1).astype(f32)
        val, (dhv, dg) = jax.value_and_grad(f, argnums=(0, 1))(t.R[0], t.K[0])
        return [dhv * lat], [dg * lat, jnp.full((1, 128), val * lat, f32)], []
    dh, dfng, lossv = _rowwise("loss", loss_fn, d.T, C, d.tr, rows=[(h, D, 0, 0), (loss_target[0], D, 0, -(C // d.tr))],
                               consts=[fng], outs=[(D, f32)], accs=[(1, D), (1, 128)])
    loss = lax.psum(lossv[0, 0], AXES)

    gl = [None] * LY
    for l in reversed(range(LY)):
        P = Ps[l]
        s1, s2, s3 = saves[l]
        dh, dg2, dss2, dgt2 = _ffn_bwd(dh, s3, modv[l], P["norm_g"][2:3], ("w13b", l), ("w2b", l), 6, d, sc, f"B{l}b.")
        dh, gm = _mixer_bwd(dh, s2, modv[l], P, l, tabs, d, sc, f"B{l}m.")
        dh, dg0, dss0, dgt0 = _ffn_bwd(dh, s1, modv[l], P["norm_g"][0:1], ("w13a", l), ("w2a", l), 0, d, sc, f"B{l}a.")
        gm.update(norm_g=jnp.concatenate([dg0, gm["norm_g1"], dg2], axis=0),
                  dmod=jnp.concatenate([dss0, dgt0, gm["dss"], gm["dgate"], dss2, dgt2], axis=1))
        gl[l] = gm
    grad_x = dh[C:][None]
    st = lambda k: jnp.stack([g[k] for g in gl])

    last = _Comm(env)
    add_rs(last, "w13a", 0)
    _exchange("exchange_last", last)
    r13a, r2a, rin, rwb, rwo, r13b, r2b = [env[("r", k)] for k in ("w13a", "w2a", "wint", "wbt", "wout", "w13b", "w2b")]
    big = {}
    def adamw_t(r, w, m, v):
        e = lambda a: a.reshape(a.shape[:1] + (1,) + a.shape[1:])
        return [o.reshape(w.shape) for o in _adamw_transposed(r.reshape(r.shape[:2] + (1,) + r.shape[2:]), e(w), e(m), e(v))]
    big["ffn1_w13"] = adamw_t(r13a, ffn1_w13, m_ffn1_w13, v_ffn1_w13)
    big["ffn2_w13"] = adamw_t(r13b, ffn2_w13, m_ffn2_w13, v_ffn2_w13)
    big["w_in"] = adamw_t(rin, w_in, m_w_in, v_w_in)
    big["w_branch"] = _adamw_transposed(rwb, w_branch, m_w_branch, v_w_branch)
    big["ffn1_w2"] = _adamw_direct(r2a, ffn1_w2, m_ffn1_w2, v_ffn1_w2, True)
    big["ffn2_w2"] = _adamw_direct(r2b, ffn2_w2, m_ffn2_w2, v_ffn2_w2, True)
    big["w_out"] = _adamw_direct(rwo, w_out, m_w_out, v_w_out, True)

    sink_g = jnp.stack([jnp.sum(g["sinkrows"].reshape(d.NKV, d.G, ATT_BLOCK), axis=-1).reshape(d.NH) for g in gl])
    small_full = dict(norm_g=st("norm_g"), b_merge=st("b_merge"), rnn_conv_w=st("rnn_conv_w"), rnn_conv_b=st("rnn_conv_b")[:, 0],
                      lru_w_a=st("lru_w_a"), lru_b_a=st("lru_b_a"), lru_w_x=st("lru_w_x"), lru_b_x=st("lru_b_x"),
                      lru_lambda=st("lru_lambda"), sc_conv_w=st("sc_conv_w"), attn_sink=sink_g, final_norm_g=dfng[0])
    names_s = list(small_full)
    dmod = st("dmod")
    pk = _pack([small_full[k] for k in names_s] + [dmod])
    g4 = _allgather("gather_small_grads", pk)
    tot = _unpack(_sum8(g4), [small_full[k].shape for k in names_s] + [dmod.shape])
    sums = dict(zip(names_s, tot[:-1]))
    dmod_sum = tot[-1].reshape(LY, 2, N_MOD * D)
    dmod_all = _unpack(g4, [small_full[k].shape for k in names_s] + [dmod.shape], (N_DEV,))[-1].reshape(N_DEV, LY, 2, N_MOD * D)
    dm_rows = jnp.concatenate([jnp.moveaxis(dmod_all[:, :, 1], 0, 1), dmod_sum[:, 0:1], jnp.zeros((LY, 7, N_MOD * D), f32)], axis=1)
    dm_cols = lax.dynamic_slice_in_dim(dm_rows, me * cols9, cols9, axis=2)
    g_ada_w, dcv = _ada_bwd(cvec, ada_w, dm_cols)
    g5 = _allgather("gather_cctx", _pack([dcv[N_DEV]]))
    g_c_ctx = _sum8(g5).reshape(-1)[:D]
    g_ada_b = dmod_sum[:, 0] + dmod_sum[:, 1]
    ada_out = _adamw_direct(g_ada_w, ada_w, m_ada_w, v_ada_w, False)

    def shard_last(a, n):
        return lax.dynamic_slice_in_dim(a, me * n, n, axis=a.ndim - 1)
    local_g = dict(c_ctx=g_c_ctx, ada_b=g_ada_b, norm_g=shard_last(sums["norm_g"], shd), b_merge=shard_last(sums["b_merge"], shd),
                   rnn_conv_w=shard_last(sums["rnn_conv_w"], d.BW // N_DEV), rnn_conv_b=sums["rnn_conv_b"], lru_w_a=sums["lru_w_a"],
                   lru_b_a=shard_last(sums["lru_b_a"], d.BW // N_DEV), lru_w_x=sums["lru_w_x"], lru_b_x=shard_last(sums["lru_b_x"], d.BW // N_DEV),
                   lru_lambda=shard_last(sums["lru_lambda"], d.BW // N_DEV), sc_conv_w=shard_last(sums["sc_conv_w"], d.BW // N_DEV),
                   attn_sink=sums["attn_sink"], final_norm_g=sums["final_norm_g"])
    wmv = dict(c_ctx=(c_ctx, m_c_ctx, v_c_ctx), ada_b=(ada_b, m_ada_b, v_ada_b), norm_g=(norm_g, m_norm_g, v_norm_g),
               b_merge=(b_merge, m_b_merge, v_b_merge), rnn_conv_w=(rnn_conv_w, m_rnn_conv_w, v_rnn_conv_w),
               rnn_conv_b=(rnn_conv_b, m_rnn_conv_b, v_rnn_conv_b), lru_w_a=(lru_w_a, m_lru_w_a, v_lru_w_a),
               lru_b_a=(lru_b_a, m_lru_b_a, v_lru_b_a), lru_w_x=(lru_w_x, m_lru_w_x, v_lru_w_x), lru_b_x=(lru_b_x, m_lru_b_x, v_lru_b_x),
               lru_lambda=(lru_lambda, m_lru_lambda, v_lru_lambda), sc_conv_w=(sc_conv_w, m_sc_conv_w, v_sc_conv_w),
               attn_sink=(attn_sink, m_attn_sink, v_attn_sink), final_norm_g=(final_norm_g, m_final_norm_g, v_final_norm_g))
    names_l = list(local_g)
    shapes_l = [wmv[k][0].shape for k in names_l]
    gp = _pack([local_g[k].reshape(wmv[k][0].shape) for k in names_l])
    outs_s = _flat_adamw(gp, _pack([wmv[k][0] for k in names_l]), _pack([wmv[k][1] for k in names_l]), _pack([wmv[k][2] for k in names_l]))
    small = {k: [local_g[k].reshape(wmv[k][0].shape)] for k in names_l}
    for o in outs_s:
        for k, a in zip(names_l, _unpack(o, shapes_l)):
            small[k].append(a)

    order = ["c_ctx", "ada_w", "ada_b", "norm_g", "ffn1_w13", "ffn1_w2", "w_in", "b_merge", "rnn_conv_w", "rnn_conv_b", "lru_w_a",
             "lru_b_a", "lru_w_x", "lru_b_x", "lru_lambda", "sc_conv_w", "attn_sink", "w_branch", "w_out", "ffn2_w13", "ffn2_w2",
             "final_norm_g"]
    allo = dict(small)
    allo.update(big)
    allo["ada_w"] = ada_out
    res = [loss, grad_x]
    for q in range(4):
        res += [allo[k][q] for k in order]
    return tuple(res)
```

```python
import functools
import math

import jax
import jax.numpy as jnp
from jax import lax
from jax.experimental import pallas as pl
from jax.experimental.pallas import tpu as pltpu

f32 = jnp.float32
bf16 = jnp.bfloat16
SDS = jax.ShapeDtypeStruct

N_DEV = 8
AXES = ("x", "y", "c")
HEAD_DIM = 128
GRID_W = 64
ATT_BLOCK = 128
HALO = 16
ROPE_BASE = 10000.0
LRU_C = 8.0
EPS = 1e-6
NEG_INF = -1e30
N_MOD = 9
N_BRANCH = 3
ADAM_LR, ADAM_B1, ADAM_B2, ADAM_EPS, ADAM_WD, ADAM_STEP = 0.001, 0.9, 0.999, 1e-08, 0.01, 10
VMEM_LIMIT = 56 * 1024 * 1024
ANY = pl.BlockSpec(memory_space=pl.ANY)


def _round_up(n, m):
    return (n + m - 1) // m * m


def _tile(n, target, align):
    best = None
    t = align
    while t <= min(n, target):
        if n % t == 0:
            best = t
        t += align
    return best if best is not None else n


def _params(sem):
    return pltpu.CompilerParams(dimension_semantics=sem, vmem_limit_bytes=VMEM_LIMIT)


_DIMS = {"nn": (((1,), (0,)), ((), ())), "nt": (((1,), (1,)), ((), ())), "tn": (((0,), (0,)), ((), ()))}


def _coords():
    return lax.axis_index("x"), lax.axis_index("y"), lax.axis_index("c")


def _peer(xyc, r):
    x, y, c = xyc
    return (1 - x if r & 4 else x, 1 - y if r & 2 else y, 1 - c if r & 1 else c)


def _index(xyc):
    return xyc[0] * 4 + xyc[1] * 2 + xyc[2]


class _Comm:
    def __init__(self, env):
        self.env, self.srcs, self.dsts, self.keys, self.items, self.fills = env, [], [], [], [], []

    def _src(self, a):
        for q, b in enumerate(self.srcs):
            if b is a:
                return q
        self.srcs.append(a)
        return len(self.srcs) - 1

    def _dst(self, key):
        if key not in self.keys:
            self.keys.append(key)
            self.dsts.append(self.env[key])
        return self.keys.index(key)

    def add(self, src, key, src_view, dst_view):
        self.items.append((self._src(src), self._dst(key), src_view, dst_view))

    def fill(self, src, key, dst_view):
        self.fills.append((self._src(src), self._dst(key), dst_view))

    def scratch(self):
        n = len(self.items)
        return [pltpu.SemaphoreType.DMA((7 * n,)), pltpu.SemaphoreType.DMA((7 * n,)), pltpu.SemaphoreType.DMA((n + len(self.fills),))]

    def store(self, arrays):
        for key, a in zip(self.keys, arrays):
            self.env[key] = a

    def descriptors(self, srcs, dsts, sems, me3):
        send, recv, lsem = sems
        me = _index(me3)
        local, sends, recvs = [], [], []
        for q, (si, di, sv, dv) in enumerate(self.items):
            local.append(pltpu.make_async_copy(sv(srcs[si], me, me, 0), dv(dsts[di], me, 0), lsem.at[q]))
            for r in range(1, N_DEV):
                p3 = _peer(me3, r)
                pi = _index(p3)
                kw = dict(send_sem=send.at[q * 7 + r - 1], recv_sem=recv.at[q * 7 + r - 1], device_id=p3,
                          device_id_type=pl.DeviceIdType.MESH)
                sends.append(pltpu.make_async_remote_copy(src_ref=sv(srcs[si], me, pi, r), dst_ref=dv(dsts[di], me, r), **kw))
                recvs.append(pltpu.make_async_remote_copy(src_ref=sv(srcs[si], me, pi, r), dst_ref=dv(dsts[di], pi, r), **kw))
        for q, (si, di, dv) in enumerate(self.fills):
            local.append(pltpu.make_async_copy(srcs[si], dv(dsts[di]), lsem.at[len(self.items) + q]))
        return local, sends, recvs


def _comm_start(descs):
    local, sends, _ = descs
    for cp in local + sends:
        cp.start()


def _comm_wait(descs):
    local, sends, recvs = descs
    for cp in recvs:
        cp.wait_recv()
    for cp in sends:
        cp.wait_send()
    for cp in local:
        cp.wait()


def _carry_call(core, name, grid, in_specs, out_specs, out_shape, scratch, args, comm):
    ni, no, nsc, ng = len(in_specs), len(out_specs), len(scratch), len(grid)
    if comm is None:
        def plain(*refs):
            core(refs[:ni], refs[ni:ni + no], refs[ni + no:])
        return pl.pallas_call(plain, name=name, grid=grid, in_specs=in_specs, out_specs=out_specs, out_shape=out_shape,
                              scratch_shapes=scratch, compiler_params=_params(("parallel",) * (ng - 1) + ("arbitrary",)))(*args)
    ns, nd = len(comm.srcs), len(comm.dsts)
    o0 = ni + ns + nd

    def body(*refs):
        pid = [pl.program_id(q) for q in range(ng)]
        first, last = pid[0] == 0, pid[0] == grid[0] - 1
        for q in range(1, ng):
            first = jnp.logical_and(first, pid[q] == 0)
            last = jnp.logical_and(last, pid[q] == grid[q] - 1)
        me3 = _coords()
        mk = lambda: comm.descriptors(refs[ni:ni + ns], refs[o0 + no:o0 + no + nd], refs[o0 + no + nd + nsc:], me3)

        @pl.when(first)
        def _():
            _comm_start(mk())

        core(refs[:ni], refs[o0:o0 + no], refs[o0 + no + nd:o0 + no + nd + nsc])

        @pl.when(last)
        def _():
            _comm_wait(mk())

    res = pl.pallas_call(
        body, name=name, grid=grid, in_specs=list(in_specs) + [ANY] * (ns + nd), out_specs=list(out_specs) + [ANY] * nd,
        out_shape=list(out_shape) + [SDS(x.shape, x.dtype) for x in comm.dsts], scratch_shapes=list(scratch) + comm.scratch(),
        input_output_aliases={ni + ns + q: no + q for q in range(nd)}, compiler_params=_params(("arbitrary",) * ng),
    )(*args, *comm.srcs, *comm.dsts)
    comm.store(res[no:])
    return res[:no]


def _mm(a, b, mode, out_dtype, name, *, M=None, N=None, K=None, a_off=0, b_off=0, b_lead=None,
        tm=768, tn=512, tk=2048, comm=None, a3=False):
    b2 = b.shape[1:] if b_lead is not None else b.shape
    if mode == "tn":
        K = a.shape[-2] if K is None else K
        M = (a.shape[0] * a.shape[2] if a3 else a.shape[1]) if M is None else M
        N = b2[1] if N is None else N
    else:
        M = a.shape[-2] if M is None else M
        K = (a.shape[0] * a.shape[2] if a3 else a.shape[1]) if K is None else K
        N = (b2[1] if mode == "nn" else b2[0]) if N is None else N
    g = math.gcd
    if mode == "tn":
        tm = _tile(g(M, a_off) if a_off else (a.shape[2] if a3 else M), tm, 128)
        tk = _tile(K, tk, 16)
    else:
        tm = _tile(M, tm, 16)
        tk = _tile(g(K, a_off) if a_off else (a.shape[2] if a3 else K), tk, 128)
    tn = _tile(g(N, b_off) if b_off else N, tn, 128)
    nk = K // tk
    ao, bo = (a_off // (tm if mode == "tn" else tk)), b_off // tn
    lead = () if b_lead is None else (b_lead,)
    ld = () if b_lead is None else (None,)
    if mode == "nn":
        a_spec = pl.BlockSpec((tm, tk), lambda i, j, k: (i, ao + k))
        b_spec = pl.BlockSpec(ld + (tk, tn), lambda i, j, k: lead + (k, bo + j))
    elif mode == "nt":
        a_spec = pl.BlockSpec((tm, tk), lambda i, j, k: (i, ao + k))
        b_spec = pl.BlockSpec(ld + (tn, tk), lambda i, j, k: lead + (bo + j, k))
    else:
        a_spec = pl.BlockSpec((tk, tm), lambda i, j, k: (k, ao + i))
        b_spec = pl.BlockSpec(ld + (tk, tn), lambda i, j, k: lead + (k, bo + j))
    if a3:
        if mode == "nn":
            nkc = a.shape[2] // tk
            a_spec = pl.BlockSpec((None, tm, tk), lambda i, j, k: (k // nkc, i, k % nkc))
        else:
            nmc = a.shape[2] // tm
            a_spec = pl.BlockSpec((None, tk, tm), lambda i, j, k: (i // nmc, k, i % nmc))
    dims = _DIMS[mode]

    def core(ins, outs, scr):
        a_ref, b_ref = ins
        o_ref = outs[0]
        k = pl.program_id(2)
        if nk == 1:
            o_ref[...] = lax.dot_general(a_ref[...], b_ref[...], dims, preferred_element_type=f32).astype(out_dtype)
        else:
            acc = scr[0]

            @pl.when(k == 0)
            def _():
                acc[...] = jnp.zeros_like(acc)

            acc[...] += lax.dot_general(a_ref[...], b_ref[...], dims, preferred_element_type=f32)

            @pl.when(k == nk - 1)
            def _():
                o_ref[...] = acc[...].astype(out_dtype)

    return _carry_call(core, name, (M // tm, N // tn, nk), [a_spec, b_spec], [pl.BlockSpec((tm, tn), lambda i, j, k: (i, j))],
                       [SDS((M, N), out_dtype)], [] if nk == 1 else [pltpu.VMEM((tm, tn), f32)], (a, b), comm)[0]


class _Tile:
    pass


def _rowwise(name, fn, T, C, tr, rows=(), halos=(), consts=(), mods=(), outs=(), accs=(), maccs=()):
    nT, nC = T // tr, C // tr
    assert T % tr == 0 and C % tr == 0 and nC >= 1 and tr % HALO == 0
    r8, n8 = tr // HALO, T // HALO
    seg_of = lambda i: jnp.where(i >= nC, 1, 0)
    in_specs, args = [], []
    for arr, w, cb, ro in rows:
        nt = arr.shape[0] // tr
        in_specs.append(pl.BlockSpec((tr, w), (lambda i, cb=cb, ro=ro, nt=nt: (jnp.clip(i + ro, 0, nt - 1), cb))))
        args.append(arr)
    for arr, w, cb in halos:
        in_specs.append(pl.BlockSpec((HALO, w), lambda i, cb=cb: (jnp.maximum(i * r8 - 1, 0), cb)))
        in_specs.append(pl.BlockSpec((tr, w), lambda i, cb=cb: (i, cb)))
        in_specs.append(pl.BlockSpec((HALO, w), lambda i, cb=cb: (jnp.minimum((i + 1) * r8, n8 - 1), cb)))
        args += [arr, arr, arr]
    for arr in consts:
        in_specs.append(pl.BlockSpec(arr.shape, lambda i, nd=arr.ndim: (0,) * nd))
        args.append(arr)
    for arr in mods:
        in_specs.append(pl.BlockSpec((None,) + arr.shape[1:], lambda i, nd=arr.ndim: (seg_of(i),) + (0,) * (nd - 1)))
        args.append(arr)
    out_specs, out_shape = [], []
    for w, dt in outs:
        out_specs.append(pl.BlockSpec((tr, w), lambda i: (i, 0)))
        out_shape.append(SDS((T, w), dt))
    for shp in accs:
        out_specs.append(pl.BlockSpec(shp, lambda i, nd=len(shp): (0,) * nd))
        out_shape.append(SDS(shp, f32))
    for shp in maccs:
        out_specs.append(pl.BlockSpec((None,) + shp, lambda i, nd=len(shp): (seg_of(i),) + (0,) * nd))
        out_shape.append(SDS((2,) + shp, f32))
    nr, nh, nk, nm, no, na, nma = len(rows), len(halos), len(consts), len(mods), len(outs), len(accs), len(maccs)

    def body(*refs):
        i = pl.program_id(0)
        t = _Tile()
        t.i, t.seg = i, seg_of(i)
        p = 0
        t.R = [refs[p + k][...].astype(f32) for k in range(nr)]
        p += nr
        pvalid = jnp.logical_and(i != 0, i != nC)
        nvalid = jnp.logical_and(i != nC - 1, i != nT - 1)
        t.H = []
        for k in range(nh):
            pr, cu, nx = refs[p][...].astype(f32), refs[p + 1][...].astype(f32), refs[p + 2][...].astype(f32)
            p += 3
            pr = jnp.where(pvalid, pr, jnp.zeros_like(pr))
            nx = jnp.where(nvalid, nx, jnp.zeros_like(nx))
            t.H.append(jnp.concatenate([pr, cu, nx], axis=0))
        t.K = [refs[p + k][...] for k in range(nk)]
        p += nk
        t.M = [refs[p + k][...] for k in range(nm)]
        p += nm
        o, a, ma = fn(t)
        for k in range(no):
            ref = refs[p + k]
            pieces = o[k] if isinstance(o[k], (list, tuple)) else [o[k]]
            c0 = 0
            for pc in pieces:
                ref[:, c0:c0 + pc.shape[1]] = pc.astype(ref.dtype)
                c0 += pc.shape[1]
        p += no
        for k in range(na):
            ref = refs[p + k]

            @pl.when(i == 0)
            def _(ref=ref):
                ref[...] = jnp.zeros_like(ref)

            ref[...] += a[k]
        p += na
        for k in range(nma):
            ref = refs[p + k]

            @pl.when(jnp.logical_or(i == 0, i == nC))
            def _(ref=ref):
                ref[...] = jnp.zeros_like(ref)

            ref[...] += ma[k]

    res = pl.pallas_call(
        body, name=name, grid=(nT,), in_specs=in_specs, out_specs=out_specs, out_shape=out_shape,
        compiler_params=_params(("arbitrary",)),
    )(*args)
    return res


@jax.custom_vjp
def _bdot(a, b):
    return jnp.dot(a.astype(bf16), b.astype(bf16), preferred_element_type=f32)


def _bdot_fwd(a, b):
    return _bdot(a, b), (a, b)


def _bdot_bwd(res, ct):
    a, b = res
    ctb = ct.astype(bf16)
    da = lax.dot_general(ctb, b.astype(bf16), _DIMS["nt"], preferred_element_type=f32)
    db = lax.dot_general(a.astype(bf16), ctb, _DIMS["tn"], preferred_element_type=f32)
    return da, db


_bdot.defvjp(_bdot_fwd, _bdot_bwd)


def _rms_mod(h, g, shift, scale):
    y = h * lax.rsqrt(jnp.mean(h * h, axis=-1, keepdims=True) + EPS) * g
    return y * (1.0 + scale) + shift


def _norm_mod_fwd(h, g, modv, s, d):
    def fn(t):
        m = t.M[0]
        return [_rms_mod(t.R[0], t.K[0], m[s:s + 1], m[s + 1:s + 2])], [], []
    return _rowwise("norm_mod", fn, d.T, d.C, d.tr, rows=[(h, d.D, 0, 0)], consts=[g], mods=[modv],
                    outs=[(d.D, bf16)])[0]


def _norm_mod_bwd(h, du, dh, g, modv, s, d):
    def fn(t):
        m = t.M[0]
        _, vjp = jax.vjp(_rms_mod, t.R[0], t.K[0], m[s:s + 1], m[s + 1:s + 2])
        dx, dg, dsh, dsc = vjp(t.R[1])
        return [t.R[2] + dx], [dg], [jnp.concatenate([dsh, dsc], axis=0)]
    return _rowwise("norm_mod_bwd", fn, d.T, d.C, d.tr, rows=[(h, d.D, 0, 0), (du, d.D, 0, 0), (dh, d.D, 0, 0)],
                    consts=[g], mods=[modv], outs=[(d.D, f32)], accs=[(1, d.D)], maccs=[(2, d.D)])


def _swiglu(g, u):
    return jax.nn.silu(g) * u


def _resid_fwd(h, o, modv, gi, coef, d):
    def fn(t):
        return [t.R[0] + coef * t.M[0][gi:gi + 1] * t.R[1]], [], []
    return _rowwise("resid", fn, d.T, d.C, d.tr, rows=[(h, d.D, 0, 0), (o, d.D, 0, 0)], mods=[modv], outs=[(d.D, f32)])[0]


def _resid_bwd(dh, o, modv, gi, coef, d):
    def fn(t):
        dhv = t.R[0]
        return [coef * t.M[0][gi:gi + 1] * dhv], [], [jnp.sum(coef * dhv * t.R[1], axis=0, keepdims=True)]
    return _rowwise("resid_bwd", fn, d.T, d.C, d.tr, rows=[(dh, d.D, 0, 0), (o, d.D, 0, 0)], mods=[modv],
                    outs=[(d.D, bf16)], maccs=[(1, d.D)])


class _Sched:
    def __init__(self, plan, env, builders):
        self.plan, self.env, self.builders = plan, env, builders

    def at(self, site):
        specs = self.plan.get(site)
        if not specs:
            return None
        cm = _Comm(self.env)
        for kind, *args in specs:
            self.builders[kind](cm, *args)
        return cm


def _ffn_up(u, w13t, d, comm):
    T, D, Fp = d.T, d.D, d.Fp
    tm, tn = _tile(T, 768, 16), _tile(Fp, 512, 128)

    def core(ins, outs, scr):
        uv, w_ref = ins[0][...], ins[1]
        gt = lax.dot_general(uv, w_ref[0], _DIMS["nt"], preferred_element_type=f32)
        up = lax.dot_general(uv, w_ref[1], _DIMS["nt"], preferred_element_type=f32)
        outs[0][0] = gt.astype(bf16)
        outs[0][1] = up.astype(bf16)
        outs[1][...] = _swiglu(gt, up).astype(bf16)

    return _carry_call(core, "ffn_up", (T // tm, Fp // tn),
                       [pl.BlockSpec((tm, D), lambda i, j: (i, 0)), pl.BlockSpec((2, tn, D), lambda i, j: (0, j, 0))],
                       [pl.BlockSpec((2, tm, tn), lambda i, j: (0, i, j)), pl.BlockSpec((tm, tn), lambda i, j: (i, j))],
                       [SDS((2, T, Fp), bf16), SDS((T, Fp), bf16)], [], (u, w13t.reshape(2, Fp, D)), comm)


def _ffn_down_dx(do, w2, gu, d, comm):
    T, D, Fp = d.T, d.D, d.Fp
    tm, tn = _tile(T, 768, 16), _tile(Fp, 512, 128)

    def core(ins, outs, scr):
        dhm = lax.dot_general(ins[0][...], ins[1][...], _DIMS["nt"], preferred_element_type=f32)
        _, vjp = jax.vjp(_swiglu, ins[2][0].astype(f32), ins[2][1].astype(f32))
        dg, du = vjp(dhm)
        outs[0][0] = dg.astype(bf16)
        outs[0][1] = du.astype(bf16)

    return _carry_call(core, "ffn_down_dx", (T // tm, Fp // tn),
                       [pl.BlockSpec((tm, D), lambda i, j: (i, 0)), pl.BlockSpec((tn, D), lambda i, j: (j, 0)),
                        pl.BlockSpec((2, tm, tn), lambda i, j: (0, i, j))],
                       [pl.BlockSpec((2, tm, tn), lambda i, j: (0, i, j))], [SDS((2, T, Fp), bf16)], [], (do, w2, gu), comm)[0]


def _ffn_fwd(h, modv, g, k13, k2, s, d, sc, site):
    u = _norm_mod_fwd(h, g, modv, s, d)
    gu, hmid = _ffn_up(u, sc.env[k13], d, sc.at(site + "up"))
    o = _mm(hmid, sc.env[k2], "nn", f32, "ffn_down", tn=1024, tk=2816, comm=sc.at(site + "down"))
    hn = _resid_fwd(h, o, modv, s + 2, 0.5, d)
    return hn, dict(h=h, u=u, gu=gu, hmid=hmid, o=o)


def _ffn_bwd(dh, sv, modv, g, k13, k2, s, d, sc, site):
    w13t, w2 = sc.env[k13], sc.env[k2]
    do, dgate = _resid_bwd(dh, sv["o"], modv, s + 2, 0.5, d)
    dgu = _ffn_down_dx(do, w2, sv["gu"], d, sc.at(site + "dx"))
    sc.env[("g",) + k13] = _mm(dgu, sv["u"], "tn", bf16, "ffn_up_dw", tm=2816, tn=1024, tk=768, comm=sc.at(site + "uw"), a3=True)
    sc.env[("g",) + k2] = _mm(sv["hmid"], do, "tn", bf16, "ffn_down_dw", tm=2816, tn=1024, tk=768, comm=sc.at(site + "dw"))
    du = _mm(dgu, w13t, "nn", f32, "ffn_up_dx", tn=1024, tk=2816, comm=sc.at(site + "ux"), a3=True)
    dhn, dg, dss = _norm_mod_bwd(sv["h"], du, dh, g, modv, s, d)
    return dhn, dg, dss, dgate


def _taps(ext, w, left, tr, sign):
    acc = None
    for k in range(w.shape[0]):
        o = HALO + sign * (k - left)
        term = w[k:k + 1] * ext[o:o + tr]
        acc = term if acc is None else acc + term
    return acc


def _gates(xa, wa, ba, wx, bx, lam, nb):
    bs = xa.shape[1] // nb
    out = []
    for dr in range(2):
        pa = jnp.concatenate([_bdot(xa[:, n * bs:(n + 1) * bs], wa[dr, n]) for n in range(nb)], axis=1)
        px = jnp.concatenate([_bdot(xa[:, n * bs:(n + 1) * bs], wx[dr, n]) for n in range(nb)], axis=1)
        rg = jax.nn.sigmoid(pa + ba[dr:dr + 1])
        ig = jax.nn.sigmoid(px + bx[dr:dr + 1])
        log_a = -LRU_C * rg * jax.nn.softplus(-lam[dr:dr + 1])
        a = jnp.exp(log_a)
        u = jnp.sqrt(1.0 - jnp.exp(2.0 * log_a)) * (ig * xa)
        out += [a, u]
    return out


def _combine_a(hf, hb, rg):
    return (hf + hb) * jax.nn.gelu(rg)


def _scan_order(kind, nT, nC):
    nL = nT - nC
    if kind == "F":
        return (lambda s: s), False
    if kind == "revF":
        return (lambda s: nT - 1 - s), True
    if kind == "B":
        return (lambda s: jnp.where(s < nC, nC - 1 - s, nT - 1 - (s - nC))), True
    return (lambda s: jnp.where(s < nL, nC + s, s - nL)), False


def _scan_fwd(au, dr, d):
    R, tc = d.BW, d.tr
    tile_of, down = _scan_order("F" if dr == 0 else "B", d.T // tc, d.C // tc)

    def body(a_ref, u_ref, h_ref, hp_ref, st):
        @pl.when(pl.program_id(0) == 0)
        def _():
            st[...] = jnp.zeros_like(st)

        def grp(gi, h):
            r = pl.multiple_of((tc // 8 - 1 - gi if down else gi) * 8, 8)
            at, ut = a_ref[pl.ds(r, 8), :], u_ref[pl.ds(r, 8), :]
            hs, hps = [None] * 8, [None] * 8
            for k in (range(7, -1, -1) if down else range(8)):
                hps[k] = h
                h = at[k:k + 1] * h + ut[k:k + 1]
                hs[k] = h
            h_ref[pl.ds(r, 8), :] = jnp.concatenate(hs, axis=0)
            hp_ref[pl.ds(r, 8), :] = jnp.concatenate(hps, axis=0)
            return h
        st[...] = lax.fori_loop(0, tc // 8, grp, st[...])

    return pl.pallas_call(
        body, name="lru_scan", grid=(d.T // tc,),
        in_specs=[pl.BlockSpec((tc, R), lambda s: (tile_of(s), 2 * dr)), pl.BlockSpec((tc, R), lambda s: (tile_of(s), 2 * dr + 1))],
        out_specs=[pl.BlockSpec((tc, R), lambda s: (tile_of(s), 0))] * 2, out_shape=[SDS((d.T, R), f32)] * 2,
        scratch_shapes=[pltpu.VMEM((1, R), f32)], compiler_params=_params(("arbitrary",)),
    )(au, au)


def _scan_bwd(dh, au, hp, dr, d):
    R, tc = d.BW, d.tr
    tile_of, down = _scan_order("revF" if dr == 0 else "revB", d.T // tc, d.C // tc)

    def body(dh_ref, a_ref, hp_ref, o_ref, st):
        @pl.when(pl.program_id(0) == 0)
        def _():
            st[...] = jnp.zeros_like(st)

        def grp(gi, carry):
            lam, an = carry
            r = pl.multiple_of((tc // 8 - 1 - gi if down else gi) * 8, 8)
            at, dt, ht = a_ref[pl.ds(r, 8), :], dh_ref[pl.ds(r, 8), :], hp_ref[pl.ds(r, 8), :]
            ls = [None] * 8
            for k in (range(7, -1, -1) if down else range(8)):
                lam = dt[k:k + 1] + an * lam
                an = at[k:k + 1]
                ls[k] = lam
            lt = jnp.concatenate(ls, axis=0)
            o_ref[pl.ds(r, 8), 0:R] = lt * ht
            o_ref[pl.ds(r, 8), R:2 * R] = lt
            return lam, an
        lam, an = lax.fori_loop(0, tc // 8, grp, (st[0:1], st[1:2]))
        st[0:1] = lam
        st[1:2] = an

    return pl.pallas_call(
        body, name="lru_scan_bwd", grid=(d.T // tc,),
        in_specs=[pl.BlockSpec((tc, R), lambda s: (tile_of(s), 0)), pl.BlockSpec((tc, R), lambda s: (tile_of(s), 2 * dr)),
                  pl.BlockSpec((tc, R), lambda s: (tile_of(s), 0))],
        out_specs=pl.BlockSpec((tc, 2 * R), lambda s: (tile_of(s), 0)), out_shape=SDS((d.T, 2 * R), f32),
        scratch_shapes=[pltpu.VMEM((2, R), f32)], compiler_params=_params(("arbitrary",)),
    )(dh, au, hp)


def _swap_pairs(x):
    w = x.shape[1]
    lane = lax.broadcasted_iota(jnp.int32, x.shape, 1)
    return jnp.where(lane % 64 < 32, pltpu.roll(x, w - 32, 1), pltpu.roll(x, 32, 1))


def _att_masks(blk, nB, nCb, G):
    rows, cols = G * ATT_BLOCK, 3 * ATT_BLOCK
    qi = lax.broadcasted_iota(jnp.int32, (rows, cols), 0) % ATT_BLOCK
    kj = lax.broadcasted_iota(jnp.int32, (rows, cols), 1)
    rel = kj - ATT_BLOCK - qi
    kb = kj // ATT_BLOCK
    one = jnp.int32(1)
    latent = jnp.where(blk >= nCb, one, 0)
    prev_ok = jnp.where(blk - 1 >= nCb, latent, 0)
    next_ok = jnp.where(blk + 1 <= nB - 1, latent, 0)
    bv = jnp.where(kb == 0, prev_ok, jnp.where(kb == 1, latent, next_ok))
    return jnp.logical_and(jnp.abs(rel) <= ATT_BLOCK, bv > 0)


def _att_specs(d):
    G, nB = d.G, d.T // ATT_BLOCK
    blk = lambda f: pl.BlockSpec((ATT_BLOCK, HEAD_DIM), lambda kh, b: (f(b), kh))
    three = [blk(lambda b: jnp.maximum(b - 1, 0)), blk(lambda b: b), blk(lambda b: jnp.minimum(b + 1, nB - 1))]
    ctxs = pl.BlockSpec((d.C, HEAD_DIM), lambda kh, b: (0, kh))
    qs = pl.BlockSpec((ATT_BLOCK, G * HEAD_DIM), lambda kh, b: (b, kh))
    sk = pl.BlockSpec((None, G * ATT_BLOCK, 1), lambda kh, b: (kh, 0, 0))
    return qs, three, ctxs, sk


def _stack_heads(x, G):
    return jnp.concatenate([x[:, g * HEAD_DIM:(g + 1) * HEAD_DIM] for g in range(G)], axis=0)


def _unstack_heads(x, G):
    return jnp.concatenate([x[g * ATT_BLOCK:(g + 1) * ATT_BLOCK] for g in range(G)], axis=1)


def _att_probs(qg, kcat, kctx, sink, valid):
    scale = HEAD_DIM ** -0.5
    s = lax.dot_general(qg, kcat, _DIMS["nt"], preferred_element_type=f32) * scale
    s = jnp.where(valid, s, NEG_INF)
    sc = lax.dot_general(qg, kctx, _DIMS["nt"], preferred_element_type=f32) * scale
    m = jnp.maximum(jnp.maximum(jnp.max(s, axis=1, keepdims=True), jnp.max(sc, axis=1, keepdims=True)), sink)
    e, ec, es = jnp.exp(s - m), jnp.exp(sc - m), jnp.exp(sink - m)
    inv = 1.0 / (jnp.sum(e, axis=1, keepdims=True) + jnp.sum(ec, axis=1, keepdims=True) + es)
    return e * inv, ec * inv, es * inv


def _attn_fwd(qr, kr, vb, sinkrows, d):
    G, nB, nCb = d.G, d.T // ATT_BLOCK, d.C // ATT_BLOCK
    qs, three, ctxs, sk = _att_specs(d)

    def body(q_ref, k0, k1, k2, v0, v1, v2, kc_ref, vc_ref, s_ref, o_ref):
        b = pl.program_id(1)
        qg = _stack_heads(q_ref[...], G)
        kcat = jnp.concatenate([k0[...], k1[...], k2[...]], axis=0)
        vcat = jnp.concatenate([v0[...], v1[...], v2[...]], axis=0)
        p, pc, _ = _att_probs(qg, kcat, kc_ref[...], s_ref[...], _att_masks(b, nB, nCb, G))
        o = jnp.dot(p.astype(bf16), vcat, preferred_element_type=f32) + jnp.dot(pc.astype(bf16), vc_ref[...], preferred_element_type=f32)
        o_ref[...] = _unstack_heads(o, G).astype(bf16)

    return pl.pallas_call(
        body, name="attn", grid=(d.NKV, nB), in_specs=[qs] + three + three + [ctxs, ctxs, sk],
        out_specs=qs, out_shape=SDS((d.T, d.NH * HEAD_DIM), bf16), compiler_params=_params(("parallel", "arbitrary")),
    )(qr, kr, kr, kr, vb, vb, vb, kr, vb, sinkrows)


def _attn_bwd(qr, kr, vb, sinkrows, dy, d):
    G, nB, nCb = d.G, d.T // ATT_BLOCK, d.C // ATT_BLOCK
    qs, three, ctxs, sk = _att_specs(d)
    KW = d.NKV * HEAD_DIM
    part = pl.BlockSpec((ATT_BLOCK, HEAD_DIM), lambda kh, b: (b, kh))

    def body(q_ref, k0, k1, k2, v0, v1, v2, kc_ref, vc_ref, s_ref, dy_ref,
             dq_ref, dk0, dk1, dk2, dv0, dv1, dv2, dkc_ref, dvc_ref, ds_ref):
        b = pl.program_id(1)
        scale = HEAD_DIM ** -0.5
        qg = _stack_heads(q_ref[...], G)
        kcat = jnp.concatenate([k0[...], k1[...], k2[...]], axis=0)
        vcat = jnp.concatenate([v0[...], v1[...], v2[...]], axis=0)
        kctx, vctx = kc_ref[...], vc_ref[...]
        p, pc, ps = _att_probs(qg, kcat, kctx, s_ref[...], _att_masks(b, nB, nCb, G))
        dog = _stack_heads(dy_ref[...], G).astype(bf16)
        dp = lax.dot_general(dog, vcat, _DIMS["nt"], preferred_element_type=f32)
        dpc = lax.dot_general(dog, vctx, _DIMS["nt"], preferred_element_type=f32)
        delta = jnp.sum(p * dp, axis=1, keepdims=True) + jnp.sum(pc * dpc, axis=1, keepdims=True)
        ds = (p * (dp - delta) * scale).astype(bf16)
        dsc = (pc * (dpc - delta) * scale).astype(bf16)
        dq = jnp.dot(ds, kcat, preferred_element_type=f32) + jnp.dot(dsc, kctx, preferred_element_type=f32)
        dq_ref[...] = _unstack_heads(dq, G)
        dk = lax.dot_general(ds, qg, _DIMS["tn"], preferred_element_type=f32)
        dv = lax.dot_general(p.astype(bf16), dog, _DIMS["tn"], preferred_element_type=f32)
        for j, (rk, rv) in enumerate(((dk0, dv0), (dk1, dv1), (dk2, dv2))):
            rk[...] = dk[j * ATT_BLOCK:(j + 1) * ATT_BLOCK]
            rv[...] = dv[j * ATT_BLOCK:(j + 1) * ATT_BLOCK]

        @pl.when(b == 0)
        def _():
            dkc_ref[...] = jnp.zeros_like(dkc_ref)
            dvc_ref[...] = jnp.zeros_like(dvc_ref)
            ds_ref[...] = jnp.zeros_like(ds_ref)

        dkc_ref[...] += lax.dot_general(dsc, qg, _DIMS["tn"], preferred_element_type=f32)
        dvc_ref[...] += lax.dot_general(pc.astype(bf16), dog, _DIMS["tn"], preferred_element_type=f32)
        ds_ref[...] += -ps * delta

    kv = SDS((d.T, KW), f32)
    return pl.pallas_call(
        body, name="attn_bwd", grid=(d.NKV, nB), in_specs=[qs] + three + three + [ctxs, ctxs, sk, qs],
        out_specs=[qs] + [part] * 6 + [ctxs, ctxs, sk],
        out_shape=[SDS((d.T, d.NH * HEAD_DIM), f32)] + [kv] * 6 + [SDS((d.C, KW), f32)] * 2 + [SDS((d.NKV, G * ATT_BLOCK, 1), f32)],
        compiler_params=_params(("parallel", "arbitrary")),
    )(qr, kr, kr, kr, vb, vb, vb, kr, vb, sinkrows, dy)


def _mixer_fwd(h, modv, P, l, tabs, d, sc, site):
    D, BW, T, C = d.D, d.BW, d.T, d.C
    u = _norm_mod_fwd(h, P["norm_g"][1:2], modv, 3, d)
    za = _mm(u, sc.env[("wint", l)], "nt", bf16, "in_proj_a", N=d.NA, comm=sc.at(site + "in_a"))
    gz = _mm(u, sc.env[("wint", l)], "nt", bf16, "in_proj_g", N=3 * D, b_off=d.NA, comm=sc.at(site + "in_g"))
    xa = _rowwise("lru_conv", lambda t: ([_taps(t.H[0], t.K[0], 2, d.tr, 1) + t.K[1]], [], []), T, C, d.tr,
                  halos=[(za, BW, 0)], consts=[P["rnn_conv_w"], P["rnn_conv_b"]], outs=[(BW, f32)])[0]
    lru = [P["lru_w_a"], P["lru_b_a"], P["lru_w_x"], P["lru_b_x"], P["lru_lambda"]]
    au = _rowwise("lru_gates", lambda t: ([_gates(t.R[0], *t.K, d.NB)], [], []), T, C, d.tr,
                  rows=[(xa, BW, 0, 0)], consts=lru, outs=[(4 * BW, f32)])[0]
    hf, hpf = _scan_fwd(au, 0, d)
    hb, hpb = _scan_fwd(au, 1, d)
    ya = _rowwise("lru_out", lambda t: ([_combine_a(*t.R)], [], []), T, C, d.tr,
                  rows=[(hf, BW, 0, 0), (hb, BW, 0, 0), (za, BW, 1, 0)], outs=[(BW, bf16)])[0]
    yb = _rowwise("sconv", lambda t: ([t.R[0] * _taps(t.H[0] * t.H[1], t.K[0], 1, d.tr, 1)], [], []), T, C, d.tr,
                  rows=[(za, BW, 2, 0)], halos=[(za, BW, 3), (za, BW, 4)], consts=[P["sc_conv_w"]], outs=[(BW, bf16)])[0]
    QW, KW = d.NH * HEAD_DIM, d.NKV * HEAD_DIM

    def rope(t):
        q, k, v, cs, sn = t.R
        cq, sq, ck, skn = jnp.tile(cs, (1, d.NH)), jnp.tile(sn, (1, d.NH)), jnp.tile(cs, (1, d.NKV)), jnp.tile(sn, (1, d.NKV))
        return [q * cq + _swap_pairs(q) * sq, k * ck + _swap_pairs(k) * skn, v], [], []
    kcb = (5 * BW + QW) // KW
    qr, kr, vb = _rowwise("rope", rope, T, C, d.tr,
                          rows=[(za, QW, (5 * BW) // QW, 0), (za, KW, kcb, 0), (za, KW, kcb + 1, 0),
                                (tabs["cos"], HEAD_DIM, 0, 0), (tabs["sin"], HEAD_DIM, 0, 0)],
                          outs=[(QW, bf16), (KW, bf16), (KW, bf16)])
    yatt = _attn_fwd(qr, kr, vb, P["sinkrows"], d)
    ys = (ya, yb, yatt)
    ps = [_mm(ys[i], sc.env[("wbt", l)], "nt", bf16, "lift", b_lead=i, tn=1024) for i in range(N_BRANCH)]

    def merge(t):
        gzv, bm = t.R[0], t.K[0]
        acc = None
        for i in range(N_BRANCH):
            term = jax.nn.sigmoid(gzv[:, i * D:(i + 1) * D] + bm[i:i + 1]) * t.R[1 + i]
            acc = term if acc is None else acc + term
        return [acc], [], []
    merged = _rowwise("merge", merge, T, C, d.trw, rows=[(gz, 3 * D, 0, 0)] + [(p, D, 0, 0) for p in ps],
                      consts=[P["b_merge"]], outs=[(D, bf16)])[0]
    y = _mm(merged, sc.env[("wout", l)], "nn", f32, "out_proj", tn=1024, comm=sc.at(site + "out"))
    hn = _resid_fwd(h, y, modv, 5, 1.0, d)
    sv = dict(h=h, u=u, za=za, gz=gz, xa=xa, au=au, hf=hf, hpf=hpf, hb=hb, hpb=hpb, ys=ys, qr=qr, kr=kr, vb=vb,
              ps=ps, merged=merged, y=y)
    return hn, sv


def _mixer_bwd(dh, sv, modv, P, l, tabs, d, sc, site):
    W = dict(wint=sc.env[("wint", l)], wbt=sc.env[("wbt", l)], wout=sc.env[("wout", l)])
    D, BW, T, C = d.D, d.BW, d.T, d.C
    QW, KW = d.NH * HEAD_DIM, d.NKV * HEAD_DIM
    za, gz = sv["za"], sv["gz"]
    dy, dgate = _resid_bwd(dh, sv["y"], modv, 5, 1.0, d)
    dmerged = _mm(dy, W["wout"], "nt", f32, "out_proj_dx", tn=1024)
    sc.env[("g", "wout", l)] = _mm(sv["merged"], dy, "tn", bf16, "out_proj_dw", tm=1024, tn=2048, tk=768)

    def merge_bwd(t):
        gzv, bm, dm = t.R[0], t.K[0], t.R[4]
        dps, dgs, dbs = [], [], []
        for i in range(N_BRANCH):
            gate = jax.nn.sigmoid(gzv[:, i * D:(i + 1) * D] + bm[i:i + 1])
            dps.append(dm * gate)
            dgi = dm * t.R[1 + i] * gate * (1.0 - gate)
            dgs.append(dgi)
            dbs.append(jnp.sum(dgi, axis=0, keepdims=True))
        return [dps, dgs], [jnp.concatenate(dbs, axis=0)], []
    dp, dgz, dbm = _rowwise("merge_bwd", merge_bwd, T, C, d.trw,
                            rows=[(gz, 3 * D, 0, 0)] + [(p, D, 0, 0) for p in sv["ps"]] + [(dmerged, D, 0, 0)],
                            consts=[P["b_merge"]], outs=[(3 * D, bf16), (3 * D, bf16)], accs=[(N_BRANCH, D)])
    dys = [_mm(dp, W["wbt"], "nn", f32, "lift_dx", K=D, a_off=i * D, b_lead=i, tn=1024) for i in range(N_BRANCH)]
    sc.env[("g", "wbt", l)] = jnp.stack([_mm(dp, sv["ys"][i], "tn", bf16, "lift_dw", M=D, a_off=i * D, tm=1024, tn=1024, tk=768)
                                         for i in range(N_BRANCH)])
    def out_bwd(t):
        _, vjp = jax.vjp(_combine_a, t.R[1], t.R[2], t.R[3])
        dhf, _, drg = vjp(t.R[0])
        return [dhf, drg], [], []
    dhs, drg = _rowwise("lru_out_bwd", out_bwd, T, C, d.tr,
                        rows=[(dys[0], BW, 0, 0), (sv["hf"], BW, 0, 0), (sv["hb"], BW, 0, 0), (za, BW, 1, 0)],
                        outs=[(BW, f32), (BW, bf16)])
    dau0 = _scan_bwd(dhs, sv["au"], sv["hpf"], 0, d)
    dau1 = _scan_bwd(dhs, sv["au"], sv["hpb"], 1, d)
    lru = [P["lru_w_a"], P["lru_b_a"], P["lru_w_x"], P["lru_b_x"], P["lru_lambda"]]

    def gates_bwd(t):
        _, vjp = jax.vjp(lambda xa, *k: _gates(xa, *k, d.NB), t.R[0], *t.K)
        d0, d1 = t.R[1], t.R[2]
        g = vjp([d0[:, :BW], d0[:, BW:], d1[:, :BW], d1[:, BW:]])
        return [g[0]], list(g[1:]), []
    dxa, dwa, dba, dwx, dbx, dlam = _rowwise(
        "lru_gates_bwd", gates_bwd, T, C, d.tr, rows=[(sv["xa"], BW, 0, 0), (dau0, 2 * BW, 0, 0), (dau1, 2 * BW, 0, 0)],
        consts=lru, outs=[(BW, f32)], accs=[p.shape for p in lru])
    sc.env[("gs", "lru_w_a", l)], sc.env[("gs", "lru_w_x", l)] = dwa, dwx

    def conv_bwd(t):
        dxe, xe, w = t.H[0], t.H[1], t.K[0]
        cur = dxe[HALO:HALO + d.tr]
        dw = jnp.concatenate([jnp.sum(cur * xe[HALO + k - 2:HALO + k - 2 + d.tr], axis=0, keepdims=True) for k in range(w.shape[0])], axis=0)
        return [_taps(dxe, w, 2, d.tr, -1)], [dw, jnp.sum(cur, axis=0, keepdims=True)], []
    drx, dcw, dcb = _rowwise("lru_conv_bwd", conv_bwd, T, C, d.tr, halos=[(dxa, BW, 0), (za, BW, 0)],
                             consts=[P["rnn_conv_w"]], outs=[(BW, bf16)], accs=[P["rnn_conv_w"].shape, (1, BW)])
    def sconv_bwd(t):
        scg, sx, sb, dyb = t.H
        w, tr = t.K[0], d.tr
        me = scg * sx
        dsb = dyb[HALO:HALO + tr] * _taps(me, w, 1, tr, 1)
        dce = dyb * sb
        dm = _taps(dce, w, 1, tr, -1)
        cur = dce[HALO:HALO + tr]
        dw = jnp.concatenate([jnp.sum(cur * me[HALO + k - 1:HALO + k - 1 + tr], axis=0, keepdims=True) for k in range(w.shape[0])], axis=0)
        return [[dsb, dm * sx[HALO:HALO + tr], dm * scg[HALO:HALO + tr]]], [dw], []
    dsc, dscw = _rowwise("sconv_bwd", sconv_bwd, T, C, d.tr, halos=[(za, BW, 3), (za, BW, 4), (za, BW, 2), (dys[1], BW, 0)],
                         consts=[P["sc_conv_w"]], outs=[(3 * BW, bf16)], accs=[P["sc_conv_w"].shape])
    dqr, dk0, dk1, dk2, dv0, dv1, dv2, dkc, dvc, dsink = _attn_bwd(sv["qr"], sv["kr"], sv["vb"], P["sinkrows"], dys[2], d)
    nB, nCb = T // ATT_BLOCK, C // ATT_BLOCK

    def att_join(t):
        dq, a1, a0, a2, b1, b0, b2, kc, vc, cs, sn = t.R
        cq, sq, ck, skn = jnp.tile(cs, (1, d.NH)), jnp.tile(sn, (1, d.NH)), jnp.tile(cs, (1, d.NKV)), jnp.tile(sn, (1, d.NKV))
        up = t.i + 1 <= nB - 1
        dn = t.i >= 1
        isc = t.seg == 0
        dk = a1 + jnp.where(up, a0, 0.0) + jnp.where(dn, a2, 0.0) + jnp.where(isc, kc, 0.0)
        dv = b1 + jnp.where(up, b0, 0.0) + jnp.where(dn, b2, 0.0) + jnp.where(isc, vc, 0.0)
        return [[dq * cq + _swap_pairs(dq * sq), dk * ck + _swap_pairs(dk * skn), dv]], [], []
    dqkv = _rowwise("attn_join", att_join, T, C, ATT_BLOCK,
                    rows=[(dqr, QW, 0, 0), (dk1, KW, 0, 0), (dk0, KW, 0, 1), (dk2, KW, 0, -1), (dv1, KW, 0, 0), (dv0, KW, 0, 1),
                          (dv2, KW, 0, -1), (dkc, KW, 0, 0), (dvc, KW, 0, 0), (tabs["cos"], HEAD_DIM, 0, 0),
                          (tabs["sin"], HEAD_DIM, 0, 0)], outs=[(QW + 2 * KW, bf16)])[0]
    dz = jnp.concatenate([drx, drg, dsc, dqkv, dgz], axis=1)
    du = _mm(dz, W["wint"], "nn", f32, "in_proj_dx", tn=1024, tk=3200, comm=sc.at(site + "in_x"))
    sc.env[("g", "wint", l)] = _mm(dz, sv["u"], "tn", bf16, "in_proj_dw", tm=2560, tn=1024, tk=768, comm=sc.at(site + "in_w"))
    dhn, dg, dss = _norm_mod_bwd(sv["h"], du, dh, P["norm_g"][1:2], modv, 3, d)
    grads = dict(norm_g1=dg, dss=dss, dgate=dgate, b_merge=dbm, rnn_conv_w=dcw, rnn_conv_b=dcb,
                 lru_b_a=dba, lru_b_x=dbx, lru_lambda=dlam, sc_conv_w=dscw, sinkrows=dsink)
    return dhn, grads


def _exchange(name, comm):
    ns, nd = len(comm.srcs), len(comm.dsts)

    def body(*refs):
        descs = comm.descriptors(refs[:ns], refs[ns + nd:ns + 2 * nd], refs[ns + 2 * nd:], _coords())
        _comm_start(descs)
        _comm_wait(descs)

    res = pl.pallas_call(
        body, name=name, in_specs=[ANY] * (ns + nd), out_specs=[ANY] * nd, out_shape=[SDS(a.shape, a.dtype) for a in comm.dsts],
        input_output_aliases={ns + q: q for q in range(nd)}, scratch_shapes=comm.scratch(),
    )(*comm.srcs, *comm.dsts)
    comm.store(res)


def _allgather(name, a):
    env = {"g": lax.empty((N_DEV,) + a.shape, a.dtype)}
    cm = _Comm(env)
    cm.add(a, "g", lambda ref, me, pi, r: ref, lambda ref, sender, r: ref.at[sender])
    _exchange(name, cm)
    return env["g"]


def _sum8(x):
    n = x.shape[1]
    tn = _tile(n, 2048, 8)

    def body(x_ref, o_ref):
        acc = x_ref[0]
        for j in range(1, N_DEV):
            acc = acc + x_ref[j]
        o_ref[...] = acc
    return pl.pallas_call(body, name="sum8", grid=(n // tn,), in_specs=[pl.BlockSpec((N_DEV, tn, 128), lambda i: (0, i, 0))],
                          out_specs=pl.BlockSpec((tn, 128), lambda i: (i, 0)), out_shape=SDS((n, 128), f32),
                          compiler_params=_params(("parallel",)))(x)


def _cast_direct(w):
    ly, R, Cn = w.shape
    tc = _tile(Cn, 512, 128)
    spec = pl.BlockSpec((None, R, tc), lambda l, j: (l, 0, j))

    def body(w_ref, o_ref):
        o_ref[...] = w_ref[...].astype(bf16)
    return pl.pallas_call(body, name="cast", grid=(ly, Cn // tc), in_specs=[spec], out_specs=spec, out_shape=SDS(w.shape, bf16),
                          compiler_params=_params(("parallel", "parallel")))(w)


def _cast_transposed(w):
    four = w.ndim == 4
    ly, I, (K, Nl) = w.shape[0], (w.shape[1] if four else 1), w.shape[-2:]
    tk = _tile(K, 256, 128)
    mid = (lambda i: (i,)) if four else (lambda i: ())
    nn = (None,) * (w.ndim - 2)

    def body(w_ref, o_ref):
        o_ref[...] = w_ref[...].T.astype(bf16)
    return pl.pallas_call(
        body, name="cast_t", grid=(ly, I, K // tk), in_specs=[pl.BlockSpec(nn + (tk, Nl), lambda l, i, k: (l,) + mid(i) + (k, 0))],
        out_specs=pl.BlockSpec(nn + (Nl, tk), lambda l, i, k: (l,) + mid(i) + (0, k)), out_shape=SDS(w.shape[:-2] + (Nl, K), bf16),
        compiler_params=_params(("parallel", "parallel", "parallel")))(w)


def _adam(g, w, m, v):
    m = ADAM_B1 * m + (1.0 - ADAM_B1) * g
    v = ADAM_B2 * v + (1.0 - ADAM_B2) * (g * g)
    m_hat = m / (1.0 - ADAM_B1 ** ADAM_STEP)
    v_hat = v / (1.0 - ADAM_B2 ** ADAM_STEP)
    delta = -ADAM_LR * (m_hat / (jnp.sqrt(v_hat) + ADAM_EPS) + ADAM_WD * w)
    return delta, m, v


def _slot_sum(x):
    acc = x[0].astype(f32)
    for r in range(1, x.shape[0]):
        acc = acc + x[r].astype(f32)
    return acc


def _adamw_direct(g, w, m, v, slots):
    ly, R, Cn = w.shape
    tr = R if R <= 1024 else _tile(R, 512, 16)
    tc = _tile(Cn, max(128, (256 * 1024) // tr // 128 * 128), 128)
    spec = pl.BlockSpec((None, tr, tc), lambda l, i, j: (l, i, j))
    gspec = pl.BlockSpec((None, N_DEV, tr, tc), lambda l, i, j: (l, 0, i, j)) if slots else spec

    def body(g_ref, w_ref, m_ref, v_ref, go, do, mo, vo):
        gv = _slot_sum(g_ref[...]) if slots else g_ref[...]
        go[...] = gv
        do[...], mo[...], vo[...] = _adam(gv, w_ref[...], m_ref[...], v_ref[...])
    return pl.pallas_call(body, name="adamw", grid=(ly, R // tr, Cn // tc), in_specs=[gspec, spec, spec, spec], out_specs=[spec] * 4,
                          out_shape=[SDS(w.shape, f32)] * 4, compiler_params=_params(("parallel",) * 3))(g, w, m, v)


def _adamw_transposed(g, w, m, v):
    four = w.ndim == 4
    ly, I, (K, Nl) = w.shape[0], (w.shape[1] if four else 1), w.shape[-2:]
    tk = _tile(K, 128, 128)
    mid = (lambda i: (i,)) if four else (lambda i: ())
    nn = (None,) * (w.ndim - 3)
    spec = pl.BlockSpec((None,) + nn + (tk, Nl), lambda l, i, k: (l,) + mid(i) + (k, 0))
    gspec = pl.BlockSpec((None, N_DEV) + nn + (Nl, tk), lambda l, i, k: (l, 0) + mid(i) + (0, k))

    def body(g_ref, w_ref, m_ref, v_ref, go, do, mo, vo):
        gv = _slot_sum(g_ref[...]).T
        go[...] = gv
        do[...], mo[...], vo[...] = _adam(gv, w_ref[...], m_ref[...], v_ref[...])
    return pl.pallas_call(body, name="adamw_t", grid=(ly, I, K // tk), in_specs=[gspec, spec, spec, spec], out_specs=[spec] * 4,
                          out_shape=[SDS(w.shape, f32)] * 4, compiler_params=_params(("parallel",) * 3))(g, w, m, v)


def _ada_fwd(cvec, ada_w, ada_b_cols):
    ly, D, cols = ada_w.shape
    tn = _tile(cols, 768, 128)

    def body(c_ref, w_ref, b_ref, o_ref):
        o_ref[...] = _bdot(jax.nn.silu(c_ref[...]), w_ref[...]) + b_ref[...]
    return pl.pallas_call(
        body, name="ada", grid=(ly, cols // tn),
        in_specs=[pl.BlockSpec((16, D), lambda l, j: (0, 0)), pl.BlockSpec((None, D, tn), lambda l, j: (l, 0, j)),
                  pl.BlockSpec((None, 1, tn), lambda l, j: (l, 0, j))],
        out_specs=pl.BlockSpec((None, 16, tn), lambda l, j: (l, 0, j)), out_shape=SDS((ly, 16, cols), f32),
        compiler_params=_params(("parallel", "parallel")))(cvec, ada_w, ada_b_cols)


def _ada_bwd(cvec, ada_w, dm):
    ly, D, cols = ada_w.shape
    tn = _tile(cols, 768, 128)

    def body(c_ref, w_ref, d_ref, gw_ref, gc_ref):
        first = jnp.logical_and(pl.program_id(0) == 0, pl.program_id(1) == 0)

        @pl.when(first)
        def _():
            gc_ref[...] = jnp.zeros_like(gc_ref)

        def f(cv, w):
            return _bdot(jax.nn.silu(cv), w)
        _, vjp = jax.vjp(f, c_ref[...], w_ref[...])
        dc, dw = vjp(d_ref[...])
        gw_ref[...] = dw
        gc_ref[...] += dc
    return pl.pallas_call(
        body, name="ada_bwd", grid=(ly, cols // tn),
        in_specs=[pl.BlockSpec((16, D), lambda l, j: (0, 0)), pl.BlockSpec((None, D, tn), lambda l, j: (l, 0, j)),
                  pl.BlockSpec((None, 16, tn), lambda l, j: (l, 0, j))],
        out_specs=[pl.BlockSpec((None, D, tn), lambda l, j: (l, 0, j)), pl.BlockSpec((16, D), lambda l, j: (0, 0))],
        out_shape=[SDS(ada_w.shape, f32), SDS((16, D), f32)], compiler_params=_params(("arbitrary", "arbitrary")))(cvec, ada_w, dm)


def _flat_adamw(g, w, m, v):
    n = w.shape[0]
    tn = _tile(n, 1024, 8)
    spec = pl.BlockSpec((tn, 128), lambda i: (i, 0))

    def body(g_ref, w_ref, m_ref, v_ref, do, mo, vo):
        do[...], mo[...], vo[...] = _adam(g_ref[...], w_ref[...], m_ref[...], v_ref[...])
    return pl.pallas_call(body, name="adamw_small", grid=(n // tn,), in_specs=[spec] * 4, out_specs=[spec] * 3,
                          out_shape=[SDS(w.shape, f32)] * 3, compiler_params=_params(("parallel",)))(g, w, m, v)


def _pack(arrs):
    flat = jnp.concatenate([a.reshape(-1).astype(f32) for a in arrs])
    n = _round_up(flat.shape[0], 512 * 128)
    return jnp.pad(flat, (0, n - flat.shape[0])).reshape(n // 128, 128)


def _unpack(flat, shapes, lead=()):
    flat = flat.reshape(lead + (-1,))
    out, off = [], 0
    for s in shapes:
        sz = math.prod(s)
        out.append(flat[..., off:off + sz].reshape(lead + tuple(s)))
        off += sz
    return out


def _unshard_last(g):
    g = jnp.moveaxis(g, 0, -2)
    return g.reshape(g.shape[:-2] + (g.shape[-2] * g.shape[-1],))


class _Dims:
    pass


def _rope_tables(L, C, NH):
    rows = L // GRID_W
    row = jnp.repeat(jnp.arange(rows), GRID_W).astype(f32)
    col = jnp.tile(jnp.arange(GRID_W), rows).astype(f32)
    half = HEAD_DIM // 2
    inv = ROPE_BASE ** (-jnp.arange(0, half, 2, dtype=f32) / half)
    ar, ac = row[:, None] * inv, col[:, None] * inv
    cos = jnp.concatenate([jnp.cos(ar), jnp.cos(ar), jnp.cos(ac), jnp.cos(ac)], axis=-1)
    sin = jnp.concatenate([-jnp.sin(ar), jnp.sin(ar), -jnp.sin(ac), jnp.sin(ac)], axis=-1)
    cos = jnp.concatenate([jnp.ones((C, HEAD_DIM), f32), cos], axis=0)
    sin = jnp.concatenate([jnp.zeros((C, HEAD_DIM), f32), sin], axis=0)
    return dict(cos=cos, sin=sin)


def kernel(x, c, ctx, c_ctx, ada_w, ada_b, norm_g, ffn1_w13, ffn1_w2, w_in, b_merge, rnn_conv_w, rnn_conv_b, lru_w_a, lru_b_a, lru_w_x, lru_b_x, lru_lambda, sc_conv_w, attn_sink, w_branch, w_out, ffn2_w13, ffn2_w2, final_norm_g, loss_target, m_c_ctx, m_ada_w, m_ada_b, m_norm_g, m_ffn1_w13, m_ffn1_w2, m_w_in, m_b_merge, m_rnn_conv_w, m_rnn_conv_b, m_lru_w_a, m_lru_b_a, m_lru_w_x, m_lru_b_x, m_lru_lambda, m_sc_conv_w, m_attn_sink, m_w_branch, m_w_out, m_ffn2_w13, m_ffn2_w2, m_final_norm_g, v_c_ctx, v_ada_w, v_ada_b, v_norm_g, v_ffn1_w13, v_ffn1_w2, v_w_in, v_b_merge, v_rnn_conv_w, v_rnn_conv_b, v_lru_w_a, v_lru_b_a, v_lru_w_x, v_lru_b_x, v_lru_lambda, v_sc_conv_w, v_attn_sink, v_w_branch, v_w_out, v_ffn2_w13, v_ffn2_w2, v_final_norm_g):
    d = _Dims()
    L, D = x.shape[1], x.shape[2]
    C = ctx.shape[1]
    LY = ada_w.shape[0]
    d.D, d.C, d.T = D, C, C + L
    d.F = ffn1_w2.shape[1] * N_DEV
    d.Fp = _round_up(d.F, 512)
    d.IN = w_in.shape[2] * N_DEV
    d.BW = w_branch.shape[2]
    d.NH = attn_sink.shape[1]
    d.NA = d.IN - N_BRANCH * D
    d.NKV = (d.NA - 5 * d.BW - d.NH * HEAD_DIM) // (2 * HEAD_DIM)
    d.G = d.NH // d.NKV
    d.NB = lru_w_a.shape[2]
    d.tr = _tile(C, 256, 8)
    d.trw = _tile(C, 128, 8)
    assert d.NH * HEAD_DIM == d.BW and L % GRID_W == 0 and C % ATT_BLOCK == 0 and L % ATT_BLOCK == 0
    sh13, sh2, shin, shd = 2 * d.F // N_DEV, d.F // N_DEV, d.IN // N_DEV, D // N_DEV
    cols9 = N_MOD * D // N_DEV
    me3 = _coords()
    me = _index(me3)

    sharded = [norm_g, b_merge, rnn_conv_w, lru_b_a, lru_b_x, lru_lambda, sc_conv_w]
    shapes1 = [a.shape for a in sharded] + [(D,)]
    g1 = _allgather("gather_small", _pack(sharded + [c.reshape(-1)]))
    parts = _unpack(g1, shapes1, (N_DEV,))
    norm_g_f, b_merge_f, conv_w_f, lru_b_a_f, lru_b_x_f, lru_lam_f, sc_w_f = [_unshard_last(p) for p in parts[:-1]]
    cvec = jnp.concatenate([parts[-1], c_ctx[None], jnp.zeros((7, D), f32)], axis=0)
    ada_b_cols = lax.dynamic_slice_in_dim(ada_b, me * cols9, cols9, axis=1)[:, None, :]
    modcols = _ada_fwd(cvec, ada_w, ada_b_cols)
    g2 = _allgather("gather_mod", modcols)
    modall = jnp.moveaxis(g2, 0, 2).reshape(LY, 16, N_MOD, D)
    mod_lat = lax.dynamic_index_in_dim(modall, me, axis=1, keepdims=False)
    modv = jnp.stack([modall[:, N_DEV], mod_lat], axis=1)

    assert LY == 2
    off13 = lambda j: (j // 4) * d.Fp + (j % 4) * sh13
    cast_t = _cast_transposed
    c13a, c13b = cast_t(ffn1_w13), cast_t(ffn2_w13)
    wt = {"w13a": (c13a, off13, sh13, 2 * d.Fp, (D,), False), "w13b": (c13b, off13, sh13, 2 * d.Fp, (D,), False),
          "w2a": (_cast_direct(ffn1_w2), lambda j: j * sh2, sh2, d.Fp, (D,), False),
          "w2b": (_cast_direct(ffn2_w2), lambda j: j * sh2, sh2, d.Fp, (D,), False),
          "wint": (cast_t(w_in), lambda j: j * shin, shin, d.IN, (D,), False),
          "wbt": (_cast_transposed(w_branch), lambda j: j * shd, shd, D, (d.BW,), True),
          "wout": (_cast_direct(w_out), lambda j: j * shd, shd, D, (D,), False)}
    env = {}
    for name, (cw, off, sh, tot, rest, three) in wt.items():
        for l in range(LY):
            env[(name, l)] = lax.empty(((N_BRANCH, tot) if three else (tot,)) + rest, bf16)
        env[("r", name)] = lax.empty((LY, N_DEV) + ((N_BRANCH, sh) if three else (sh,)) + rest, bf16)
    zpad = jnp.zeros((max(d.Fp - d.F, 16), D), bf16)

    def part(sh, h):
        return (0, sh) if h is None else (h * (sh // 2), sh // 2)

    def rows_of(ref, start, n, three):
        sl = pl.ds(start if isinstance(start, int) else pl.multiple_of(start, 16), n)
        return ref.at[:, sl] if three else ref.at[sl]

    def add_gather(cm, name, l, h=None):
        cw, off, sh, tot, rest, three = wt[name]
        r0, n = part(sh, h)
        cm.add(cw, (name, l), lambda ref, me_, pi, r: rows_of(ref.at[l], r0, n, three),
               lambda ref, sender, r: rows_of(ref, off(sender) + r0, n, three))
        if d.Fp > d.F and h in (None, 0) and name[:2] in ("w1", "w2"):
            for base in ((0, d.Fp) if name[:3] == "w13" else (0,)):
                cm.fill(zpad, (name, l), lambda ref, base=base: ref.at[pl.ds(base + d.F, d.Fp - d.F)])

    def add_rs(cm, name, l, h=None):
        cw, off, sh, tot, rest, three = wt[name]
        r0, n = part(sh, h)
        cm.add(env[("g", name, l)], ("r", name), lambda ref, me_, pi, r: rows_of(ref, off(pi) + r0, n, three),
               lambda ref, sender, r: rows_of(ref.at[l, r], r0, n, three))

    plan = {"F0a.up": [("ag", "wint", 0)], "F0a.down": [("ag", "wbt", 0), ("ag", "wout", 0)],
            "F0m.in_a": [("ag", "w13b", 0, 0)], "F0m.in_g": [("ag", "w13b", 0, 1)],
            "F0b.up": [("ag", "w2b", 0), ("ag", "w13a", 1, 0)], "F0b.down": [("ag", "w13a", 1, 1)],
            "F1a.up": [("ag", "w2a", 1), ("ag", "wint", 1, 0)], "F1a.down": [("ag", "wint", 1, 1)],
            "F1m.in_a": [("ag", "wbt", 1), ("ag", "wout", 1)], "F1m.in_g": [("ag", "w13b", 1, 0)],
            "F1m.out": [("ag", "w13b", 1, 1)], "F1b.up": [("ag", "w2b", 1)], "B0m.in_x": [("lru",)]}
    for l in range(LY):
        plan.update({f"B{l}b.dw": [("rs", "w13b", l, 0)], f"B{l}b.ux": [("rs", "w13b", l, 1), ("rs", "w2b", l)],
                     f"B{l}m.in_w": [("rs", "wout", l), ("rs", "wbt", l)], f"B{l}a.dx": [("rs", "wint", l, 0)],
                     f"B{l}a.uw": [("rs", "wint", l, 1)], f"B{l}a.dw": [("rs", "w13a", l, 0)],
                     f"B{l}a.ux": [("rs", "w13a", l, 1), ("rs", "w2a", l)]})

    def add_lru(cm):
        for nm in ("lru_w_a", "lru_w_x"):
            gs = jnp.stack([env[("gs", nm, l)] for l in range(LY)]).reshape(-1, 128)
            env[("all", nm)] = lax.empty((N_DEV,) + gs.shape, f32)
            cm.add(gs, ("all", nm), lambda ref, me_, pi, r: ref, lambda ref, sender, r: ref.at[sender])
    sc = _Sched(plan, env, {"ag": add_gather, "rs": add_rs, "lru": add_lru})
    first = _Comm(env)
    add_gather(first, "w13a", 0)
    add_gather(first, "w2a", 0)
    _exchange("gather_first", first)

    tabs = _rope_tables(L, C, d.NH)
    h = jnp.concatenate([ctx[0], x[0]], axis=0)
    Ps, saves = [], []
    for l in range(LY):
        sinkrows = jnp.repeat(attn_sink[l].reshape(d.NKV, d.G), ATT_BLOCK, axis=1)[:, :, None]
        Ps.append(dict(norm_g=norm_g_f[l], b_merge=b_merge_f[l], rnn_conv_w=conv_w_f[l], rnn_conv_b=rnn_conv_b[l][None],
                       lru_w_a=lru_w_a[l], lru_b_a=lru_b_a_f[l], lru_w_x=lru_w_x[l], lru_b_x=lru_b_x_f[l], lru_lambda=lru_lam_f[l],
                       sc_conv_w=sc_w_f[l], sinkrows=sinkrows))
    for l in range(LY):
        P = Ps[l]
        h, s1 = _ffn_fwd(h, modv[l], P["norm_g"][0:1], ("w13a", l), ("w2a", l), 0, d, sc, f"F{l}a.")
        h, s2 = _mixer_fwd(h, modv[l], P, l, tabs, d, sc, f"F{l}m.")
        h, s3 = _ffn_fwd(h, modv[l], P["norm_g"][2:3], ("w13b", l), ("w2b", l), 6, d, sc, f"F{l}b.")
        saves.append((s1, s2, s3))

    fng = final_norm_g[None]

    def loss_fn(t):
        def f(hv, g):
            y = hv * lax.rsqrt(jnp.mean(hv * hv, axis=-1, keepdims=True) + EPS) * g
            e = y - t.R[1]
            return 0.5 * jnp.sum(jnp.mean(e * e, axis=-1))
        lat = (t.seg == 1).astype(f32)
        val, (dhv, dg) = jax.value_and_grad(f, argnums=(0, 1))(t.R[0], t.K[0])
        return [dhv * lat], [dg * lat, jnp.full((1, 128), val * lat, f32)], []
    dh, dfng, lossv = _rowwise("loss", loss_fn, d.T, C, d.tr, rows=[(h, D, 0, 0), (loss_target[0], D, 0, -(C // d.tr))],
                               consts=[fng], outs=[(D, f32)], accs=[(1, D), (1, 128)])
    loss = lax.psum(lossv[0, 0], AXES)

    gl = [None] * LY
    for l in reversed(range(LY)):
        P = Ps[l]
        s1, s2, s3 = saves[l]
        dh, dg2, dss2, dgt2 = _ffn_bwd(dh, s3, modv[l], P["norm_g"][2:3], ("w13b", l), ("w2b", l), 6, d, sc, f"B{l}b.")
        dh, gm = _mixer_bwd(dh, s2, modv[l], P, l, tabs, d, sc, f"B{l}m.")
        dh, dg0, dss0, dgt0 = _ffn_bwd(dh, s1, modv[l], P["norm_g"][0:1], ("w13a", l), ("w2a", l), 0, d, sc, f"B{l}a.")
        gm.update(norm_g=jnp.concatenate([dg0, gm["norm_g1"], dg2], axis=0),
                  dmod=jnp.concatenate([dss0, dgt0, gm["dss"], gm["dgate"], dss2, dgt2], axis=1))
        gl[l] = gm
    grad_x = dh[C:][None]
    st = lambda k: jnp.stack([g[k] for g in gl])

    r13a, r2a, rin, rwb, rwo, r13b, r2b = [env[("r", k)] for k in ("w13a", "w2a", "wint", "wbt", "wout", "w13b", "w2b")]
    big = {}
    big["ffn1_w13"] = _adamw_transposed(r13a, ffn1_w13, m_ffn1_w13, v_ffn1_w13)
    big["ffn2_w13"] = _adamw_transposed(r13b, ffn2_w13, m_ffn2_w13, v_ffn2_w13)
    big["w_in"] = _adamw_transposed(rin, w_in, m_w_in, v_w_in)
    big["w_branch"] = _adamw_transposed(rwb, w_branch, m_w_branch, v_w_branch)
    big["ffn1_w2"] = _adamw_direct(r2a, ffn1_w2, m_ffn1_w2, v_ffn1_w2, True)
    big["ffn2_w2"] = _adamw_direct(r2b, ffn2_w2, m_ffn2_w2, v_ffn2_w2, True)
    big["w_out"] = _adamw_direct(rwo, w_out, m_w_out, v_w_out, True)

    sink_g = jnp.stack([jnp.sum(g["sinkrows"].reshape(d.NKV, d.G, ATT_BLOCK), axis=-1).reshape(d.NH) for g in gl])
    small_full = dict(norm_g=st("norm_g"), b_merge=st("b_merge"), rnn_conv_w=st("rnn_conv_w"), rnn_conv_b=st("rnn_conv_b")[:, 0],
                      lru_b_a=st("lru_b_a"), lru_b_x=st("lru_b_x"), lru_lambda=st("lru_lambda"), sc_conv_w=st("sc_conv_w"), attn_sink=sink_g, final_norm_g=dfng[0])
    names_s = list(small_full)
    dmod = st("dmod")
    pk = _pack([small_full[k] for k in names_s] + [dmod])
    g4 = _allgather("gather_small_grads", pk)
    tot = _unpack(_sum8(g4), [small_full[k].shape for k in names_s] + [dmod.shape])
    sums = dict(zip(names_s, tot[:-1]))
    dmod_sum = tot[-1].reshape(LY, 2, N_MOD * D)
    dmod_all = _unpack(g4, [small_full[k].shape for k in names_s] + [dmod.shape], (N_DEV,))[-1].reshape(N_DEV, LY, 2, N_MOD * D)
    dm_rows = jnp.concatenate([jnp.moveaxis(dmod_all[:, :, 1], 0, 1), dmod_sum[:, 0:1], jnp.zeros((LY, 7, N_MOD * D), f32)], axis=1)
    dm_cols = lax.dynamic_slice_in_dim(dm_rows, me * cols9, cols9, axis=2)
    g_ada_w, dcv = _ada_bwd(cvec, ada_w, dm_cols)
    g5 = _allgather("gather_cctx", _pack([dcv[N_DEV]]))
    g_c_ctx = _sum8(g5).reshape(-1)[:D]
    g_ada_b = dmod_sum[:, 0] + dmod_sum[:, 1]
    ada_out = _adamw_direct(g_ada_w, ada_w, m_ada_w, v_ada_w, False)

    def shard_last(a, n):
        return lax.dynamic_slice_in_dim(a, me * n, n, axis=a.ndim - 1)
    local_g = dict(c_ctx=g_c_ctx, ada_b=g_ada_b, norm_g=shard_last(sums["norm_g"], shd), b_merge=shard_last(sums["b_merge"], shd),
                   rnn_conv_w=shard_last(sums["rnn_conv_w"], d.BW // N_DEV), rnn_conv_b=sums["rnn_conv_b"],
                   lru_b_a=shard_last(sums["lru_b_a"], d.BW // N_DEV), lru_b_x=shard_last(sums["lru_b_x"], d.BW // N_DEV),
                   lru_lambda=shard_last(sums["lru_lambda"], d.BW // N_DEV), sc_conv_w=shard_last(sums["sc_conv_w"], d.BW // N_DEV),
                   attn_sink=sums["attn_sink"], final_norm_g=sums["final_norm_g"])
    wmv = dict(c_ctx=(c_ctx, m_c_ctx, v_c_ctx), ada_b=(ada_b, m_ada_b, v_ada_b), norm_g=(norm_g, m_norm_g, v_norm_g),
               b_merge=(b_merge, m_b_merge, v_b_merge), rnn_conv_w=(rnn_conv_w, m_rnn_conv_w, v_rnn_conv_w),
               rnn_conv_b=(rnn_conv_b, m_rnn_conv_b, v_rnn_conv_b),
               lru_b_a=(lru_b_a, m_lru_b_a, v_lru_b_a), lru_b_x=(lru_b_x, m_lru_b_x, v_lru_b_x),
               lru_lambda=(lru_lambda, m_lru_lambda, v_lru_lambda), sc_conv_w=(sc_conv_w, m_sc_conv_w, v_sc_conv_w),
               attn_sink=(attn_sink, m_attn_sink, v_attn_sink), final_norm_g=(final_norm_g, m_final_norm_g, v_final_norm_g))
    names_l = list(local_g)
    shapes_l = [wmv[k][0].shape for k in names_l]
    gp = _pack([local_g[k].reshape(wmv[k][0].shape) for k in names_l])
    outs_s = _flat_adamw(gp, _pack([wmv[k][0] for k in names_l]), _pack([wmv[k][1] for k in names_l]), _pack([wmv[k][2] for k in names_l]))
    small = {k: [local_g[k].reshape(wmv[k][0].shape)] for k in names_l}
    for o in outs_s:
        for k, a in zip(names_l, _unpack(o, shapes_l)):
            small[k].append(a)
    for nm, (w_, m_, v_) in (("lru_w_a", (lru_w_a, m_lru_w_a, v_lru_w_a)), ("lru_w_x", (lru_w_x, m_lru_w_x, v_lru_w_x))):
        gsum = _sum8(env[("all", nm)])
        flat = lambda a: a.reshape(-1, 128)
        small[nm] = [a.reshape(w_.shape) for a in (gsum, *_flat_adamw(gsum, flat(w_), flat(m_), flat(v_)))]

    order = ["c_ctx", "ada_w", "ada_b", "norm_g", "ffn1_w13", "ffn1_w2", "w_in", "b_merge", "rnn_conv_w", "rnn_conv_b", "lru_w_a",
             "lru_b_a", "lru_w_x", "lru_b_x", "lru_lambda", "sc_conv_w", "attn_sink", "w_branch", "w_out", "ffn2_w13", "ffn2_w2",
             "final_norm_g"]
    allo = dict(small)
    allo.update(big)
    allo["ada_w"] = ada_out
    res = [loss, grad_x]
    for q in range(4):
        res += [allo[k][q] for k in order]
    return tuple(res)
```

```python
import functools
import math

import jax
import jax.numpy as jnp
from jax import lax
from jax.experimental import pallas as pl
from jax.experimental.pallas import tpu as pltpu

f32 = jnp.float32
bf16 = jnp.bfloat16
SDS = jax.ShapeDtypeStruct

N_DEV = 8
AXES = ("x", "y", "c")
HEAD_DIM = 128
GRID_W = 64
ATT_BLOCK = 128
HALO = 16
ROPE_BASE = 10000.0
LRU_C = 8.0
EPS = 1e-6
NEG_INF = -1e30
N_MOD = 9
N_BRANCH = 3
ADAM_LR, ADAM_B1, ADAM_B2, ADAM_EPS, ADAM_WD, ADAM_STEP = 0.001, 0.9, 0.999, 1e-08, 0.01, 10
VMEM_LIMIT = 56 * 1024 * 1024
ANY = pl.BlockSpec(memory_space=pl.ANY)


def _round_up(n, m):
    return (n + m - 1) // m * m


def _tile(n, target, align):
    best = None
    t = align
    while t <= min(n, target):
        if n % t == 0:
            best = t
        t += align
    return best if best is not None else n


def _params(sem):
    return pltpu.CompilerParams(dimension_semantics=sem, vmem_limit_bytes=VMEM_LIMIT)


_DIMS = {"nn": (((1,), (0,)), ((), ())), "nt": (((1,), (1,)), ((), ())), "tn": (((0,), (0,)), ((), ()))}


def _coords():
    return lax.axis_index("x"), lax.axis_index("y"), lax.axis_index("c")


def _peer(xyc, r):
    x, y, c = xyc
    return (1 - x if r & 4 else x, 1 - y if r & 2 else y, 1 - c if r & 1 else c)


def _index(xyc):
    return xyc[0] * 4 + xyc[1] * 2 + xyc[2]


class _Comm:
    def __init__(self, env):
        self.env, self.srcs, self.dsts, self.keys, self.items, self.fills = env, [], [], [], [], []

    def _src(self, a):
        for q, b in enumerate(self.srcs):
            if b is a:
                return q
        self.srcs.append(a)
        return len(self.srcs) - 1

    def _dst(self, key):
        if key not in self.keys:
            self.keys.append(key)
            self.dsts.append(self.env[key])
        return self.keys.index(key)

    def add(self, src, key, src_view, dst_view):
        self.items.append((self._src(src), self._dst(key), src_view, dst_view))

    def fill(self, src, key, dst_view):
        self.fills.append((self._src(src), self._dst(key), dst_view))

    def scratch(self):
        n = len(self.items)
        return [pltpu.SemaphoreType.DMA((7 * n,)), pltpu.SemaphoreType.DMA((7 * n,)), pltpu.SemaphoreType.DMA((n + len(self.fills),))]

    def store(self, arrays):
        for key, a in zip(self.keys, arrays):
            self.env[key] = a

    def descriptors(self, srcs, dsts, sems, me3):
        send, recv, lsem = sems
        me = _index(me3)
        local, sends, recvs = [], [], []
        for q, (si, di, sv, dv) in enumerate(self.items):
            local.append(pltpu.make_async_copy(sv(srcs[si], me, me, 0), dv(dsts[di], me, 0), lsem.at[q]))
            for r in range(1, N_DEV):
                p3 = _peer(me3, r)
                pi = _index(p3)
                kw = dict(send_sem=send.at[q * 7 + r - 1], recv_sem=recv.at[q * 7 + r - 1], device_id=p3,
                          device_id_type=pl.DeviceIdType.MESH)
                sends.append(pltpu.make_async_remote_copy(src_ref=sv(srcs[si], me, pi, r), dst_ref=dv(dsts[di], me, r), **kw))
                recvs.append(pltpu.make_async_remote_copy(src_ref=sv(srcs[si], me, pi, r), dst_ref=dv(dsts[di], pi, r), **kw))
        for q, (si, di, dv) in enumerate(self.fills):
            local.append(pltpu.make_async_copy(srcs[si], dv(dsts[di]), lsem.at[len(self.items) + q]))
        return local, sends, recvs


def _comm_start(descs):
    local, sends, _ = descs
    for cp in local + sends:
        cp.start()


def _comm_wait(descs):
    local, sends, recvs = descs
    for cp in recvs:
        cp.wait_recv()
    for cp in sends:
        cp.wait_send()
    for cp in local:
        cp.wait()


def _carry_call(core, name, grid, in_specs, out_specs, out_shape, scratch, args, comm):
    ni, no, nsc, ng = len(in_specs), len(out_specs), len(scratch), len(grid)
    if comm is None:
        def plain(*refs):
            core(refs[:ni], refs[ni:ni + no], refs[ni + no:])
        return pl.pallas_call(plain, name=name, grid=grid, in_specs=in_specs, out_specs=out_specs, out_shape=out_shape,
                              scratch_shapes=scratch, compiler_params=_params(("parallel",) * (ng - 1) + ("arbitrary",)))(*args)
    ns, nd = len(comm.srcs), len(comm.dsts)
    o0 = ni + ns + nd

    def body(*refs):
        pid = [pl.program_id(q) for q in range(ng)]
        first, last = pid[0] == 0, pid[0] == grid[0] - 1
        for q in range(1, ng):
            first = jnp.logical_and(first, pid[q] == 0)
            last = jnp.logical_and(last, pid[q] == grid[q] - 1)
        me3 = _coords()
        mk = lambda: comm.descriptors(refs[ni:ni + ns], refs[o0 + no:o0 + no + nd], refs[o0 + no + nd + nsc:], me3)

        @pl.when(first)
        def _():
            _comm_start(mk())

        core(refs[:ni], refs[o0:o0 + no], refs[o0 + no + nd:o0 + no + nd + nsc])

        @pl.when(last)
        def _():
            _comm_wait(mk())

    res = pl.pallas_call(
        body, name=name, grid=grid, in_specs=list(in_specs) + [ANY] * (ns + nd), out_specs=list(out_specs) + [ANY] * nd,
        out_shape=list(out_shape) + [SDS(x.shape, x.dtype) for x in comm.dsts], scratch_shapes=list(scratch) + comm.scratch(),
        input_output_aliases={ni + ns + q: no + q for q in range(nd)}, compiler_params=_params(("arbitrary",) * ng),
    )(*args, *comm.srcs, *comm.dsts)
    comm.store(res[no:])
    return res[:no]


def _mm(a, b, mode, out_dtype, name, *, M=None, N=None, K=None, a_off=0, b_off=0, b_lead=None,
        tm=768, tn=512, tk=2048, comm=None, a3=False, resid=None):
    b2 = b.shape[1:] if b_lead is not None else b.shape
    if mode == "tn":
        K = a.shape[-2] if K is None else K
        M = (a.shape[0] * a.shape[2] if a3 else a.shape[1]) if M is None else M
        N = b2[1] if N is None else N
    else:
        M = a.shape[-2] if M is None else M
        K = (a.shape[0] * a.shape[2] if a3 else a.shape[1]) if K is None else K
        N = (b2[1] if mode == "nn" else b2[0]) if N is None else N
    g = math.gcd
    if mode == "tn":
        tm = _tile(g(M, a_off) if a_off else (a.shape[2] if a3 else M), tm, 128)
        tk = _tile(K, tk, 16)
    else:
        tm = _tile(M, tm, 16)
        tk = _tile(g(K, a_off) if a_off else (a.shape[2] if a3 else K), tk, 128)
    tn = _tile(g(N, b_off) if b_off else N, tn, 128)
    nk = K // tk
    ao, bo = (a_off // (tm if mode == "tn" else tk)), b_off // tn
    lead = () if b_lead is None else (b_lead,)
    ld = () if b_lead is None else (None,)
    if mode == "nn":
        a_spec = pl.BlockSpec((tm, tk), lambda i, j, k: (i, ao + k))
        b_spec = pl.BlockSpec(ld + (tk, tn), lambda i, j, k: lead + (k, bo + j))
    elif mode == "nt":
        a_spec = pl.BlockSpec((tm, tk), lambda i, j, k: (i, ao + k))
        b_spec = pl.BlockSpec(ld + (tn, tk), lambda i, j, k: lead + (bo + j, k))
    else:
        a_spec = pl.BlockSpec((tk, tm), lambda i, j, k: (k, ao + i))
        b_spec = pl.BlockSpec(ld + (tk, tn), lambda i, j, k: lead + (k, bo + j))
    if a3:
        if mode == "nn":
            nkc = a.shape[2] // tk
            a_spec = pl.BlockSpec((None, tm, tk), lambda i, j, k: (k // nkc, i, k % nkc))
        else:
            nmc = a.shape[2] // tm
            a_spec = pl.BlockSpec((None, tk, tm), lambda i, j, k: (i // nmc, k, i % nmc))
    dims = _DIMS[mode]

    def finish(ins, outs, val):
        outs[0][...] = val.astype(out_dtype)
        if resid is not None:
            gi, coef, C = resid[2:]
            row = pl.program_id(0) * tm + lax.broadcasted_iota(jnp.int32, (tm, 1), 0)
            gate = jnp.where(row < C, ins[3][0, gi:gi + 1], ins[3][1, gi:gi + 1])
            outs[1][...] = ins[2][...] + coef * gate * val

    def core(ins, outs, scr):
        a_ref, b_ref = ins[0], ins[1]
        k = pl.program_id(2)
        if nk == 1:
            finish(ins, outs, lax.dot_general(a_ref[...], b_ref[...], dims, preferred_element_type=f32))
        else:
            acc = scr[0]

            @pl.when(k == 0)
            def _():
                acc[...] = jnp.zeros_like(acc)

            acc[...] += lax.dot_general(a_ref[...], b_ref[...], dims, preferred_element_type=f32)

            @pl.when(k == nk - 1)
            def _():
                finish(ins, outs, acc[...])

    o_spec = pl.BlockSpec((tm, tn), lambda i, j, k: (i, j))
    in_specs, out_specs, out_shape, args = [a_spec, b_spec], [o_spec], [SDS((M, N), out_dtype)], (a, b)
    if resid is not None:
        in_specs += [o_spec, pl.BlockSpec((2, N_MOD, tn), lambda i, j, k: (0, 0, j))]
        out_specs, out_shape, args = out_specs + [o_spec], out_shape + [SDS((M, N), f32)], args + tuple(resid[:2])
    res = _carry_call(core, name, (M // tm, N // tn, nk), in_specs, out_specs, out_shape,
                      [] if nk == 1 else [pltpu.VMEM((tm, tn), f32)], args, comm)
    return res[0] if resid is None else res


class _Tile:
    pass


def _rowwise(name, fn, T, C, tr, rows=(), halos=(), consts=(), mods=(), outs=(), accs=(), maccs=()):
    nT, nC = T // tr, C // tr
    assert T % tr == 0 and C % tr == 0 and nC >= 1 and tr % HALO == 0
    r8, n8 = tr // HALO, T // HALO
    seg_of = lambda i: jnp.where(i >= nC, 1, 0)
    in_specs, args = [], []
    for arr, w, cb, ro in rows:
        nt = arr.shape[0] // tr
        in_specs.append(pl.BlockSpec((tr, w), (lambda i, cb=cb, ro=ro, nt=nt: (jnp.clip(i + ro, 0, nt - 1), cb))))
        args.append(arr)
    for arr, w, cb in halos:
        in_specs.append(pl.BlockSpec((HALO, w), lambda i, cb=cb: (jnp.maximum(i * r8 - 1, 0), cb)))
        in_specs.append(pl.BlockSpec((tr, w), lambda i, cb=cb: (i, cb)))
        in_specs.append(pl.BlockSpec((HALO, w), lambda i, cb=cb: (jnp.minimum((i + 1) * r8, n8 - 1), cb)))
        args += [arr, arr, arr]
    for arr in consts:
        in_specs.append(pl.BlockSpec(arr.shape, lambda i, nd=arr.ndim: (0,) * nd))
        args.append(arr)
    for arr in mods:
        in_specs.append(pl.BlockSpec((None,) + arr.shape[1:], lambda i, nd=arr.ndim: (seg_of(i),) + (0,) * (nd - 1)))
        args.append(arr)
    out_specs, out_shape = [], []
    for w, dt in outs:
        out_specs.append(pl.BlockSpec((tr, w), lambda i: (i, 0)))
        out_shape.append(SDS((T, w), dt))
    for shp in accs:
        out_specs.append(pl.BlockSpec(shp, lambda i, nd=len(shp): (0,) * nd))
        out_shape.append(SDS(shp, f32))
    for shp in maccs:
        out_specs.append(pl.BlockSpec((None,) + shp, lambda i, nd=len(shp): (seg_of(i),) + (0,) * nd))
        out_shape.append(SDS((2,) + shp, f32))
    nr, nh, nk, nm, no, na, nma = len(rows), len(halos), len(consts), len(mods), len(outs), len(accs), len(maccs)

    def body(*refs):
        i = pl.program_id(0)
        t = _Tile()
        t.i, t.seg = i, seg_of(i)
        p = 0
        t.R = [refs[p + k][...].astype(f32) for k in range(nr)]
        p += nr
        pvalid = jnp.logical_and(i != 0, i != nC)
        nvalid = jnp.logical_and(i != nC - 1, i != nT - 1)
        t.H = []
        for k in range(nh):
            pr, cu, nx = refs[p][...].astype(f32), refs[p + 1][...].astype(f32), refs[p + 2][...].astype(f32)
            p += 3
            pr = jnp.where(pvalid, pr, jnp.zeros_like(pr))
            nx = jnp.where(nvalid, nx, jnp.zeros_like(nx))
            t.H.append(jnp.concatenate([pr, cu, nx], axis=0))
        t.K = [refs[p + k][...] for k in range(nk)]
        p += nk
        t.M = [refs[p + k][...] for k in range(nm)]
        p += nm
        o, a, ma = fn(t)
        for k in range(no):
            ref = refs[p + k]
            pieces = o[k] if isinstance(o[k], (list, tuple)) else [o[k]]
            c0 = 0
            for pc in pieces:
                ref[:, c0:c0 + pc.shape[1]] = pc.astype(ref.dtype)
                c0 += pc.shape[1]
        p += no
        for k in range(na):
            ref = refs[p + k]

            @pl.when(i == 0)
            def _(ref=ref):
                ref[...] = jnp.zeros_like(ref)

            ref[...] += a[k]
        p += na
        for k in range(nma):
            ref = refs[p + k]

            @pl.when(jnp.logical_or(i == 0, i == nC))
            def _(ref=ref):
                ref[...] = jnp.zeros_like(ref)

            ref[...] += ma[k]

    res = pl.pallas_call(
        body, name=name, grid=(nT,), in_specs=in_specs, out_specs=out_specs, out_shape=out_shape,
        compiler_params=_params(("arbitrary",)),
    )(*args)
    return res


@jax.custom_vjp
def _bdot(a, b):
    return jnp.dot(a.astype(bf16), b.astype(bf16), preferred_element_type=f32)


def _bdot_fwd(a, b):
    return _bdot(a, b), (a, b)


def _bdot_bwd(res, ct):
    a, b = res
    ctb = ct.astype(bf16)
    da = lax.dot_general(ctb, b.astype(bf16), _DIMS["nt"], preferred_element_type=f32)
    db = lax.dot_general(a.astype(bf16), ctb, _DIMS["tn"], preferred_element_type=f32)
    return da, db


_bdot.defvjp(_bdot_fwd, _bdot_bwd)


def _rms_mod(h, g, shift, scale):
    y = h * lax.rsqrt(jnp.mean(h * h, axis=-1, keepdims=True) + EPS) * g
    return y * (1.0 + scale) + shift


def _norm_mod_fwd(h, g, modv, s, d):
    def fn(t):
        m = t.M[0]
        return [_rms_mod(t.R[0], t.K[0], m[s:s + 1], m[s + 1:s + 2])], [], []
    return _rowwise("norm_mod", fn, d.T, d.C, d.tr, rows=[(h, d.D, 0, 0)], consts=[g], mods=[modv],
                    outs=[(d.D, bf16)])[0]


def _norm_mod_bwd(h, du, dh, g, modv, s, d):
    def fn(t):
        m = t.M[0]
        _, vjp = jax.vjp(_rms_mod, t.R[0], t.K[0], m[s:s + 1], m[s + 1:s + 2])
        dx, dg, dsh, dsc = vjp(t.R[1])
        return [t.R[2] + dx], [dg], [jnp.concatenate([dsh, dsc], axis=0)]
    return _rowwise("norm_mod_bwd", fn, d.T, d.C, d.tr, rows=[(h, d.D, 0, 0), (du, d.D, 0, 0), (dh, d.D, 0, 0)],
                    consts=[g], mods=[modv], outs=[(d.D, f32)], accs=[(1, d.D)], maccs=[(2, d.D)])


def _swiglu(g, u):
    return jax.nn.silu(g) * u


def _resid_bwd(dh, o, modv, gi, coef, d):
    def fn(t):
        dhv = t.R[0]
        return [coef * t.M[0][gi:gi + 1] * dhv], [], [jnp.sum(coef * dhv * t.R[1], axis=0, keepdims=True)]
    return _rowwise("resid_bwd", fn, d.T, d.C, d.tr, rows=[(dh, d.D, 0, 0), (o, d.D, 0, 0)], mods=[modv],
                    outs=[(d.D, bf16)], maccs=[(1, d.D)])


class _Sched:
    def __init__(self, plan, env, builders):
        self.plan, self.env, self.builders = plan, env, builders

    def at(self, site):
        specs = self.plan.get(site)
        if not specs:
            return None
        cm = _Comm(self.env)
        for kind, *args in specs:
            self.builders[kind](cm, *args)
        return cm


def _ffn_up(u, w13t, d, comm):
    T, D, Fp = d.T, d.D, d.Fp
    tm, tn = _tile(T, 768, 16), _tile(Fp, 512, 128)

    def core(ins, outs, scr):
        uv, w_ref = ins[0][...], ins[1]
        gt = lax.dot_general(uv, w_ref[0], _DIMS["nt"], preferred_element_type=f32)
        up = lax.dot_general(uv, w_ref[1], _DIMS["nt"], preferred_element_type=f32)
        outs[0][0] = gt.astype(bf16)
        outs[0][1] = up.astype(bf16)
        outs[1][...] = _swiglu(gt, up).astype(bf16)

    return _carry_call(core, "ffn_up", (T // tm, Fp // tn),
                       [pl.BlockSpec((tm, D), lambda i, j: (i, 0)), pl.BlockSpec((2, tn, D), lambda i, j: (0, j, 0))],
                       [pl.BlockSpec((2, tm, tn), lambda i, j: (0, i, j)), pl.BlockSpec((tm, tn), lambda i, j: (i, j))],
                       [SDS((2, T, Fp), bf16), SDS((T, Fp), bf16)], [], (u, w13t.reshape(2, Fp, D)), comm)


def _ffn_down_dx(do, w2, gu, d, comm):
    T, D, Fp = d.T, d.D, d.Fp
    tm, tn = _tile(T, 768, 16), _tile(Fp, 512, 128)

    def core(ins, outs, scr):
        dhm = lax.dot_general(ins[0][...], ins[1][...], _DIMS["nt"], preferred_element_type=f32)
        _, vjp = jax.vjp(_swiglu, ins[2][0].astype(f32), ins[2][1].astype(f32))
        dg, du = vjp(dhm)
        outs[0][0] = dg.astype(bf16)
        outs[0][1] = du.astype(bf16)

    return _carry_call(core, "ffn_down_dx", (T // tm, Fp // tn),
                       [pl.BlockSpec((tm, D), lambda i, j: (i, 0)), pl.BlockSpec((tn, D), lambda i, j: (j, 0)),
                        pl.BlockSpec((2, tm, tn), lambda i, j: (0, i, j))],
                       [pl.BlockSpec((2, tm, tn), lambda i, j: (0, i, j))], [SDS((2, T, Fp), bf16)], [], (do, w2, gu), comm)[0]


def _ffn_fwd(h, modv, g, k13, k2, s, d, sc, site):
    u = _norm_mod_fwd(h, g, modv, s, d)
    gu, hmid = _ffn_up(u, sc.env[k13], d, sc.at(site + "up"))
    o, hn = _mm(hmid, sc.env[k2], "nn", bf16, "ffn_down", tn=1024, tk=2816, comm=sc.at(site + "down"),
                resid=(h, modv, s + 2, 0.5, d.C))
    return hn, dict(h=h, u=u, gu=gu, hmid=hmid, o=o)


def _ffn_bwd(dh, sv, modv, g, k13, k2, s, d, sc, site):
    w13t, w2 = sc.env[k13], sc.env[k2]
    do, dgate = _resid_bwd(dh, sv["o"], modv, s + 2, 0.5, d)
    dgu = _ffn_down_dx(do, w2, sv["gu"], d, sc.at(site + "dx"))
    sc.env[("g",) + k13] = _mm(dgu, sv["u"], "tn", bf16, "ffn_up_dw", tm=2816, tn=1024, tk=768, comm=sc.at(site + "uw"), a3=True)
    sc.env[("g",) + k2] = _mm(sv["hmid"], do, "tn", bf16, "ffn_down_dw", tm=2816, tn=1024, tk=768, comm=sc.at(site + "dw"))
    du = _mm(dgu, w13t, "nn", f32, "ffn_up_dx", tn=1024, tk=2816, comm=sc.at(site + "ux"), a3=True)
    dhn, dg, dss = _norm_mod_bwd(sv["h"], du, dh, g, modv, s, d)
    return dhn, dg, dss, dgate


def _taps(ext, w, left, tr, sign):
    acc = None
    for k in range(w.shape[0]):
        o = HALO + sign * (k - left)
        term = w[k:k + 1] * ext[o:o + tr]
        acc = term if acc is None else acc + term
    return acc


def _gates(xa, wa, ba, wx, bx, lam, nb):
    bs = xa.shape[1] // nb
    out = []
    for dr in range(2):
        pa = jnp.concatenate([_bdot(xa[:, n * bs:(n + 1) * bs], wa[dr, n]) for n in range(nb)], axis=1)
        px = jnp.concatenate([_bdot(xa[:, n * bs:(n + 1) * bs], wx[dr, n]) for n in range(nb)], axis=1)
        rg = jax.nn.sigmoid(pa + ba[dr:dr + 1])
        ig = jax.nn.sigmoid(px + bx[dr:dr + 1])
        log_a = -LRU_C * rg * jax.nn.softplus(-lam[dr:dr + 1])
        a = jnp.exp(log_a)
        u = jnp.sqrt(1.0 - jnp.exp(2.0 * log_a)) * (ig * xa)
        out += [a, u]
    return out


def _combine_a(hf, hb, rg):
    return (hf + hb) * jax.nn.gelu(rg)


def _scan_order(kind, nT, nC):
    nL = nT - nC
    if kind == "F":
        return (lambda s: s), False
    if kind == "revF":
        return (lambda s: nT - 1 - s), True
    if kind == "B":
        return (lambda s: jnp.where(s < nC, nC - 1 - s, nT - 1 - (s - nC))), True
    return (lambda s: jnp.where(s < nL, nC + s, s - nL)), False


def _scan_fwd(au, dr, d):
    R, tc = d.BW, d.tr
    tile_of, down = _scan_order("F" if dr == 0 else "B", d.T // tc, d.C // tc)

    def body(a_ref, u_ref, h_ref, hp_ref, st):
        @pl.when(pl.program_id(0) == 0)
        def _():
            st[...] = jnp.zeros_like(st)

        def grp(gi, h):
            r = pl.multiple_of((tc // 8 - 1 - gi if down else gi) * 8, 8)
            at, ut = a_ref[pl.ds(r, 8), :], u_ref[pl.ds(r, 8), :]
            hs, hps = [None] * 8, [None] * 8
            for k in (range(7, -1, -1) if down else range(8)):
                hps[k] = h
                h = at[k:k + 1] * h + ut[k:k + 1]
                hs[k] = h
            h_ref[pl.ds(r, 8), :] = jnp.concatenate(hs, axis=0)
            hp_ref[pl.ds(r, 8), :] = jnp.concatenate(hps, axis=0)
            return h
        st[...] = lax.fori_loop(0, tc // 8, grp, st[...])

    return pl.pallas_call(
        body, name="lru_scan", grid=(d.T // tc,),
        in_specs=[pl.BlockSpec((tc, R), lambda s: (tile_of(s), 2 * dr)), pl.BlockSpec((tc, R), lambda s: (tile_of(s), 2 * dr + 1))],
        out_specs=[pl.BlockSpec((tc, R), lambda s: (tile_of(s), 0))] * 2, out_shape=[SDS((d.T, R), f32)] * 2,
        scratch_shapes=[pltpu.VMEM((1, R), f32)], compiler_params=_params(("arbitrary",)),
    )(au, au)


def _scan_bwd(dh, au, hp, dr, d):
    R, tc = d.BW, d.tr
    tile_of, down = _scan_order("revF" if dr == 0 else "revB", d.T // tc, d.C // tc)

    def body(dh_ref, a_ref, hp_ref, o_ref, st):
        @pl.when(pl.program_id(0) == 0)
        def _():
            st[...] = jnp.zeros_like(st)

        def grp(gi, carry):
            lam, an = carry
            r = pl.multiple_of((tc // 8 - 1 - gi if down else gi) * 8, 8)
            at, dt, ht = a_ref[pl.ds(r, 8), :], dh_ref[pl.ds(r, 8), :], hp_ref[pl.ds(r, 8), :]
            ls = [None] * 8
            for k in (range(7, -1, -1) if down else range(8)):
                lam = dt[k:k + 1] + an * lam
                an = at[k:k + 1]
                ls[k] = lam
            lt = jnp.concatenate(ls, axis=0)
            o_ref[pl.ds(r, 8), 0:R] = lt * ht
            o_ref[pl.ds(r, 8), R:2 * R] = lt
            return lam, an
        lam, an = lax.fori_loop(0, tc // 8, grp, (st[0:1], st[1:2]))
        st[0:1] = lam
        st[1:2] = an

    return pl.pallas_call(
        body, name="lru_scan_bwd", grid=(d.T // tc,),
        in_specs=[pl.BlockSpec((tc, R), lambda s: (tile_of(s), 0)), pl.BlockSpec((tc, R), lambda s: (tile_of(s), 2 * dr)),
                  pl.BlockSpec((tc, R), lambda s: (tile_of(s), 0))],
        out_specs=pl.BlockSpec((tc, 2 * R), lambda s: (tile_of(s), 0)), out_shape=SDS((d.T, 2 * R), f32),
        scratch_shapes=[pltpu.VMEM((2, R), f32)], compiler_params=_params(("arbitrary",)),
    )(dh, au, hp)


def _swap_pairs(x):
    w = x.shape[1]
    lane = lax.broadcasted_iota(jnp.int32, x.shape, 1)
    return jnp.where(lane % 64 < 32, pltpu.roll(x, w - 32, 1), pltpu.roll(x, 32, 1))


def _att_masks(blk, nB, nCb, G):
    rows, cols = G * ATT_BLOCK, 3 * ATT_BLOCK
    qi = lax.broadcasted_iota(jnp.int32, (rows, cols), 0) % ATT_BLOCK
    kj = lax.broadcasted_iota(jnp.int32, (rows, cols), 1)
    rel = kj - ATT_BLOCK - qi
    kb = kj // ATT_BLOCK
    one = jnp.int32(1)
    latent = jnp.where(blk >= nCb, one, 0)
    prev_ok = jnp.where(blk - 1 >= nCb, latent, 0)
    next_ok = jnp.where(blk + 1 <= nB - 1, latent, 0)
    bv = jnp.where(kb == 0, prev_ok, jnp.where(kb == 1, latent, next_ok))
    return jnp.logical_and(jnp.abs(rel) <= ATT_BLOCK, bv > 0)


def _att_specs(d):
    G, nB = d.G, d.T // ATT_BLOCK
    blk = lambda f: pl.BlockSpec((ATT_BLOCK, HEAD_DIM), lambda kh, b: (f(b), kh))
    three = [blk(lambda b: jnp.maximum(b - 1, 0)), blk(lambda b: b), blk(lambda b: jnp.minimum(b + 1, nB - 1))]
    ctxs = pl.BlockSpec((d.C, HEAD_DIM), lambda kh, b: (0, kh))
    qs = pl.BlockSpec((ATT_BLOCK, G * HEAD_DIM), lambda kh, b: (b, kh))
    sk = pl.BlockSpec((None, G * ATT_BLOCK, 1), lambda kh, b: (kh, 0, 0))
    return qs, three, ctxs, sk


def _stack_heads(x, G):
    return jnp.concatenate([x[:, g * HEAD_DIM:(g + 1) * HEAD_DIM] for g in range(G)], axis=0)


def _unstack_heads(x, G):
    return jnp.concatenate([x[g * ATT_BLOCK:(g + 1) * ATT_BLOCK] for g in range(G)], axis=1)


def _att_probs(qg, kcat, kctx, sink, valid):
    scale = HEAD_DIM ** -0.5
    s = lax.dot_general(qg, kcat, _DIMS["nt"], preferred_element_type=f32) * scale
    s = jnp.where(valid, s, NEG_INF)
    sc = lax.dot_general(qg, kctx, _DIMS["nt"], preferred_element_type=f32) * scale
    m = jnp.maximum(jnp.maximum(jnp.max(s, axis=1, keepdims=True), jnp.max(sc, axis=1, keepdims=True)), sink)
    e, ec, es = jnp.exp(s - m), jnp.exp(sc - m), jnp.exp(sink - m)
    inv = 1.0 / (jnp.sum(e, axis=1, keepdims=True) + jnp.sum(ec, axis=1, keepdims=True) + es)
    return e * inv, ec * inv, es * inv


def _attn_fwd(qr, kr, vb, sinkrows, d):
    G, nB, nCb = d.G, d.T // ATT_BLOCK, d.C // ATT_BLOCK
    qs, three, ctxs, sk = _att_specs(d)

    def body(q_ref, k0, k1, k2, v0, v1, v2, kc_ref, vc_ref, s_ref, o_ref):
        b = pl.program_id(1)
        qg = _stack_heads(q_ref[...], G)
        kcat = jnp.concatenate([k0[...], k1[...], k2[...]], axis=0)
        vcat = jnp.concatenate([v0[...], v1[...], v2[...]], axis=0)
        p, pc, _ = _att_probs(qg, kcat, kc_ref[...], s_ref[...], _att_masks(b, nB, nCb, G))
        o = jnp.dot(p.astype(bf16), vcat, preferred_element_type=f32) + jnp.dot(pc.astype(bf16), vc_ref[...], preferred_element_type=f32)
        o_ref[...] = _unstack_heads(o, G).astype(bf16)

    return pl.pallas_call(
        body, name="attn", grid=(d.NKV, nB), in_specs=[qs] + three + three + [ctxs, ctxs, sk],
        out_specs=qs, out_shape=SDS((d.T, d.NH * HEAD_DIM), bf16), compiler_params=_params(("parallel", "arbitrary")),
    )(qr, kr, kr, kr, vb, vb, vb, kr, vb, sinkrows)


def _attn_bwd(qr, kr, vb, sinkrows, dy, d):
    G, nB, nCb = d.G, d.T // ATT_BLOCK, d.C // ATT_BLOCK
    qs, three, ctxs, sk = _att_specs(d)
    KW = d.NKV * HEAD_DIM
    part = pl.BlockSpec((ATT_BLOCK, HEAD_DIM), lambda kh, b: (b, kh))

    def body(q_ref, k0, k1, k2, v0, v1, v2, kc_ref, vc_ref, s_ref, dy_ref,
             dq_ref, dk0, dk1, dk2, dv0, dv1, dv2, dkc_ref, dvc_ref, ds_ref):
        b = pl.program_id(1)
        scale = HEAD_DIM ** -0.5
        qg = _stack_heads(q_ref[...], G)
        kcat = jnp.concatenate([k0[...], k1[...], k2[...]], axis=0)
        vcat = jnp.concatenate([v0[...], v1[...], v2[...]], axis=0)
        kctx, vctx = kc_ref[...], vc_ref[...]
        p, pc, ps = _att_probs(qg, kcat, kctx, s_ref[...], _att_masks(b, nB, nCb, G))
        dog = _stack_heads(dy_ref[...], G).astype(bf16)
        dp = lax.dot_general(dog, vcat, _DIMS["nt"], preferred_element_type=f32)
        dpc = lax.dot_general(dog, vctx, _DIMS["nt"], preferred_element_type=f32)
        delta = jnp.sum(p * dp, axis=1, keepdims=True) + jnp.sum(pc * dpc, axis=1, keepdims=True)
        ds = (p * (dp - delta) * scale).astype(bf16)
        dsc = (pc * (dpc - delta) * scale).astype(bf16)
        dq = jnp.dot(ds, kcat, preferred_element_type=f32) + jnp.dot(dsc, kctx, preferred_element_type=f32)
        dq_ref[...] = _unstack_heads(dq, G)
        dk = lax.dot_general(ds, qg, _DIMS["tn"], preferred_element_type=f32)
        dv = lax.dot_general(p.astype(bf16), dog, _DIMS["tn"], preferred_element_type=f32)
        for j, (rk, rv) in enumerate(((dk0, dv0), (dk1, dv1), (dk2, dv2))):
            rk[...] = dk[j * ATT_BLOCK:(j + 1) * ATT_BLOCK]
            rv[...] = dv[j * ATT_BLOCK:(j + 1) * ATT_BLOCK]

        @pl.when(b == 0)
        def _():
            dkc_ref[...] = jnp.zeros_like(dkc_ref)
            dvc_ref[...] = jnp.zeros_like(dvc_ref)
            ds_ref[...] = jnp.zeros_like(ds_ref)

        dkc_ref[...] += lax.dot_general(dsc, qg, _DIMS["tn"], preferred_element_type=f32)
        dvc_ref[...] += lax.dot_general(pc.astype(bf16), dog, _DIMS["tn"], preferred_element_type=f32)
        ds_ref[...] += -ps * delta

    kv = SDS((d.T, KW), f32)
    return pl.pallas_call(
        body, name="attn_bwd", grid=(d.NKV, nB), in_specs=[qs] + three + three + [ctxs, ctxs, sk, qs],
        out_specs=[qs] + [part] * 6 + [ctxs, ctxs, sk],
        out_shape=[SDS((d.T, d.NH * HEAD_DIM), f32)] + [kv] * 6 + [SDS((d.C, KW), f32)] * 2 + [SDS((d.NKV, G * ATT_BLOCK, 1), f32)],
        compiler_params=_params(("parallel", "arbitrary")),
    )(qr, kr, kr, kr, vb, vb, vb, kr, vb, sinkrows, dy)


def _mixer_fwd(h, modv, P, l, tabs, d, sc, site):
    D, BW, T, C = d.D, d.BW, d.T, d.C
    u = _norm_mod_fwd(h, P["norm_g"][1:2], modv, 3, d)
    za = _mm(u, sc.env[("wint", l)], "nt", bf16, "in_proj_a", N=d.NA, comm=sc.at(site + "in_a"))
    gz = _mm(u, sc.env[("wint", l)], "nt", bf16, "in_proj_g", N=3 * D, b_off=d.NA, comm=sc.at(site + "in_g"))
    xa = _rowwise("lru_conv", lambda t: ([_taps(t.H[0], t.K[0], 2, d.tr, 1) + t.K[1]], [], []), T, C, d.tr,
                  halos=[(za, BW, 0)], consts=[P["rnn_conv_w"], P["rnn_conv_b"]], outs=[(BW, f32)])[0]
    lru = [P["lru_w_a"], P["lru_b_a"], P["lru_w_x"], P["lru_b_x"], P["lru_lambda"]]
    au = _rowwise("lru_gates", lambda t: ([_gates(t.R[0], *t.K, d.NB)], [], []), T, C, d.tr,
                  rows=[(xa, BW, 0, 0)], consts=lru, outs=[(4 * BW, f32)])[0]
    hf, hpf = _scan_fwd(au, 0, d)
    hb, hpb = _scan_fwd(au, 1, d)
    ya = _rowwise("lru_out", lambda t: ([_combine_a(*t.R)], [], []), T, C, d.tr,
                  rows=[(hf, BW, 0, 0), (hb, BW, 0, 0), (za, BW, 1, 0)], outs=[(BW, bf16)])[0]
    yb = _rowwise("sconv", lambda t: ([t.R[0] * _taps(t.H[0] * t.H[1], t.K[0], 1, d.tr, 1)], [], []), T, C, d.tr,
                  rows=[(za, BW, 2, 0)], halos=[(za, BW, 3), (za, BW, 4)], consts=[P["sc_conv_w"]], outs=[(BW, bf16)])[0]
    QW, KW = d.NH * HEAD_DIM, d.NKV * HEAD_DIM

    def rope(t):
        q, k, v, cs, sn = t.R
        cq, sq, ck, skn = jnp.tile(cs, (1, d.NH)), jnp.tile(sn, (1, d.NH)), jnp.tile(cs, (1, d.NKV)), jnp.tile(sn, (1, d.NKV))
        return [q * cq + _swap_pairs(q) * sq, k * ck + _swap_pairs(k) * skn, v], [], []
    kcb = (5 * BW + QW) // KW
    qr, kr, vb = _rowwise("rope", rope, T, C, d.tr,
                          rows=[(za, QW, (5 * BW) // QW, 0), (za, KW, kcb, 0), (za, KW, kcb + 1, 0),
                                (tabs["cos"], HEAD_DIM, 0, 0), (tabs["sin"], HEAD_DIM, 0, 0)],
                          outs=[(QW, bf16), (KW, bf16), (KW, bf16)])
    yatt = _attn_fwd(qr, kr, vb, P["sinkrows"], d)
    ys = (ya, yb, yatt)
    ps = [_mm(ys[i], sc.env[("wbt", l)], "nt", bf16, "lift", b_lead=i, tn=1024) for i in range(N_BRANCH)]

    def merge(t):
        gzv, bm = t.R[0], t.K[0]
        acc = None
        for i in range(N_BRANCH):
            term = jax.nn.sigmoid(gzv[:, i * D:(i + 1) * D] + bm[i:i + 1]) * t.R[1 + i]
            acc = term if acc is None else acc + term
        return [acc], [], []
    merged = _rowwise("merge", merge, T, C, d.trw, rows=[(gz, 3 * D, 0, 0)] + [(p, D, 0, 0) for p in ps],
                      consts=[P["b_merge"]], outs=[(D, bf16)])[0]
    y, hn = _mm(merged, sc.env[("wout", l)], "nn", bf16, "out_proj", tn=1024, comm=sc.at(site + "out"),
                resid=(h, modv, 5, 1.0, d.C))
    sv = dict(h=h, u=u, za=za, gz=gz, xa=xa, au=au, hf=hf, hpf=hpf, hb=hb, hpb=hpb, ys=ys, qr=qr, kr=kr, vb=vb,
              ps=ps, merged=merged, y=y)
    return hn, sv


def _mixer_bwd(dh, sv, modv, P, l, tabs, d, sc, site):
    W = dict(wint=sc.env[("wint", l)], wbt=sc.env[("wbt", l)], wout=sc.env[("wout", l)])
    D, BW, T, C = d.D, d.BW, d.T, d.C
    QW, KW = d.NH * HEAD_DIM, d.NKV * HEAD_DIM
    za, gz = sv["za"], sv["gz"]
    dy, dgate = _resid_bwd(dh, sv["y"], modv, 5, 1.0, d)
    dmerged = _mm(dy, W["wout"], "nt", f32, "out_proj_dx", tn=1024)
    sc.env[("g", "wout", l)] = _mm(sv["merged"], dy, "tn", bf16, "out_proj_dw", tm=1024, tn=2048, tk=768)

    def merge_bwd(t):
        gzv, bm, dm = t.R[0], t.K[0], t.R[4]
        dps, dgs, dbs = [], [], []
        for i in range(N_BRANCH):
            gate = jax.nn.sigmoid(gzv[:, i * D:(i + 1) * D] + bm[i:i + 1])
            dps.append(dm * gate)
            dgi = dm * t.R[1 + i] * gate * (1.0 - gate)
            dgs.append(dgi)
            dbs.append(jnp.sum(dgi, axis=0, keepdims=True))
        return [dps, dgs], [jnp.concatenate(dbs, axis=0)], []
    dp, dgz, dbm = _rowwise("merge_bwd", merge_bwd, T, C, d.trw,
                            rows=[(gz, 3 * D, 0, 0)] + [(p, D, 0, 0) for p in sv["ps"]] + [(dmerged, D, 0, 0)],
                            consts=[P["b_merge"]], outs=[(3 * D, bf16), (3 * D, bf16)], accs=[(N_BRANCH, D)])
    dys = [_mm(dp, W["wbt"], "nn", f32, "lift_dx", K=D, a_off=i * D, b_lead=i, tn=1024) for i in range(N_BRANCH)]
    sc.env[("g", "wbt", l)] = jnp.stack([_mm(dp, sv["ys"][i], "tn", bf16, "lift_dw", M=D, a_off=i * D, tm=1024, tn=1024, tk=768)
                                         for i in range(N_BRANCH)])
    def out_bwd(t):
        _, vjp = jax.vjp(_combine_a, t.R[1], t.R[2], t.R[3])
        dhf, _, drg = vjp(t.R[0])
        return [dhf, drg], [], []
    dhs, drg = _rowwise("lru_out_bwd", out_bwd, T, C, d.tr,
                        rows=[(dys[0], BW, 0, 0), (sv["hf"], BW, 0, 0), (sv["hb"], BW, 0, 0), (za, BW, 1, 0)],
                        outs=[(BW, f32), (BW, bf16)])
    dau0 = _scan_bwd(dhs, sv["au"], sv["hpf"], 0, d)
    dau1 = _scan_bwd(dhs, sv["au"], sv["hpb"], 1, d)
    lru = [P["lru_w_a"], P["lru_b_a"], P["lru_w_x"], P["lru_b_x"], P["lru_lambda"]]

    def gates_bwd(t):
        _, vjp = jax.vjp(lambda xa, *k: _gates(xa, *k, d.NB), t.R[0], *t.K)
        d0, d1 = t.R[1], t.R[2]
        g = vjp([d0[:, :BW], d0[:, BW:], d1[:, :BW], d1[:, BW:]])
        return [g[0]], list(g[1:]), []
    dxa, dwa, dba, dwx, dbx, dlam = _rowwise(
        "lru_gates_bwd", gates_bwd, T, C, d.tr, rows=[(sv["xa"], BW, 0, 0), (dau0, 2 * BW, 0, 0), (dau1, 2 * BW, 0, 0)],
        consts=lru, outs=[(BW, f32)], accs=[p.shape for p in lru])
    sc.env[("gs", "lru_w_a", l)], sc.env[("gs", "lru_w_x", l)] = dwa, dwx

    def conv_bwd(t):
        dxe, xe, w = t.H[0], t.H[1], t.K[0]
        cur = dxe[HALO:HALO + d.tr]
        dw = jnp.concatenate([jnp.sum(cur * xe[HALO + k - 2:HALO + k - 2 + d.tr], axis=0, keepdims=True) for k in range(w.shape[0])], axis=0)
        return [_taps(dxe, w, 2, d.tr, -1)], [dw, jnp.sum(cur, axis=0, keepdims=True)], []
    drx, dcw, dcb = _rowwise("lru_conv_bwd", conv_bwd, T, C, d.tr, halos=[(dxa, BW, 0), (za, BW, 0)],
                             consts=[P["rnn_conv_w"]], outs=[(BW, bf16)], accs=[P["rnn_conv_w"].shape, (1, BW)])
    def sconv_bwd(t):
        scg, sx, sb, dyb = t.H
        w, tr = t.K[0], d.tr
        me = scg * sx
        dsb = dyb[HALO:HALO + tr] * _taps(me, w, 1, tr, 1)
        dce = dyb * sb
        dm = _taps(dce, w, 1, tr, -1)
        cur = dce[HALO:HALO + tr]
        dw = jnp.concatenate([jnp.sum(cur * me[HALO + k - 1:HALO + k - 1 + tr], axis=0, keepdims=True) for k in range(w.shape[0])], axis=0)
        return [[dsb, dm * sx[HALO:HALO + tr], dm * scg[HALO:HALO + tr]]], [dw], []
    dsc, dscw = _rowwise("sconv_bwd", sconv_bwd, T, C, d.tr, halos=[(za, BW, 3), (za, BW, 4), (za, BW, 2), (dys[1], BW, 0)],
                         consts=[P["sc_conv_w"]], outs=[(3 * BW, bf16)], accs=[P["sc_conv_w"].shape])
    dqr, dk0, dk1, dk2, dv0, dv1, dv2, dkc, dvc, dsink = _attn_bwd(sv["qr"], sv["kr"], sv["vb"], P["sinkrows"], dys[2], d)
    nB, nCb = T // ATT_BLOCK, C // ATT_BLOCK

    def att_join(t):
        dq, a1, a0, a2, b1, b0, b2, kc, vc, cs, sn = t.R
        cq, sq, ck, skn = jnp.tile(cs, (1, d.NH)), jnp.tile(sn, (1, d.NH)), jnp.tile(cs, (1, d.NKV)), jnp.tile(sn, (1, d.NKV))
        up = t.i + 1 <= nB - 1
        dn = t.i >= 1
        isc = t.seg == 0
        dk = a1 + jnp.where(up, a0, 0.0) + jnp.where(dn, a2, 0.0) + jnp.where(isc, kc, 0.0)
        dv = b1 + jnp.where(up, b0, 0.0) + jnp.where(dn, b2, 0.0) + jnp.where(isc, vc, 0.0)
        return [[dq * cq + _swap_pairs(dq * sq), dk * ck + _swap_pairs(dk * skn), dv]], [], []
    dqkv = _rowwise("attn_join", att_join, T, C, ATT_BLOCK,
                    rows=[(dqr, QW, 0, 0), (dk1, KW, 0, 0), (dk0, KW, 0, 1), (dk2, KW, 0, -1), (dv1, KW, 0, 0), (dv0, KW, 0, 1),
                          (dv2, KW, 0, -1), (dkc, KW, 0, 0), (dvc, KW, 0, 0), (tabs["cos"], HEAD_DIM, 0, 0),
                          (tabs["sin"], HEAD_DIM, 0, 0)], outs=[(QW + 2 * KW, bf16)])[0]
    dz = jnp.concatenate([drx, drg, dsc, dqkv, dgz], axis=1)
    du = _mm(dz, W["wint"], "nn", f32, "in_proj_dx", tn=1024, tk=3200, comm=sc.at(site + "in_x"))
    sc.env[("g", "wint", l)] = _mm(dz, sv["u"], "tn", bf16, "in_proj_dw", tm=2560, tn=1024, tk=768, comm=sc.at(site + "in_w"))
    dhn, dg, dss = _norm_mod_bwd(sv["h"], du, dh, P["norm_g"][1:2], modv, 3, d)
    grads = dict(norm_g1=dg, dss=dss, dgate=dgate, b_merge=dbm, rnn_conv_w=dcw, rnn_conv_b=dcb,
                 lru_b_a=dba, lru_b_x=dbx, lru_lambda=dlam, sc_conv_w=dscw, sinkrows=dsink)
    return dhn, grads


def _exchange(name, comm):
    ns, nd = len(comm.srcs), len(comm.dsts)

    def body(*refs):
        descs = comm.descriptors(refs[:ns], refs[ns + nd:ns + 2 * nd], refs[ns + 2 * nd:], _coords())
        _comm_start(descs)
        _comm_wait(descs)

    res = pl.pallas_call(
        body, name=name, in_specs=[ANY] * (ns + nd), out_specs=[ANY] * nd, out_shape=[SDS(a.shape, a.dtype) for a in comm.dsts],
        input_output_aliases={ns + q: q for q in range(nd)}, scratch_shapes=comm.scratch(),
    )(*comm.srcs, *comm.dsts)
    comm.store(res)


def _allgather(name, a):
    env = {"g": lax.empty((N_DEV,) + a.shape, a.dtype)}
    cm = _Comm(env)
    cm.add(a, "g", lambda ref, me, pi, r: ref, lambda ref, sender, r: ref.at[sender])
    _exchange(name, cm)
    return env["g"]


def _sum8(x):
    n = x.shape[1]
    tn = _tile(n, 2048, 8)

    def body(x_ref, o_ref):
        acc = x_ref[0]
        for j in range(1, N_DEV):
            acc = acc + x_ref[j]
        o_ref[...] = acc
    return pl.pallas_call(body, name="sum8", grid=(n // tn,), in_specs=[pl.BlockSpec((N_DEV, tn, 128), lambda i: (0, i, 0))],
                          out_specs=pl.BlockSpec((tn, 128), lambda i: (i, 0)), out_shape=SDS((n, 128), f32),
                          compiler_params=_params(("parallel",)))(x)


def _cast_direct(w):
    ly, R, Cn = w.shape
    tc = _tile(Cn, 512, 128)
    spec = pl.BlockSpec((None, R, tc), lambda l, j: (l, 0, j))

    def body(w_ref, o_ref):
        o_ref[...] = w_ref[...].astype(bf16)
    return pl.pallas_call(body, name="cast", grid=(ly, Cn // tc), in_specs=[spec], out_specs=spec, out_shape=SDS(w.shape, bf16),
                          compiler_params=_params(("parallel", "parallel")))(w)


def _cast_transposed(w):
    four = w.ndim == 4
    ly, I, (K, Nl) = w.shape[0], (w.shape[1] if four else 1), w.shape[-2:]
    tk = _tile(K, 256, 128)
    mid = (lambda i: (i,)) if four else (lambda i: ())
    nn = (None,) * (w.ndim - 2)

    def body(w_ref, o_ref):
        o_ref[...] = w_ref[...].T.astype(bf16)
    return pl.pallas_call(
        body, name="cast_t", grid=(ly, I, K // tk), in_specs=[pl.BlockSpec(nn + (tk, Nl), lambda l, i, k: (l,) + mid(i) + (k, 0))],
        out_specs=pl.BlockSpec(nn + (Nl, tk), lambda l, i, k: (l,) + mid(i) + (0, k)), out_shape=SDS(w.shape[:-2] + (Nl, K), bf16),
        compiler_params=_params(("parallel", "parallel", "parallel")))(w)


def _adam(g, w, m, v):
    m = ADAM_B1 * m + (1.0 - ADAM_B1) * g
    v = ADAM_B2 * v + (1.0 - ADAM_B2) * (g * g)
    m_hat = m / (1.0 - ADAM_B1 ** ADAM_STEP)
    v_hat = v / (1.0 - ADAM_B2 ** ADAM_STEP)
    delta = -ADAM_LR * (m_hat / (jnp.sqrt(v_hat) + ADAM_EPS) + ADAM_WD * w)
    return delta, m, v


def _slot_sum(x):
    acc = x[0].astype(f32)
    for r in range(1, x.shape[0]):
        acc = acc + x[r].astype(f32)
    return acc


def _adamw_direct(g, w, m, v, slots):
    ly, R, Cn = w.shape
    tr = R if R <= 1024 else _tile(R, 512, 16)
    tc = _tile(Cn, max(128, (256 * 1024) // tr // 128 * 128), 128)
    spec = pl.BlockSpec((None, tr, tc), lambda l, i, j: (l, i, j))
    gspec = pl.BlockSpec((None, N_DEV, tr, tc), lambda l, i, j: (l, 0, i, j)) if slots else spec

    def body(g_ref, w_ref, m_ref, v_ref, go, do, mo, vo):
        gv = _slot_sum(g_ref[...]) if slots else g_ref[...]
        go[...] = gv
        do[...], mo[...], vo[...] = _adam(gv, w_ref[...], m_ref[...], v_ref[...])
    return pl.pallas_call(body, name="adamw", grid=(ly, R // tr, Cn // tc), in_specs=[gspec, spec, spec, spec], out_specs=[spec] * 4,
                          out_shape=[SDS(w.shape, f32)] * 4, compiler_params=_params(("parallel",) * 3))(g, w, m, v)


def _adamw_transposed(g, w, m, v):
    four = w.ndim == 4
    ly, I, (K, Nl) = w.shape[0], (w.shape[1] if four else 1), w.shape[-2:]
    tk = _tile(K, 128, 128)
    mid = (lambda i: (i,)) if four else (lambda i: ())
    nn = (None,) * (w.ndim - 3)
    spec = pl.BlockSpec((None,) + nn + (tk, Nl), lambda l, i, k: (l,) + mid(i) + (k, 0))
    gspec = pl.BlockSpec((None, N_DEV) + nn + (Nl, tk), lambda l, i, k: (l, 0) + mid(i) + (0, k))

    def body(g_ref, w_ref, m_ref, v_ref, go, do, mo, vo):
        gv = _slot_sum(g_ref[...]).T
        go[...] = gv
        do[...], mo[...], vo[...] = _adam(gv, w_ref[...], m_ref[...], v_ref[...])
    return pl.pallas_call(body, name="adamw_t", grid=(ly, I, K // tk), in_specs=[gspec, spec, spec, spec], out_specs=[spec] * 4,
                          out_shape=[SDS(w.shape, f32)] * 4, compiler_params=_params(("parallel",) * 3))(g, w, m, v)


def _ada_fwd(cvec, ada_w, ada_b_cols):
    ly, D, cols = ada_w.shape
    tn = _tile(cols, 768, 128)

    def body(c_ref, w_ref, b_ref, o_ref):
        o_ref[...] = _bdot(jax.nn.silu(c_ref[...]), w_ref[...]) + b_ref[...]
    return pl.pallas_call(
        body, name="ada", grid=(ly, cols // tn),
        in_specs=[pl.BlockSpec((16, D), lambda l, j: (0, 0)), pl.BlockSpec((None, D, tn), lambda l, j: (l, 0, j)),
                  pl.BlockSpec((None, 1, tn), lambda l, j: (l, 0, j))],
        out_specs=pl.BlockSpec((None, 16, tn), lambda l, j: (l, 0, j)), out_shape=SDS((ly, 16, cols), f32),
        compiler_params=_params(("parallel", "parallel")))(cvec, ada_w, ada_b_cols)


def _ada_bwd(cvec, ada_w, dm):
    ly, D, cols = ada_w.shape
    tn = _tile(cols, 768, 128)

    def body(c_ref, w_ref, d_ref, gw_ref, gc_ref):
        first = jnp.logical_and(pl.program_id(0) == 0, pl.program_id(1) == 0)

        @pl.when(first)
        def _():
            gc_ref[...] = jnp.zeros_like(gc_ref)

        def f(cv, w):
            return _bdot(jax.nn.silu(cv), w)
        _, vjp = jax.vjp(f, c_ref[...], w_ref[...])
        dc, dw = vjp(d_ref[...])
        gw_ref[...] = dw
        gc_ref[...] += dc
    return pl.pallas_call(
        body, name="ada_bwd", grid=(ly, cols // tn),
        in_specs=[pl.BlockSpec((16, D), lambda l, j: (0, 0)), pl.BlockSpec((None, D, tn), lambda l, j: (l, 0, j)),
                  pl.BlockSpec((None, 16, tn), lambda l, j: (l, 0, j))],
        out_specs=[pl.BlockSpec((None, D, tn), lambda l, j: (l, 0, j)), pl.BlockSpec((16, D), lambda l, j: (0, 0))],
        out_shape=[SDS(ada_w.shape, f32), SDS((16, D), f32)], compiler_params=_params(("arbitrary", "arbitrary")))(cvec, ada_w, dm)


def _flat_adamw(g, w, m, v):
    n = w.shape[0]
    tn = _tile(n, 1024, 8)
    spec = pl.BlockSpec((tn, 128), lambda i: (i, 0))

    def body(g_ref, w_ref, m_ref, v_ref, do, mo, vo):
        do[...], mo[...], vo[...] = _adam(g_ref[...], w_ref[...], m_ref[...], v_ref[...])
    return pl.pallas_call(body, name="adamw_small", grid=(n // tn,), in_specs=[spec] * 4, out_specs=[spec] * 3,
                          out_shape=[SDS(w.shape, f32)] * 3, compiler_params=_params(("parallel",)))(g, w, m, v)


def _pack(arrs):
    flat = jnp.concatenate([a.reshape(-1).astype(f32) for a in arrs])
    n = _round_up(flat.shape[0], 512 * 128)
    return jnp.pad(flat, (0, n - flat.shape[0])).reshape(n // 128, 128)


def _unpack(flat, shapes, lead=()):
    flat = flat.reshape(lead + (-1,))
    out, off = [], 0
    for s in shapes:
        sz = math.prod(s)
        out.append(flat[..., off:off + sz].reshape(lead + tuple(s)))
        off += sz
    return out


def _unshard_last(g):
    g = jnp.moveaxis(g, 0, -2)
    return g.reshape(g.shape[:-2] + (g.shape[-2] * g.shape[-1],))


class _Dims:
    pass


def _rope_tables(L, C, NH):
    rows = L // GRID_W
    row = jnp.repeat(jnp.arange(rows), GRID_W).astype(f32)
    col = jnp.tile(jnp.arange(GRID_W), rows).astype(f32)
    half = HEAD_DIM // 2
    inv = ROPE_BASE ** (-jnp.arange(0, half, 2, dtype=f32) / half)
    ar, ac = row[:, None] * inv, col[:, None] * inv
    cos = jnp.concatenate([jnp.cos(ar), jnp.cos(ar), jnp.cos(ac), jnp.cos(ac)], axis=-1)
    sin = jnp.concatenate([-jnp.sin(ar), jnp.sin(ar), -jnp.sin(ac), jnp.sin(ac)], axis=-1)
    cos = jnp.concatenate([jnp.ones((C, HEAD_DIM), f32), cos], axis=0)
    sin = jnp.concatenate([jnp.zeros((C, HEAD_DIM), f32), sin], axis=0)
    return dict(cos=cos, sin=sin)


def kernel(x, c, ctx, c_ctx, ada_w, ada_b, norm_g, ffn1_w13, ffn1_w2, w_in, b_merge, rnn_conv_w, rnn_conv_b, lru_w_a, lru_b_a, lru_w_x, lru_b_x, lru_lambda, sc_conv_w, attn_sink, w_branch, w_out, ffn2_w13, ffn2_w2, final_norm_g, loss_target, m_c_ctx, m_ada_w, m_ada_b, m_norm_g, m_ffn1_w13, m_ffn1_w2, m_w_in, m_b_merge, m_rnn_conv_w, m_rnn_conv_b, m_lru_w_a, m_lru_b_a, m_lru_w_x, m_lru_b_x, m_lru_lambda, m_sc_conv_w, m_attn_sink, m_w_branch, m_w_out, m_ffn2_w13, m_ffn2_w2, m_final_norm_g, v_c_ctx, v_ada_w, v_ada_b, v_norm_g, v_ffn1_w13, v_ffn1_w2, v_w_in, v_b_merge, v_rnn_conv_w, v_rnn_conv_b, v_lru_w_a, v_lru_b_a, v_lru_w_x, v_lru_b_x, v_lru_lambda, v_sc_conv_w, v_attn_sink, v_w_branch, v_w_out, v_ffn2_w13, v_ffn2_w2, v_final_norm_g):
    d = _Dims()
    L, D = x.shape[1], x.shape[2]
    C = ctx.shape[1]
    LY = ada_w.shape[0]
    d.D, d.C, d.T = D, C, C + L
    d.F = ffn1_w2.shape[1] * N_DEV
    d.Fp = _round_up(d.F, 512)
    d.IN = w_in.shape[2] * N_DEV
    d.BW = w_branch.shape[2]
    d.NH = attn_sink.shape[1]
    d.NA = d.IN - N_BRANCH * D
    d.NKV = (d.NA - 5 * d.BW - d.NH * HEAD_DIM) // (2 * HEAD_DIM)
    d.G = d.NH // d.NKV
    d.NB = lru_w_a.shape[2]
    d.tr = _tile(C, 256, 8)
    d.trw = _tile(C, 128, 8)
    assert d.NH * HEAD_DIM == d.BW and L % GRID_W == 0 and C % ATT_BLOCK == 0 and L % ATT_BLOCK == 0
    sh13, sh2, shin, shd = 2 * d.F // N_DEV, d.F // N_DEV, d.IN // N_DEV, D // N_DEV
    cols9 = N_MOD * D // N_DEV
    me3 = _coords()
    me = _index(me3)

    sharded = [norm_g, b_merge, rnn_conv_w, lru_b_a, lru_b_x, lru_lambda, sc_conv_w]
    shapes1 = [a.shape for a in sharded] + [(D,)]
    g1 = _allgather("gather_small", _pack(sharded + [c.reshape(-1)]))
    parts = _unpack(g1, shapes1, (N_DEV,))
    norm_g_f, b_merge_f, conv_w_f, lru_b_a_f, lru_b_x_f, lru_lam_f, sc_w_f = [_unshard_last(p) for p in parts[:-1]]
    cvec = jnp.concatenate([parts[-1], c_ctx[None], jnp.zeros((7, D), f32)], axis=0)
    ada_b_cols = lax.dynamic_slice_in_dim(ada_b, me * cols9, cols9, axis=1)[:, None, :]
    modcols = _ada_fwd(cvec, ada_w, ada_b_cols)
    g2 = _allgather("gather_mod", modcols)
    modall = jnp.moveaxis(g2, 0, 2).reshape(LY, 16, N_MOD, D)
    mod_lat = lax.dynamic_index_in_dim(modall, me, axis=1, keepdims=False)
    modv = jnp.stack([modall[:, N_DEV], mod_lat], axis=1)

    assert LY == 2
    off13 = lambda j: (j // 4) * d.Fp + (j % 4) * sh13
    tr_ = lambda a: jnp.swapaxes(a, 1, 2)
    cast_t = lambda w: _cast_direct(tr_(w))
    c13a, c13b = cast_t(ffn1_w13), cast_t(ffn2_w13)
    wt = {"w13a": (c13a, off13, sh13, 2 * d.Fp, (D,), False), "w13b": (c13b, off13, sh13, 2 * d.Fp, (D,), False),
          "w2a": (_cast_direct(ffn1_w2), lambda j: j * sh2, sh2, d.Fp, (D,), False),
          "w2b": (_cast_direct(ffn2_w2), lambda j: j * sh2, sh2, d.Fp, (D,), False),
          "wint": (cast_t(w_in), lambda j: j * shin, shin, d.IN, (D,), False),
          "wbt": (_cast_transposed(w_branch), lambda j: j * shd, shd, D, (d.BW,), True),
          "wout": (_cast_direct(w_out), lambda j: j * shd, shd, D, (D,), False)}
    env = {}
    for name, (cw, off, sh, tot, rest, three) in wt.items():
        for l in range(LY):
            env[(name, l)] = lax.empty(((N_BRANCH, tot) if three else (tot,)) + rest, bf16)
        env[("r", name)] = lax.empty((LY, N_DEV) + ((N_BRANCH, sh) if three else (sh,)) + rest, bf16)
    zpad = jnp.zeros((max(d.Fp - d.F, 16), D), bf16)

    def part(sh, h):
        return (0, sh) if h is None else (h * (sh // 2), sh // 2)

    def rows_of(ref, start, n, three):
        sl = pl.ds(start if isinstance(start, int) else pl.multiple_of(start, 16), n)
        return ref.at[:, sl] if three else ref.at[sl]

    def add_gather(cm, name, l, h=None):
        cw, off, sh, tot, rest, three = wt[name]
        r0, n = part(sh, h)
        cm.add(cw, (name, l), lambda ref, me_, pi, r: rows_of(ref.at[l], r0, n, three),
               lambda ref, sender, r: rows_of(ref, off(sender) + r0, n, three))
        if d.Fp > d.F and h in (None, 0) and name[:2] in ("w1", "w2"):
            for base in ((0, d.Fp) if name[:3] == "w13" else (0,)):
                cm.fill(zpad, (name, l), lambda ref, base=base: ref.at[pl.ds(base + d.F, d.Fp - d.F)])

    def add_rs(cm, name, l, h=None):
        cw, off, sh, tot, rest, three = wt[name]
        r0, n = part(sh, h)
        cm.add(env[("g", name, l)], ("r", name), lambda ref, me_, pi, r: rows_of(ref, off(pi) + r0, n, three),
               lambda ref, sender, r: rows_of(ref.at[l, r], r0, n, three))

    plan = {"F0a.up": [("ag", "wint", 0)], "F0a.down": [("ag", "wbt", 0), ("ag", "wout", 0)],
            "F0m.in_a": [("ag", "w13b", 0, 0)], "F0m.in_g": [("ag", "w13b", 0, 1)],
            "F0b.up": [("ag", "w2b", 0), ("ag", "w13a", 1, 0)], "F0b.down": [("ag", "w13a", 1, 1)],
            "F1a.up": [("ag", "w2a", 1), ("ag", "wint", 1, 0)], "F1a.down": [("ag", "wint", 1, 1)],
            "F1m.in_a": [("ag", "wbt", 1), ("ag", "wout", 1)], "F1m.in_g": [("ag", "w13b", 1, 0)],
            "F1m.out": [("ag", "w13b", 1, 1)], "F1b.up": [("ag", "w2b", 1)], "B0m.in_x": [("lru",)]}
    for l in range(LY):
        plan.update({f"B{l}b.dw": [("rs", "w13b", l, 0)], f"B{l}b.ux": [("rs", "w13b", l, 1)],
                     f"B{l}m.in_w": [("rs", "wout", l), ("rs", "wbt", l), ("rs", "w2b", l)], f"B{l}a.dx": [("rs", "wint", l, 0)],
                     f"B{l}a.uw": [("rs", "wint", l, 1)], f"B{l}a.dw": [("rs", "w13a", l, 0)]})
    plan.update({"B1a.ux": [("rs", "w13a", 1, 1)], "B0b.dx": [("rs", "w2a", 1)], "B0a.ux": [("rs", "w13a", 0, 1), ("rs", "w2a", 0)]})

    def add_lru(cm):
        for nm in ("lru_w_a", "lru_w_x"):
            gs = jnp.stack([env[("gs", nm, l)] for l in range(LY)]).reshape(-1, 128)
            env[("all", nm)] = lax.empty((N_DEV,) + gs.shape, f32)
            cm.add(gs, ("all", nm), lambda ref, me_, pi, r: ref, lambda ref, sender, r: ref.at[sender])
    sc = _Sched(plan, env, {"ag": add_gather, "rs": add_rs, "lru": add_lru})
    first = _Comm(env)
    add_gather(first, "w13a", 0)
    add_gather(first, "w2a", 0)
    _exchange("gather_first", first)

    tabs = _rope_tables(L, C, d.NH)
    h = jnp.concatenate([ctx[0], x[0]], axis=0)
    Ps, saves = [], []
    for l in range(LY):
        sinkrows = jnp.repeat(attn_sink[l].reshape(d.NKV, d.G), ATT_BLOCK, axis=1)[:, :, None]
        Ps.append(dict(norm_g=norm_g_f[l], b_merge=b_merge_f[l], rnn_conv_w=conv_w_f[l], rnn_conv_b=rnn_conv_b[l][None],
                       lru_w_a=lru_w_a[l], lru_b_a=lru_b_a_f[l], lru_w_x=lru_w_x[l], lru_b_x=lru_b_x_f[l], lru_lambda=lru_lam_f[l],
                       sc_conv_w=sc_w_f[l], sinkrows=sinkrows))
    for l in range(LY):
        P = Ps[l]
        h, s1 = _ffn_fwd(h, modv[l], P["norm_g"][0:1], ("w13a", l), ("w2a", l), 0, d, sc, f"F{l}a.")
        h, s2 = _mixer_fwd(h, modv[l], P, l, tabs, d, sc, f"F{l}m.")
        h, s3 = _ffn_fwd(h, modv[l], P["norm_g"][2:3], ("w13b", l), ("w2b", l), 6, d, sc, f"F{l}b.")
        saves.append((s1, s2, s3))

    fng = final_norm_g[None]

    def loss_fn(t):
        def f(hv, g):
            y = hv * lax.rsqrt(jnp.mean(hv * hv, axis=-1, keepdims=True) + EPS) * g
            e = y - t.R[1]
            return 0.5 * jnp.sum(jnp.mean(e * e, axis=-1))
        lat = (t.seg == 1).astype(f32)
        val, (dhv, dg) = jax.value_and_grad(f, argnums=(0, 1))(t.R[0], t.K[0])
        return [dhv * lat], [dg * lat, jnp.full((1, 128), val * lat, f32)], []
    dh, dfng, lossv = _rowwise("loss", loss_fn, d.T, C, d.tr, rows=[(h, D, 0, 0), (loss_target[0], D, 0, -(C // d.tr))],
                               consts=[fng], outs=[(D, f32)], accs=[(1, D), (1, 128)])
    loss = lax.psum(lossv[0, 0], AXES)

    gl = [None] * LY
    for l in reversed(range(LY)):
        P = Ps[l]
        s1, s2, s3 = saves[l]
        dh, dg2, dss2, dgt2 = _ffn_bwd(dh, s3, modv[l], P["norm_g"][2:3], ("w13b", l), ("w2b", l), 6, d, sc, f"B{l}b.")
        dh, gm = _mixer_bwd(dh, s2, modv[l], P, l, tabs, d, sc, f"B{l}m.")
        dh, dg0, dss0, dgt0 = _ffn_bwd(dh, s1, modv[l], P["norm_g"][0:1], ("w13a", l), ("w2a", l), 0, d, sc, f"B{l}a.")
        gm.update(norm_g=jnp.concatenate([dg0, gm["norm_g1"], dg2], axis=0),
                  dmod=jnp.concatenate([dss0, dgt0, gm["dss"], gm["dgate"], dss2, dgt2], axis=1))
        gl[l] = gm
    grad_x = dh[C:][None]
    st = lambda k: jnp.stack([g[k] for g in gl])

    r13a, r2a, rin, rwb, rwo, r13b, r2b = [env[("r", k)] for k in ("w13a", "w2a", "wint", "wbt", "wout", "w13b", "w2b")]
    big = {}
    adamw_tr = lambda r, w, m, v: [tr_(o) for o in _adamw_direct(r, tr_(w), tr_(m), tr_(v), True)]
    big["ffn1_w13"] = adamw_tr(r13a, ffn1_w13, m_ffn1_w13, v_ffn1_w13)
    big["ffn2_w13"] = adamw_tr(r13b, ffn2_w13, m_ffn2_w13, v_ffn2_w13)
    big["w_in"] = adamw_tr(rin, w_in, m_w_in, v_w_in)
    big["w_branch"] = _adamw_transposed(rwb, w_branch, m_w_branch, v_w_branch)
    big["ffn1_w2"] = _adamw_direct(r2a, ffn1_w2, m_ffn1_w2, v_ffn1_w2, True)
    big["ffn2_w2"] = _adamw_direct(r2b, ffn2_w2, m_ffn2_w2, v_ffn2_w2, True)
    big["w_out"] = _adamw_direct(rwo, w_out, m_w_out, v_w_out, True)

    sink_g = jnp.stack([jnp.sum(g["sinkrows"].reshape(d.NKV, d.G, ATT_BLOCK), axis=-1).reshape(d.NH) for g in gl])
    small_full = dict(norm_g=st("norm_g"), b_merge=st("b_merge"), rnn_conv_w=st("rnn_conv_w"), rnn_conv_b=st("rnn_conv_b")[:, 0],
                      lru_b_a=st("lru_b_a"), lru_b_x=st("lru_b_x"), lru_lambda=st("lru_lambda"), sc_conv_w=st("sc_conv_w"), attn_sink=sink_g, final_norm_g=dfng[0])
    names_s = list(small_full)
    dmod = st("dmod")
    pk = _pack([small_full[k] for k in names_s] + [dmod])
    g4 = _allgather("gather_small_grads", pk)
    tot = _unpack(_sum8(g4), [small_full[k].shape for k in names_s] + [dmod.shape])
    sums = dict(zip(names_s, tot[:-1]))
    dmod_sum = tot[-1].reshape(LY, 2, N_MOD * D)
    dmod_all = _unpack(g4, [small_full[k].shape for k in names_s] + [dmod.shape], (N_DEV,))[-1].reshape(N_DEV, LY, 2, N_MOD * D)
    dm_rows = jnp.concatenate([jnp.moveaxis(dmod_all[:, :, 1], 0, 1), dmod_sum[:, 0:1], jnp.zeros((LY, 7, N_MOD * D), f32)], axis=1)
    dm_cols = lax.dynamic_slice_in_dim(dm_rows, me * cols9, cols9, axis=2)
    g_ada_w, dcv = _ada_bwd(cvec, ada_w, dm_cols)
    g5 = _allgather("gather_cctx", _pack([dcv[N_DEV]]))
    g_c_ctx = _sum8(g5).reshape(-1)[:D]
    g_ada_b = dmod_sum[:, 0] + dmod_sum[:, 1]
    ada_out = _adamw_direct(g_ada_w, ada_w, m_ada_w, v_ada_w, False)

    def shard_last(a, n):
        return lax.dynamic_slice_in_dim(a, me * n, n, axis=a.ndim - 1)
    local_g = dict(c_ctx=g_c_ctx, ada_b=g_ada_b, norm_g=shard_last(sums["norm_g"], shd), b_merge=shard_last(sums["b_merge"], shd),
                   rnn_conv_w=shard_last(sums["rnn_conv_w"], d.BW // N_DEV), rnn_conv_b=sums["rnn_conv_b"],
                   lru_b_a=shard_last(sums["lru_b_a"], d.BW // N_DEV), lru_b_x=shard_last(sums["lru_b_x"], d.BW // N_DEV),
                   lru_lambda=shard_last(sums["lru_lambda"], d.BW // N_DEV), sc_conv_w=shard_last(sums["sc_conv_w"], d.BW // N_DEV),
                   attn_sink=sums["attn_sink"], final_norm_g=sums["final_norm_g"])
    wmv = dict(c_ctx=(c_ctx, m_c_ctx, v_c_ctx), ada_b=(ada_b, m_ada_b, v_ada_b), norm_g=(norm_g, m_norm_g, v_norm_g),
               b_merge=(b_merge, m_b_merge, v_b_merge), rnn_conv_w=(rnn_conv_w, m_rnn_conv_w, v_rnn_conv_w),
               rnn_conv_b=(rnn_conv_b, m_rnn_conv_b, v_rnn_conv_b),
               lru_b_a=(lru_b_a, m_lru_b_a, v_lru_b_a), lru_b_x=(lru_b_x, m_lru_b_x, v_lru_b_x),
               lru_lambda=(lru_lambda, m_lru_lambda, v_lru_lambda), sc_conv_w=(sc_conv_w, m_sc_conv_w, v_sc_conv_w),
               attn_sink=(attn_sink, m_attn_sink, v_attn_sink), final_norm_g=(final_norm_g, m_final_norm_g, v_final_norm_g))
    names_l = list(local_g)
    shapes_l = [wmv[k][0].shape for k in names_l]
    gp = _pack([local_g[k].reshape(wmv[k][0].shape) for k in names_l])
    outs_s = _flat_adamw(gp, _pack([wmv[k][0] for k in names_l]), _pack([wmv[k][1] for k in names_l]), _pack([wmv[k][2] for k in names_l]))
    small = {k: [local_g[k].reshape(wmv[k][0].shape)] for k in names_l}
    for o in outs_s:
        for k, a in zip(names_l, _unpack(o, shapes_l)):
            small[k].append(a)
    for nm, (w_, m_, v_) in (("lru_w_a", (lru_w_a, m_lru_w_a, v_lru_w_a)), ("lru_w_x", (lru_w_x, m_lru_w_x, v_lru_w_x))):
        gsum = _sum8(env[("all", nm)])
        flat = lambda a: a.reshape(-1, 128)
        small[nm] = [a.reshape(w_.shape) for a in (gsum, *_flat_adamw(gsum, flat(w_), flat(m_), flat(v_)))]

    order = ["c_ctx", "ada_w", "ada_b", "norm_g", "ffn1_w13", "ffn1_w2", "w_in", "b_merge", "rnn_conv_w", "rnn_conv_b", "lru_w_a",
             "lru_b_a", "lru_w_x", "lru_b_x", "lru_lambda", "sc_conv_w", "attn_sink", "w_branch", "w_out", "ffn2_w13", "ffn2_w2",
             "final_norm_g"]
    allo = dict(small)
    allo.update(big)
    allo["ada_w"] = ada_out
    res = [loss, grad_x]
    for q in range(4):
        res += [allo[k][q] for k in order]
    return tuple(res)
```

```python
import functools
import math

import jax
import jax.numpy as jnp
from jax import lax
from jax.experimental import pallas as pl
from jax.experimental.pallas import tpu as pltpu

f32 = jnp.float32
bf16 = jnp.bfloat16
SDS = jax.ShapeDtypeStruct

N_DEV = 8
AXES = ("x", "y", "c")
HEAD_DIM = 128
GRID_W = 64
ATT_BLOCK = 128
HALO = 16
ROPE_BASE = 10000.0
LRU_C = 8.0
EPS = 1e-6
NEG_INF = -1e30
N_MOD = 9
N_BRANCH = 3
ADAM_LR, ADAM_B1, ADAM_B2, ADAM_EPS, ADAM_WD, ADAM_STEP = 0.001, 0.9, 0.999, 1e-08, 0.01, 10
VMEM_LIMIT = 56 * 1024 * 1024
ANY = pl.BlockSpec(memory_space=pl.ANY)


def _round_up(n, m):
    return (n + m - 1) // m * m


def _tile(n, target, align):
    best = None
    t = align
    while t <= min(n, target):
        if n % t == 0:
            best = t
        t += align
    return best if best is not None else n


def _params(sem):
    return pltpu.CompilerParams(dimension_semantics=sem, vmem_limit_bytes=VMEM_LIMIT)


_DIMS = {"nn": (((1,), (0,)), ((), ())), "nt": (((1,), (1,)), ((), ())), "tn": (((0,), (0,)), ((), ()))}


def _coords():
    return lax.axis_index("x"), lax.axis_index("y"), lax.axis_index("c")


def _peer(xyc, r):
    x, y, c = xyc
    return (1 - x if r & 4 else x, 1 - y if r & 2 else y, 1 - c if r & 1 else c)


def _index(xyc):
    return xyc[0] * 4 + xyc[1] * 2 + xyc[2]


class _Comm:
    def __init__(self, env):
        self.env, self.srcs, self.dsts, self.keys, self.items, self.fills, self.gathers = env, [], [], [], [], [], []

    def gather(self, src, key, src_view, dst_view):
        self.gathers.append((self._src(src), self._dst(key), src_view, dst_view))

    def _src(self, a):
        for q, b in enumerate(self.srcs):
            if b is a:
                return q
        self.srcs.append(a)
        return len(self.srcs) - 1

    def _dst(self, key):
        if key not in self.keys:
            self.keys.append(key)
            self.dsts.append(self.env[key])
        return self.keys.index(key)

    def add(self, src, key, src_view, dst_view):
        self.items.append((self._src(src), self._dst(key), src_view, dst_view))

    def fill(self, src, key, dst_view):
        self.fills.append((self._src(src), self._dst(key), dst_view))

    def scratch(self):
        n = len(self.items) + len(self.gathers)
        return [pltpu.SemaphoreType.DMA((7 * n,)), pltpu.SemaphoreType.DMA((7 * n,)), pltpu.SemaphoreType.DMA((n + len(self.fills),))]

    def store(self, arrays):
        for key, a in zip(self.keys, arrays):
            self.env[key] = a

    def descriptors(self, srcs, dsts, sems, me3):
        send, recv, lsem = sems
        me = _index(me3)
        local, sends, recvs = [], [], []
        for q, (si, di, sv, dv) in enumerate(self.items):
            local.append(pltpu.make_async_copy(sv(srcs[si], me, me, 0), dv(dsts[di], me, 0), lsem.at[q]))
            for r in range(1, N_DEV):
                p3 = _peer(me3, r)
                pi = _index(p3)
                kw = dict(send_sem=send.at[q * 7 + r - 1], recv_sem=recv.at[q * 7 + r - 1], device_id=p3,
                          device_id_type=pl.DeviceIdType.MESH)
                sends.append(pltpu.make_async_remote_copy(src_ref=sv(srcs[si], me, pi, r), dst_ref=dv(dsts[di], me, r), **kw))
                recvs.append(pltpu.make_async_remote_copy(src_ref=sv(srcs[si], me, pi, r), dst_ref=dv(dsts[di], pi, r), **kw))
        n1 = len(self.items)
        landed, passed = [], []
        sib = _peer(me3, 1)
        for q, (si, di, sv, dv) in enumerate(self.gathers):
            def rc(src_ref, dst_ref, k, dev, base=(n1 + q) * 7):
                return pltpu.make_async_remote_copy(src_ref=src_ref, dst_ref=dst_ref, send_sem=send.at[base + k - 1],
                                                    recv_sem=recv.at[base + k - 1], device_id=dev, device_id_type=pl.DeviceIdType.MESH)
            mine = sv(srcs[si])
            local.append(pltpu.make_async_copy(mine, dv(dsts[di], me), lsem.at[n1 + q]))
            for r in (1, 2, 4, 6):
                sends.append(rc(mine, dv(dsts[di], me), r, _peer(me3, r)))
            for r in (2, 4, 6):
                blk = dv(dsts[di], _index(_peer(me3, r)))
                landed.append(rc(mine, blk, r, _peer(me3, r)))
                passed.append(rc(blk, blk, r ^ 1, sib))
            for k in (1, 3, 5, 7):
                recvs.append(rc(mine, dv(dsts[di], _index(_peer(me3, k))), k, sib))
        for q, (si, di, dv) in enumerate(self.fills):
            local.append(pltpu.make_async_copy(srcs[si], dv(dsts[di]), lsem.at[n1 + len(self.gathers) + q]))
        return local, sends, recvs, landed, passed


def _comm_start(descs):
    for cp in descs[0] + descs[1]:
        cp.start()


def _comm_mid(descs):
    for cp in descs[3]:
        cp.wait_recv()
    for cp in descs[4]:
        cp.start()


def _comm_wait(descs):
    local, sends, recvs, _, passed = descs
    sends = sends + passed
    for cp in recvs:
        cp.wait_recv()
    for cp in sends:
        cp.wait_send()
    for cp in local:
        cp.wait()


def _carry_call(core, name, grid, in_specs, out_specs, out_shape, scratch, args, comm):
    ni, no, nsc, ng = len(in_specs), len(out_specs), len(scratch), len(grid)
    if comm is None:
        def plain(*refs):
            core(refs[:ni], refs[ni:ni + no], refs[ni + no:])
        return pl.pallas_call(plain, name=name, grid=grid, in_specs=in_specs, out_specs=out_specs, out_shape=out_shape,
                              scratch_shapes=scratch, compiler_params=_params(("parallel",) * (ng - 1) + ("arbitrary",)))(*args)
    ns, nd = len(comm.srcs), len(comm.dsts)
    o0 = ni + ns + nd

    def body(*refs):
        pid = [pl.program_id(q) for q in range(ng)]
        first, last = pid[0] == 0, pid[0] == grid[0] - 1
        for q in range(1, ng):
            first = jnp.logical_and(first, pid[q] == 0)
            last = jnp.logical_and(last, pid[q] == grid[q] - 1)
        me3 = _coords()
        mk = lambda: comm.descriptors(refs[ni:ni + ns], refs[o0 + no:o0 + no + nd], refs[o0 + no + nd + nsc:], me3)

        @pl.when(first)
        def _():
            _comm_start(mk())

        if comm.gathers:
            lin = pid[0]
            for q in range(1, ng):
                lin = lin * grid[q] + pid[q]

            @pl.when(lin == min(math.prod(grid) - 1, int(0.85 * math.prod(grid))))
            def _():
                _comm_mid(mk())

        core(refs[:ni], refs[o0:o0 + no], refs[o0 + no + nd:o0 + no + nd + nsc])

        @pl.when(last)
        def _():
            _comm_wait(mk())

    res = pl.pallas_call(
        body, name=name, grid=grid, in_specs=list(in_specs) + [ANY] * (ns + nd), out_specs=list(out_specs) + [ANY] * nd,
        out_shape=list(out_shape) + [SDS(x.shape, x.dtype) for x in comm.dsts], scratch_shapes=list(scratch) + comm.scratch(),
        input_output_aliases={ni + ns + q: no + q for q in range(nd)}, compiler_params=_params(("arbitrary",) * ng),
    )(*args, *comm.srcs, *comm.dsts)
    comm.store(res[no:])
    return res[:no]


def _mm(a, b, mode, out_dtype, name, *, M=None, N=None, K=None, a_off=0, b_off=0, b_lead=None,
        tm=768, tn=512, tk=2048, comm=None, a3=False, resid=None):
    b2 = b.shape[1:] if b_lead is not None else b.shape
    if mode == "tn":
        K = a.shape[-2] if K is None else K
        M = (a.shape[0] * a.shape[2] if a3 else a.shape[1]) if M is None else M
        N = b2[1] if N is None else N
    else:
        M = a.shape[-2] if M is None else M
        K = (a.shape[0] * a.shape[2] if a3 else a.shape[1]) if K is None else K
        N = (b2[1] if mode == "nn" else b2[0]) if N is None else N
    g = math.gcd
    if mode == "tn":
        tm = _tile(g(M, a_off) if a_off else (a.shape[2] if a3 else M), tm, 128)
        tk = _tile(K, tk, 16)
    else:
        tm = _tile(M, tm, 16)
        tk = _tile(g(K, a_off) if a_off else (a.shape[2] if a3 else K), tk, 128)
    tn = _tile(g(N, b_off) if b_off else N, tn, 128)
    nk = K // tk
    ao, bo = (a_off // (tm if mode == "tn" else tk)), b_off // tn
    lead = () if b_lead is None else (b_lead,)
    ld = () if b_lead is None else (None,)
    if mode == "nn":
        a_spec = pl.BlockSpec((tm, tk), lambda i, j, k: (i, ao + k))
        b_spec = pl.BlockSpec(ld + (tk, tn), lambda i, j, k: lead + (k, bo + j))
    elif mode == "nt":
        a_spec = pl.BlockSpec((tm, tk), lambda i, j, k: (i, ao + k))
        b_spec = pl.BlockSpec(ld + (tn, tk), lambda i, j, k: lead + (bo + j, k))
    else:
        a_spec = pl.BlockSpec((tk, tm), lambda i, j, k: (k, ao + i))
        b_spec = pl.BlockSpec(ld + (tk, tn), lambda i, j, k: lead + (k, bo + j))
    if a3:
        if mode == "nn":
            nkc = a.shape[2] // tk
            a_spec = pl.BlockSpec((None, tm, tk), lambda i, j, k: (k // nkc, i, k % nkc))
        else:
            nmc = a.shape[2] // tm
            a_spec = pl.BlockSpec((None, tk, tm), lambda i, j, k: (i // nmc, k, i % nmc))
    dims = _DIMS[mode]

    def finish(ins, outs, val):
        outs[0][...] = val.astype(out_dtype)
        if resid is not None:
            gi, coef, C = resid[2:]
            row = pl.program_id(0) * tm + lax.broadcasted_iota(jnp.int32, (tm, 1), 0)
            gate = jnp.where(row < C, ins[3][0, gi:gi + 1], ins[3][1, gi:gi + 1])
            outs[1][...] = ins[2][...] + coef * gate * val

    def core(ins, outs, scr):
        a_ref, b_ref = ins[0], ins[1]
        k = pl.program_id(2)
        if nk == 1:
            finish(ins, outs, lax.dot_general(a_ref[...], b_ref[...], dims, preferred_element_type=f32))
        else:
            acc = scr[0]

            @pl.when(k == 0)
            def _():
                acc[...] = jnp.zeros_like(acc)

            acc[...] += lax.dot_general(a_ref[...], b_ref[...], dims, preferred_element_type=f32)

            @pl.when(k == nk - 1)
            def _():
                finish(ins, outs, acc[...])

    o_spec = pl.BlockSpec((tm, tn), lambda i, j, k: (i, j))
    in_specs, out_specs, out_shape, args = [a_spec, b_spec], [o_spec], [SDS((M, N), out_dtype)], (a, b)
    if resid is not None:
        in_specs += [o_spec, pl.BlockSpec((2, N_MOD, tn), lambda i, j, k: (0, 0, j))]
        out_specs, out_shape, args = out_specs + [o_spec], out_shape + [SDS((M, N), f32)], args + tuple(resid[:2])
    res = _carry_call(core, name, (M // tm, N // tn, nk), in_specs, out_specs, out_shape,
                      [] if nk == 1 else [pltpu.VMEM((tm, tn), f32)], args, comm)
    return res[0] if resid is None else res


class _Tile:
    pass


def _rowwise(name, fn, T, C, tr, rows=(), halos=(), consts=(), mods=(), outs=(), accs=(), maccs=()):
    nT, nC = T // tr, C // tr
    assert T % tr == 0 and C % tr == 0 and nC >= 1 and tr % HALO == 0
    r8, n8 = tr // HALO, T // HALO
    seg_of = lambda i: jnp.where(i >= nC, 1, 0)
    in_specs, args = [], []
    for arr, w, cb, ro in rows:
        nt = arr.shape[0] // tr
        in_specs.append(pl.BlockSpec((tr, w), (lambda i, cb=cb, ro=ro, nt=nt: (jnp.clip(i + ro, 0, nt - 1), cb))))
        args.append(arr)
    for arr, w, cb in halos:
        in_specs.append(pl.BlockSpec((HALO, w), lambda i, cb=cb: (jnp.maximum(i * r8 - 1, 0), cb)))
        in_specs.append(pl.BlockSpec((tr, w), lambda i, cb=cb: (i, cb)))
        in_specs.append(pl.BlockSpec((HALO, w), lambda i, cb=cb: (jnp.minimum((i + 1) * r8, n8 - 1), cb)))
        args += [arr, arr, arr]
    for arr in consts:
        in_specs.append(pl.BlockSpec(arr.shape, lambda i, nd=arr.ndim: (0,) * nd))
        args.append(arr)
    for arr in mods:
        in_specs.append(pl.BlockSpec((None,) + arr.shape[1:], lambda i, nd=arr.ndim: (seg_of(i),) + (0,) * (nd - 1)))
        args.append(arr)
    out_specs, out_shape = [], []
    for w, dt in outs:
        out_specs.append(pl.BlockSpec((tr, w), lambda i: (i, 0)))
        out_shape.append(SDS((T, w), dt))
    for shp in accs:
        out_specs.append(pl.BlockSpec(shp, lambda i, nd=len(shp): (0,) * nd))
        out_shape.append(SDS(shp, f32))
    for shp in maccs:
        out_specs.append(pl.BlockSpec((None,) + shp, lambda i, nd=len(shp): (seg_of(i),) + (0,) * nd))
        out_shape.append(SDS((2,) + shp, f32))
    nr, nh, nk, nm, no, na, nma = len(rows), len(halos), len(consts), len(mods), len(outs), len(accs), len(maccs)

    def body(*refs):
        i = pl.program_id(0)
        t = _Tile()
        t.i, t.seg = i, seg_of(i)
        p = 0
        t.R = [refs[p + k][...].astype(f32) for k in range(nr)]
        p += nr
        pvalid = jnp.logical_and(i != 0, i != nC)
        nvalid = jnp.logical_and(i != nC - 1, i != nT - 1)
        t.H = []
        for k in range(nh):
            pr, cu, nx = refs[p][...].astype(f32), refs[p + 1][...].astype(f32), refs[p + 2][...].astype(f32)
            p += 3
            pr = jnp.where(pvalid, pr, jnp.zeros_like(pr))
            nx = jnp.where(nvalid, nx, jnp.zeros_like(nx))
            t.H.append(jnp.concatenate([pr, cu, nx], axis=0))
        t.K = [refs[p + k][...] for k in range(nk)]
        p += nk
        t.M = [refs[p + k][...] for k in range(nm)]
        p += nm
        o, a, ma = fn(t)
        for k in range(no):
            ref = refs[p + k]
            pieces = o[k] if isinstance(o[k], (list, tuple)) else [o[k]]
            c0 = 0
            for pc in pieces:
                ref[:, c0:c0 + pc.shape[1]] = pc.astype(ref.dtype)
                c0 += pc.shape[1]
        p += no
        for k in range(na):
            ref = refs[p + k]

            @pl.when(i == 0)
            def _(ref=ref):
                ref[...] = jnp.zeros_like(ref)

            ref[...] += a[k]
        p += na
        for k in range(nma):
            ref = refs[p + k]

            @pl.when(jnp.logical_or(i == 0, i == nC))
            def _(ref=ref):
                ref[...] = jnp.zeros_like(ref)

            ref[...] += ma[k]

    res = pl.pallas_call(
        body, name=name, grid=(nT,), in_specs=in_specs, out_specs=out_specs, out_shape=out_shape,
        compiler_params=_params(("arbitrary",)),
    )(*args)
    return res


@jax.custom_vjp
def _bdot(a, b):
    return jnp.dot(a.astype(bf16), b.astype(bf16), preferred_element_type=f32)


def _bdot_fwd(a, b):
    return _bdot(a, b), (a, b)


def _bdot_bwd(res, ct):
    a, b = res
    ctb = ct.astype(bf16)
    da = lax.dot_general(ctb, b.astype(bf16), _DIMS["nt"], preferred_element_type=f32)
    db = lax.dot_general(a.astype(bf16), ctb, _DIMS["tn"], preferred_element_type=f32)
    return da, db


_bdot.defvjp(_bdot_fwd, _bdot_bwd)


def _rms_mod(h, g, shift, scale):
    y = h * lax.rsqrt(jnp.mean(h * h, axis=-1, keepdims=True) + EPS) * g
    return y * (1.0 + scale) + shift


def _norm_mod_fwd(h, g, modv, s, d):
    def fn(t):
        m = t.M[0]
        return [_rms_mod(t.R[0], t.K[0], m[s:s + 1], m[s + 1:s + 2])], [], []
    return _rowwise("norm_mod", fn, d.T, d.C, d.tr, rows=[(h, d.D, 0, 0)], consts=[g], mods=[modv],
                    outs=[(d.D, bf16)])[0]


def _norm_mod_bwd(h, du, dh, g, modv, s, d, nxt=None):
    def fn(t):
        m = t.M[0]
        _, vjp = jax.vjp(_rms_mod, t.R[0], t.K[0], m[s:s + 1], m[s + 1:s + 2])
        dx, dg, dsh, dsc = vjp(t.R[1])
        dhn = t.R[2] + dx
        dss = jnp.concatenate([dsh, dsc], axis=0)
        if nxt is None:
            return [dhn], [dg], [dss]
        gi, coef = nxt[2:]
        return ([dhn, coef * t.M[1][gi:gi + 1] * dhn], [dg], [dss, jnp.sum(coef * dhn * t.R[3], axis=0, keepdims=True)])
    rows = [(h, d.D, 0, 0), (du, d.D, 0, 0), (dh, d.D, 0, 0)]
    if nxt is None:
        return _rowwise("norm_mod_bwd", fn, d.T, d.C, d.tr, rows=rows, consts=[g], mods=[modv], outs=[(d.D, f32)],
                        accs=[(1, d.D)], maccs=[(2, d.D)])
    dhn, do, dg, dss, dgate = _rowwise("norm_mod_bwd", fn, d.T, d.C, d.tr, rows=rows + [(nxt[0], d.D, 0, 0)], consts=[g],
                                       mods=[modv, nxt[1]], outs=[(d.D, f32), (d.D, bf16)], accs=[(1, d.D)], maccs=[(2, d.D), (1, d.D)])
    return dhn, dg, dss, (do, dgate)


def _swiglu(g, u):
    return jax.nn.silu(g) * u


def _resid_bwd(dh, o, modv, gi, coef, d):
    def fn(t):
        dhv = t.R[0]
        return [coef * t.M[0][gi:gi + 1] * dhv], [], [jnp.sum(coef * dhv * t.R[1], axis=0, keepdims=True)]
    return _rowwise("resid_bwd", fn, d.T, d.C, d.tr, rows=[(dh, d.D, 0, 0), (o, d.D, 0, 0)], mods=[modv],
                    outs=[(d.D, bf16)], maccs=[(1, d.D)])


class _Sched:
    def __init__(self, plan, env, builders):
        self.plan, self.env, self.builders = plan, env, builders

    def at(self, site):
        specs = self.plan.get(site)
        if not specs:
            return None
        cm = _Comm(self.env)
        for kind, *args in specs:
            self.builders[kind](cm, *args)
        return cm


def _ffn_up(u, w13t, d, comm):
    T, D, Fp = d.T, d.D, d.Fp
    tm, tn = _tile(T, 768, 16), _tile(Fp, 512, 128)

    def core(ins, outs, scr):
        uv, w_ref = ins[0][...], ins[1]
        gt = lax.dot_general(uv, w_ref[0], _DIMS["nt"], preferred_element_type=f32)
        up = lax.dot_general(uv, w_ref[1], _DIMS["nt"], preferred_element_type=f32)
        outs[0][0] = gt.astype(bf16)
        outs[0][1] = up.astype(bf16)
        outs[1][...] = _swiglu(gt, up).astype(bf16)

    return _carry_call(core, "ffn_up", (T // tm, Fp // tn),
                       [pl.BlockSpec((tm, D), lambda i, j: (i, 0)), pl.BlockSpec((2, tn, D), lambda i, j: (0, j, 0))],
                       [pl.BlockSpec((2, tm, tn), lambda i, j: (0, i, j)), pl.BlockSpec((tm, tn), lambda i, j: (i, j))],
                       [SDS((2, T, Fp), bf16), SDS((T, Fp), bf16)], [], (u, w13t.reshape(2, Fp, D)), comm)


def _ffn_down_dx(do, w2, gu, d, comm):
    T, D, Fp = d.T, d.D, d.Fp
    tm, tn = _tile(T, 768, 16), _tile(Fp, 512, 128)

    def core(ins, outs, scr):
        dhm = lax.dot_general(ins[0][...], ins[1][...], _DIMS["nt"], preferred_element_type=f32)
        _, vjp = jax.vjp(_swiglu, ins[2][0].astype(f32), ins[2][1].astype(f32))
        dg, du = vjp(dhm)
        outs[0][0] = dg.astype(bf16)
        outs[0][1] = du.astype(bf16)

    return _carry_call(core, "ffn_down_dx", (T // tm, Fp // tn),
                       [pl.BlockSpec((tm, D), lambda i, j: (i, 0)), pl.BlockSpec((tn, D), lambda i, j: (j, 0)),
                        pl.BlockSpec((2, tm, tn), lambda i, j: (0, i, j))],
                       [pl.BlockSpec((2, tm, tn), lambda i, j: (0, i, j))], [SDS((2, T, Fp), bf16)], [], (do, w2, gu), comm)[0]


def _ffn_fwd(h, modv, g, k13, k2, s, d, sc, site):
    u = _norm_mod_fwd(h, g, modv, s, d)
    gu, hmid = _ffn_up(u, sc.env[k13], d, sc.at(site + "up"))
    o, hn = _mm(hmid, sc.env[k2], "nn", bf16, "ffn_down", tn=1024, tk=2816, comm=sc.at(site + "down"),
                resid=(h, modv, s + 2, 0.5, d.C))
    return hn, dict(h=h, u=u, gu=gu, hmid=hmid, o=o)


def _ffn_bwd(dh, sv, modv, g, k13, k2, s, d, sc, site, pre=None, nxt=None):
    w13t, w2 = sc.env[k13], sc.env[k2]
    do, dgate = pre if pre is not None else _resid_bwd(dh, sv["o"], modv, s + 2, 0.5, d)
    dgu = _ffn_down_dx(do, w2, sv["gu"], d, sc.at(site + "dx"))
    sc.env[("g",) + k13] = _mm(dgu, sv["u"], "tn", bf16, "ffn_up_dw", tm=2816, tn=1024, tk=768, comm=sc.at(site + "uw"), a3=True)
    sc.env[("g",) + k2] = _mm(sv["hmid"], do, "tn", bf16, "ffn_down_dw", tm=2816, tn=1024, tk=768, comm=sc.at(site + "dw"))
    du = _mm(dgu, w13t, "nn", f32, "ffn_up_dx", tn=1024, tk=2816, comm=sc.at(site + "ux"), a3=True)
    res = _norm_mod_bwd(sv["h"], du, dh, g, modv, s, d, nxt)
    return res[0], res[1], res[2], dgate, (res[3] if nxt is not None else None)


def _taps(ext, w, left, tr, sign):
    acc = None
    for k in range(w.shape[0]):
        o = HALO + sign * (k - left)
        term = w[k:k + 1] * ext[o:o + tr]
        acc = term if acc is None else acc + term
    return acc


def _gates(xa, wa, ba, wx, bx, lam, nb):
    bs = xa.shape[1] // nb
    out = []
    for dr in range(2):
        pa = jnp.concatenate([_bdot(xa[:, n * bs:(n + 1) * bs], wa[dr, n]) for n in range(nb)], axis=1)
        px = jnp.concatenate([_bdot(xa[:, n * bs:(n + 1) * bs], wx[dr, n]) for n in range(nb)], axis=1)
        rg = jax.nn.sigmoid(pa + ba[dr:dr + 1])
        ig = jax.nn.sigmoid(px + bx[dr:dr + 1])
        log_a = -LRU_C * rg * jax.nn.softplus(-lam[dr:dr + 1])
        a = jnp.exp(log_a)
        u = jnp.sqrt(1.0 - jnp.exp(2.0 * log_a)) * (ig * xa)
        out += [a, u]
    return out


def _combine_a(hf, hb, rg):
    return (hf + hb) * jax.nn.gelu(rg)


def _scan_order(kind, nT, nC):
    nL = nT - nC
    if kind == "F":
        return (lambda s: s), False
    if kind == "revF":
        return (lambda s: nT - 1 - s), True
    if kind == "B":
        return (lambda s: jnp.where(s < nC, nC - 1 - s, nT - 1 - (s - nC))), True
    return (lambda s: jnp.where(s < nL, nC + s, s - nL)), False


def _scan_fwd(au, dr, d):
    R, tc = d.BW, d.tr
    tile_of, down = _scan_order("F" if dr == 0 else "B", d.T // tc, d.C // tc)

    def body(a_ref, u_ref, h_ref, hp_ref, st):
        @pl.when(pl.program_id(0) == 0)
        def _():
            st[...] = jnp.zeros_like(st)

        def grp(gi, h):
            r = pl.multiple_of((tc // 8 - 1 - gi if down else gi) * 8, 8)
            at, ut = a_ref[pl.ds(r, 8), :], u_ref[pl.ds(r, 8), :]
            hs, hps = [None] * 8, [None] * 8
            for k in (range(7, -1, -1) if down else range(8)):
                hps[k] = h
                h = at[k:k + 1] * h + ut[k:k + 1]
                hs[k] = h
            h_ref[pl.ds(r, 8), :] = jnp.concatenate(hs, axis=0)
            hp_ref[pl.ds(r, 8), :] = jnp.concatenate(hps, axis=0)
            return h
        st[...] = lax.fori_loop(0, tc // 8, grp, st[...])

    return pl.pallas_call(
        body, name="lru_scan", grid=(d.T // tc,),
        in_specs=[pl.BlockSpec((tc, R), lambda s: (tile_of(s), 2 * dr)), pl.BlockSpec((tc, R), lambda s: (tile_of(s), 2 * dr + 1))],
        out_specs=[pl.BlockSpec((tc, R), lambda s: (tile_of(s), 0))] * 2, out_shape=[SDS((d.T, R), f32)] * 2,
        scratch_shapes=[pltpu.VMEM((1, R), f32)], compiler_params=_params(("arbitrary",)),
    )(au, au)


def _scan_bwd(dh, au, hp, dr, d):
    R, tc = d.BW, d.tr
    tile_of, down = _scan_order("revF" if dr == 0 else "revB", d.T // tc, d.C // tc)

    def body(dh_ref, a_ref, hp_ref, o_ref, st):
        @pl.when(pl.program_id(0) == 0)
        def _():
            st[...] = jnp.zeros_like(st)

        def grp(gi, carry):
            lam, an = carry
            r = pl.multiple_of((tc // 8 - 1 - gi if down else gi) * 8, 8)
            at, dt, ht = a_ref[pl.ds(r, 8), :], dh_ref[pl.ds(r, 8), :], hp_ref[pl.ds(r, 8), :]
            ls = [None] * 8
            for k in (range(7, -1, -1) if down else range(8)):
                lam = dt[k:k + 1] + an * lam
                an = at[k:k + 1]
                ls[k] = lam
            lt = jnp.concatenate(ls, axis=0)
            o_ref[pl.ds(r, 8), 0:R] = lt * ht
            o_ref[pl.ds(r, 8), R:2 * R] = lt
            return lam, an
        lam, an = lax.fori_loop(0, tc // 8, grp, (st[0:1], st[1:2]))
        st[0:1] = lam
        st[1:2] = an

    return pl.pallas_call(
        body, name="lru_scan_bwd", grid=(d.T // tc,),
        in_specs=[pl.BlockSpec((tc, R), lambda s: (tile_of(s), 0)), pl.BlockSpec((tc, R), lambda s: (tile_of(s), 2 * dr)),
                  pl.BlockSpec((tc, R), lambda s: (tile_of(s), 0))],
        out_specs=pl.BlockSpec((tc, 2 * R), lambda s: (tile_of(s), 0)), out_shape=SDS((d.T, 2 * R), f32),
        scratch_shapes=[pltpu.VMEM((2, R), f32)], compiler_params=_params(("arbitrary",)),
    )(dh, au, hp)


def _swap_pairs(x):
    w = x.shape[1]
    lane = lax.broadcasted_iota(jnp.int32, x.shape, 1)
    return jnp.where(lane % 64 < 32, pltpu.roll(x, w - 32, 1), pltpu.roll(x, 32, 1))


def _att_masks(blk, nB, nCb, G):
    rows, cols = G * ATT_BLOCK, 3 * ATT_BLOCK
    qi = lax.broadcasted_iota(jnp.int32, (rows, cols), 0) % ATT_BLOCK
    kj = lax.broadcasted_iota(jnp.int32, (rows, cols), 1)
    rel = kj - ATT_BLOCK - qi
    kb = kj // ATT_BLOCK
    one = jnp.int32(1)
    latent = jnp.where(blk >= nCb, one, 0)
    prev_ok = jnp.where(blk - 1 >= nCb, latent, 0)
    next_ok = jnp.where(blk + 1 <= nB - 1, latent, 0)
    bv = jnp.where(kb == 0, prev_ok, jnp.where(kb == 1, latent, next_ok))
    return jnp.logical_and(jnp.abs(rel) <= ATT_BLOCK, bv > 0)


def _att_specs(d):
    G, nB = d.G, d.T // ATT_BLOCK
    blk = lambda f: pl.BlockSpec((ATT_BLOCK, HEAD_DIM), lambda kh, b: (f(b), kh))
    three = [blk(lambda b: jnp.maximum(b - 1, 0)), blk(lambda b: b), blk(lambda b: jnp.minimum(b + 1, nB - 1))]
    ctxs = pl.BlockSpec((d.C, HEAD_DIM), lambda kh, b: (0, kh))
    qs = pl.BlockSpec((ATT_BLOCK, G * HEAD_DIM), lambda kh, b: (b, kh))
    sk = pl.BlockSpec((None, G * ATT_BLOCK, 1), lambda kh, b: (kh, 0, 0))
    return qs, three, ctxs, sk


def _stack_heads(x, G):
    return jnp.concatenate([x[:, g * HEAD_DIM:(g + 1) * HEAD_DIM] for g in range(G)], axis=0)


def _unstack_heads(x, G):
    return jnp.concatenate([x[g * ATT_BLOCK:(g + 1) * ATT_BLOCK] for g in range(G)], axis=1)


def _att_probs(qg, kcat, kctx, sink, valid):
    scale = HEAD_DIM ** -0.5
    s = lax.dot_general(qg, kcat, _DIMS["nt"], preferred_element_type=f32) * scale
    s = jnp.where(valid, s, NEG_INF)
    sc = lax.dot_general(qg, kctx, _DIMS["nt"], preferred_element_type=f32) * scale
    m = jnp.maximum(jnp.maximum(jnp.max(s, axis=1, keepdims=True), jnp.max(sc, axis=1, keepdims=True)), sink)
    e, ec, es = jnp.exp(s - m), jnp.exp(sc - m), jnp.exp(sink - m)
    inv = 1.0 / (jnp.sum(e, axis=1, keepdims=True) + jnp.sum(ec, axis=1, keepdims=True) + es)
    return e * inv, ec * inv, es * inv


def _attn_fwd(qr, kr, vb, sinkrows, d):
    G, nB, nCb = d.G, d.T // ATT_BLOCK, d.C // ATT_BLOCK
    qs, three, ctxs, sk = _att_specs(d)

    def body(q_ref, k0, k1, k2, v0, v1, v2, kc_ref, vc_ref, s_ref, o_ref):
        b = pl.program_id(1)
        qg = _stack_heads(q_ref[...], G)
        kcat = jnp.concatenate([k0[...], k1[...], k2[...]], axis=0)
        vcat = jnp.concatenate([v0[...], v1[...], v2[...]], axis=0)
        p, pc, _ = _att_probs(qg, kcat, kc_ref[...], s_ref[...], _att_masks(b, nB, nCb, G))
        o = jnp.dot(p.astype(bf16), vcat, preferred_element_type=f32) + jnp.dot(pc.astype(bf16), vc_ref[...], preferred_element_type=f32)
        o_ref[...] = _unstack_heads(o, G).astype(bf16)

    return pl.pallas_call(
        body, name="attn", grid=(d.NKV, nB), in_specs=[qs] + three + three + [ctxs, ctxs, sk],
        out_specs=qs, out_shape=SDS((d.T, d.NH * HEAD_DIM), bf16), compiler_params=_params(("parallel", "arbitrary")),
    )(qr, kr, kr, kr, vb, vb, vb, kr, vb, sinkrows)


def _attn_bwd(qr, kr, vb, sinkrows, dy, d):
    G, nB, nCb = d.G, d.T // ATT_BLOCK, d.C // ATT_BLOCK
    qs, three, ctxs, sk = _att_specs(d)
    KW = d.NKV * HEAD_DIM
    part = pl.BlockSpec((ATT_BLOCK, HEAD_DIM), lambda kh, b: (b, kh))

    def body(q_ref, k0, k1, k2, v0, v1, v2, kc_ref, vc_ref, s_ref, dy_ref,
             dq_ref, dk0, dk1, dk2, dv0, dv1, dv2, dkc_ref, dvc_ref, ds_ref):
        b = pl.program_id(1)
        scale = HEAD_DIM ** -0.5
        qg = _stack_heads(q_ref[...], G)
        kcat = jnp.concatenate([k0[...], k1[...], k2[...]], axis=0)
        vcat = jnp.concatenate([v0[...], v1[...], v2[...]], axis=0)
        kctx, vctx = kc_ref[...], vc_ref[...]
        p, pc, ps = _att_probs(qg, kcat, kctx, s_ref[...], _att_masks(b, nB, nCb, G))
        dog = _stack_heads(dy_ref[...], G).astype(bf16)
        dp = lax.dot_general(dog, vcat, _DIMS["nt"], preferred_element_type=f32)
        dpc = lax.dot_general(dog, vctx, _DIMS["nt"], preferred_element_type=f32)
        delta = jnp.sum(p * dp, axis=1, keepdims=True) + jnp.sum(pc * dpc, axis=1, keepdims=True)
        ds = (p * (dp - delta) * scale).astype(bf16)
        dsc = (pc * (dpc - delta) * scale).astype(bf16)
        dq = jnp.dot(ds, kcat, preferred_element_type=f32) + jnp.dot(dsc, kctx, preferred_element_type=f32)
        dq_ref[...] = _unstack_heads(dq, G)
        dk = lax.dot_general(ds, qg, _DIMS["tn"], preferred_element_type=f32)
        dv = lax.dot_general(p.astype(bf16), dog, _DIMS["tn"], preferred_element_type=f32)
        for j, (rk, rv) in enumerate(((dk0, dv0), (dk1, dv1), (dk2, dv2))):
            rk[...] = dk[j * ATT_BLOCK:(j + 1) * ATT_BLOCK]
            rv[...] = dv[j * ATT_BLOCK:(j + 1) * ATT_BLOCK]

        @pl.when(b == 0)
        def _():
            dkc_ref[...] = jnp.zeros_like(dkc_ref)
            dvc_ref[...] = jnp.zeros_like(dvc_ref)
            ds_ref[...] = jnp.zeros_like(ds_ref)

        dkc_ref[...] += lax.dot_general(dsc, qg, _DIMS["tn"], preferred_element_type=f32)
        dvc_ref[...] += lax.dot_general(pc.astype(bf16), dog, _DIMS["tn"], preferred_element_type=f32)
        ds_ref[...] += -ps * delta

    kv = SDS((d.T, KW), f32)
    return pl.pallas_call(
        body, name="attn_bwd", grid=(d.NKV, nB), in_specs=[qs] + three + three + [ctxs, ctxs, sk, qs],
        out_specs=[qs] + [part] * 6 + [ctxs, ctxs, sk],
        out_shape=[SDS((d.T, d.NH * HEAD_DIM), f32)] + [kv] * 6 + [SDS((d.C, KW), f32)] * 2 + [SDS((d.NKV, G * ATT_BLOCK, 1), f32)],
        compiler_params=_params(("parallel", "arbitrary")),
    )(qr, kr, kr, kr, vb, vb, vb, kr, vb, sinkrows, dy)


def _mixer_fwd(h, modv, P, l, tabs, d, sc, site):
    D, BW, T, C = d.D, d.BW, d.T, d.C
    u = _norm_mod_fwd(h, P["norm_g"][1:2], modv, 3, d)
    za = _mm(u, sc.env[("wint", l)], "nt", bf16, "in_proj_a", N=d.NA, comm=sc.at(site + "in_a"))
    gz = _mm(u, sc.env[("wint", l)], "nt", bf16, "in_proj_g", N=3 * D, b_off=d.NA, comm=sc.at(site + "in_g"))
    xa = _rowwise("lru_conv", lambda t: ([_taps(t.H[0], t.K[0], 2, d.tr, 1) + t.K[1]], [], []), T, C, d.tr,
                  halos=[(za, BW, 0)], consts=[P["rnn_conv_w"], P["rnn_conv_b"]], outs=[(BW, f32)])[0]
    lru = [P["lru_w_a"], P["lru_b_a"], P["lru_w_x"], P["lru_b_x"], P["lru_lambda"]]
    au = _rowwise("lru_gates", lambda t: ([_gates(t.R[0], *t.K, d.NB)], [], []), T, C, d.tr,
                  rows=[(xa, BW, 0, 0)], consts=lru, outs=[(4 * BW, f32)])[0]
    hf, hpf = _scan_fwd(au, 0, d)
    hb, hpb = _scan_fwd(au, 1, d)
    ya = _rowwise("lru_out", lambda t: ([_combine_a(*t.R)], [], []), T, C, d.tr,
                  rows=[(hf, BW, 0, 0), (hb, BW, 0, 0), (za, BW, 1, 0)], outs=[(BW, bf16)])[0]
    yb = _rowwise("sconv", lambda t: ([t.R[0] * _taps(t.H[0] * t.H[1], t.K[0], 1, d.tr, 1)], [], []), T, C, d.tr,
                  rows=[(za, BW, 2, 0)], halos=[(za, BW, 3), (za, BW, 4)], consts=[P["sc_conv_w"]], outs=[(BW, bf16)])[0]
    QW, KW = d.NH * HEAD_DIM, d.NKV * HEAD_DIM

    def rope(t):
        q, k, v, cs, sn = t.R
        cq, sq, ck, skn = jnp.tile(cs, (1, d.NH)), jnp.tile(sn, (1, d.NH)), jnp.tile(cs, (1, d.NKV)), jnp.tile(sn, (1, d.NKV))
        return [q * cq + _swap_pairs(q) * sq, k * ck + _swap_pairs(k) * skn, v], [], []
    kcb = (5 * BW + QW) // KW
    qr, kr, vb = _rowwise("rope", rope, T, C, d.tr,
                          rows=[(za, QW, (5 * BW) // QW, 0), (za, KW, kcb, 0), (za, KW, kcb + 1, 0),
                                (tabs["cos"], HEAD_DIM, 0, 0), (tabs["sin"], HEAD_DIM, 0, 0)],
                          outs=[(QW, bf16), (KW, bf16), (KW, bf16)])
    yatt = _attn_fwd(qr, kr, vb, P["sinkrows"], d)
    ys = (ya, yb, yatt)
    ps = [_mm(ys[i], sc.env[("wbt", l)], "nt", bf16, "lift", b_lead=i, tn=1024) for i in range(N_BRANCH)]

    def merge(t):
        gzv, bm = t.R[0], t.K[0]
        acc = None
        for i in range(N_BRANCH):
            term = jax.nn.sigmoid(gzv[:, i * D:(i + 1) * D] + bm[i:i + 1]) * t.R[1 + i]
            acc = term if acc is None else acc + term
        return [acc], [], []
    merged = _rowwise("merge", merge, T, C, d.trw, rows=[(gz, 3 * D, 0, 0)] + [(p, D, 0, 0) for p in ps],
                      consts=[P["b_merge"]], outs=[(D, bf16)])[0]
    y, hn = _mm(merged, sc.env[("wout", l)], "nn", bf16, "out_proj", tn=1024, comm=sc.at(site + "out"),
                resid=(h, modv, 5, 1.0, d.C))
    sv = dict(h=h, u=u, za=za, gz=gz, xa=xa, au=au, hf=hf, hpf=hpf, hb=hb, hpb=hpb, ys=ys, qr=qr, kr=kr, vb=vb,
              ps=ps, merged=merged, y=y)
    return hn, sv


def _mixer_bwd(dh, sv, modv, P, l, tabs, d, sc, site, pre=None, nxt=None):
    W = dict(wint=sc.env[("wint", l)], wbt=sc.env[("wbt", l)], wout=sc.env[("wout", l)])
    D, BW, T, C = d.D, d.BW, d.T, d.C
    QW, KW = d.NH * HEAD_DIM, d.NKV * HEAD_DIM
    za, gz = sv["za"], sv["gz"]
    dy, dgate = pre if pre is not None else _resid_bwd(dh, sv["y"], modv, 5, 1.0, d)
    dmerged = _mm(dy, W["wout"], "nt", f32, "out_proj_dx", tn=1024)
    sc.env[("g", "wout", l)] = _mm(sv["merged"], dy, "tn", bf16, "out_proj_dw", tm=1024, tn=2048, tk=768)

    def merge_bwd(t):
        gzv, bm, dm = t.R[0], t.K[0], t.R[4]
        dps, dgs, dbs = [], [], []
        for i in range(N_BRANCH):
            gate = jax.nn.sigmoid(gzv[:, i * D:(i + 1) * D] + bm[i:i + 1])
            dps.append(dm * gate)
            dgi = dm * t.R[1 + i] * gate * (1.0 - gate)
            dgs.append(dgi)
            dbs.append(jnp.sum(dgi, axis=0, keepdims=True))
        return [dps, dgs], [jnp.concatenate(dbs, axis=0)], []
    dp, dgz, dbm = _rowwise("merge_bwd", merge_bwd, T, C, d.trw,
                            rows=[(gz, 3 * D, 0, 0)] + [(p, D, 0, 0) for p in sv["ps"]] + [(dmerged, D, 0, 0)],
                            consts=[P["b_merge"]], outs=[(3 * D, bf16), (3 * D, bf16)], accs=[(N_BRANCH, D)])
    dys = [_mm(dp, W["wbt"], "nn", f32, "lift_dx", K=D, a_off=i * D, b_lead=i, tn=1024) for i in range(N_BRANCH)]
    sc.env[("g", "wbt", l)] = jnp.stack([_mm(dp, sv["ys"][i], "tn", bf16, "lift_dw", M=D, a_off=i * D, tm=1024, tn=1024, tk=768)
                                         for i in range(N_BRANCH)])
    def out_bwd(t):
        _, vjp = jax.vjp(_combine_a, t.R[1], t.R[2], t.R[3])
        dhf, _, drg = vjp(t.R[0])
        return [dhf, drg], [], []
    dhs, drg = _rowwise("lru_out_bwd", out_bwd, T, C, d.tr,
                        rows=[(dys[0], BW, 0, 0), (sv["hf"], BW, 0, 0), (sv["hb"], BW, 0, 0), (za, BW, 1, 0)],
                        outs=[(BW, f32), (BW, bf16)])
    dau0 = _scan_bwd(dhs, sv["au"], sv["hpf"], 0, d)
    dau1 = _scan_bwd(dhs, sv["au"], sv["hpb"], 1, d)
    lru = [P["lru_w_a"], P["lru_b_a"], P["lru_w_x"], P["lru_b_x"], P["lru_lambda"]]

    def gates_bwd(t):
        _, vjp = jax.vjp(lambda xa, *k: _gates(xa, *k, d.NB), t.R[0], *t.K)
        d0, d1 = t.R[1], t.R[2]
        g = vjp([d0[:, :BW], d0[:, BW:], d1[:, :BW], d1[:, BW:]])
        return [g[0]], list(g[1:]), []
    dxa, dwa, dba, dwx, dbx, dlam = _rowwise(
        "lru_gates_bwd", gates_bwd, T, C, d.tr, rows=[(sv["xa"], BW, 0, 0), (dau0, 2 * BW, 0, 0), (dau1, 2 * BW, 0, 0)],
        consts=lru, outs=[(BW, f32)], accs=[p.shape for p in lru])
    sc.env[("gs", "lru_w_a", l)], sc.env[("gs", "lru_w_x", l)] = dwa, dwx

    def conv_bwd(t):
        dxe, xe, w = t.H[0], t.H[1], t.K[0]
        cur = dxe[HALO:HALO + d.tr]
        dw = jnp.concatenate([jnp.sum(cur * xe[HALO + k - 2:HALO + k - 2 + d.tr], axis=0, keepdims=True) for k in range(w.shape[0])], axis=0)
        return [_taps(dxe, w, 2, d.tr, -1)], [dw, jnp.sum(cur, axis=0, keepdims=True)], []
    drx, dcw, dcb = _rowwise("lru_conv_bwd", conv_bwd, T, C, d.tr, halos=[(dxa, BW, 0), (za, BW, 0)],
                             consts=[P["rnn_conv_w"]], outs=[(BW, bf16)], accs=[P["rnn_conv_w"].shape, (1, BW)])
    def sconv_bwd(t):
        scg, sx, sb, dyb = t.H
        w, tr = t.K[0], d.tr
        me = scg * sx
        dsb = dyb[HALO:HALO + tr] * _taps(me, w, 1, tr, 1)
        dce = dyb * sb
        dm = _taps(dce, w, 1, tr, -1)
        cur = dce[HALO:HALO + tr]
        dw = jnp.concatenate([jnp.sum(cur * me[HALO + k - 1:HALO + k - 1 + tr], axis=0, keepdims=True) for k in range(w.shape[0])], axis=0)
        return [[dsb, dm * sx[HALO:HALO + tr], dm * scg[HALO:HALO + tr]]], [dw], []
    dsc, dscw = _rowwise("sconv_bwd", sconv_bwd, T, C, d.tr, halos=[(za, BW, 3), (za, BW, 4), (za, BW, 2), (dys[1], BW, 0)],
                         consts=[P["sc_conv_w"]], outs=[(3 * BW, bf16)], accs=[P["sc_conv_w"].shape])
    dqr, dk0, dk1, dk2, dv0, dv1, dv2, dkc, dvc, dsink = _attn_bwd(sv["qr"], sv["kr"], sv["vb"], P["sinkrows"], dys[2], d)
    nB, nCb = T // ATT_BLOCK, C // ATT_BLOCK

    def att_join(t):
        dq, a1, a0, a2, b1, b0, b2, kc, vc, cs, sn = t.R
        cq, sq, ck, skn = jnp.tile(cs, (1, d.NH)), jnp.tile(sn, (1, d.NH)), jnp.tile(cs, (1, d.NKV)), jnp.tile(sn, (1, d.NKV))
        up = t.i + 1 <= nB - 1
        dn = t.i >= 1
        isc = t.seg == 0
        dk = a1 + jnp.where(up, a0, 0.0) + jnp.where(dn, a2, 0.0) + jnp.where(isc, kc, 0.0)
        dv = b1 + jnp.where(up, b0, 0.0) + jnp.where(dn, b2, 0.0) + jnp.where(isc, vc, 0.0)
        return [[dq * cq + _swap_pairs(dq * sq), dk * ck + _swap_pairs(dk * skn), dv]], [], []
    dqkv = _rowwise("attn_join", att_join, T, C, ATT_BLOCK,
                    rows=[(dqr, QW, 0, 0), (dk1, KW, 0, 0), (dk0, KW, 0, 1), (dk2, KW, 0, -1), (dv1, KW, 0, 0), (dv0, KW, 0, 1),
                          (dv2, KW, 0, -1), (dkc, KW, 0, 0), (dvc, KW, 0, 0), (tabs["cos"], HEAD_DIM, 0, 0),
                          (tabs["sin"], HEAD_DIM, 0, 0)], outs=[(QW + 2 * KW, bf16)])[0]
    dz = jnp.concatenate([drx, drg, dsc, dqkv, dgz], axis=1)
    du = _mm(dz, W["wint"], "nn", f32, "in_proj_dx", tn=1024, tk=3200, comm=sc.at(site + "in_x"))
    sc.env[("g", "wint", l)] = _mm(dz, sv["u"], "tn", bf16, "in_proj_dw", tm=2560, tn=1024, tk=768, comm=sc.at(site + "in_w"))
    res = _norm_mod_bwd(sv["h"], du, dh, P["norm_g"][1:2], modv, 3, d, nxt)
    grads = dict(norm_g1=res[1], dss=res[2], dgate=dgate, b_merge=dbm, rnn_conv_w=dcw, rnn_conv_b=dcb,
                 lru_b_a=dba, lru_b_x=dbx, lru_lambda=dlam, sc_conv_w=dscw, sinkrows=dsink)
    return res[0], grads, (res[3] if nxt is not None else None)


def _exchange(name, comm):
    ns, nd = len(comm.srcs), len(comm.dsts)

    def body(*refs):
        descs = comm.descriptors(refs[:ns], refs[ns + nd:ns + 2 * nd], refs[ns + 2 * nd:], _coords())
        _comm_start(descs)
        _comm_mid(descs)
        _comm_wait(descs)

    res = pl.pallas_call(
        body, name=name, in_specs=[ANY] * (ns + nd), out_specs=[ANY] * nd, out_shape=[SDS(a.shape, a.dtype) for a in comm.dsts],
        input_output_aliases={ns + q: q for q in range(nd)}, scratch_shapes=comm.scratch(),
    )(*comm.srcs, *comm.dsts)
    comm.store(res)


def _allgather(name, a):
    env = {"g": lax.empty((N_DEV,) + a.shape, a.dtype)}
    cm = _Comm(env)
    cm.add(a, "g", lambda ref, me, pi, r: ref, lambda ref, sender, r: ref.at[sender])
    _exchange(name, cm)
    return env["g"]


def _sum8(x):
    n = x.shape[1]
    tn = _tile(n, 2048, 8)

    def body(x_ref, o_ref):
        acc = x_ref[0]
        for j in range(1, N_DEV):
            acc = acc + x_ref[j]
        o_ref[...] = acc
    return pl.pallas_call(body, name="sum8", grid=(n // tn,), in_specs=[pl.BlockSpec((N_DEV, tn, 128), lambda i: (0, i, 0))],
                          out_specs=pl.BlockSpec((tn, 128), lambda i: (i, 0)), out_shape=SDS((n, 128), f32),
                          compiler_params=_params(("parallel",)))(x)


def _cast_direct(w):
    ly, R, Cn = w.shape
    tc = _tile(Cn, 512, 128)
    spec = pl.BlockSpec((None, R, tc), lambda l, j: (l, 0, j))

    def body(w_ref, o_ref):
        o_ref[...] = w_ref[...].astype(bf16)
    return pl.pallas_call(body, name="cast", grid=(ly, Cn // tc), in_specs=[spec], out_specs=spec, out_shape=SDS(w.shape, bf16),
                          compiler_params=_params(("parallel", "parallel")))(w)


def _cast_transposed(w):
    four = w.ndim == 4
    ly, I, (K, Nl) = w.shape[0], (w.shape[1] if four else 1), w.shape[-2:]
    tk = _tile(K, 256, 128)
    mid = (lambda i: (i,)) if four else (lambda i: ())
    nn = (None,) * (w.ndim - 2)

    def body(w_ref, o_ref):
        o_ref[...] = w_ref[...].T.astype(bf16)
    return pl.pallas_call(
        body, name="cast_t", grid=(ly, I, K // tk), in_specs=[pl.BlockSpec(nn + (tk, Nl), lambda l, i, k: (l,) + mid(i) + (k, 0))],
        out_specs=pl.BlockSpec(nn + (Nl, tk), lambda l, i, k: (l,) + mid(i) + (0, k)), out_shape=SDS(w.shape[:-2] + (Nl, K), bf16),
        compiler_params=_params(("parallel", "parallel", "parallel")))(w)


def _adam(g, w, m, v):
    m = ADAM_B1 * m + (1.0 - ADAM_B1) * g
    v = ADAM_B2 * v + (1.0 - ADAM_B2) * (g * g)
    m_hat = m / (1.0 - ADAM_B1 ** ADAM_STEP)
    v_hat = v / (1.0 - ADAM_B2 ** ADAM_STEP)
    delta = -ADAM_LR * (m_hat / (jnp.sqrt(v_hat) + ADAM_EPS) + ADAM_WD * w)
    return delta, m, v


def _slot_sum(x):
    acc = x[0].astype(f32)
    for r in range(1, x.shape[0]):
        acc = acc + x[r].astype(f32)
    return acc


def _adamw_direct(g, w, m, v, slots):
    ly, R, Cn = w.shape
    tr = R if R <= 1024 else _tile(R, 512, 16)
    tc = _tile(Cn, max(128, (256 * 1024) // tr // 128 * 128), 128)
    spec = pl.BlockSpec((None, tr, tc), lambda l, i, j: (l, i, j))
    gspec = pl.BlockSpec((None, N_DEV, tr, tc), lambda l, i, j: (l, 0, i, j)) if slots else spec

    def body(g_ref, w_ref, m_ref, v_ref, go, do, mo, vo):
        gv = _slot_sum(g_ref[...]) if slots else g_ref[...]
        go[...] = gv
        do[...], mo[...], vo[...] = _adam(gv, w_ref[...], m_ref[...], v_ref[...])
    return pl.pallas_call(body, name="adamw", grid=(ly, R // tr, Cn // tc), in_specs=[gspec, spec, spec, spec], out_specs=[spec] * 4,
                          out_shape=[SDS(w.shape, f32)] * 4, compiler_params=_params(("parallel",) * 3))(g, w, m, v)


def _adamw_transposed(g, w, m, v):
    four = w.ndim == 4
    ly, I, (K, Nl) = w.shape[0], (w.shape[1] if four else 1), w.shape[-2:]
    tk = _tile(K, 128, 128)
    mid = (lambda i: (i,)) if four else (lambda i: ())
    nn = (None,) * (w.ndim - 3)
    spec = pl.BlockSpec((None,) + nn + (tk, Nl), lambda l, i, k: (l,) + mid(i) + (k, 0))
    gspec = pl.BlockSpec((None, N_DEV) + nn + (Nl, tk), lambda l, i, k: (l, 0) + mid(i) + (0, k))

    def body(g_ref, w_ref, m_ref, v_ref, go, do, mo, vo):
        gv = _slot_sum(g_ref[...]).T
        go[...] = gv
        do[...], mo[...], vo[...] = _adam(gv, w_ref[...], m_ref[...], v_ref[...])
    return pl.pallas_call(body, name="adamw_t", grid=(ly, I, K // tk), in_specs=[gspec, spec, spec, spec], out_specs=[spec] * 4,
                          out_shape=[SDS(w.shape, f32)] * 4, compiler_params=_params(("parallel",) * 3))(g, w, m, v)


def _ada_fwd(cvec, ada_w, ada_b_cols):
    ly, D, cols = ada_w.shape
    tn = _tile(cols, 768, 128)

    def body(c_ref, w_ref, b_ref, o_ref):
        o_ref[...] = _bdot(jax.nn.silu(c_ref[...]), w_ref[...]) + b_ref[...]
    return pl.pallas_call(
        body, name="ada", grid=(ly, cols // tn),
        in_specs=[pl.BlockSpec((16, D), lambda l, j: (0, 0)), pl.BlockSpec((None, D, tn), lambda l, j: (l, 0, j)),
                  pl.BlockSpec((None, 1, tn), lambda l, j: (l, 0, j))],
        out_specs=pl.BlockSpec((None, 16, tn), lambda l, j: (l, 0, j)), out_shape=SDS((ly, 16, cols), f32),
        compiler_params=_params(("parallel", "parallel")))(cvec, ada_w, ada_b_cols)


def _ada_bwd(cvec, ada_w, dm):
    ly, D, cols = ada_w.shape
    tn = _tile(cols, 768, 128)

    def body(c_ref, w_ref, d_ref, gw_ref, gc_ref):
        first = jnp.logical_and(pl.program_id(0) == 0, pl.program_id(1) == 0)

        @pl.when(first)
        def _():
            gc_ref[...] = jnp.zeros_like(gc_ref)

        def f(cv, w):
            return _bdot(jax.nn.silu(cv), w)
        _, vjp = jax.vjp(f, c_ref[...], w_ref[...])
        dc, dw = vjp(d_ref[...])
        gw_ref[...] = dw
        gc_ref[...] += dc
    return pl.pallas_call(
        body, name="ada_bwd", grid=(ly, cols // tn),
        in_specs=[pl.BlockSpec((16, D), lambda l, j: (0, 0)), pl.BlockSpec((None, D, tn), lambda l, j: (l, 0, j)),
                  pl.BlockSpec((None, 16, tn), lambda l, j: (l, 0, j))],
        out_specs=[pl.BlockSpec((None, D, tn), lambda l, j: (l, 0, j)), pl.BlockSpec((16, D), lambda l, j: (0, 0))],
        out_shape=[SDS(ada_w.shape, f32), SDS((16, D), f32)], compiler_params=_params(("arbitrary", "arbitrary")))(cvec, ada_w, dm)


def _flat_adamw(g, w, m, v):
    n = w.shape[0]
    tn = _tile(n, 1024, 8)
    spec = pl.BlockSpec((tn, 128), lambda i: (i, 0))

    def body(g_ref, w_ref, m_ref, v_ref, do, mo, vo):
        do[...], mo[...], vo[...] = _adam(g_ref[...], w_ref[...], m_ref[...], v_ref[...])
    return pl.pallas_call(body, name="adamw_small", grid=(n // tn,), in_specs=[spec] * 4, out_specs=[spec] * 3,
                          out_shape=[SDS(w.shape, f32)] * 3, compiler_params=_params(("parallel",)))(g, w, m, v)


def _pack(arrs):
    flat = jnp.concatenate([a.reshape(-1).astype(f32) for a in arrs])
    n = _round_up(flat.shape[0], 512 * 128)
    return jnp.pad(flat, (0, n - flat.shape[0])).reshape(n // 128, 128)


def _unpack(flat, shapes, lead=()):
    flat = flat.reshape(lead + (-1,))
    out, off = [], 0
    for s in shapes:
        sz = math.prod(s)
        out.append(flat[..., off:off + sz].reshape(lead + tuple(s)))
        off += sz
    return out


def _unshard_last(g):
    g = jnp.moveaxis(g, 0, -2)
    return g.reshape(g.shape[:-2] + (g.shape[-2] * g.shape[-1],))


class _Dims:
    pass


def _rope_tables(L, C, NH):
    rows = L // GRID_W
    row = jnp.repeat(jnp.arange(rows), GRID_W).astype(f32)
    col = jnp.tile(jnp.arange(GRID_W), rows).astype(f32)
    half = HEAD_DIM // 2
    inv = ROPE_BASE ** (-jnp.arange(0, half, 2, dtype=f32) / half)
    ar, ac = row[:, None] * inv, col[:, None] * inv
    cos = jnp.concatenate([jnp.cos(ar), jnp.cos(ar), jnp.cos(ac), jnp.cos(ac)], axis=-1)
    sin = jnp.concatenate([-jnp.sin(ar), jnp.sin(ar), -jnp.sin(ac), jnp.sin(ac)], axis=-1)
    cos = jnp.concatenate([jnp.ones((C, HEAD_DIM), f32), cos], axis=0)
    sin = jnp.concatenate([jnp.zeros((C, HEAD_DIM), f32), sin], axis=0)
    return dict(cos=cos, sin=sin)


def kernel(x, c, ctx, c_ctx, ada_w, ada_b, norm_g, ffn1_w13, ffn1_w2, w_in, b_merge, rnn_conv_w, rnn_conv_b, lru_w_a, lru_b_a, lru_w_x, lru_b_x, lru_lambda, sc_conv_w, attn_sink, w_branch, w_out, ffn2_w13, ffn2_w2, final_norm_g, loss_target, m_c_ctx, m_ada_w, m_ada_b, m_norm_g, m_ffn1_w13, m_ffn1_w2, m_w_in, m_b_merge, m_rnn_conv_w, m_rnn_conv_b, m_lru_w_a, m_lru_b_a, m_lru_w_x, m_lru_b_x, m_lru_lambda, m_sc_conv_w, m_attn_sink, m_w_branch, m_w_out, m_ffn2_w13, m_ffn2_w2, m_final_norm_g, v_c_ctx, v_ada_w, v_ada_b, v_norm_g, v_ffn1_w13, v_ffn1_w2, v_w_in, v_b_merge, v_rnn_conv_w, v_rnn_conv_b, v_lru_w_a, v_lru_b_a, v_lru_w_x, v_lru_b_x, v_lru_lambda, v_sc_conv_w, v_attn_sink, v_w_branch, v_w_out, v_ffn2_w13, v_ffn2_w2, v_final_norm_g):
    d = _Dims()
    L, D = x.shape[1], x.shape[2]
    C = ctx.shape[1]
    LY = ada_w.shape[0]
    d.D, d.C, d.T = D, C, C + L
    d.F = ffn1_w2.shape[1] * N_DEV
    d.Fp = _round_up(d.F, 512)
    d.IN = w_in.shape[2] * N_DEV
    d.BW = w_branch.shape[2]
    d.NH = attn_sink.shape[1]
    d.NA = d.IN - N_BRANCH * D
    d.NKV = (d.NA - 5 * d.BW - d.NH * HEAD_DIM) // (2 * HEAD_DIM)
    d.G = d.NH // d.NKV
    d.NB = lru_w_a.shape[2]
    d.tr = _tile(C, 256, 8)
    d.trw = _tile(C, 128, 8)
    assert d.NH * HEAD_DIM == d.BW and L % GRID_W == 0 and C % ATT_BLOCK == 0 and L % ATT_BLOCK == 0
    sh13, sh2, shin, shd = 2 * d.F // N_DEV, d.F // N_DEV, d.IN // N_DEV, D // N_DEV
    cols9 = N_MOD * D // N_DEV
    me3 = _coords()
    me = _index(me3)

    sharded = [norm_g, b_merge, rnn_conv_w, lru_b_a, lru_b_x, lru_lambda, sc_conv_w]
    shapes1 = [a.shape for a in sharded] + [(D,)]
    g1 = _allgather("gather_small", _pack(sharded + [c.reshape(-1)]))
    parts = _unpack(g1, shapes1, (N_DEV,))
    norm_g_f, b_merge_f, conv_w_f, lru_b_a_f, lru_b_x_f, lru_lam_f, sc_w_f = [_unshard_last(p) for p in parts[:-1]]
    cvec = jnp.concatenate([parts[-1], c_ctx[None], jnp.zeros((7, D), f32)], axis=0)
    ada_b_cols = lax.dynamic_slice_in_dim(ada_b, me * cols9, cols9, axis=1)[:, None, :]
    modcols = _ada_fwd(cvec, ada_w, ada_b_cols)
    g2 = _allgather("gather_mod", modcols)
    modall = jnp.moveaxis(g2, 0, 2).reshape(LY, 16, N_MOD, D)
    mod_lat = lax.dynamic_index_in_dim(modall, me, axis=1, keepdims=False)
    modv = jnp.stack([modall[:, N_DEV], mod_lat], axis=1)

    assert LY == 2
    off13 = lambda j: (j // 4) * d.Fp + (j % 4) * sh13
    tr_ = lambda a: jnp.swapaxes(a, 1, 2)
    cast_t = lambda w: _cast_direct(tr_(w))
    c13a, c13b = cast_t(ffn1_w13), cast_t(ffn2_w13)
    wt = {"w13a": (c13a, off13, sh13, 2 * d.Fp, (D,), False), "w13b": (c13b, off13, sh13, 2 * d.Fp, (D,), False),
          "w2a": (_cast_direct(ffn1_w2), lambda j: j * sh2, sh2, d.Fp, (D,), False),
          "w2b": (_cast_direct(ffn2_w2), lambda j: j * sh2, sh2, d.Fp, (D,), False),
          "wint": (cast_t(w_in), lambda j: j * shin, shin, d.IN, (D,), False),
          "wbt": (_cast_transposed(w_branch), lambda j: j * shd, shd, D, (d.BW,), True),
          "wout": (_cast_direct(w_out), lambda j: j * shd, shd, D, (D,), False)}
    env = {}
    for name, (cw, off, sh, tot, rest, three) in wt.items():
        for l in range(LY):
            env[(name, l)] = lax.empty(((N_BRANCH, tot) if three else (tot,)) + rest, bf16)
        env[("r", name)] = lax.empty((LY, N_DEV) + ((N_BRANCH, sh) if three else (sh,)) + rest, bf16)
    zpad = jnp.zeros((max(d.Fp - d.F, 16), D), bf16)

    def part(sh, h):
        return (0, sh) if h is None else (h * (sh // 2), sh // 2)

    def rows_of(ref, start, n, three):
        sl = pl.ds(start if isinstance(start, int) else pl.multiple_of(start, 16), n)
        return ref.at[:, sl] if three else ref.at[sl]

    def add_gather(cm, name, l, h=None):
        cw, off, sh, tot, rest, three = wt[name]
        r0, n = part(sh, h)
        cm.gather(cw, (name, l), lambda ref: rows_of(ref.at[l], r0, n, three),
                  lambda ref, origin: rows_of(ref, off(origin) + r0, n, three))
        if d.Fp > d.F and h in (None, 0) and name[:2] in ("w1", "w2"):
            for base in ((0, d.Fp) if name[:3] == "w13" else (0,)):
                cm.fill(zpad, (name, l), lambda ref, base=base: ref.at[pl.ds(base + d.F, d.Fp - d.F)])

    def add_rs(cm, name, l, h=None):
        cw, off, sh, tot, rest, three = wt[name]
        r0, n = part(sh, h)
        cm.add(env[("g", name, l)], ("r", name), lambda ref, me_, pi, r: rows_of(ref, off(pi) + r0, n, three),
               lambda ref, sender, r: rows_of(ref.at[l, r], r0, n, three))

    plan = {"F0b.up": [("ag", "w2b", 0), ("ag", "w13a", 1, 0)], "F0b.down": [("ag", "w13a", 1, 1)], "F1b.up": [("ag", "w2b", 1)],
            "B0m.in_x": [("lru",)]}
    for l in range(LY):
        plan.update({f"F{l}a.up": [("ag", "w2a", l), ("ag", "wint", l, 0)], f"F{l}a.down": [("ag", "wint", l, 1), ("ag", "wbt", l)],
                     f"F{l}m.in_a": [("ag", "wout", l), ("ag", "w13b", l, 0)], f"F{l}m.in_g": [("ag", "w13b", l, 1)]})
    for l in range(LY):
        plan.update({f"B{l}b.dw": [("rs", "w13b", l, 0)], f"B{l}b.ux": [("rs", "w13b", l, 1)],
                     f"B{l}m.in_w": [("rs", "wout", l), ("rs", "wbt", l), ("rs", "w2b", l)], f"B{l}a.dx": [("rs", "wint", l, 0)],
                     f"B{l}a.uw": [("rs", "wint", l, 1)], f"B{l}a.dw": [("rs", "w13a", l, 0)]})
    plan.update({"B1a.ux": [("rs", "w13a", 1, 1)], "B0b.dx": [("rs", "w2a", 1)], "B0a.ux": [("rs", "w13a", 0, 1), ("rs", "w2a", 0)]})

    def add_lru(cm):
        for nm in ("lru_w_a", "lru_w_x"):
            gs = jnp.stack([env[("gs", nm, l)] for l in range(LY)]).reshape(-1, 128)
            env[("all", nm)] = lax.empty((N_DEV,) + gs.shape, f32)
            cm.add(gs, ("all", nm), lambda ref, me_, pi, r: ref, lambda ref, sender, r: ref.at[sender])
    sc = _Sched(plan, env, {"ag": add_gather, "rs": add_rs, "lru": add_lru})
    first = _Comm(env)
    add_gather(first, "w13a", 0)
    _exchange("gather_first", first)

    tabs = _rope_tables(L, C, d.NH)
    h = jnp.concatenate([ctx[0], x[0]], axis=0)
    Ps, saves = [], []
    for l in range(LY):
        sinkrows = jnp.repeat(attn_sink[l].reshape(d.NKV, d.G), ATT_BLOCK, axis=1)[:, :, None]
        Ps.append(dict(norm_g=norm_g_f[l], b_merge=b_merge_f[l], rnn_conv_w=conv_w_f[l], rnn_conv_b=rnn_conv_b[l][None],
                       lru_w_a=lru_w_a[l], lru_b_a=lru_b_a_f[l], lru_w_x=lru_w_x[l], lru_b_x=lru_b_x_f[l], lru_lambda=lru_lam_f[l],
                       sc_conv_w=sc_w_f[l], sinkrows=sinkrows))
    for l in range(LY):
        P = Ps[l]
        h, s1 = _ffn_fwd(h, modv[l], P["norm_g"][0:1], ("w13a", l), ("w2a", l), 0, d, sc, f"F{l}a.")
        h, s2 = _mixer_fwd(h, modv[l], P, l, tabs, d, sc, f"F{l}m.")
        h, s3 = _ffn_fwd(h, modv[l], P["norm_g"][2:3], ("w13b", l), ("w2b", l), 6, d, sc, f"F{l}b.")
        saves.append((s1, s2, s3))

    fng = final_norm_g[None]

    def loss_fn(t):
        def f(hv, g):
            y = hv * lax.rsqrt(jnp.mean(hv * hv, axis=-1, keepdims=True) + EPS) * g
            e = y - t.R[1]
            return 0.5 * jnp.sum(jnp.mean(e * e, axis=-1))
        lat = (t.seg == 1).astype(f32)
        val, (dhv, dg) = jax.value_and_grad(f, argnums=(0, 1))(t.R[0], t.K[0])
        return [dhv * lat], [dg * lat, jnp.full((1, 128), val * lat, f32)], []
    dh, dfng, lossv = _rowwise("loss", loss_fn, d.T, C, d.tr, rows=[(h, D, 0, 0), (loss_target[0], D, 0, -(C // d.tr))],
                               consts=[fng], outs=[(D, f32)], accs=[(1, D), (1, 128)])
    loss = lax.psum(lossv[0, 0], AXES)

    gl = [None] * LY
    pre = None
    for l in reversed(range(LY)):
        P = Ps[l]
        s1, s2, s3 = saves[l]
        dh, dg2, dss2, dgt2, pre = _ffn_bwd(dh, s3, modv[l], P["norm_g"][2:3], ("w13b", l), ("w2b", l), 6, d, sc, f"B{l}b.",
                                            pre, (s2["y"], modv[l], 5, 1.0))
        dh, gm, pre = _mixer_bwd(dh, s2, modv[l], P, l, tabs, d, sc, f"B{l}m.", pre, (s1["o"], modv[l], 2, 0.5))
        below = (saves[l - 1][2]["o"], modv[l - 1], 8, 0.5) if l > 0 else None
        dh, dg0, dss0, dgt0, pre = _ffn_bwd(dh, s1, modv[l], P["norm_g"][0:1], ("w13a", l), ("w2a", l), 0, d, sc, f"B{l}a.",
                                            pre, below)
        gm.update(norm_g=jnp.concatenate([dg0, gm["norm_g1"], dg2], axis=0),
                  dmod=jnp.concatenate([dss0, dgt0, gm["dss"], gm["dgate"], dss2, dgt2], axis=1))
        gl[l] = gm
    grad_x = dh[C:][None]
    st = lambda k: jnp.stack([g[k] for g in gl])

    r13a, r2a, rin, rwb, rwo, r13b, r2b = [env[("r", k)] for k in ("w13a", "w2a", "wint", "wbt", "wout", "w13b", "w2b")]
    big = {}
    adamw_tr = lambda r, w, m, v: [tr_(o) for o in _adamw_direct(r, tr_(w), tr_(m), tr_(v), True)]
    big["ffn1_w13"] = adamw_tr(r13a, ffn1_w13, m_ffn1_w13, v_ffn1_w13)
    big["ffn2_w13"] = adamw_tr(r13b, ffn2_w13, m_ffn2_w13, v_ffn2_w13)
    big["w_in"] = adamw_tr(rin, w_in, m_w_in, v_w_in)
    big["w_branch"] = _adamw_transposed(rwb, w_branch, m_w_branch, v_w_branch)
    big["ffn1_w2"] = _adamw_direct(r2a, ffn1_w2, m_ffn1_w2, v_ffn1_w2, True)
    big["ffn2_w2"] = _adamw_direct(r2b, ffn2_w2, m_ffn2_w2, v_ffn2_w2, True)
    big["w_out"] = _adamw_direct(rwo, w_out, m_w_out, v_w_out, True)

    sink_g = jnp.stack([jnp.sum(g["sinkrows"].reshape(d.NKV, d.G, ATT_BLOCK), axis=-1).reshape(d.NH) for g in gl])
    small_full = dict(norm_g=st("norm_g"), b_merge=st("b_merge"), rnn_conv_w=st("rnn_conv_w"), rnn_conv_b=st("rnn_conv_b")[:, 0],
                      lru_b_a=st("lru_b_a"), lru_b_x=st("lru_b_x"), lru_lambda=st("lru_lambda"), sc_conv_w=st("sc_conv_w"), attn_sink=sink_g, final_norm_g=dfng[0])
    names_s = list(small_full)
    dmod = st("dmod")
    pk = _pack([small_full[k] for k in names_s] + [dmod])
    g4 = _allgather("gather_small_grads", pk)
    tot = _unpack(_sum8(g4), [small_full[k].shape for k in names_s] + [dmod.shape])
    sums = dict(zip(names_s, tot[:-1]))
    dmod_sum = tot[-1].reshape(LY, 2, N_MOD * D)
    dmod_all = _unpack(g4, [small_full[k].shape for k in names_s] + [dmod.shape], (N_DEV,))[-1].reshape(N_DEV, LY, 2, N_MOD * D)
    dm_rows = jnp.concatenate([jnp.moveaxis(dmod_all[:, :, 1], 0, 1), dmod_sum[:, 0:1], jnp.zeros((LY, 7, N_MOD * D), f32)], axis=1)
    dm_cols = lax.dynamic_slice_in_dim(dm_rows, me * cols9, cols9, axis=2)
    g_ada_w, dcv = _ada_bwd(cvec, ada_w, dm_cols)
    g5 = _allgather("gather_cctx", _pack([dcv[N_DEV]]))
    g_c_ctx = _sum8(g5).reshape(-1)[:D]
    g_ada_b = dmod_sum[:, 0] + dmod_sum[:, 1]
    ada_out = _adamw_direct(g_ada_w, ada_w, m_ada_w, v_ada_w, False)

    def shard_last(a, n):
        return lax.dynamic_slice_in_dim(a, me * n, n, axis=a.ndim - 1)
    local_g = dict(c_ctx=g_c_ctx, ada_b=g_ada_b, norm_g=shard_last(sums["norm_g"], shd), b_merge=shard_last(sums["b_merge"], shd),
                   rnn_conv_w=shard_last(sums["rnn_conv_w"], d.BW // N_DEV), rnn_conv_b=sums["rnn_conv_b"],
                   lru_b_a=shard_last(sums["lru_b_a"], d.BW // N_DEV), lru_b_x=shard_last(sums["lru_b_x"], d.BW // N_DEV),
                   lru_lambda=shard_last(sums["lru_lambda"], d.BW // N_DEV), sc_conv_w=shard_last(sums["sc_conv_w"], d.BW // N_DEV),
                   attn_sink=sums["attn_sink"], final_norm_g=sums["final_norm_g"])
    wmv = dict(c_ctx=(c_ctx, m_c_ctx, v_c_ctx), ada_b=(ada_b, m_ada_b, v_ada_b), norm_g=(norm_g, m_norm_g, v_norm_g),
               b_merge=(b_merge, m_b_merge, v_b_merge), rnn_conv_w=(rnn_conv_w, m_rnn_conv_w, v_rnn_conv_w),
               rnn_conv_b=(rnn_conv_b, m_rnn_conv_b, v_rnn_conv_b),
               lru_b_a=(lru_b_a, m_lru_b_a, v_lru_b_a), lru_b_x=(lru_b_x, m_lru_b_x, v_lru_b_x),
               lru_lambda=(lru_lambda, m_lru_lambda, v_lru_lambda), sc_conv_w=(sc_conv_w, m_sc_conv_w, v_sc_conv_w),
               attn_sink=(attn_sink, m_attn_sink, v_attn_sink), final_norm_g=(final_norm_g, m_final_norm_g, v_final_norm_g))
    names_l = list(local_g)
    shapes_l = [wmv[k][0].shape for k in names_l]
    gp = _pack([local_g[k].reshape(wmv[k][0].shape) for k in names_l])
    outs_s = _flat_adamw(gp, _pack([wmv[k][0] for k in names_l]), _pack([wmv[k][1] for k in names_l]), _pack([wmv[k][2] for k in names_l]))
    small = {k: [local_g[k].reshape(wmv[k][0].shape)] for k in names_l}
    for o in outs_s:
        for k, a in zip(names_l, _unpack(o, shapes_l)):
            small[k].append(a)
    for nm, (w_, m_, v_) in (("lru_w_a", (lru_w_a, m_lru_w_a, v_lru_w_a)), ("lru_w_x", (lru_w_x, m_lru_w_x, v_lru_w_x))):
        gsum = _sum8(env[("all", nm)])
        flat = lambda a: a.reshape(-1, 128)
        small[nm] = [a.reshape(w_.shape) for a in (gsum, *_flat_adamw(gsum, flat(w_), flat(m_), flat(v_)))]

    order = ["c_ctx", "ada_w", "ada_b", "norm_g", "ffn1_w13", "ffn1_w2", "w_in", "b_merge", "rnn_conv_w", "rnn_conv_b", "lru_w_a",
             "lru_b_a", "lru_w_x", "lru_b_x", "lru_lambda", "sc_conv_w", "attn_sink", "w_branch", "w_out", "ffn2_w13", "ffn2_w2",
             "final_norm_g"]
    allo = dict(small)
    allo.update(big)
    allo["ada_w"] = ada_out
    res = [loss, grad_x]
    for q in range(4):
        res += [allo[k][q] for k in order]
    return tuple(res)
```

```python
import functools
import math

import jax
import jax.numpy as jnp
from jax import lax
from jax.experimental import pallas as pl
from jax.experimental.pallas import tpu as pltpu

f32 = jnp.float32
bf16 = jnp.bfloat16
SDS = jax.ShapeDtypeStruct

N_DEV = 8
AXES = ("x", "y", "c")
HEAD_DIM = 128
GRID_W = 64
ATT_BLOCK = 128
HALO = 16
ROPE_BASE = 10000.0
LRU_C = 8.0
EPS = 1e-6
NEG_INF = -1e30
N_MOD = 9
N_BRANCH = 3
ADAM_LR, ADAM_B1, ADAM_B2, ADAM_EPS, ADAM_WD, ADAM_STEP = 0.001, 0.9, 0.999, 1e-08, 0.01, 10
VMEM_LIMIT = 56 * 1024 * 1024
ANY = pl.BlockSpec(memory_space=pl.ANY)


def _round_up(n, m):
    return (n + m - 1) // m * m


def _tile(n, target, align):
    best = None
    t = align
    while t <= min(n, target):
        if n % t == 0:
            best = t
        t += align
    return best if best is not None else n


def _params(sem):
    return pltpu.CompilerParams(dimension_semantics=sem, vmem_limit_bytes=VMEM_LIMIT)


_DIMS = {"nn": (((1,), (0,)), ((), ())), "nt": (((1,), (1,)), ((), ())), "tn": (((0,), (0,)), ((), ()))}


def _coords():
    return lax.axis_index("x"), lax.axis_index("y"), lax.axis_index("c")


def _peer(xyc, r):
    x, y, c = xyc
    return (1 - x if r & 4 else x, 1 - y if r & 2 else y, 1 - c if r & 1 else c)


def _index(xyc):
    return xyc[0] * 4 + xyc[1] * 2 + xyc[2]


class _Comm:
    def __init__(self, env):
        self.env, self.srcs, self.dsts, self.keys, self.items, self.fills, self.gathers = env, [], [], [], [], [], []

    def gather(self, src, key, src_view, dst_view):
        self.gathers.append((self._src(src), self._dst(key), src_view, dst_view))

    def _src(self, a):
        for q, b in enumerate(self.srcs):
            if b is a:
                return q
        self.srcs.append(a)
        return len(self.srcs) - 1

    def _dst(self, key):
        if key not in self.keys:
            self.keys.append(key)
            self.dsts.append(self.env[key])
        return self.keys.index(key)

    def add(self, src, key, src_view, dst_view):
        self.items.append((self._src(src), self._dst(key), src_view, dst_view))

    def fill(self, src, key, dst_view):
        self.fills.append((self._src(src), self._dst(key), dst_view))

    def scratch(self):
        n = len(self.items) + len(self.gathers)
        return [pltpu.SemaphoreType.DMA((7 * n,)), pltpu.SemaphoreType.DMA((7 * n,)), pltpu.SemaphoreType.DMA((n + len(self.fills),))]

    def store(self, arrays):
        for key, a in zip(self.keys, arrays):
            self.env[key] = a

    def descriptors(self, srcs, dsts, sems, me3):
        send, recv, lsem = sems
        me = _index(me3)
        local, sends, recvs = [], [], []
        for q, (si, di, sv, dv) in enumerate(self.items):
            local.append(pltpu.make_async_copy(sv(srcs[si], me, me, 0), dv(dsts[di], me, 0), lsem.at[q]))
            for r in range(1, N_DEV):
                p3 = _peer(me3, r)
                pi = _index(p3)
                kw = dict(send_sem=send.at[q * 7 + r - 1], recv_sem=recv.at[q * 7 + r - 1], device_id=p3,
                          device_id_type=pl.DeviceIdType.MESH)
                sends.append(pltpu.make_async_remote_copy(src_ref=sv(srcs[si], me, pi, r), dst_ref=dv(dsts[di], me, r), **kw))
                recvs.append(pltpu.make_async_remote_copy(src_ref=sv(srcs[si], me, pi, r), dst_ref=dv(dsts[di], pi, r), **kw))
        n1 = len(self.items)
        landed, passed = [], []
        sib = _peer(me3, 1)
        for q, (si, di, sv, dv) in enumerate(self.gathers):
            def rc(src_ref, dst_ref, k, dev, base=(n1 + q) * 7):
                return pltpu.make_async_remote_copy(src_ref=src_ref, dst_ref=dst_ref, send_sem=send.at[base + k - 1],
                                                    recv_sem=recv.at[base + k - 1], device_id=dev, device_id_type=pl.DeviceIdType.MESH)
            mine = sv(srcs[si])
            local.append(pltpu.make_async_copy(mine, dv(dsts[di], me), lsem.at[n1 + q]))
            for r in (1, 2, 4, 6):
                sends.append(rc(mine, dv(dsts[di], me), r, _peer(me3, r)))
            for r in (2, 4, 6):
                blk = dv(dsts[di], _index(_peer(me3, r)))
                landed.append(rc(mine, blk, r, _peer(me3, r)))
                passed.append(rc(blk, blk, r ^ 1, sib))
            for k in (1, 3, 5, 7):
                recvs.append(rc(mine, dv(dsts[di], _index(_peer(me3, k))), k, sib))
        for q, (si, di, dv) in enumerate(self.fills):
            local.append(pltpu.make_async_copy(srcs[si], dv(dsts[di]), lsem.at[n1 + len(self.gathers) + q]))
        return local, sends, recvs, landed, passed


def _comm_start(descs):
    for cp in descs[0] + descs[1]:
        cp.start()


def _comm_mid(descs):
    for cp in descs[3]:
        cp.wait_recv()
    for cp in descs[4]:
        cp.start()


def _comm_wait(descs):
    local, sends, recvs, _, passed = descs
    sends = sends + passed
    for cp in recvs:
        cp.wait_recv()
    for cp in sends:
        cp.wait_send()
    for cp in local:
        cp.wait()


def _carry_call(core, name, grid, in_specs, out_specs, out_shape, scratch, args, comm):
    ni, no, nsc, ng = len(in_specs), len(out_specs), len(scratch), len(grid)
    if comm is None:
        def plain(*refs):
            core(refs[:ni], refs[ni:ni + no], refs[ni + no:])
        return pl.pallas_call(plain, name=name, grid=grid, in_specs=in_specs, out_specs=out_specs, out_shape=out_shape,
                              scratch_shapes=scratch, compiler_params=_params(("parallel",) * (ng - 1) + ("arbitrary",)))(*args)
    ns, nd = len(comm.srcs), len(comm.dsts)
    o0 = ni + ns + nd

    def body(*refs):
        pid = [pl.program_id(q) for q in range(ng)]
        first, last = pid[0] == 0, pid[0] == grid[0] - 1
        for q in range(1, ng):
            first = jnp.logical_and(first, pid[q] == 0)
            last = jnp.logical_and(last, pid[q] == grid[q] - 1)
        me3 = _coords()
        mk = lambda: comm.descriptors(refs[ni:ni + ns], refs[o0 + no:o0 + no + nd], refs[o0 + no + nd + nsc:], me3)

        @pl.when(first)
        def _():
            _comm_start(mk())

        if comm.gathers:
            lin = pid[0]
            for q in range(1, ng):
                lin = lin * grid[q] + pid[q]

            @pl.when(lin == min(math.prod(grid) - 1, int(0.85 * math.prod(grid))))
            def _():
                _comm_mid(mk())

        core(refs[:ni], refs[o0:o0 + no], refs[o0 + no + nd:o0 + no + nd + nsc])

        @pl.when(last)
        def _():
            _comm_wait(mk())

    res = pl.pallas_call(
        body, name=name, grid=grid, in_specs=list(in_specs) + [ANY] * (ns + nd), out_specs=list(out_specs) + [ANY] * nd,
        out_shape=list(out_shape) + [SDS(x.shape, x.dtype) for x in comm.dsts], scratch_shapes=list(scratch) + comm.scratch(),
        input_output_aliases={ni + ns + q: no + q for q in range(nd)}, compiler_params=_params(("arbitrary",) * ng),
    )(*args, *comm.srcs, *comm.dsts)
    comm.store(res[no:])
    return res[:no]


def _mm(a, b, mode, out_dtype, name, *, M=None, N=None, K=None, a_off=0, b_off=0, b_lead=None,
        tm=768, tn=512, tk=2048, comm=None, a3=False, resid=None):
    b2 = b.shape[1:] if b_lead is not None else b.shape
    if mode == "tn":
        K = a.shape[-2] if K is None else K
        M = (a.shape[0] * a.shape[2] if a3 else a.shape[1]) if M is None else M
        N = b2[1] if N is None else N
    else:
        M = a.shape[-2] if M is None else M
        K = (a.shape[0] * a.shape[2] if a3 else a.shape[1]) if K is None else K
        N = (b2[1] if mode == "nn" else b2[0]) if N is None else N
    g = math.gcd
    if mode == "tn":
        tm = _tile(g(M, a_off) if a_off else (a.shape[2] if a3 else M), tm, 128)
        tk = _tile(K, tk, 16)
    else:
        tm = _tile(M, tm, 16)
        tk = _tile(g(K, a_off) if a_off else (a.shape[2] if a3 else K), tk, 128)
    tn = _tile(g(N, b_off) if b_off else N, tn, 128)
    nk = K // tk
    ao, bo = (a_off // (tm if mode == "tn" else tk)), b_off // tn
    lead = () if b_lead is None else (b_lead,)
    ld = () if b_lead is None else (None,)
    if mode == "nn":
        a_spec = pl.BlockSpec((tm, tk), lambda i, j, k: (i, ao + k))
        b_spec = pl.BlockSpec(ld + (tk, tn), lambda i, j, k: lead + (k, bo + j))
    elif mode == "nt":
        a_spec = pl.BlockSpec((tm, tk), lambda i, j, k: (i, ao + k))
        b_spec = pl.BlockSpec(ld + (tn, tk), lambda i, j, k: lead + (bo + j, k))
    else:
        a_spec = pl.BlockSpec((tk, tm), lambda i, j, k: (k, ao + i))
        b_spec = pl.BlockSpec(ld + (tk, tn), lambda i, j, k: lead + (k, bo + j))
    if a3:
        if mode == "nn":
            nkc = a.shape[2] // tk
            a_spec = pl.BlockSpec((None, tm, tk), lambda i, j, k: (k // nkc, i, k % nkc))
        else:
            nmc = a.shape[2] // tm
            a_spec = pl.BlockSpec((None, tk, tm), lambda i, j, k: (i // nmc, k, i % nmc))
    dims = _DIMS[mode]

    def finish(ins, outs, val):
        outs[0][...] = val.astype(out_dtype)
        if resid is not None:
            gi, coef, C = resid[2:]
            row = pl.program_id(0) * tm + lax.broadcasted_iota(jnp.int32, (tm, 1), 0)
            gate = jnp.where(row < C, ins[3][0, gi:gi + 1], ins[3][1, gi:gi + 1])
            outs[1][...] = ins[2][...] + coef * gate * val

    def core(ins, outs, scr):
        a_ref, b_ref = ins[0], ins[1]
        k = pl.program_id(2)
        if nk == 1:
            finish(ins, outs, lax.dot_general(a_ref[...], b_ref[...], dims, preferred_element_type=f32))
        else:
            acc = scr[0]

            @pl.when(k == 0)
            def _():
                acc[...] = jnp.zeros_like(acc)

            acc[...] += lax.dot_general(a_ref[...], b_ref[...], dims, preferred_element_type=f32)

            @pl.when(k == nk - 1)
            def _():
                finish(ins, outs, acc[...])

    o_spec = pl.BlockSpec((tm, tn), lambda i, j, k: (i, j))
    in_specs, out_specs, out_shape, args = [a_spec, b_spec], [o_spec], [SDS((M, N), out_dtype)], (a, b)
    if resid is not None:
        in_specs += [o_spec, pl.BlockSpec((2, N_MOD, tn), lambda i, j, k: (0, 0, j))]
        out_specs, out_shape, args = out_specs + [o_spec], out_shape + [SDS((M, N), f32)], args + tuple(resid[:2])
    res = _carry_call(core, name, (M // tm, N // tn, nk), in_specs, out_specs, out_shape,
                      [] if nk == 1 else [pltpu.VMEM((tm, tn), f32)], args, comm)
    return res[0] if resid is None else res


class _Tile:
    pass


def _rowwise(name, fn, T, C, tr, rows=(), halos=(), consts=(), mods=(), outs=(), accs=(), maccs=()):
    nT, nC = T // tr, C // tr
    assert T % tr == 0 and C % tr == 0 and nC >= 1 and tr % HALO == 0
    r8, n8 = tr // HALO, T // HALO
    seg_of = lambda i: jnp.where(i >= nC, 1, 0)
    in_specs, args = [], []
    for arr, w, cb, ro in rows:
        nt = arr.shape[0] // tr
        in_specs.append(pl.BlockSpec((tr, w), (lambda i, cb=cb, ro=ro, nt=nt: (jnp.clip(i + ro, 0, nt - 1), cb))))
        args.append(arr)
    for arr, w, cb in halos:
        in_specs.append(pl.BlockSpec((HALO, w), lambda i, cb=cb: (jnp.maximum(i * r8 - 1, 0), cb)))
        in_specs.append(pl.BlockSpec((tr, w), lambda i, cb=cb: (i, cb)))
        in_specs.append(pl.BlockSpec((HALO, w), lambda i, cb=cb: (jnp.minimum((i + 1) * r8, n8 - 1), cb)))
        args += [arr, arr, arr]
    for arr in consts:
        in_specs.append(pl.BlockSpec(arr.shape, lambda i, nd=arr.ndim: (0,) * nd))
        args.append(arr)
    for arr in mods:
        in_specs.append(pl.BlockSpec((None,) + arr.shape[1:], lambda i, nd=arr.ndim: (seg_of(i),) + (0,) * (nd - 1)))
        args.append(arr)
    out_specs, out_shape = [], []
    for w, dt in outs:
        out_specs.append(pl.BlockSpec((tr, w), lambda i: (i, 0)))
        out_shape.append(SDS((T, w), dt))
    for shp in accs:
        out_specs.append(pl.BlockSpec(shp, lambda i, nd=len(shp): (0,) * nd))
        out_shape.append(SDS(shp, f32))
    for shp in maccs:
        out_specs.append(pl.BlockSpec((None,) + shp, lambda i, nd=len(shp): (seg_of(i),) + (0,) * nd))
        out_shape.append(SDS((2,) + shp, f32))
    nr, nh, nk, nm, no, na, nma = len(rows), len(halos), len(consts), len(mods), len(outs), len(accs), len(maccs)

    def body(*refs):
        i = pl.program_id(0)
        t = _Tile()
        t.i, t.seg = i, seg_of(i)
        p = 0
        t.R = [refs[p + k][...].astype(f32) for k in range(nr)]
        p += nr
        pvalid = jnp.logical_and(i != 0, i != nC)
        nvalid = jnp.logical_and(i != nC - 1, i != nT - 1)
        t.H = []
        for k in range(nh):
            pr, cu, nx = refs[p][...].astype(f32), refs[p + 1][...].astype(f32), refs[p + 2][...].astype(f32)
            p += 3
            pr = jnp.where(pvalid, pr, jnp.zeros_like(pr))
            nx = jnp.where(nvalid, nx, jnp.zeros_like(nx))
            t.H.append(jnp.concatenate([pr, cu, nx], axis=0))
        t.K = [refs[p + k][...] for k in range(nk)]
        p += nk
        t.M = [refs[p + k][...] for k in range(nm)]
        p += nm
        o, a, ma = fn(t)
        for k in range(no):
            ref = refs[p + k]
            pieces = o[k] if isinstance(o[k], (list, tuple)) else [o[k]]
            c0 = 0
            for pc in pieces:
                ref[:, c0:c0 + pc.shape[1]] = pc.astype(ref.dtype)
                c0 += pc.shape[1]
        p += no
        for k in range(na):
            ref = refs[p + k]

            @pl.when(i == 0)
            def _(ref=ref):
                ref[...] = jnp.zeros_like(ref)

            ref[...] += a[k]
        p += na
        for k in range(nma):
            ref = refs[p + k]

            @pl.when(jnp.logical_or(i == 0, i == nC))
            def _(ref=ref):
                ref[...] = jnp.zeros_like(ref)

            ref[...] += ma[k]

    res = pl.pallas_call(
        body, name=name, grid=(nT,), in_specs=in_specs, out_specs=out_specs, out_shape=out_shape,
        compiler_params=_params(("arbitrary",)),
    )(*args)
    return res


@jax.custom_vjp
def _bdot(a, b):
    return jnp.dot(a.astype(bf16), b.astype(bf16), preferred_element_type=f32)


def _bdot_fwd(a, b):
    return _bdot(a, b), (a, b)


def _bdot_bwd(res, ct):
    a, b = res
    ctb = ct.astype(bf16)
    da = lax.dot_general(ctb, b.astype(bf16), _DIMS["nt"], preferred_element_type=f32)
    db = lax.dot_general(a.astype(bf16), ctb, _DIMS["tn"], preferred_element_type=f32)
    return da, db


_bdot.defvjp(_bdot_fwd, _bdot_bwd)


def _rms_mod(h, g, shift, scale):
    y = h * lax.rsqrt(jnp.mean(h * h, axis=-1, keepdims=True) + EPS) * g
    return y * (1.0 + scale) + shift


def _norm_mod_fwd(h, g, modv, s, d):
    def fn(t):
        m = t.M[0]
        return [_rms_mod(t.R[0], t.K[0], m[s:s + 1], m[s + 1:s + 2])], [], []
    return _rowwise("norm_mod", fn, d.T, d.C, d.tr, rows=[(h, d.D, 0, 0)], consts=[g], mods=[modv],
                    outs=[(d.D, bf16)])[0]


def _norm_mod_bwd(h, du, dh, g, modv, s, d, nxt=None):
    def fn(t):
        m = t.M[0]
        _, vjp = jax.vjp(_rms_mod, t.R[0], t.K[0], m[s:s + 1], m[s + 1:s + 2])
        dx, dg, dsh, dsc = vjp(t.R[1])
        dhn = t.R[2] + dx
        dss = jnp.concatenate([dsh, dsc], axis=0)
        if nxt is None:
            return [dhn], [dg], [dss]
        gi, coef = nxt[2:]
        return ([dhn, coef * t.M[1][gi:gi + 1] * dhn], [dg], [dss, jnp.sum(coef * dhn * t.R[3], axis=0, keepdims=True)])
    rows = [(h, d.D, 0, 0), (du, d.D, 0, 0), (dh, d.D, 0, 0)]
    if nxt is None:
        return _rowwise("norm_mod_bwd", fn, d.T, d.C, d.tr, rows=rows, consts=[g], mods=[modv], outs=[(d.D, f32)],
                        accs=[(1, d.D)], maccs=[(2, d.D)])
    dhn, do, dg, dss, dgate = _rowwise("norm_mod_bwd", fn, d.T, d.C, d.tr, rows=rows + [(nxt[0], d.D, 0, 0)], consts=[g],
                                       mods=[modv, nxt[1]], outs=[(d.D, f32), (d.D, bf16)], accs=[(1, d.D)], maccs=[(2, d.D), (1, d.D)])
    return dhn, dg, dss, (do, dgate)


def _swiglu(g, u):
    return jax.nn.silu(g) * u


def _resid_bwd(dh, o, modv, gi, coef, d):
    def fn(t):
        dhv = t.R[0]
        return [coef * t.M[0][gi:gi + 1] * dhv], [], [jnp.sum(coef * dhv * t.R[1], axis=0, keepdims=True)]
    return _rowwise("resid_bwd", fn, d.T, d.C, d.tr, rows=[(dh, d.D, 0, 0), (o, d.D, 0, 0)], mods=[modv],
                    outs=[(d.D, bf16)], maccs=[(1, d.D)])


class _Sched:
    def __init__(self, plan, env, builders):
        self.plan, self.env, self.builders = plan, env, builders

    def at(self, site):
        specs = self.plan.get(site)
        if not specs:
            return None
        cm = _Comm(self.env)
        for kind, *args in specs:
            self.builders[kind](cm, *args)
        return cm


def _ffn_up(u, w13t, d, comm):
    T, D, Fp = d.T, d.D, d.Fp
    tm, tn = _tile(T, 768, 16), _tile(Fp, 512, 128)

    def core(ins, outs, scr):
        uv, w_ref = ins[0][...], ins[1]
        gt = lax.dot_general(uv, w_ref[0], _DIMS["nt"], preferred_element_type=f32)
        up = lax.dot_general(uv, w_ref[1], _DIMS["nt"], preferred_element_type=f32)
        outs[0][0] = gt.astype(bf16)
        outs[0][1] = up.astype(bf16)
        outs[1][...] = _swiglu(gt, up).astype(bf16)

    return _carry_call(core, "ffn_up", (T // tm, Fp // tn),
                       [pl.BlockSpec((tm, D), lambda i, j: (i, 0)), pl.BlockSpec((2, tn, D), lambda i, j: (0, j, 0))],
                       [pl.BlockSpec((2, tm, tn), lambda i, j: (0, i, j)), pl.BlockSpec((tm, tn), lambda i, j: (i, j))],
                       [SDS((2, T, Fp), bf16), SDS((T, Fp), bf16)], [], (u, w13t.reshape(2, Fp, D)), comm)


def _ffn_down_dx(do, w2, gu, d, comm):
    T, D, Fp = d.T, d.D, d.Fp
    tm, tn = _tile(T, 768, 16), _tile(Fp, 512, 128)

    def core(ins, outs, scr):
        dhm = lax.dot_general(ins[0][...], ins[1][...], _DIMS["nt"], preferred_element_type=f32)
        _, vjp = jax.vjp(_swiglu, ins[2][0].astype(f32), ins[2][1].astype(f32))
        dg, du = vjp(dhm)
        outs[0][0] = dg.astype(bf16)
        outs[0][1] = du.astype(bf16)

    return _carry_call(core, "ffn_down_dx", (T // tm, Fp // tn),
                       [pl.BlockSpec((tm, D), lambda i, j: (i, 0)), pl.BlockSpec((tn, D), lambda i, j: (j, 0)),
                        pl.BlockSpec((2, tm, tn), lambda i, j: (0, i, j))],
                       [pl.BlockSpec((2, tm, tn), lambda i, j: (0, i, j))], [SDS((2, T, Fp), bf16)], [], (do, w2, gu), comm)[0]


def _ffn_fwd(h, modv, g, k13, k2, s, d, sc, site):
    u = _norm_mod_fwd(h, g, modv, s, d)
    gu, hmid = _ffn_up(u, sc.env[k13], d, sc.at(site + "up"))
    o, hn = _mm(hmid, sc.env[k2], "nn", bf16, "ffn_down", tn=512, tk=5632, comm=sc.at(site + "down"),
                resid=(h, modv, s + 2, 0.5, d.C))
    return hn, dict(h=h, u=u, gu=gu, hmid=hmid, o=o)


def _ffn_bwd(dh, sv, modv, g, k13, k2, s, d, sc, site, pre=None, nxt=None):
    w13t, w2 = sc.env[k13], sc.env[k2]
    do, dgate = pre if pre is not None else _resid_bwd(dh, sv["o"], modv, s + 2, 0.5, d)
    dgu = _ffn_down_dx(do, w2, sv["gu"], d, sc.at(site + "dx"))
    sc.env[("g",) + k13] = _mm(dgu, sv["u"], "tn", bf16, "ffn_up_dw", tm=2816, tn=1024, tk=768, comm=sc.at(site + "uw"), a3=True)
    sc.env[("g",) + k2] = _mm(sv["hmid"], do, "tn", bf16, "ffn_down_dw", tm=2816, tn=1024, tk=768, comm=sc.at(site + "dw"))
    du = _mm(dgu, w13t, "nn", f32, "ffn_up_dx", tn=1024, tk=2816, comm=sc.at(site + "ux"), a3=True)
    res = _norm_mod_bwd(sv["h"], du, dh, g, modv, s, d, nxt)
    return res[0], res[1], res[2], dgate, (res[3] if nxt is not None else None)


def _taps(ext, w, left, tr, sign):
    acc = None
    for k in range(w.shape[0]):
        o = HALO + sign * (k - left)
        term = w[k:k + 1] * ext[o:o + tr]
        acc = term if acc is None else acc + term
    return acc


def _gates(xa, wa, ba, wx, bx, lam, nb):
    bs = xa.shape[1] // nb
    out = []
    for dr in range(2):
        pa = jnp.concatenate([_bdot(xa[:, n * bs:(n + 1) * bs], wa[dr, n]) for n in range(nb)], axis=1)
        px = jnp.concatenate([_bdot(xa[:, n * bs:(n + 1) * bs], wx[dr, n]) for n in range(nb)], axis=1)
        rg = jax.nn.sigmoid(pa + ba[dr:dr + 1])
        ig = jax.nn.sigmoid(px + bx[dr:dr + 1])
        log_a = -LRU_C * rg * jax.nn.softplus(-lam[dr:dr + 1])
        a = jnp.exp(log_a)
        u = jnp.sqrt(1.0 - jnp.exp(2.0 * log_a)) * (ig * xa)
        out += [a, u]
    return out


def _combine_a(hf, hb, rg):
    return (hf + hb) * jax.nn.gelu(rg)


def _scan_order(kind, nT, nC):
    nL = nT - nC
    if kind == "F":
        return (lambda s: s), False
    if kind == "revF":
        return (lambda s: nT - 1 - s), True
    if kind == "B":
        return (lambda s: jnp.where(s < nC, nC - 1 - s, nT - 1 - (s - nC))), True
    return (lambda s: jnp.where(s < nL, nC + s, s - nL)), False


def _scan8(c, b, down):
    row = lax.broadcasted_iota(jnp.int32, c.shape, 0)
    for s in (1, 2, 4):
        sh = 8 - s if down else s
        cs, bs = pltpu.roll(c, sh, 0), pltpu.roll(b, sh, 0)
        ok = (row < 8 - s) if down else (row >= s)
        b = jnp.where(ok, c * bs + b, b)
        c = jnp.where(ok, c * cs, c)
    return c, b


def _prev8(x, x_in, down):
    row = lax.broadcasted_iota(jnp.int32, x.shape, 0)
    return jnp.where(row == (7 if down else 0), x_in, pltpu.roll(x, 7 if down else 1, 0))


def _scan_fwd(au, dr, d):
    R, tc = d.BW, d.tr
    tile_of, down = _scan_order("F" if dr == 0 else "B", d.T // tc, d.C // tc)

    def body(a_ref, u_ref, h_ref, hp_ref, st):
        @pl.when(pl.program_id(0) == 0)
        def _():
            st[...] = jnp.zeros_like(st)

        def grp(gi, h):
            r = pl.multiple_of((tc // 8 - 1 - gi if down else gi) * 8, 8)
            cc, bb = _scan8(a_ref[pl.ds(r, 8), :], u_ref[pl.ds(r, 8), :], down)
            hs = cc * h + bb
            h_ref[pl.ds(r, 8), :] = hs
            hp_ref[pl.ds(r, 8), :] = _prev8(hs, h, down)
            return hs[0:1] if down else hs[7:8]
        st[...] = lax.fori_loop(0, tc // 8, grp, st[...])

    return pl.pallas_call(
        body, name="lru_scan", grid=(d.T // tc,),
        in_specs=[pl.BlockSpec((tc, R), lambda s: (tile_of(s), 2 * dr)), pl.BlockSpec((tc, R), lambda s: (tile_of(s), 2 * dr + 1))],
        out_specs=[pl.BlockSpec((tc, R), lambda s: (tile_of(s), 0))] * 2, out_shape=[SDS((d.T, R), f32)] * 2,
        scratch_shapes=[pltpu.VMEM((1, R), f32)], compiler_params=_params(("arbitrary",)),
    )(au, au)


def _scan_bwd(dh, au, hp, dr, d):
    R, tc = d.BW, d.tr
    tile_of, down = _scan_order("revF" if dr == 0 else "revB", d.T // tc, d.C // tc)

    def body(dh_ref, a_ref, hp_ref, o_ref, st):
        @pl.when(pl.program_id(0) == 0)
        def _():
            st[...] = jnp.zeros_like(st)

        def grp(gi, carry):
            lam, an = carry
            r = pl.multiple_of((tc // 8 - 1 - gi if down else gi) * 8, 8)
            at, dt, ht = a_ref[pl.ds(r, 8), :], dh_ref[pl.ds(r, 8), :], hp_ref[pl.ds(r, 8), :]
            cc, bb = _scan8(_prev8(at, an, down), dt, down)
            lt = cc * lam + bb
            o_ref[pl.ds(r, 8), 0:R] = lt * ht
            o_ref[pl.ds(r, 8), R:2 * R] = lt
            return (lt[0:1], at[0:1]) if down else (lt[7:8], at[7:8])
        lam, an = lax.fori_loop(0, tc // 8, grp, (st[0:1], st[1:2]))
        st[0:1] = lam
        st[1:2] = an

    return pl.pallas_call(
        body, name="lru_scan_bwd", grid=(d.T // tc,),
        in_specs=[pl.BlockSpec((tc, R), lambda s: (tile_of(s), 0)), pl.BlockSpec((tc, R), lambda s: (tile_of(s), 2 * dr)),
                  pl.BlockSpec((tc, R), lambda s: (tile_of(s), 0))],
        out_specs=pl.BlockSpec((tc, 2 * R), lambda s: (tile_of(s), 0)), out_shape=SDS((d.T, 2 * R), f32),
        scratch_shapes=[pltpu.VMEM((2, R), f32)], compiler_params=_params(("arbitrary",)),
    )(dh, au, hp)


def _swap_pairs(x):
    w = x.shape[1]
    lane = lax.broadcasted_iota(jnp.int32, x.shape, 1)
    return jnp.where(lane % 64 < 32, pltpu.roll(x, w - 32, 1), pltpu.roll(x, 32, 1))


def _att_masks(blk, nB, nCb, G):
    rows, cols = G * ATT_BLOCK, 3 * ATT_BLOCK
    qi = lax.broadcasted_iota(jnp.int32, (rows, cols), 0) % ATT_BLOCK
    kj = lax.broadcasted_iota(jnp.int32, (rows, cols), 1)
    rel = kj - ATT_BLOCK - qi
    kb = kj // ATT_BLOCK
    one = jnp.int32(1)
    latent = jnp.where(blk >= nCb, one, 0)
    prev_ok = jnp.where(blk - 1 >= nCb, latent, 0)
    next_ok = jnp.where(blk + 1 <= nB - 1, latent, 0)
    bv = jnp.where(kb == 0, prev_ok, jnp.where(kb == 1, latent, next_ok))
    return jnp.logical_and(jnp.abs(rel) <= ATT_BLOCK, bv > 0)


def _att_specs(d):
    G, nB = d.G, d.T // ATT_BLOCK
    blk = lambda f: pl.BlockSpec((ATT_BLOCK, HEAD_DIM), lambda kh, b: (f(b), kh))
    three = [blk(lambda b: jnp.maximum(b - 1, 0)), blk(lambda b: b), blk(lambda b: jnp.minimum(b + 1, nB - 1))]
    ctxs = pl.BlockSpec((d.C, HEAD_DIM), lambda kh, b: (0, kh))
    qs = pl.BlockSpec((ATT_BLOCK, G * HEAD_DIM), lambda kh, b: (b, kh))
    sk = pl.BlockSpec((None, G * ATT_BLOCK, 1), lambda kh, b: (kh, 0, 0))
    return qs, three, ctxs, sk


def _stack_heads(x, G):
    return jnp.concatenate([x[:, g * HEAD_DIM:(g + 1) * HEAD_DIM] for g in range(G)], axis=0)


def _unstack_heads(x, G):
    return jnp.concatenate([x[g * ATT_BLOCK:(g + 1) * ATT_BLOCK] for g in range(G)], axis=1)


def _att_probs(qg, kcat, kctx, sink, valid):
    scale = HEAD_DIM ** -0.5
    s = lax.dot_general(qg, kcat, _DIMS["nt"], preferred_element_type=f32) * scale
    s = jnp.where(valid, s, NEG_INF)
    sc = lax.dot_general(qg, kctx, _DIMS["nt"], preferred_element_type=f32) * scale
    m = jnp.maximum(jnp.maximum(jnp.max(s, axis=1, keepdims=True), jnp.max(sc, axis=1, keepdims=True)), sink)
    e, ec, es = jnp.exp(s - m), jnp.exp(sc - m), jnp.exp(sink - m)
    inv = 1.0 / (jnp.sum(e, axis=1, keepdims=True) + jnp.sum(ec, axis=1, keepdims=True) + es)
    return e, ec, es, inv


def _attn_fwd(qr, kr, vb, sinkrows, d):
    G, nB, nCb = d.G, d.T // ATT_BLOCK, d.C // ATT_BLOCK
    qs, three, ctxs, sk = _att_specs(d)

    def body(q_ref, k0, k1, k2, v0, v1, v2, kc_ref, vc_ref, s_ref, o_ref):
        b = pl.program_id(1)
        qg = _stack_heads(q_ref[...], G)
        kcat = jnp.concatenate([k0[...], k1[...], k2[...]], axis=0)
        vcat = jnp.concatenate([v0[...], v1[...], v2[...]], axis=0)
        e, ec, _, inv = _att_probs(qg, kcat, kc_ref[...], s_ref[...], _att_masks(b, nB, nCb, G))
        p, pc = e * inv, ec * inv
        o = jnp.dot(p.astype(bf16), vcat, preferred_element_type=f32) + jnp.dot(pc.astype(bf16), vc_ref[...], preferred_element_type=f32)
        o_ref[...] = _unstack_heads(o, G).astype(bf16)

    return pl.pallas_call(
        body, name="attn", grid=(d.NKV, nB), in_specs=[qs] + three + three + [ctxs, ctxs, sk],
        out_specs=qs, out_shape=SDS((d.T, d.NH * HEAD_DIM), bf16), compiler_params=_params(("parallel", "arbitrary")),
    )(qr, kr, kr, kr, vb, vb, vb, kr, vb, sinkrows)


def _attn_bwd(qr, kr, vb, sinkrows, dy, d):
    G, nB, nCb = d.G, d.T // ATT_BLOCK, d.C // ATT_BLOCK
    qs, three, ctxs, sk = _att_specs(d)
    KW = d.NKV * HEAD_DIM
    part = pl.BlockSpec((ATT_BLOCK, HEAD_DIM), lambda kh, b: (b, kh))

    def body(q_ref, k0, k1, k2, v0, v1, v2, kc_ref, vc_ref, s_ref, dy_ref,
             dq_ref, dk0, dk1, dk2, dv0, dv1, dv2, dkc_ref, dvc_ref, ds_ref):
        b = pl.program_id(1)
        scale = HEAD_DIM ** -0.5
        qg = _stack_heads(q_ref[...], G)
        kcat = jnp.concatenate([k0[...], k1[...], k2[...]], axis=0)
        vcat = jnp.concatenate([v0[...], v1[...], v2[...]], axis=0)
        kctx, vctx = kc_ref[...], vc_ref[...]
        e, ec, es, inv = _att_probs(qg, kcat, kctx, s_ref[...], _att_masks(b, nB, nCb, G))
        p, pc, ps = e * inv, ec * inv, es * inv
        dog = _stack_heads(dy_ref[...], G).astype(bf16)
        dp = lax.dot_general(dog, vcat, _DIMS["nt"], preferred_element_type=f32)
        dpc = lax.dot_general(dog, vctx, _DIMS["nt"], preferred_element_type=f32)
        delta = jnp.sum(p * dp, axis=1, keepdims=True) + jnp.sum(pc * dpc, axis=1, keepdims=True)
        ds = (p * (dp - delta) * scale).astype(bf16)
        dsc = (pc * (dpc - delta) * scale).astype(bf16)
        dq = jnp.dot(ds, kcat, preferred_element_type=f32) + jnp.dot(dsc, kctx, preferred_element_type=f32)
        dq_ref[...] = _unstack_heads(dq, G)
        dk = lax.dot_general(ds, qg, _DIMS["tn"], preferred_element_type=f32)
        dv = lax.dot_general(p.astype(bf16), dog, _DIMS["tn"], preferred_element_type=f32)
        for j, (rk, rv) in enumerate(((dk0, dv0), (dk1, dv1), (dk2, dv2))):
            rk[...] = dk[j * ATT_BLOCK:(j + 1) * ATT_BLOCK]
            rv[...] = dv[j * ATT_BLOCK:(j + 1) * ATT_BLOCK]

        @pl.when(b == 0)
        def _():
            dkc_ref[...] = jnp.zeros_like(dkc_ref)
            dvc_ref[...] = jnp.zeros_like(dvc_ref)
            ds_ref[...] = jnp.zeros_like(ds_ref)

        dkc_ref[...] += lax.dot_general(dsc, qg, _DIMS["tn"], preferred_element_type=f32)
        dvc_ref[...] += lax.dot_general(pc.astype(bf16), dog, _DIMS["tn"], preferred_element_type=f32)
        ds_ref[...] += -ps * delta

    kv = SDS((d.T, KW), f32)
    return pl.pallas_call(
        body, name="attn_bwd", grid=(d.NKV, nB), in_specs=[qs] + three + three + [ctxs, ctxs, sk, qs],
        out_specs=[qs] + [part] * 6 + [ctxs, ctxs, sk],
        out_shape=[SDS((d.T, d.NH * HEAD_DIM), f32)] + [kv] * 6 + [SDS((d.C, KW), f32)] * 2 + [SDS((d.NKV, G * ATT_BLOCK, 1), f32)],
        compiler_params=_params(("parallel", "arbitrary")),
    )(qr, kr, kr, kr, vb, vb, vb, kr, vb, sinkrows, dy)


def _mixer_fwd(h, modv, P, l, tabs, d, sc, site):
    D, BW, T, C = d.D, d.BW, d.T, d.C
    u = _norm_mod_fwd(h, P["norm_g"][1:2], modv, 3, d)
    za = _mm(u, sc.env[("wint", l)], "nt", bf16, "in_proj_a", N=d.NA, tm=1408, comm=sc.at(site + "in_a"))
    gz = _mm(u, sc.env[("wint", l)], "nt", bf16, "in_proj_g", N=3 * D, tm=1408, b_off=d.NA, comm=sc.at(site + "in_g"))
    xa = _rowwise("lru_conv", lambda t: ([_taps(t.H[0], t.K[0], 2, d.tr, 1) + t.K[1]], [], []), T, C, d.tr,
                  halos=[(za, BW, 0)], consts=[P["rnn_conv_w"], P["rnn_conv_b"]], outs=[(BW, f32)])[0]
    lru = [P["lru_w_a"], P["lru_b_a"], P["lru_w_x"], P["lru_b_x"], P["lru_lambda"]]
    au = _rowwise("lru_gates", lambda t: ([_gates(t.R[0], *t.K, d.NB)], [], []), T, C, d.tr,
                  rows=[(xa, BW, 0, 0)], consts=lru, outs=[(4 * BW, f32)])[0]
    hf, hpf = _scan_fwd(au, 0, d)
    hb, hpb = _scan_fwd(au, 1, d)
    ya = _rowwise("lru_out", lambda t: ([_combine_a(*t.R)], [], []), T, C, d.tr,
                  rows=[(hf, BW, 0, 0), (hb, BW, 0, 0), (za, BW, 1, 0)], outs=[(BW, bf16)])[0]
    yb = _rowwise("sconv", lambda t: ([t.R[0] * _taps(t.H[0] * t.H[1], t.K[0], 1, d.tr, 1)], [], []), T, C, d.tr,
                  rows=[(za, BW, 2, 0)], halos=[(za, BW, 3), (za, BW, 4)], consts=[P["sc_conv_w"]], outs=[(BW, bf16)])[0]
    QW, KW = d.NH * HEAD_DIM, d.NKV * HEAD_DIM

    def rope(t):
        q, k, v, cs, sn = t.R
        cq, sq, ck, skn = jnp.tile(cs, (1, d.NH)), jnp.tile(sn, (1, d.NH)), jnp.tile(cs, (1, d.NKV)), jnp.tile(sn, (1, d.NKV))
        return [q * cq + _swap_pairs(q) * sq, k * ck + _swap_pairs(k) * skn, v], [], []
    kcb = (5 * BW + QW) // KW
    qr, kr, vb = _rowwise("rope", rope, T, C, d.tr,
                          rows=[(za, QW, (5 * BW) // QW, 0), (za, KW, kcb, 0), (za, KW, kcb + 1, 0),
                                (tabs["cos"], HEAD_DIM, 0, 0), (tabs["sin"], HEAD_DIM, 0, 0)],
                          outs=[(QW, bf16), (KW, bf16), (KW, bf16)])
    yatt = _attn_fwd(qr, kr, vb, P["sinkrows"], d)
    ys = (ya, yb, yatt)
    ps = [_mm(ys[i], sc.env[("wbt", l)], "nt", bf16, "lift", b_lead=i, tm=1408, tn=1024) for i in range(N_BRANCH)]

    def merge(t):
        gzv, bm = t.R[0], t.K[0]
        acc = None
        for i in range(N_BRANCH):
            term = jax.nn.sigmoid(gzv[:, i * D:(i + 1) * D] + bm[i:i + 1]) * t.R[1 + i]
            acc = term if acc is None else acc + term
        return [acc], [], []
    merged = _rowwise("merge", merge, T, C, d.trw, rows=[(gz, 3 * D, 0, 0)] + [(p, D, 0, 0) for p in ps],
                      consts=[P["b_merge"]], outs=[(D, bf16)])[0]
    y, hn = _mm(merged, sc.env[("wout", l)], "nn", bf16, "out_proj", tn=1024, comm=sc.at(site + "out"),
                resid=(h, modv, 5, 1.0, d.C))
    sv = dict(h=h, u=u, za=za, gz=gz, xa=xa, au=au, hf=hf, hpf=hpf, hb=hb, hpb=hpb, ys=ys, qr=qr, kr=kr, vb=vb,
              ps=ps, merged=merged, y=y)
    return hn, sv


def _mixer_bwd(dh, sv, modv, P, l, tabs, d, sc, site, pre=None, nxt=None):
    W = dict(wint=sc.env[("wint", l)], wbt=sc.env[("wbt", l)], wout=sc.env[("wout", l)])
    D, BW, T, C = d.D, d.BW, d.T, d.C
    QW, KW = d.NH * HEAD_DIM, d.NKV * HEAD_DIM
    za, gz = sv["za"], sv["gz"]
    dy, dgate = pre if pre is not None else _resid_bwd(dh, sv["y"], modv, 5, 1.0, d)
    dmerged = _mm(dy, W["wout"], "nt", f32, "out_proj_dx", tn=1024)
    sc.env[("g", "wout", l)] = _mm(sv["merged"], dy, "tn", bf16, "out_proj_dw", tm=1024, tn=2048, tk=768)

    def merge_bwd(t):
        gzv, bm, dm = t.R[0], t.K[0], t.R[4]
        dps, dgs, dbs = [], [], []
        for i in range(N_BRANCH):
            gate = jax.nn.sigmoid(gzv[:, i * D:(i + 1) * D] + bm[i:i + 1])
            dps.append(dm * gate)
            dgi = dm * t.R[1 + i] * gate * (1.0 - gate)
            dgs.append(dgi)
            dbs.append(jnp.sum(dgi, axis=0, keepdims=True))
        return [dps, dgs], [jnp.concatenate(dbs, axis=0)], []
    dp, dgz, dbm = _rowwise("merge_bwd", merge_bwd, T, C, d.trw,
                            rows=[(gz, 3 * D, 0, 0)] + [(p, D, 0, 0) for p in sv["ps"]] + [(dmerged, D, 0, 0)],
                            consts=[P["b_merge"]], outs=[(3 * D, bf16), (3 * D, bf16)], accs=[(N_BRANCH, D)])
    dys = [_mm(dp, W["wbt"], "nn", f32, "lift_dx", K=D, a_off=i * D, b_lead=i, tn=1024) for i in range(N_BRANCH)]
    sc.env[("g", "wbt", l)] = jnp.stack([_mm(dp, sv["ys"][i], "tn", bf16, "lift_dw", M=D, a_off=i * D, tm=1024, tn=1024, tk=768)
                                         for i in range(N_BRANCH)])
    def out_bwd(t):
        _, vjp = jax.vjp(_combine_a, t.R[1], t.R[2], t.R[3])
        dhf, _, drg = vjp(t.R[0])
        return [dhf, drg], [], []
    dhs, drg = _rowwise("lru_out_bwd", out_bwd, T, C, d.tr,
                        rows=[(dys[0], BW, 0, 0), (sv["hf"], BW, 0, 0), (sv["hb"], BW, 0, 0), (za, BW, 1, 0)],
                        outs=[(BW, f32), (BW, bf16)])
    dau0 = _scan_bwd(dhs, sv["au"], sv["hpf"], 0, d)
    dau1 = _scan_bwd(dhs, sv["au"], sv["hpb"], 1, d)
    lru = [P["lru_w_a"], P["lru_b_a"], P["lru_w_x"], P["lru_b_x"], P["lru_lambda"]]

    def gates_bwd(t):
        _, vjp = jax.vjp(lambda xa, *k: _gates(xa, *k, d.NB), t.R[0], *t.K)
        d0, d1 = t.R[1], t.R[2]
        g = vjp([d0[:, :BW], d0[:, BW:], d1[:, :BW], d1[:, BW:]])
        return [g[0]], list(g[1:]), []
    dxa, dwa, dba, dwx, dbx, dlam = _rowwise(
        "lru_gates_bwd", gates_bwd, T, C, d.tr, rows=[(sv["xa"], BW, 0, 0), (dau0, 2 * BW, 0, 0), (dau1, 2 * BW, 0, 0)],
        consts=lru, outs=[(BW, f32)], accs=[p.shape for p in lru])
    sc.env[("gs", "lru_w_a", l)], sc.env[("gs", "lru_w_x", l)] = dwa, dwx

    def conv_bwd(t):
        dxe, xe, w = t.H[0], t.H[1], t.K[0]
        cur = dxe[HALO:HALO + d.tr]
        dw = jnp.concatenate([jnp.sum(cur * xe[HALO + k - 2:HALO + k - 2 + d.tr], axis=0, keepdims=True) for k in range(w.shape[0])], axis=0)
        return [_taps(dxe, w, 2, d.tr, -1)], [dw, jnp.sum(cur, axis=0, keepdims=True)], []
    drx, dcw, dcb = _rowwise("lru_conv_bwd", conv_bwd, T, C, d.tr, halos=[(dxa, BW, 0), (za, BW, 0)],
                             consts=[P["rnn_conv_w"]], outs=[(BW, bf16)], accs=[P["rnn_conv_w"].shape, (1, BW)])
    def sconv_bwd(t):
        scg, sx, sb, dyb = t.H
        w, tr = t.K[0], d.tr
        me = scg * sx
        dsb = dyb[HALO:HALO + tr] * _taps(me, w, 1, tr, 1)
        dce = dyb * sb
        dm = _taps(dce, w, 1, tr, -1)
        cur = dce[HALO:HALO + tr]
        dw = jnp.concatenate([jnp.sum(cur * me[HALO + k - 1:HALO + k - 1 + tr], axis=0, keepdims=True) for k in range(w.shape[0])], axis=0)
        return [[dsb, dm * sx[HALO:HALO + tr], dm * scg[HALO:HALO + tr]]], [dw], []
    dsc, dscw = _rowwise("sconv_bwd", sconv_bwd, T, C, d.tr, halos=[(za, BW, 3), (za, BW, 4), (za, BW, 2), (dys[1], BW, 0)],
                         consts=[P["sc_conv_w"]], outs=[(3 * BW, bf16)], accs=[P["sc_conv_w"].shape])
    dqr, dk0, dk1, dk2, dv0, dv1, dv2, dkc, dvc, dsink = _attn_bwd(sv["qr"], sv["kr"], sv["vb"], P["sinkrows"], dys[2], d)
    nB, nCb = T // ATT_BLOCK, C // ATT_BLOCK

    def att_join(t):
        dq, a1, a0, a2, b1, b0, b2, kc, vc, cs, sn = t.R
        cq, sq, ck, skn = jnp.tile(cs, (1, d.NH)), jnp.tile(sn, (1, d.NH)), jnp.tile(cs, (1, d.NKV)), jnp.tile(sn, (1, d.NKV))
        up = t.i + 1 <= nB - 1
        dn = t.i >= 1
        isc = t.seg == 0
        dk = a1 + jnp.where(up, a0, 0.0) + jnp.where(dn, a2, 0.0) + jnp.where(isc, kc, 0.0)
        dv = b1 + jnp.where(up, b0, 0.0) + jnp.where(dn, b2, 0.0) + jnp.where(isc, vc, 0.0)
        return [[dq * cq + _swap_pairs(dq * sq), dk * ck + _swap_pairs(dk * skn), dv]], [], []
    dqkv = _rowwise("attn_join", att_join, T, C, ATT_BLOCK,
                    rows=[(dqr, QW, 0, 0), (dk1, KW, 0, 0), (dk0, KW, 0, 1), (dk2, KW, 0, -1), (dv1, KW, 0, 0), (dv0, KW, 0, 1),
                          (dv2, KW, 0, -1), (dkc, KW, 0, 0), (dvc, KW, 0, 0), (tabs["cos"], HEAD_DIM, 0, 0),
                          (tabs["sin"], HEAD_DIM, 0, 0)], outs=[(QW + 2 * KW, bf16)])[0]
    dz = jnp.concatenate([drx, drg, dsc, dqkv, dgz], axis=1)
    du = _mm(dz, W["wint"], "nn", f32, "in_proj_dx", tn=1024, tk=3200, comm=sc.at(site + "in_x"))
    sc.env[("g", "wint", l)] = _mm(dz, sv["u"], "tn", bf16, "in_proj_dw", tm=2560, tn=1024, tk=768, comm=sc.at(site + "in_w"))
    res = _norm_mod_bwd(sv["h"], du, dh, P["norm_g"][1:2], modv, 3, d, nxt)
    grads = dict(norm_g1=res[1], dss=res[2], dgate=dgate, b_merge=dbm, rnn_conv_w=dcw, rnn_conv_b=dcb,
                 lru_b_a=dba, lru_b_x=dbx, lru_lambda=dlam, sc_conv_w=dscw, sinkrows=dsink)
    return res[0], grads, (res[3] if nxt is not None else None)


def _exchange(name, comm):
    ns, nd = len(comm.srcs), len(comm.dsts)

    def body(*refs):
        descs = comm.descriptors(refs[:ns], refs[ns + nd:ns + 2 * nd], refs[ns + 2 * nd:], _coords())
        _comm_start(descs)
        _comm_mid(descs)
        _comm_wait(descs)

    res = pl.pallas_call(
        body, name=name, in_specs=[ANY] * (ns + nd), out_specs=[ANY] * nd, out_shape=[SDS(a.shape, a.dtype) for a in comm.dsts],
        input_output_aliases={ns + q: q for q in range(nd)}, scratch_shapes=comm.scratch(),
    )(*comm.srcs, *comm.dsts)
    comm.store(res)


def _allgather(name, a):
    env = {"g": lax.empty((N_DEV,) + a.shape, a.dtype)}
    cm = _Comm(env)
    cm.add(a, "g", lambda ref, me, pi, r: ref, lambda ref, sender, r: ref.at[sender])
    _exchange(name, cm)
    return env["g"]


def _sum8(x):
    n = x.shape[1]
    tn = _tile(n, 2048, 8)

    def body(x_ref, o_ref):
        acc = x_ref[0]
        for j in range(1, N_DEV):
            acc = acc + x_ref[j]
        o_ref[...] = acc
    return pl.pallas_call(body, name="sum8", grid=(n // tn,), in_specs=[pl.BlockSpec((N_DEV, tn, 128), lambda i: (0, i, 0))],
                          out_specs=pl.BlockSpec((tn, 128), lambda i: (i, 0)), out_shape=SDS((n, 128), f32),
                          compiler_params=_params(("parallel",)))(x)


def _cast_direct(w):
    ly, R, Cn = w.shape
    tc = _tile(Cn, 512, 128)
    spec = pl.BlockSpec((None, R, tc), lambda l, j: (l, 0, j))

    def body(w_ref, o_ref):
        o_ref[...] = w_ref[...].astype(bf16)
    return pl.pallas_call(body, name="cast", grid=(ly, Cn // tc), in_specs=[spec], out_specs=spec, out_shape=SDS(w.shape, bf16),
                          compiler_params=_params(("parallel", "parallel")))(w)


def _cast_transposed(w):
    four = w.ndim == 4
    ly, I, (K, Nl) = w.shape[0], (w.shape[1] if four else 1), w.shape[-2:]
    tk = _tile(K, 256, 128)
    mid = (lambda i: (i,)) if four else (lambda i: ())
    nn = (None,) * (w.ndim - 2)

    def body(w_ref, o_ref):
        o_ref[...] = w_ref[...].T.astype(bf16)
    return pl.pallas_call(
        body, name="cast_t", grid=(ly, I, K // tk), in_specs=[pl.BlockSpec(nn + (tk, Nl), lambda l, i, k: (l,) + mid(i) + (k, 0))],
        out_specs=pl.BlockSpec(nn + (Nl, tk), lambda l, i, k: (l,) + mid(i) + (0, k)), out_shape=SDS(w.shape[:-2] + (Nl, K), bf16),
        compiler_params=_params(("parallel", "parallel", "parallel")))(w)


def _adam(g, w, m, v):
    m = ADAM_B1 * m + (1.0 - ADAM_B1) * g
    v = ADAM_B2 * v + (1.0 - ADAM_B2) * (g * g)
    m_hat = m / (1.0 - ADAM_B1 ** ADAM_STEP)
    v_hat = v / (1.0 - ADAM_B2 ** ADAM_STEP)
    delta = -ADAM_LR * (m_hat / (jnp.sqrt(v_hat) + ADAM_EPS) + ADAM_WD * w)
    return delta, m, v


def _slot_sum(x):
    acc = x[0].astype(f32)
    for r in range(1, x.shape[0]):
        acc = acc + x[r].astype(f32)
    return acc


def _adamw_direct(g, w, m, v, slots):
    ly, R, Cn = w.shape
    tr = R if R <= 1024 else _tile(R, 512, 16)
    tc = _tile(Cn, max(128, (256 * 1024) // tr // 128 * 128), 128)
    spec = pl.BlockSpec((None, tr, tc), lambda l, i, j: (l, i, j))
    gspec = pl.BlockSpec((None, N_DEV, tr, tc), lambda l, i, j: (l, 0, i, j)) if slots else spec

    def body(g_ref, w_ref, m_ref, v_ref, go, do, mo, vo):
        gv = _slot_sum(g_ref[...]) if slots else g_ref[...]
        go[...] = gv
        do[...], mo[...], vo[...] = _adam(gv, w_ref[...], m_ref[...], v_ref[...])
    return pl.pallas_call(body, name="adamw", grid=(ly, R // tr, Cn // tc), in_specs=[gspec, spec, spec, spec], out_specs=[spec] * 4,
                          out_shape=[SDS(w.shape, f32)] * 4, compiler_params=_params(("parallel",) * 3))(g, w, m, v)


def _adamw_transposed(g, w, m, v):
    four = w.ndim == 4
    ly, I, (K, Nl) = w.shape[0], (w.shape[1] if four else 1), w.shape[-2:]
    tk = _tile(K, 128, 128)
    mid = (lambda i: (i,)) if four else (lambda i: ())
    nn = (None,) * (w.ndim - 3)
    spec = pl.BlockSpec((None,) + nn + (tk, Nl), lambda l, i, k: (l,) + mid(i) + (k, 0))
    gspec = pl.BlockSpec((None, N_DEV) + nn + (Nl, tk), lambda l, i, k: (l, 0) + mid(i) + (0, k))

    def body(g_ref, w_ref, m_ref, v_ref, go, do, mo, vo):
        gv = _slot_sum(g_ref[...]).T
        go[...] = gv
        do[...], mo[...], vo[...] = _adam(gv, w_ref[...], m_ref[...], v_ref[...])
    return pl.pallas_call(body, name="adamw_t", grid=(ly, I, K // tk), in_specs=[gspec, spec, spec, spec], out_specs=[spec] * 4,
                          out_shape=[SDS(w.shape, f32)] * 4, compiler_params=_params(("parallel",) * 3))(g, w, m, v)


def _ada_fwd(cvec, ada_w, ada_b_cols):
    ly, D, cols = ada_w.shape
    tn = _tile(cols, 768, 128)

    def body(c_ref, w_ref, b_ref, o_ref):
        o_ref[...] = _bdot(jax.nn.silu(c_ref[...]), w_ref[...]) + b_ref[...]
    return pl.pallas_call(
        body, name="ada", grid=(ly, cols // tn),
        in_specs=[pl.BlockSpec((16, D), lambda l, j: (0, 0)), pl.BlockSpec((None, D, tn), lambda l, j: (l, 0, j)),
                  pl.BlockSpec((None, 1, tn), lambda l, j: (l, 0, j))],
        out_specs=pl.BlockSpec((None, 16, tn), lambda l, j: (l, 0, j)), out_shape=SDS((ly, 16, cols), f32),
        compiler_params=_params(("parallel", "parallel")))(cvec, ada_w, ada_b_cols)


def _ada_bwd(cvec, ada_w, dm):
    ly, D, cols = ada_w.shape
    tn = _tile(cols, 768, 128)

    def body(c_ref, w_ref, d_ref, gw_ref, gc_ref):
        first = jnp.logical_and(pl.program_id(0) == 0, pl.program_id(1) == 0)

        @pl.when(first)
        def _():
            gc_ref[...] = jnp.zeros_like(gc_ref)

        def f(cv, w):
            return _bdot(jax.nn.silu(cv), w)
        _, vjp = jax.vjp(f, c_ref[...], w_ref[...])
        dc, dw = vjp(d_ref[...])
        gw_ref[...] = dw
        gc_ref[...] += dc
    return pl.pallas_call(
        body, name="ada_bwd", grid=(ly, cols // tn),
        in_specs=[pl.BlockSpec((16, D), lambda l, j: (0, 0)), pl.BlockSpec((None, D, tn), lambda l, j: (l, 0, j)),
                  pl.BlockSpec((None, 16, tn), lambda l, j: (l, 0, j))],
        out_specs=[pl.BlockSpec((None, D, tn), lambda l, j: (l, 0, j)), pl.BlockSpec((16, D), lambda l, j: (0, 0))],
        out_shape=[SDS(ada_w.shape, f32), SDS((16, D), f32)], compiler_params=_params(("arbitrary", "arbitrary")))(cvec, ada_w, dm)


def _flat_adamw(g, w, m, v):
    n = w.shape[0]
    tn = _tile(n, 1024, 8)
    spec = pl.BlockSpec((tn, 128), lambda i: (i, 0))

    def body(g_ref, w_ref, m_ref, v_ref, do, mo, vo):
        do[...], mo[...], vo[...] = _adam(g_ref[...], w_ref[...], m_ref[...], v_ref[...])
    return pl.pallas_call(body, name="adamw_small", grid=(n // tn,), in_specs=[spec] * 4, out_specs=[spec] * 3,
                          out_shape=[SDS(w.shape, f32)] * 3, compiler_params=_params(("parallel",)))(g, w, m, v)


def _pack(arrs):
    flat = jnp.concatenate([a.reshape(-1).astype(f32) for a in arrs])
    n = _round_up(flat.shape[0], 512 * 128)
    return jnp.pad(flat, (0, n - flat.shape[0])).reshape(n // 128, 128)


def _unpack(flat, shapes, lead=()):
    flat = flat.reshape(lead + (-1,))
    out, off = [], 0
    for s in shapes:
        sz = math.prod(s)
        out.append(flat[..., off:off + sz].reshape(lead + tuple(s)))
        off += sz
    return out


def _unshard_last(g):
    g = jnp.moveaxis(g, 0, -2)
    return g.reshape(g.shape[:-2] + (g.shape[-2] * g.shape[-1],))


class _Dims:
    pass


def _rope_tables(L, C, NH):
    rows = L // GRID_W
    row = jnp.repeat(jnp.arange(rows), GRID_W).astype(f32)
    col = jnp.tile(jnp.arange(GRID_W), rows).astype(f32)
    half = HEAD_DIM // 2
    inv = ROPE_BASE ** (-jnp.arange(0, half, 2, dtype=f32) / half)
    ar, ac = row[:, None] * inv, col[:, None] * inv
    cos = jnp.concatenate([jnp.cos(ar), jnp.cos(ar), jnp.cos(ac), jnp.cos(ac)], axis=-1)
    sin = jnp.concatenate([-jnp.sin(ar), jnp.sin(ar), -jnp.sin(ac), jnp.sin(ac)], axis=-1)
    cos = jnp.concatenate([jnp.ones((C, HEAD_DIM), f32), cos], axis=0)
    sin = jnp.concatenate([jnp.zeros((C, HEAD_DIM), f32), sin], axis=0)
    return dict(cos=cos, sin=sin)


def kernel(x, c, ctx, c_ctx, ada_w, ada_b, norm_g, ffn1_w13, ffn1_w2, w_in, b_merge, rnn_conv_w, rnn_conv_b, lru_w_a, lru_b_a, lru_w_x, lru_b_x, lru_lambda, sc_conv_w, attn_sink, w_branch, w_out, ffn2_w13, ffn2_w2, final_norm_g, loss_target, m_c_ctx, m_ada_w, m_ada_b, m_norm_g, m_ffn1_w13, m_ffn1_w2, m_w_in, m_b_merge, m_rnn_conv_w, m_rnn_conv_b, m_lru_w_a, m_lru_b_a, m_lru_w_x, m_lru_b_x, m_lru_lambda, m_sc_conv_w, m_attn_sink, m_w_branch, m_w_out, m_ffn2_w13, m_ffn2_w2, m_final_norm_g, v_c_ctx, v_ada_w, v_ada_b, v_norm_g, v_ffn1_w13, v_ffn1_w2, v_w_in, v_b_merge, v_rnn_conv_w, v_rnn_conv_b, v_lru_w_a, v_lru_b_a, v_lru_w_x, v_lru_b_x, v_lru_lambda, v_sc_conv_w, v_attn_sink, v_w_branch, v_w_out, v_ffn2_w13, v_ffn2_w2, v_final_norm_g):
    d = _Dims()
    L, D = x.shape[1], x.shape[2]
    C = ctx.shape[1]
    LY = ada_w.shape[0]
    d.D, d.C, d.T = D, C, C + L
    d.F = ffn1_w2.shape[1] * N_DEV
    d.Fp = _round_up(d.F, 512)
    d.IN = w_in.shape[2] * N_DEV
    d.BW = w_branch.shape[2]
    d.NH = attn_sink.shape[1]
    d.NA = d.IN - N_BRANCH * D
    d.NKV = (d.NA - 5 * d.BW - d.NH * HEAD_DIM) // (2 * HEAD_DIM)
    d.G = d.NH // d.NKV
    d.NB = lru_w_a.shape[2]
    d.tr = _tile(C, 256, 8)
    d.trw = _tile(C, 128, 8)
    assert d.NH * HEAD_DIM == d.BW and L % GRID_W == 0 and C % ATT_BLOCK == 0 and L % ATT_BLOCK == 0
    sh13, sh2, shin, shd = 2 * d.F // N_DEV, d.F // N_DEV, d.IN // N_DEV, D // N_DEV
    cols9 = N_MOD * D // N_DEV
    me3 = _coords()
    me = _index(me3)

    sharded = [norm_g, b_merge, rnn_conv_w, lru_b_a, lru_b_x, lru_lambda, sc_conv_w]
    shapes1 = [a.shape for a in sharded] + [(D,)]
    g1 = _allgather("gather_small", _pack(sharded + [c.reshape(-1)]))
    parts = _unpack(g1, shapes1, (N_DEV,))
    norm_g_f, b_merge_f, conv_w_f, lru_b_a_f, lru_b_x_f, lru_lam_f, sc_w_f = [_unshard_last(p) for p in parts[:-1]]
    cvec = jnp.concatenate([parts[-1], c_ctx[None], jnp.zeros((7, D), f32)], axis=0)
    ada_b_cols = lax.dynamic_slice_in_dim(ada_b, me * cols9, cols9, axis=1)[:, None, :]
    modcols = _ada_fwd(cvec, ada_w, ada_b_cols)
    g2 = _allgather("gather_mod", modcols)
    modall = jnp.moveaxis(g2, 0, 2).reshape(LY, 16, N_MOD, D)
    mod_lat = lax.dynamic_index_in_dim(modall, me, axis=1, keepdims=False)
    modv = jnp.stack([modall[:, N_DEV], mod_lat], axis=1)

    assert LY == 2
    off13 = lambda j: (j // 4) * d.Fp + (j % 4) * sh13
    tr_ = lambda a: jnp.swapaxes(a, 1, 2)
    cast_t = lambda w: _cast_direct(tr_(w))
    c13a, c13b = cast_t(ffn1_w13), cast_t(ffn2_w13)
    wt = {"w13a": (c13a, off13, sh13, 2 * d.Fp, (D,), False), "w13b": (c13b, off13, sh13, 2 * d.Fp, (D,), False),
          "w2a": (_cast_direct(ffn1_w2), lambda j: j * sh2, sh2, d.Fp, (D,), False),
          "w2b": (_cast_direct(ffn2_w2), lambda j: j * sh2, sh2, d.Fp, (D,), False),
          "wint": (cast_t(w_in), lambda j: j * shin, shin, d.IN, (D,), False),
          "wbt": (_cast_transposed(w_branch), lambda j: j * shd, shd, D, (d.BW,), True),
          "wout": (_cast_direct(w_out), lambda j: j * shd, shd, D, (D,), False)}
    env = {}
    for name, (cw, off, sh, tot, rest, three) in wt.items():
        for l in range(LY):
            env[(name, l)] = lax.empty(((N_BRANCH, tot) if three else (tot,)) + rest, bf16)
        env[("r", name)] = lax.empty((LY, N_DEV) + ((N_BRANCH, sh) if three else (sh,)) + rest, bf16)
    zpad = jnp.zeros((max(d.Fp - d.F, 16), D), bf16)

    def part(sh, h):
        return (0, sh) if h is None else (h * (sh // 2), sh // 2)

    def rows_of(ref, start, n, three):
        sl = pl.ds(start if isinstance(start, int) else pl.multiple_of(start, 16), n)
        return ref.at[:, sl] if three else ref.at[sl]

    def add_gather(cm, name, l, h=None):
        cw, off, sh, tot, rest, three = wt[name]
        r0, n = part(sh, h)
        cm.gather(cw, (name, l), lambda ref: rows_of(ref.at[l], r0, n, three),
                  lambda ref, origin: rows_of(ref, off(origin) + r0, n, three))
        if d.Fp > d.F and h in (None, 0) and name[:2] in ("w1", "w2"):
            for base in ((0, d.Fp) if name[:3] == "w13" else (0,)):
                cm.fill(zpad, (name, l), lambda ref, base=base: ref.at[pl.ds(base + d.F, d.Fp - d.F)])

    def add_rs(cm, name, l, h=None):
        cw, off, sh, tot, rest, three = wt[name]
        r0, n = part(sh, h)
        cm.add(env[("g", name, l)], ("r", name), lambda ref, me_, pi, r: rows_of(ref, off(pi) + r0, n, three),
               lambda ref, sender, r: rows_of(ref.at[l, r], r0, n, three))

    plan = {"F0b.up": [("ag", "w2b", 0), ("ag", "w13a", 1, 0)], "F0b.down": [("ag", "w13a", 1, 1)], "F1b.up": [("ag", "w2b", 1)],
            "B0m.in_x": [("lru",)]}
    for l in range(LY):
        plan.update({f"F{l}a.up": [("ag", "w2a", l), ("ag", "wint", l, 0)], f"F{l}a.down": [("ag", "wint", l, 1), ("ag", "wbt", l)],
                     f"F{l}m.in_a": [("ag", "wout", l), ("ag", "w13b", l, 0)], f"F{l}m.in_g": [("ag", "w13b", l, 1)]})
    for l in range(LY):
        plan.update({f"B{l}b.dw": [("rs", "w13b", l, 0)], f"B{l}b.ux": [("rs", "w13b", l, 1)],
                     f"B{l}m.in_w": [("rs", "wout", l), ("rs", "wbt", l), ("rs", "w2b", l)], f"B{l}a.dx": [("rs", "wint", l, 0)],
                     f"B{l}a.uw": [("rs", "wint", l, 1)], f"B{l}a.dw": [("rs", "w13a", l, 0)]})
    plan.update({"B1a.ux": [("rs", "w13a", 1, 1)], "B0b.dx": [("rs", "w2a", 1)], "B0a.ux": [("rs", "w13a", 0, 1), ("rs", "w2a", 0)]})

    def add_lru(cm):
        for nm in ("lru_w_a", "lru_w_x"):
            gs = jnp.stack([env[("gs", nm, l)] for l in range(LY)]).reshape(-1, 128)
            env[("all", nm)] = lax.empty((N_DEV,) + gs.shape, f32)
            cm.add(gs, ("all", nm), lambda ref, me_, pi, r: ref, lambda ref, sender, r: ref.at[sender])
    sc = _Sched(plan, env, {"ag": add_gather, "rs": add_rs, "lru": add_lru})
    first = _Comm(env)
    add_gather(first, "w13a", 0)
    _exchange("gather_first", first)

    tabs = _rope_tables(L, C, d.NH)
    h = jnp.concatenate([ctx[0], x[0]], axis=0)
    Ps, saves = [], []
    for l in range(LY):
        sinkrows = jnp.repeat(attn_sink[l].reshape(d.NKV, d.G), ATT_BLOCK, axis=1)[:, :, None]
        Ps.append(dict(norm_g=norm_g_f[l], b_merge=b_merge_f[l], rnn_conv_w=conv_w_f[l], rnn_conv_b=rnn_conv_b[l][None],
                       lru_w_a=lru_w_a[l], lru_b_a=lru_b_a_f[l], lru_w_x=lru_w_x[l], lru_b_x=lru_b_x_f[l], lru_lambda=lru_lam_f[l],
                       sc_conv_w=sc_w_f[l], sinkrows=sinkrows))
    for l in range(LY):
        P = Ps[l]
        h, s1 = _ffn_fwd(h, modv[l], P["norm_g"][0:1], ("w13a", l), ("w2a", l), 0, d, sc, f"F{l}a.")
        h, s2 = _mixer_fwd(h, modv[l], P, l, tabs, d, sc, f"F{l}m.")
        h, s3 = _ffn_fwd(h, modv[l], P["norm_g"][2:3], ("w13b", l), ("w2b", l), 6, d, sc, f"F{l}b.")
        saves.append((s1, s2, s3))

    fng = final_norm_g[None]

    def loss_fn(t):
        def f(hv, g):
            y = hv * lax.rsqrt(jnp.mean(hv * hv, axis=-1, keepdims=True) + EPS) * g
            e = y - t.R[1]
            return 0.5 * jnp.sum(jnp.mean(e * e, axis=-1))
        lat = (t.seg == 1).astype(f32)
        val, (dhv, dg) = jax.value_and_grad(f, argnums=(0, 1))(t.R[0], t.K[0])
        return [dhv * lat], [dg * lat, jnp.full((1, 128), val * lat, f32)], []
    dh, dfng, lossv = _rowwise("loss", loss_fn, d.T, C, d.tr, rows=[(h, D, 0, 0), (loss_target[0], D, 0, -(C // d.tr))],
                               consts=[fng], outs=[(D, f32)], accs=[(1, D), (1, 128)])
    loss = lax.psum(lossv[0, 0], AXES)

    gl = [None] * LY
    pre = None
    for l in reversed(range(LY)):
        P = Ps[l]
        s1, s2, s3 = saves[l]
        dh, dg2, dss2, dgt2, pre = _ffn_bwd(dh, s3, modv[l], P["norm_g"][2:3], ("w13b", l), ("w2b", l), 6, d, sc, f"B{l}b.",
                                            pre, (s2["y"], modv[l], 5, 1.0))
        dh, gm, pre = _mixer_bwd(dh, s2, modv[l], P, l, tabs, d, sc, f"B{l}m.", pre, (s1["o"], modv[l], 2, 0.5))
        below = (saves[l - 1][2]["o"], modv[l - 1], 8, 0.5) if l > 0 else None
        dh, dg0, dss0, dgt0, pre = _ffn_bwd(dh, s1, modv[l], P["norm_g"][0:1], ("w13a", l), ("w2a", l), 0, d, sc, f"B{l}a.",
                                            pre, below)
        gm.update(norm_g=jnp.concatenate([dg0, gm["norm_g1"], dg2], axis=0),
                  dmod=jnp.concatenate([dss0, dgt0, gm["dss"], gm["dgate"], dss2, dgt2], axis=1))
        gl[l] = gm
    grad_x = dh[C:][None]
    st = lambda k: jnp.stack([g[k] for g in gl])

    r13a, r2a, rin, rwb, rwo, r13b, r2b = [env[("r", k)] for k in ("w13a", "w2a", "wint", "wbt", "wout", "w13b", "w2b")]
    big = {}
    adamw_tr = lambda r, w, m, v: [tr_(o) for o in _adamw_direct(r, tr_(w), tr_(m), tr_(v), True)]
    big["ffn1_w13"] = adamw_tr(r13a, ffn1_w13, m_ffn1_w13, v_ffn1_w13)
    big["ffn2_w13"] = adamw_tr(r13b, ffn2_w13, m_ffn2_w13, v_ffn2_w13)
    big["w_in"] = adamw_tr(rin, w_in, m_w_in, v_w_in)
    big["w_branch"] = _adamw_transposed(rwb, w_branch, m_w_branch, v_w_branch)
    big["ffn1_w2"] = _adamw_direct(r2a, ffn1_w2, m_ffn1_w2, v_ffn1_w2, True)
    big["ffn2_w2"] = _adamw_direct(r2b, ffn2_w2, m_ffn2_w2, v_ffn2_w2, True)
    big["w_out"] = _adamw_direct(rwo, w_out, m_w_out, v_w_out, True)

    sink_g = jnp.stack([jnp.sum(g["sinkrows"].reshape(d.NKV, d.G, ATT_BLOCK), axis=-1).reshape(d.NH) for g in gl])
    small_full = dict(norm_g=st("norm_g"), b_merge=st("b_merge"), rnn_conv_w=st("rnn_conv_w"), rnn_conv_b=st("rnn_conv_b")[:, 0],
                      lru_b_a=st("lru_b_a"), lru_b_x=st("lru_b_x"), lru_lambda=st("lru_lambda"), sc_conv_w=st("sc_conv_w"), attn_sink=sink_g, final_norm_g=dfng[0])
    names_s = list(small_full)
    dmod = st("dmod")
    pk = _pack([small_full[k] for k in names_s] + [dmod])
    g4 = _allgather("gather_small_grads", pk)
    tot = _unpack(_sum8(g4), [small_full[k].shape for k in names_s] + [dmod.shape])
    sums = dict(zip(names_s, tot[:-1]))
    dmod_sum = tot[-1].reshape(LY, 2, N_MOD * D)
    dmod_all = _unpack(g4, [small_full[k].shape for k in names_s] + [dmod.shape], (N_DEV,))[-1].reshape(N_DEV, LY, 2, N_MOD * D)
    dm_rows = jnp.concatenate([jnp.moveaxis(dmod_all[:, :, 1], 0, 1), dmod_sum[:, 0:1], jnp.zeros((LY, 7, N_MOD * D), f32)], axis=1)
    dm_cols = lax.dynamic_slice_in_dim(dm_rows, me * cols9, cols9, axis=2)
    g_ada_w, dcv = _ada_bwd(cvec, ada_w, dm_cols)
    g5 = _allgather("gather_cctx", _pack([dcv[N_DEV]]))
    g_c_ctx = _sum8(g5).reshape(-1)[:D]
    g_ada_b = dmod_sum[:, 0] + dmod_sum[:, 1]
    ada_out = _adamw_direct(g_ada_w, ada_w, m_ada_w, v_ada_w, False)

    def shard_last(a, n):
        return lax.dynamic_slice_in_dim(a, me * n, n, axis=a.ndim - 1)
    local_g = dict(c_ctx=g_c_ctx, ada_b=g_ada_b, norm_g=shard_last(sums["norm_g"], shd), b_merge=shard_last(sums["b_merge"], shd),
                   rnn_conv_w=shard_last(sums["rnn_conv_w"], d.BW // N_DEV), rnn_conv_b=sums["rnn_conv_b"],
                   lru_b_a=shard_last(sums["lru_b_a"], d.BW // N_DEV), lru_b_x=shard_last(sums["lru_b_x"], d.BW // N_DEV),
                   lru_lambda=shard_last(sums["lru_lambda"], d.BW // N_DEV), sc_conv_w=shard_last(sums["sc_conv_w"], d.BW // N_DEV),
                   attn_sink=sums["attn_sink"], final_norm_g=sums["final_norm_g"])
    wmv = dict(c_ctx=(c_ctx, m_c_ctx, v_c_ctx), ada_b=(ada_b, m_ada_b, v_ada_b), norm_g=(norm_g, m_norm_g, v_norm_g),
               b_merge=(b_merge, m_b_merge, v_b_merge), rnn_conv_w=(rnn_conv_w, m_rnn_conv_w, v_rnn_conv_w),
               rnn_conv_b=(rnn_conv_b, m_rnn_conv_b, v_rnn_conv_b),
               lru_b_a=(lru_b_a, m_lru_b_a, v_lru_b_a), lru_b_x=(lru_b_x, m_lru_b_x, v_lru_b_x),
               lru_lambda=(lru_lambda, m_lru_lambda, v_lru_lambda), sc_conv_w=(sc_conv_w, m_sc_conv_w, v_sc_conv_w),
               attn_sink=(attn_sink, m_attn_sink, v_attn_sink), final_norm_g=(final_norm_g, m_final_norm_g, v_final_norm_g))
    names_l = list(local_g)
    shapes_l = [wmv[k][0].shape for k in names_l]
    gp = _pack([local_g[k].reshape(wmv[k][0].shape) for k in names_l])
    outs_s = _flat_adamw(gp, _pack([wmv[k][0] for k in names_l]), _pack([wmv[k][1] for k in names_l]), _pack([wmv[k][2] for k in names_l]))
    small = {k: [local_g[k].reshape(wmv[k][0].shape)] for k in names_l}
    for o in outs_s:
        for k, a in zip(names_l, _unpack(o, shapes_l)):
            small[k].append(a)
    for nm, (w_, m_, v_) in (("lru_w_a", (lru_w_a, m_lru_w_a, v_lru_w_a)), ("lru_w_x", (lru_w_x, m_lru_w_x, v_lru_w_x))):
        gsum = _sum8(env[("all", nm)])
        flat = lambda a: a.reshape(-1, 128)
        small[nm] = [a.reshape(w_.shape) for a in (gsum, *_flat_adamw(gsum, flat(w_), flat(m_), flat(v_)))]

    order = ["c_ctx", "ada_w", "ada_b", "norm_g", "ffn1_w13", "ffn1_w2", "w_in", "b_merge", "rnn_conv_w", "rnn_conv_b", "lru_w_a",
             "lru_b_a", "lru_w_x", "lru_b_x", "lru_lambda", "sc_conv_w", "attn_sink", "w_branch", "w_out", "ffn2_w13", "ffn2_w2",
             "final_norm_g"]
    allo = dict(small)
    allo.update(big)
    allo["ada_w"] = ada_out
    res = [loss, grad_x]
    for q in range(4):
        res += [allo[k][q] for k in order]
    return tuple(res)
```

```python
import functools
import math

import jax
import jax.numpy as jnp
from jax import lax
from jax.experimental import pallas as pl
from jax.experimental.pallas import tpu as pltpu

f32 = jnp.float32
bf16 = jnp.bfloat16
SDS = jax.ShapeDtypeStruct

N_DEV = 8
AXES = ("x", "y", "c")
HEAD_DIM = 128
GRID_W = 64
ATT_BLOCK = 128
HALO = 16
ROPE_BASE = 10000.0
LRU_C = 8.0
EPS = 1e-6
NEG_INF = -1e30
N_MOD = 9
N_BRANCH = 3
ADAM_LR, ADAM_B1, ADAM_B2, ADAM_EPS, ADAM_WD, ADAM_STEP = 0.001, 0.9, 0.999, 1e-08, 0.01, 10
VMEM_LIMIT = 56 * 1024 * 1024
ANY = pl.BlockSpec(memory_space=pl.ANY)


def _round_up(n, m):
    return (n + m - 1) // m * m


def _tile(n, target, align):
    best = None
    t = align
    while t <= min(n, target):
        if n % t == 0:
            best = t
        t += align
    return best if best is not None else n


def _params(sem):
    return pltpu.CompilerParams(dimension_semantics=sem, vmem_limit_bytes=VMEM_LIMIT)


_DIMS = {"nn": (((1,), (0,)), ((), ())), "nt": (((1,), (1,)), ((), ())), "tn": (((0,), (0,)), ((), ()))}


def _coords():
    return lax.axis_index("x"), lax.axis_index("y"), lax.axis_index("c")


def _peer(xyc, r):
    x, y, c = xyc
    return (1 - x if r & 4 else x, 1 - y if r & 2 else y, 1 - c if r & 1 else c)


def _index(xyc):
    return xyc[0] * 4 + xyc[1] * 2 + xyc[2]


class _Comm:
    def __init__(self, env):
        self.env, self.srcs, self.dsts, self.keys, self.items, self.fills, self.gathers = env, [], [], [], [], [], []

    def gather(self, src, key, src_view, dst_view):
        self.gathers.append((self._src(src), self._dst(key), src_view, dst_view))

    def _src(self, a):
        for q, b in enumerate(self.srcs):
            if b is a:
                return q
        self.srcs.append(a)
        return len(self.srcs) - 1

    def _dst(self, key):
        if key not in self.keys:
            self.keys.append(key)
            self.dsts.append(self.env[key])
        return self.keys.index(key)

    def add(self, src, key, src_view, dst_view):
        self.items.append((self._src(src), self._dst(key), src_view, dst_view))

    def fill(self, src, key, dst_view):
        self.fills.append((self._src(src), self._dst(key), dst_view))

    def scratch(self):
        n = len(self.items) + len(self.gathers)
        return [pltpu.SemaphoreType.DMA((7 * n,)), pltpu.SemaphoreType.DMA((7 * n,)), pltpu.SemaphoreType.DMA((n + len(self.fills),))]

    def store(self, arrays):
        for key, a in zip(self.keys, arrays):
            self.env[key] = a

    def descriptors(self, srcs, dsts, sems, me3):
        send, recv, lsem = sems
        me = _index(me3)
        local, sends, recvs = [], [], []
        for q, (si, di, sv, dv) in enumerate(self.items):
            local.append(pltpu.make_async_copy(sv(srcs[si], me, me, 0), dv(dsts[di], me, 0), lsem.at[q]))
            for r in range(1, N_DEV):
                p3 = _peer(me3, r)
                pi = _index(p3)
                kw = dict(send_sem=send.at[q * 7 + r - 1], recv_sem=recv.at[q * 7 + r - 1], device_id=p3,
                          device_id_type=pl.DeviceIdType.MESH)
                sends.append(pltpu.make_async_remote_copy(src_ref=sv(srcs[si], me, pi, r), dst_ref=dv(dsts[di], me, r), **kw))
                recvs.append(pltpu.make_async_remote_copy(src_ref=sv(srcs[si], me, pi, r), dst_ref=dv(dsts[di], pi, r), **kw))
        n1 = len(self.items)
        landed, passed = [], []
        sib = _peer(me3, 1)
        for q, (si, di, sv, dv) in enumerate(self.gathers):
            def rc(src_ref, dst_ref, k, dev, base=(n1 + q) * 7):
                return pltpu.make_async_remote_copy(src_ref=src_ref, dst_ref=dst_ref, send_sem=send.at[base + k - 1],
                                                    recv_sem=recv.at[base + k - 1], device_id=dev, device_id_type=pl.DeviceIdType.MESH)
            mine = sv(srcs[si])
            local.append(pltpu.make_async_copy(mine, dv(dsts[di], me), lsem.at[n1 + q]))
            for r in (1, 2, 4, 6):
                sends.append(rc(mine, dv(dsts[di], me), r, _peer(me3, r)))
            for r in (2, 4, 6):
                blk = dv(dsts[di], _index(_peer(me3, r)))
                landed.append(rc(mine, blk, r, _peer(me3, r)))
                passed.append(rc(blk, blk, r ^ 1, sib))
            for k in (1, 3, 5, 7):
                recvs.append(rc(mine, dv(dsts[di], _index(_peer(me3, k))), k, sib))
        for q, (si, di, dv) in enumerate(self.fills):
            local.append(pltpu.make_async_copy(srcs[si], dv(dsts[di]), lsem.at[n1 + len(self.gathers) + q]))
        return local, sends, recvs, landed, passed


def _comm_start(descs):
    for cp in descs[0] + descs[1]:
        cp.start()


def _comm_mid(descs):
    for cp in descs[3]:
        cp.wait_recv()
    for cp in descs[4]:
        cp.start()


def _comm_wait(descs):
    local, sends, recvs, _, passed = descs
    sends = sends + passed
    for cp in recvs:
        cp.wait_recv()
    for cp in sends:
        cp.wait_send()
    for cp in local:
        cp.wait()


def _carry_call(core, name, grid, in_specs, out_specs, out_shape, scratch, args, comm):
    ni, no, nsc, ng = len(in_specs), len(out_specs), len(scratch), len(grid)
    if comm is None:
        def plain(*refs):
            core(refs[:ni], refs[ni:ni + no], refs[ni + no:])
        return pl.pallas_call(plain, name=name, grid=grid, in_specs=in_specs, out_specs=out_specs, out_shape=out_shape,
                              scratch_shapes=scratch, compiler_params=_params(("parallel",) * (ng - 1) + ("arbitrary",)))(*args)
    ns, nd = len(comm.srcs), len(comm.dsts)
    o0 = ni + ns + nd

    def body(*refs):
        pid = [pl.program_id(q) for q in range(ng)]
        first, last = pid[0] == 0, pid[0] == grid[0] - 1
        for q in range(1, ng):
            first = jnp.logical_and(first, pid[q] == 0)
            last = jnp.logical_and(last, pid[q] == grid[q] - 1)
        me3 = _coords()
        mk = lambda: comm.descriptors(refs[ni:ni + ns], refs[o0 + no:o0 + no + nd], refs[o0 + no + nd + nsc:], me3)

        @pl.when(first)
        def _():
            _comm_start(mk())

        if comm.gathers:
            lin = pid[0]
            for q in range(1, ng):
                lin = lin * grid[q] + pid[q]

            @pl.when(lin == min(math.prod(grid) - 1, int(0.85 * math.prod(grid))))
            def _():
                _comm_mid(mk())

        core(refs[:ni], refs[o0:o0 + no], refs[o0 + no + nd:o0 + no + nd + nsc])

        @pl.when(last)
        def _():
            _comm_wait(mk())

    res = pl.pallas_call(
        body, name=name, grid=grid, in_specs=list(in_specs) + [ANY] * (ns + nd), out_specs=list(out_specs) + [ANY] * nd,
        out_shape=list(out_shape) + [SDS(x.shape, x.dtype) for x in comm.dsts], scratch_shapes=list(scratch) + comm.scratch(),
        input_output_aliases={ni + ns + q: no + q for q in range(nd)}, compiler_params=_params(("arbitrary",) * ng),
    )(*args, *comm.srcs, *comm.dsts)
    comm.store(res[no:])
    return res[:no]


def _mm(a, b, mode, out_dtype, name, *, M=None, N=None, K=None, a_off=0, b_off=0, b_lead=None,
        tm=768, tn=512, tk=2048, comm=None, a3=False, resid=None):
    b2 = b.shape[1:] if b_lead is not None else b.shape
    if mode == "tn":
        K = a.shape[-2] if K is None else K
        M = (a.shape[0] * a.shape[2] if a3 else a.shape[1]) if M is None else M
        N = b2[1] if N is None else N
    else:
        M = a.shape[-2] if M is None else M
        K = (a.shape[0] * a.shape[2] if a3 else a.shape[1]) if K is None else K
        N = (b2[1] if mode == "nn" else b2[0]) if N is None else N
    g = math.gcd
    if mode == "tn":
        tm = _tile(g(M, a_off) if a_off else (a.shape[2] if a3 else M), tm, 128)
        tk = _tile(K, tk, 16)
    else:
        tm = _tile(M, tm, 16)
        tk = _tile(g(K, a_off) if a_off else (a.shape[2] if a3 else K), tk, 128)
    tn = _tile(g(N, b_off) if b_off else N, tn, 128)
    nk = K // tk
    ao, bo = (a_off // (tm if mode == "tn" else tk)), b_off // tn
    lead = () if b_lead is None else (b_lead,)
    ld = () if b_lead is None else (None,)
    if mode == "nn":
        a_spec = pl.BlockSpec((tm, tk), lambda i, j, k: (i, ao + k))
        b_spec = pl.BlockSpec(ld + (tk, tn), lambda i, j, k: lead + (k, bo + j))
    elif mode == "nt":
        a_spec = pl.BlockSpec((tm, tk), lambda i, j, k: (i, ao + k))
        b_spec = pl.BlockSpec(ld + (tn, tk), lambda i, j, k: lead + (bo + j, k))
    else:
        a_spec = pl.BlockSpec((tk, tm), lambda i, j, k: (k, ao + i))
        b_spec = pl.BlockSpec(ld + (tk, tn), lambda i, j, k: lead + (k, bo + j))
    if a3:
        if mode == "nn":
            nkc = a.shape[2] // tk
            a_spec = pl.BlockSpec((None, tm, tk), lambda i, j, k: (k // nkc, i, k % nkc))
        else:
            nmc = a.shape[2] // tm
            a_spec = pl.BlockSpec((None, tk, tm), lambda i, j, k: (i // nmc, k, i % nmc))
    dims = _DIMS[mode]

    def finish(ins, outs, val):
        outs[0][...] = val.astype(out_dtype)
        if resid is not None:
            gi, coef, C = resid[2:]
            row = pl.program_id(0) * tm + lax.broadcasted_iota(jnp.int32, (tm, 1), 0)
            gate = jnp.where(row < C, ins[3][0, gi:gi + 1], ins[3][1, gi:gi + 1])
            outs[1][...] = ins[2][...] + coef * gate * val

    def core(ins, outs, scr):
        a_ref, b_ref = ins[0], ins[1]
        k = pl.program_id(2)
        if nk == 1:
            finish(ins, outs, lax.dot_general(a_ref[...], b_ref[...], dims, preferred_element_type=f32))
        else:
            acc = scr[0]

            @pl.when(k == 0)
            def _():
                acc[...] = jnp.zeros_like(acc)

            acc[...] += lax.dot_general(a_ref[...], b_ref[...], dims, preferred_element_type=f32)

            @pl.when(k == nk - 1)
            def _():
                finish(ins, outs, acc[...])

    o_spec = pl.BlockSpec((tm, tn), lambda i, j, k: (i, j))
    in_specs, out_specs, out_shape, args = [a_spec, b_spec], [o_spec], [SDS((M, N), out_dtype)], (a, b)
    if resid is not None:
        in_specs += [o_spec, pl.BlockSpec((2, N_MOD, tn), lambda i, j, k: (0, 0, j))]
        out_specs, out_shape, args = out_specs + [o_spec], out_shape + [SDS((M, N), f32)], args + tuple(resid[:2])
    res = _carry_call(core, name, (M // tm, N // tn, nk), in_specs, out_specs, out_shape,
                      [] if nk == 1 else [pltpu.VMEM((tm, tn), f32)], args, comm)
    return res[0] if resid is None else res


class _Tile:
    pass


def _rowwise(name, fn, T, C, tr, rows=(), halos=(), consts=(), mods=(), outs=(), accs=(), maccs=()):
    nT, nC = T // tr, C // tr
    assert T % tr == 0 and C % tr == 0 and nC >= 1 and tr % HALO == 0
    r8, n8 = tr // HALO, T // HALO
    seg_of = lambda i: jnp.where(i >= nC, 1, 0)
    in_specs, args = [], []
    for arr, w, cb, ro in rows:
        nt = arr.shape[0] // tr
        in_specs.append(pl.BlockSpec((tr, w), (lambda i, cb=cb, ro=ro, nt=nt: (jnp.clip(i + ro, 0, nt - 1), cb))))
        args.append(arr)
    for arr, w, cb in halos:
        in_specs.append(pl.BlockSpec((HALO, w), lambda i, cb=cb: (jnp.maximum(i * r8 - 1, 0), cb)))
        in_specs.append(pl.BlockSpec((tr, w), lambda i, cb=cb: (i, cb)))
        in_specs.append(pl.BlockSpec((HALO, w), lambda i, cb=cb: (jnp.minimum((i + 1) * r8, n8 - 1), cb)))
        args += [arr, arr, arr]
    for arr in consts:
        in_specs.append(pl.BlockSpec(arr.shape, lambda i, nd=arr.ndim: (0,) * nd))
        args.append(arr)
    for arr in mods:
        in_specs.append(pl.BlockSpec((None,) + arr.shape[1:], lambda i, nd=arr.ndim: (seg_of(i),) + (0,) * (nd - 1)))
        args.append(arr)
    out_specs, out_shape = [], []
    for w, dt in outs:
        out_specs.append(pl.BlockSpec((tr, w), lambda i: (i, 0)))
        out_shape.append(SDS((T, w), dt))
    for shp in accs:
        out_specs.append(pl.BlockSpec(shp, lambda i, nd=len(shp): (0,) * nd))
        out_shape.append(SDS(shp, f32))
    for shp in maccs:
        out_specs.append(pl.BlockSpec((None,) + shp, lambda i, nd=len(shp): (seg_of(i),) + (0,) * nd))
        out_shape.append(SDS((2,) + shp, f32))
    nr, nh, nk, nm, no, na, nma = len(rows), len(halos), len(consts), len(mods), len(outs), len(accs), len(maccs)

    def body(*refs):
        i = pl.program_id(0)
        t = _Tile()
        t.i, t.seg = i, seg_of(i)
        p = 0
        t.R = [refs[p + k][...].astype(f32) for k in range(nr)]
        p += nr
        pvalid = jnp.logical_and(i != 0, i != nC)
        nvalid = jnp.logical_and(i != nC - 1, i != nT - 1)
        t.H = []
        for k in range(nh):
            pr, cu, nx = refs[p][...].astype(f32), refs[p + 1][...].astype(f32), refs[p + 2][...].astype(f32)
            p += 3
            pr = jnp.where(pvalid, pr, jnp.zeros_like(pr))
            nx = jnp.where(nvalid, nx, jnp.zeros_like(nx))
            t.H.append(jnp.concatenate([pr, cu, nx], axis=0))
        t.K = [refs[p + k][...] for k in range(nk)]
        p += nk
        t.M = [refs[p + k][...] for k in range(nm)]
        p += nm
        o, a, ma = fn(t)
        for k in range(no):
            ref = refs[p + k]
            pieces = o[k] if isinstance(o[k], (list, tuple)) else [o[k]]
            c0 = 0
            for pc in pieces:
                ref[:, c0:c0 + pc.shape[1]] = pc.astype(ref.dtype)
                c0 += pc.shape[1]
        p += no
        for k in range(na):
            ref = refs[p + k]

            @pl.when(i == 0)
            def _(ref=ref):
                ref[...] = jnp.zeros_like(ref)

            ref[...] += a[k]
        p += na
        for k in range(nma):
            ref = refs[p + k]

            @pl.when(jnp.logical_or(i == 0, i == nC))
            def _(ref=ref):
                ref[...] = jnp.zeros_like(ref)

            ref[...] += ma[k]

    res = pl.pallas_call(
        body, name=name, grid=(nT,), in_specs=in_specs, out_specs=out_specs, out_shape=out_shape,
        compiler_params=_params(("arbitrary",)),
    )(*args)
    return res


@jax.custom_vjp
def _bdot(a, b):
    return jnp.dot(a.astype(bf16), b.astype(bf16), preferred_element_type=f32)


def _bdot_fwd(a, b):
    return _bdot(a, b), (a, b)


def _bdot_bwd(res, ct):
    a, b = res
    ctb = ct.astype(bf16)
    da = lax.dot_general(ctb, b.astype(bf16), _DIMS["nt"], preferred_element_type=f32)
    db = lax.dot_general(a.astype(bf16), ctb, _DIMS["tn"], preferred_element_type=f32)
    return da, db


_bdot.defvjp(_bdot_fwd, _bdot_bwd)


def _rms_mod(h, g, shift, scale):
    y = h * lax.rsqrt(jnp.mean(h * h, axis=-1, keepdims=True) + EPS) * g
    return y * (1.0 + scale) + shift


def _norm_mod_fwd(h, g, modv, s, d):
    def fn(t):
        m = t.M[0]
        return [_rms_mod(t.R[0], t.K[0], m[s:s + 1], m[s + 1:s + 2])], [], []
    return _rowwise("norm_mod", fn, d.T, d.C, d.tr, rows=[(h, d.D, 0, 0)], consts=[g], mods=[modv],
                    outs=[(d.D, bf16)])[0]


def _norm_mod_bwd(h, du, dh, g, modv, s, d, nxt=None):
    def fn(t):
        m = t.M[0]
        _, vjp = jax.vjp(_rms_mod, t.R[0], t.K[0], m[s:s + 1], m[s + 1:s + 2])
        dx, dg, dsh, dsc = vjp(t.R[1])
        dhn = t.R[2] + dx
        dss = jnp.concatenate([dsh, dsc], axis=0)
        if nxt is None:
            return [dhn], [dg], [dss]
        gi, coef = nxt[2:]
        return ([dhn, coef * t.M[1][gi:gi + 1] * dhn], [dg], [dss, jnp.sum(coef * dhn * t.R[3], axis=0, keepdims=True)])
    rows = [(h, d.D, 0, 0), (du, d.D, 0, 0), (dh, d.D, 0, 0)]
    if nxt is None:
        return _rowwise("norm_mod_bwd", fn, d.T, d.C, d.tr, rows=rows, consts=[g], mods=[modv], outs=[(d.D, f32)],
                        accs=[(1, d.D)], maccs=[(2, d.D)])
    dhn, do, dg, dss, dgate = _rowwise("norm_mod_bwd", fn, d.T, d.C, d.tr, rows=rows + [(nxt[0], d.D, 0, 0)], consts=[g],
                                       mods=[modv, nxt[1]], outs=[(d.D, f32), (d.D, bf16)], accs=[(1, d.D)], maccs=[(2, d.D), (1, d.D)])
    return dhn, dg, dss, (do, dgate)


def _swiglu(g, u):
    return jax.nn.silu(g) * u


def _resid_bwd(dh, o, modv, gi, coef, d):
    def fn(t):
        dhv = t.R[0]
        return [coef * t.M[0][gi:gi + 1] * dhv], [], [jnp.sum(coef * dhv * t.R[1], axis=0, keepdims=True)]
    return _rowwise("resid_bwd", fn, d.T, d.C, d.tr, rows=[(dh, d.D, 0, 0), (o, d.D, 0, 0)], mods=[modv],
                    outs=[(d.D, bf16)], maccs=[(1, d.D)])


class _Sched:
    def __init__(self, plan, env, builders):
        self.plan, self.env, self.builders = plan, env, builders

    def at(self, site):
        specs = self.plan.get(site)
        if not specs:
            return None
        cm = _Comm(self.env)
        for kind, *args in specs:
            self.builders[kind](cm, *args)
        return cm


def _ffn_up(u, w13t, d, comm):
    T, D, Fp = d.T, d.D, d.Fp
    tm, tn = _tile(T, 768, 16), _tile(Fp, 512, 128)

    def core(ins, outs, scr):
        uv, w_ref = ins[0][...], ins[1]
        gt = lax.dot_general(uv, w_ref[0], _DIMS["nt"], preferred_element_type=f32)
        up = lax.dot_general(uv, w_ref[1], _DIMS["nt"], preferred_element_type=f32)
        outs[0][0] = gt.astype(bf16)
        outs[0][1] = up.astype(bf16)
        outs[1][...] = _swiglu(gt, up).astype(bf16)

    return _carry_call(core, "ffn_up", (T // tm, Fp // tn),
                       [pl.BlockSpec((tm, D), lambda i, j: (i, 0)), pl.BlockSpec((2, tn, D), lambda i, j: (0, j, 0))],
                       [pl.BlockSpec((2, tm, tn), lambda i, j: (0, i, j)), pl.BlockSpec((tm, tn), lambda i, j: (i, j))],
                       [SDS((2, T, Fp), bf16), SDS((T, Fp), bf16)], [], (u, w13t.reshape(2, Fp, D)), comm)


def _ffn_down_dx(do, w2, gu, d, comm):
    T, D, Fp = d.T, d.D, d.Fp
    tm, tn = _tile(T, 768, 16), _tile(Fp, 512, 128)

    def core(ins, outs, scr):
        dhm = lax.dot_general(ins[0][...], ins[1][...], _DIMS["nt"], preferred_element_type=f32)
        gt, up = ins[2][0].astype(f32), ins[2][1].astype(f32)
        sig = jax.nn.sigmoid(gt)
        silu = gt * sig
        outs[0][0] = (dhm * up * (sig + silu * (1.0 - sig))).astype(bf16)
        outs[0][1] = (dhm * silu).astype(bf16)

    return _carry_call(core, "ffn_down_dx", (T // tm, Fp // tn),
                       [pl.BlockSpec((tm, D), lambda i, j: (i, 0)), pl.BlockSpec((tn, D), lambda i, j: (j, 0)),
                        pl.BlockSpec((2, tm, tn), lambda i, j: (0, i, j))],
                       [pl.BlockSpec((2, tm, tn), lambda i, j: (0, i, j))], [SDS((2, T, Fp), bf16)], [], (do, w2, gu), comm)[0]


def _ffn_fwd(h, modv, g, k13, k2, s, d, sc, site):
    u = _norm_mod_fwd(h, g, modv, s, d)
    gu, hmid = _ffn_up(u, sc.env[k13], d, sc.at(site + "up"))
    o, hn = _mm(hmid, sc.env[k2], "nn", bf16, "ffn_down", tn=512, tk=5632, comm=sc.at(site + "down"),
                resid=(h, modv, s + 2, 0.5, d.C))
    return hn, dict(h=h, u=u, gu=gu, hmid=hmid, o=o)


def _ffn_bwd(dh, sv, modv, g, k13, k2, s, d, sc, site, pre=None, nxt=None):
    w13t, w2 = sc.env[k13], sc.env[k2]
    do, dgate = pre if pre is not None else _resid_bwd(dh, sv["o"], modv, s + 2, 0.5, d)
    dgu = _ffn_down_dx(do, w2, sv["gu"], d, sc.at(site + "dx"))
    sc.env[("g",) + k13] = _mm(dgu, sv["u"], "tn", bf16, "ffn_up_dw", tm=2816, tn=1024, tk=768, comm=sc.at(site + "uw"), a3=True)
    sc.env[("g",) + k2] = _mm(sv["hmid"], do, "tn", bf16, "ffn_down_dw", tm=2816, tn=1024, tk=768, comm=sc.at(site + "dw"))
    du = _mm(dgu, w13t, "nn", f32, "ffn_up_dx", tn=1024, tk=2816, comm=sc.at(site + "ux"), a3=True)
    res = _norm_mod_bwd(sv["h"], du, dh, g, modv, s, d, nxt)
    return res[0], res[1], res[2], dgate, (res[3] if nxt is not None else None)


def _taps(ext, w, left, tr, sign):
    acc = None
    for k in range(w.shape[0]):
        o = HALO + sign * (k - left)
        term = w[k:k + 1] * ext[o:o + tr]
        acc = term if acc is None else acc + term
    return acc


def _gates(xa, wa, ba, wx, bx, lam, nb):
    bs = xa.shape[1] // nb
    out = []
    for dr in range(2):
        pa = jnp.concatenate([_bdot(xa[:, n * bs:(n + 1) * bs], wa[dr, n]) for n in range(nb)], axis=1)
        px = jnp.concatenate([_bdot(xa[:, n * bs:(n + 1) * bs], wx[dr, n]) for n in range(nb)], axis=1)
        rg = jax.nn.sigmoid(pa + ba[dr:dr + 1])
        ig = jax.nn.sigmoid(px + bx[dr:dr + 1])
        log_a = -LRU_C * rg * jax.nn.softplus(-lam[dr:dr + 1])
        a = jnp.exp(log_a)
        u = jnp.sqrt(1.0 - jnp.exp(2.0 * log_a)) * (ig * xa)
        out += [a, u]
    return out


def _combine_a(hf, hb, rg):
    return (hf + hb) * jax.nn.gelu(rg)


def _scan_order(kind, nT, nC):
    nL = nT - nC
    if kind == "F":
        return (lambda s: s), False
    if kind == "revF":
        return (lambda s: nT - 1 - s), True
    if kind == "B":
        return (lambda s: jnp.where(s < nC, nC - 1 - s, nT - 1 - (s - nC))), True
    return (lambda s: jnp.where(s < nL, nC + s, s - nL)), False


def _scan8(c, b, down):
    row = lax.broadcasted_iota(jnp.int32, c.shape, 0)
    for s in (1, 2, 4):
        sh = 8 - s if down else s
        cs, bs = pltpu.roll(c, sh, 0), pltpu.roll(b, sh, 0)
        ok = (row < 8 - s) if down else (row >= s)
        b = jnp.where(ok, c * bs + b, b)
        c = jnp.where(ok, c * cs, c)
    return c, b


def _prev8(x, x_in, down):
    row = lax.broadcasted_iota(jnp.int32, x.shape, 0)
    return jnp.where(row == (7 if down else 0), x_in, pltpu.roll(x, 7 if down else 1, 0))


def _scan_fwd(au, dr, d):
    R, tc = d.BW, d.tr
    tile_of, down = _scan_order("F" if dr == 0 else "B", d.T // tc, d.C // tc)

    def body(a_ref, u_ref, h_ref, hp_ref, st):
        @pl.when(pl.program_id(0) == 0)
        def _():
            st[...] = jnp.zeros_like(st)

        def grp(gi, h):
            r = pl.multiple_of((tc // 8 - 1 - gi if down else gi) * 8, 8)
            cc, bb = _scan8(a_ref[pl.ds(r, 8), :], u_ref[pl.ds(r, 8), :], down)
            hs = cc * h + bb
            h_ref[pl.ds(r, 8), :] = hs
            hp_ref[pl.ds(r, 8), :] = _prev8(hs, h, down)
            return hs[0:1] if down else hs[7:8]
        st[...] = lax.fori_loop(0, tc // 8, grp, st[...])

    return pl.pallas_call(
        body, name="lru_scan", grid=(d.T // tc,),
        in_specs=[pl.BlockSpec((tc, R), lambda s: (tile_of(s), 2 * dr)), pl.BlockSpec((tc, R), lambda s: (tile_of(s), 2 * dr + 1))],
        out_specs=[pl.BlockSpec((tc, R), lambda s: (tile_of(s), 0))] * 2, out_shape=[SDS((d.T, R), f32)] * 2,
        scratch_shapes=[pltpu.VMEM((1, R), f32)], compiler_params=_params(("arbitrary",)),
    )(au, au)


def _scan_bwd(dh, au, hp, dr, d):
    R, tc = d.BW, d.tr
    tile_of, down = _scan_order("revF" if dr == 0 else "revB", d.T // tc, d.C // tc)

    def body(dh_ref, a_ref, hp_ref, o_ref, st):
        @pl.when(pl.program_id(0) == 0)
        def _():
            st[...] = jnp.zeros_like(st)

        def grp(gi, carry):
            lam, an = carry
            r = pl.multiple_of((tc // 8 - 1 - gi if down else gi) * 8, 8)
            at, dt, ht = a_ref[pl.ds(r, 8), :], dh_ref[pl.ds(r, 8), :], hp_ref[pl.ds(r, 8), :]
            cc, bb = _scan8(_prev8(at, an, down), dt, down)
            lt = cc * lam + bb
            o_ref[pl.ds(r, 8), 0:R] = lt * ht
            o_ref[pl.ds(r, 8), R:2 * R] = lt
            return (lt[0:1], at[0:1]) if down else (lt[7:8], at[7:8])
        lam, an = lax.fori_loop(0, tc // 8, grp, (st[0:1], st[1:2]))
        st[0:1] = lam
        st[1:2] = an

    return pl.pallas_call(
        body, name="lru_scan_bwd", grid=(d.T // tc,),
        in_specs=[pl.BlockSpec((tc, R), lambda s: (tile_of(s), 0)), pl.BlockSpec((tc, R), lambda s: (tile_of(s), 2 * dr)),
                  pl.BlockSpec((tc, R), lambda s: (tile_of(s), 0))],
        out_specs=pl.BlockSpec((tc, 2 * R), lambda s: (tile_of(s), 0)), out_shape=SDS((d.T, 2 * R), f32),
        scratch_shapes=[pltpu.VMEM((2, R), f32)], compiler_params=_params(("arbitrary",)),
    )(dh, au, hp)


def _swap_pairs(x):
    w = x.shape[1]
    lane = lax.broadcasted_iota(jnp.int32, x.shape, 1)
    return jnp.where(lane % 64 < 32, pltpu.roll(x, w - 32, 1), pltpu.roll(x, 32, 1))


def _att_bias(G):
    rows, cols = G * ATT_BLOCK, 3 * ATT_BLOCK
    qi = lax.broadcasted_iota(jnp.int32, (rows, cols), 0) % ATT_BLOCK
    kj = lax.broadcasted_iota(jnp.int32, (rows, cols), 1)
    band = jnp.abs(kj - ATT_BLOCK - qi) <= ATT_BLOCK
    kb = kj // ATT_BLOCK
    out = [jnp.zeros((rows, cols), bool)]
    for p in (False, True):
        for n in (False, True):
            out.append(band & ((kb == 1) | ((kb == 0) & p) | ((kb == 2) & n)))
    return jnp.where(jnp.stack(out), 0.0, NEG_INF).astype(f32)


def _att_specs(d):
    G, nB, nCb = d.G, d.T // ATT_BLOCK, d.C // ATT_BLOCK
    KW, QW = d.NKV * HEAD_DIM, d.NH * HEAD_DIM
    blk = lambda f: pl.BlockSpec((ATT_BLOCK, KW), lambda b: (f(b), 0))
    three = [blk(lambda b: jnp.maximum(b - 1, 0)), blk(lambda b: b), blk(lambda b: jnp.minimum(b + 1, nB - 1))]
    ctxs = pl.BlockSpec((d.C, KW), lambda b: (0, 0))
    qs = pl.BlockSpec((ATT_BLOCK, QW), lambda b: (b, 0))
    sk = pl.BlockSpec((d.NKV, G * ATT_BLOCK, 1), lambda b: (0, 0, 0))
    variant = lambda b: jnp.where(b < nCb, 0, 1 + 2 * jnp.where(b - 1 >= nCb, 1, 0) + jnp.where(b + 1 <= nB - 1, 1, 0))
    bias = pl.BlockSpec((None, G * ATT_BLOCK, 3 * ATT_BLOCK), lambda b: (variant(b), 0, 0))
    return qs, three, ctxs, sk, bias


def _stack_heads(x, G):
    return jnp.concatenate([x[:, g * HEAD_DIM:(g + 1) * HEAD_DIM] for g in range(G)], axis=0)


def _unstack_heads(x, G):
    return jnp.concatenate([x[g * ATT_BLOCK:(g + 1) * ATT_BLOCK] for g in range(G)], axis=1)


def _att_probs(qg, kcat, kctx, sink, bias):
    scale = HEAD_DIM ** -0.5
    s = lax.dot_general(qg, kcat, _DIMS["nt"], preferred_element_type=f32) * scale + bias
    sc = lax.dot_general(qg, kctx, _DIMS["nt"], preferred_element_type=f32) * scale
    m = jnp.maximum(jnp.maximum(jnp.max(s, axis=1, keepdims=True), jnp.max(sc, axis=1, keepdims=True)), sink)
    e, ec, es = jnp.exp(s - m), jnp.exp(sc - m), jnp.exp(sink - m)
    inv = 1.0 / (jnp.sum(e, axis=1, keepdims=True) + jnp.sum(ec, axis=1, keepdims=True) + es)
    return e, ec, es, inv


def _attn_fwd(qr, kr, vb, sinkrows, d):
    G, nB, nCb = d.G, d.T // ATT_BLOCK, d.C // ATT_BLOCK
    qs, three, ctxs, sk, bs = _att_specs(d)

    def body(q_ref, k0, k1, k2, v0, v1, v2, kc_ref, vc_ref, s_ref, b_ref, o_ref):
        bias = b_ref[...]
        for kh in range(d.NKV):
            ks, qsl = slice(kh * HEAD_DIM, (kh + 1) * HEAD_DIM), slice(kh * G * HEAD_DIM, (kh + 1) * G * HEAD_DIM)
            qg = _stack_heads(q_ref[:, qsl], G)
            kcat = jnp.concatenate([k0[:, ks], k1[:, ks], k2[:, ks]], axis=0)
            vcat = jnp.concatenate([v0[:, ks], v1[:, ks], v2[:, ks]], axis=0)
            e, ec, _, inv = _att_probs(qg, kcat, kc_ref[:, ks], s_ref[kh], bias)
            p, pc = e * inv, ec * inv
            o = jnp.dot(p.astype(bf16), vcat, preferred_element_type=f32) + jnp.dot(pc.astype(bf16), vc_ref[:, ks], preferred_element_type=f32)
            o_ref[:, qsl] = _unstack_heads(o, G).astype(bf16)

    return pl.pallas_call(
        body, name="attn", grid=(nB,), in_specs=[qs] + three + three + [ctxs, ctxs, sk, bs],
        out_specs=qs, out_shape=SDS((d.T, d.NH * HEAD_DIM), bf16), compiler_params=_params(("arbitrary",)),
    )(qr, kr, kr, kr, vb, vb, vb, kr, vb, sinkrows, _att_bias(G))


def _attn_bwd(qr, kr, vb, sinkrows, dy, d):
    G, nB, nCb = d.G, d.T // ATT_BLOCK, d.C // ATT_BLOCK
    qs, three, ctxs, sk, bs = _att_specs(d)
    KW = d.NKV * HEAD_DIM
    part = pl.BlockSpec((ATT_BLOCK, KW), lambda b: (b, 0))

    def body(q_ref, k0, k1, k2, v0, v1, v2, kc_ref, vc_ref, s_ref, dy_ref, b_ref,
             dq_ref, dk0, dk1, dk2, dv0, dv1, dv2, dkc_ref, dvc_ref, ds_ref):
        scale = HEAD_DIM ** -0.5
        bias = b_ref[...]

        @pl.when(pl.program_id(0) == 0)
        def _():
            dkc_ref[...] = jnp.zeros_like(dkc_ref)
            dvc_ref[...] = jnp.zeros_like(dvc_ref)
            ds_ref[...] = jnp.zeros_like(ds_ref)

        for kh in range(d.NKV):
            ks, qsl = slice(kh * HEAD_DIM, (kh + 1) * HEAD_DIM), slice(kh * G * HEAD_DIM, (kh + 1) * G * HEAD_DIM)
            qg = _stack_heads(q_ref[:, qsl], G)
            kcat = jnp.concatenate([k0[:, ks], k1[:, ks], k2[:, ks]], axis=0)
            vcat = jnp.concatenate([v0[:, ks], v1[:, ks], v2[:, ks]], axis=0)
            kctx, vctx = kc_ref[:, ks], vc_ref[:, ks]
            e, ec, es, inv = _att_probs(qg, kcat, kctx, s_ref[kh], bias)
            p, pc, ps = e * inv, ec * inv, es * inv
            dog = _stack_heads(dy_ref[:, qsl], G).astype(bf16)
            dp = lax.dot_general(dog, vcat, _DIMS["nt"], preferred_element_type=f32)
            dpc = lax.dot_general(dog, vctx, _DIMS["nt"], preferred_element_type=f32)
            delta = jnp.sum(p * dp, axis=1, keepdims=True) + jnp.sum(pc * dpc, axis=1, keepdims=True)
            ds = (p * (dp - delta) * scale).astype(bf16)
            dsc = (pc * (dpc - delta) * scale).astype(bf16)
            dq = jnp.dot(ds, kcat, preferred_element_type=f32) + jnp.dot(dsc, kctx, preferred_element_type=f32)
            dq_ref[:, qsl] = _unstack_heads(dq, G)
            dk = lax.dot_general(ds, qg, _DIMS["tn"], preferred_element_type=f32)
            dv = lax.dot_general(p.astype(bf16), dog, _DIMS["tn"], preferred_element_type=f32)
            for j, (rk, rv) in enumerate(((dk0, dv0), (dk1, dv1), (dk2, dv2))):
                rk[:, ks] = dk[j * ATT_BLOCK:(j + 1) * ATT_BLOCK]
                rv[:, ks] = dv[j * ATT_BLOCK:(j + 1) * ATT_BLOCK]
            dkc_ref[:, ks] += lax.dot_general(dsc, qg, _DIMS["tn"], preferred_element_type=f32)
            dvc_ref[:, ks] += lax.dot_general(pc.astype(bf16), dog, _DIMS["tn"], preferred_element_type=f32)
            ds_ref[kh] += -ps * delta

    kv = SDS((d.T, KW), f32)
    return pl.pallas_call(
        body, name="attn_bwd", grid=(nB,), in_specs=[qs] + three + three + [ctxs, ctxs, sk, qs, bs],
        out_specs=[qs] + [part] * 6 + [ctxs, ctxs, sk],
        out_shape=[SDS((d.T, d.NH * HEAD_DIM), f32)] + [kv] * 6 + [SDS((d.C, KW), f32)] * 2 + [SDS((d.NKV, G * ATT_BLOCK, 1), f32)],
        compiler_params=_params(("arbitrary",)),
    )(qr, kr, kr, kr, vb, vb, vb, kr, vb, sinkrows, dy, _att_bias(G))


def _mixer_fwd(h, modv, P, l, tabs, d, sc, site):
    D, BW, T, C = d.D, d.BW, d.T, d.C
    u = _norm_mod_fwd(h, P["norm_g"][1:2], modv, 3, d)
    za = _mm(u, sc.env[("wint", l)], "nt", bf16, "in_proj_a", N=d.NA, tm=1408, comm=sc.at(site + "in_a"))
    gz = _mm(u, sc.env[("wint", l)], "nt", bf16, "in_proj_g", N=3 * D, tm=1408, b_off=d.NA, comm=sc.at(site + "in_g"))
    xa = _rowwise("lru_conv", lambda t: ([_taps(t.H[0], t.K[0], 2, d.tr, 1) + t.K[1]], [], []), T, C, d.tr,
                  halos=[(za, BW, 0)], consts=[P["rnn_conv_w"], P["rnn_conv_b"]], outs=[(BW, f32)])[0]
    lru = [P["lru_w_a"], P["lru_b_a"], P["lru_w_x"], P["lru_b_x"], P["lru_lambda"]]
    au = _rowwise("lru_gates", lambda t: ([_gates(t.R[0], *t.K, d.NB)], [], []), T, C, d.tr,
                  rows=[(xa, BW, 0, 0)], consts=lru, outs=[(4 * BW, f32)])[0]
    hf, hpf = _scan_fwd(au, 0, d)
    hb, hpb = _scan_fwd(au, 1, d)
    ya = _rowwise("lru_out", lambda t: ([_combine_a(*t.R)], [], []), T, C, d.tr,
                  rows=[(hf, BW, 0, 0), (hb, BW, 0, 0), (za, BW, 1, 0)], outs=[(BW, bf16)])[0]
    yb = _rowwise("sconv", lambda t: ([t.R[0] * _taps(t.H[0] * t.H[1], t.K[0], 1, d.tr, 1)], [], []), T, C, d.tr,
                  rows=[(za, BW, 2, 0)], halos=[(za, BW, 3), (za, BW, 4)], consts=[P["sc_conv_w"]], outs=[(BW, bf16)])[0]
    QW, KW = d.NH * HEAD_DIM, d.NKV * HEAD_DIM

    def rope(t):
        q, k, v, cs, sn = t.R
        cq, sq, ck, skn = jnp.tile(cs, (1, d.NH)), jnp.tile(sn, (1, d.NH)), jnp.tile(cs, (1, d.NKV)), jnp.tile(sn, (1, d.NKV))
        return [q * cq + _swap_pairs(q) * sq, k * ck + _swap_pairs(k) * skn, v], [], []
    kcb = (5 * BW + QW) // KW
    qr, kr, vb = _rowwise("rope", rope, T, C, d.tr,
                          rows=[(za, QW, (5 * BW) // QW, 0), (za, KW, kcb, 0), (za, KW, kcb + 1, 0),
                                (tabs["cos"], HEAD_DIM, 0, 0), (tabs["sin"], HEAD_DIM, 0, 0)],
                          outs=[(QW, bf16), (KW, bf16), (KW, bf16)])
    yatt = _attn_fwd(qr, kr, vb, P["sinkrows"], d)
    ys = (ya, yb, yatt)
    ps = [_mm(ys[i], sc.env[("wbt", l)], "nt", bf16, "lift", b_lead=i, tm=1408, tn=1024) for i in range(N_BRANCH)]

    def merge(t):
        gzv, bm = t.R[0], t.K[0]
        acc = None
        for i in range(N_BRANCH):
            term = jax.nn.sigmoid(gzv[:, i * D:(i + 1) * D] + bm[i:i + 1]) * t.R[1 + i]
            acc = term if acc is None else acc + term
        return [acc], [], []
    merged = _rowwise("merge", merge, T, C, d.trw, rows=[(gz, 3 * D, 0, 0)] + [(p, D, 0, 0) for p in ps],
                      consts=[P["b_merge"]], outs=[(D, bf16)])[0]
    y, hn = _mm(merged, sc.env[("wout", l)], "nn", bf16, "out_proj", tn=1024, comm=sc.at(site + "out"),
                resid=(h, modv, 5, 1.0, d.C))
    sv = dict(h=h, u=u, za=za, gz=gz, xa=xa, au=au, hf=hf, hpf=hpf, hb=hb, hpb=hpb, ys=ys, qr=qr, kr=kr, vb=vb,
              ps=ps, merged=merged, y=y)
    return hn, sv


def _mixer_bwd(dh, sv, modv, P, l, tabs, d, sc, site, pre=None, nxt=None):
    W = dict(wint=sc.env[("wint", l)], wbt=sc.env[("wbt", l)], wout=sc.env[("wout", l)])
    D, BW, T, C = d.D, d.BW, d.T, d.C
    QW, KW = d.NH * HEAD_DIM, d.NKV * HEAD_DIM
    za, gz = sv["za"], sv["gz"]
    dy, dgate = pre if pre is not None else _resid_bwd(dh, sv["y"], modv, 5, 1.0, d)
    dmerged = _mm(dy, W["wout"], "nt", f32, "out_proj_dx", tn=1024)
    sc.env[("g", "wout", l)] = _mm(sv["merged"], dy, "tn", bf16, "out_proj_dw", tm=1024, tn=2048, tk=768)

    def merge_bwd(t):
        gzv, bm, dm = t.R[0], t.K[0], t.R[4]
        dps, dgs, dbs = [], [], []
        for i in range(N_BRANCH):
            gate = jax.nn.sigmoid(gzv[:, i * D:(i + 1) * D] + bm[i:i + 1])
            dps.append(dm * gate)
            dgi = dm * t.R[1 + i] * gate * (1.0 - gate)
            dgs.append(dgi)
            dbs.append(jnp.sum(dgi, axis=0, keepdims=True))
        return [dps, dgs], [jnp.concatenate(dbs, axis=0)], []
    dp, dgz, dbm = _rowwise("merge_bwd", merge_bwd, T, C, d.trw,
                            rows=[(gz, 3 * D, 0, 0)] + [(p, D, 0, 0) for p in sv["ps"]] + [(dmerged, D, 0, 0)],
                            consts=[P["b_merge"]], outs=[(3 * D, bf16), (3 * D, bf16)], accs=[(N_BRANCH, D)])
    dys = [_mm(dp, W["wbt"], "nn", f32, "lift_dx", K=D, a_off=i * D, b_lead=i, tn=1024) for i in range(N_BRANCH)]
    sc.env[("g", "wbt", l)] = jnp.stack([_mm(dp, sv["ys"][i], "tn", bf16, "lift_dw", M=D, a_off=i * D, tm=1024, tn=1024, tk=768)
                                         for i in range(N_BRANCH)])
    def out_bwd(t):
        _, vjp = jax.vjp(_combine_a, t.R[1], t.R[2], t.R[3])
        dhf, _, drg = vjp(t.R[0])
        return [dhf, drg], [], []
    dhs, drg = _rowwise("lru_out_bwd", out_bwd, T, C, d.tr,
                        rows=[(dys[0], BW, 0, 0), (sv["hf"], BW, 0, 0), (sv["hb"], BW, 0, 0), (za, BW, 1, 0)],
                        outs=[(BW, f32), (BW, bf16)])
    dau0 = _scan_bwd(dhs, sv["au"], sv["hpf"], 0, d)
    dau1 = _scan_bwd(dhs, sv["au"], sv["hpb"], 1, d)
    lru = [P["lru_w_a"], P["lru_b_a"], P["lru_w_x"], P["lru_b_x"], P["lru_lambda"]]

    def gates_bwd(t):
        _, vjp = jax.vjp(lambda xa, *k: _gates(xa, *k, d.NB), t.R[0], *t.K)
        d0, d1 = t.R[1], t.R[2]
        g = vjp([d0[:, :BW], d0[:, BW:], d1[:, :BW], d1[:, BW:]])
        return [g[0]], list(g[1:]), []
    dxa, dwa, dba, dwx, dbx, dlam = _rowwise(
        "lru_gates_bwd", gates_bwd, T, C, d.tr, rows=[(sv["xa"], BW, 0, 0), (dau0, 2 * BW, 0, 0), (dau1, 2 * BW, 0, 0)],
        consts=lru, outs=[(BW, f32)], accs=[p.shape for p in lru])
    sc.env[("gs", "lru_w_a", l)], sc.env[("gs", "lru_w_x", l)] = dwa, dwx

    def conv_bwd(t):
        dxe, xe, w = t.H[0], t.H[1], t.K[0]
        cur = dxe[HALO:HALO + d.tr]
        dw = jnp.concatenate([jnp.sum(cur * xe[HALO + k - 2:HALO + k - 2 + d.tr], axis=0, keepdims=True) for k in range(w.shape[0])], axis=0)
        return [_taps(dxe, w, 2, d.tr, -1)], [dw, jnp.sum(cur, axis=0, keepdims=True)], []
    drx, dcw, dcb = _rowwise("lru_conv_bwd", conv_bwd, T, C, d.tr, halos=[(dxa, BW, 0), (za, BW, 0)],
                             consts=[P["rnn_conv_w"]], outs=[(BW, bf16)], accs=[P["rnn_conv_w"].shape, (1, BW)])
    def sconv_bwd(t):
        scg, sx, sb, dyb = t.H
        w, tr = t.K[0], d.tr
        me = scg * sx
        dsb = dyb[HALO:HALO + tr] * _taps(me, w, 1, tr, 1)
        dce = dyb * sb
        dm = _taps(dce, w, 1, tr, -1)
        cur = dce[HALO:HALO + tr]
        dw = jnp.concatenate([jnp.sum(cur * me[HALO + k - 1:HALO + k - 1 + tr], axis=0, keepdims=True) for k in range(w.shape[0])], axis=0)
        return [[dsb, dm * sx[HALO:HALO + tr], dm * scg[HALO:HALO + tr]]], [dw], []
    dsc, dscw = _rowwise("sconv_bwd", sconv_bwd, T, C, d.tr, halos=[(za, BW, 3), (za, BW, 4), (za, BW, 2), (dys[1], BW, 0)],
                         consts=[P["sc_conv_w"]], outs=[(3 * BW, bf16)], accs=[P["sc_conv_w"].shape])
    dqr, dk0, dk1, dk2, dv0, dv1, dv2, dkc, dvc, dsink = _attn_bwd(sv["qr"], sv["kr"], sv["vb"], P["sinkrows"], dys[2], d)
    nB, nCb = T // ATT_BLOCK, C // ATT_BLOCK

    def att_join(t):
        dq, a1, a0, a2, b1, b0, b2, kc, vc, cs, sn = t.R
        cq, sq, ck, skn = jnp.tile(cs, (1, d.NH)), jnp.tile(sn, (1, d.NH)), jnp.tile(cs, (1, d.NKV)), jnp.tile(sn, (1, d.NKV))
        up = t.i + 1 <= nB - 1
        dn = t.i >= 1
        isc = t.seg == 0
        dk = a1 + jnp.where(up, a0, 0.0) + jnp.where(dn, a2, 0.0) + jnp.where(isc, kc, 0.0)
        dv = b1 + jnp.where(up, b0, 0.0) + jnp.where(dn, b2, 0.0) + jnp.where(isc, vc, 0.0)
        return [[dq * cq + _swap_pairs(dq * sq), dk * ck + _swap_pairs(dk * skn), dv]], [], []
    dqkv = _rowwise("attn_join", att_join, T, C, ATT_BLOCK,
                    rows=[(dqr, QW, 0, 0), (dk1, KW, 0, 0), (dk0, KW, 0, 1), (dk2, KW, 0, -1), (dv1, KW, 0, 0), (dv0, KW, 0, 1),
                          (dv2, KW, 0, -1), (dkc, KW, 0, 0), (dvc, KW, 0, 0), (tabs["cos"], HEAD_DIM, 0, 0),
                          (tabs["sin"], HEAD_DIM, 0, 0)], outs=[(QW + 2 * KW, bf16)])[0]
    dz = jnp.concatenate([drx, drg, dsc, dqkv, dgz], axis=1)
    du = _mm(dz, W["wint"], "nn", f32, "in_proj_dx", tn=1024, tk=3200, comm=sc.at(site + "in_x"))
    sc.env[("g", "wint", l)] = _mm(dz, sv["u"], "tn", bf16, "in_proj_dw", tm=2560, tn=1024, tk=768, comm=sc.at(site + "in_w"))
    res = _norm_mod_bwd(sv["h"], du, dh, P["norm_g"][1:2], modv, 3, d, nxt)
    grads = dict(norm_g1=res[1], dss=res[2], dgate=dgate, b_merge=dbm, rnn_conv_w=dcw, rnn_conv_b=dcb,
                 lru_b_a=dba, lru_b_x=dbx, lru_lambda=dlam, sc_conv_w=dscw, sinkrows=dsink)
    return res[0], grads, (res[3] if nxt is not None else None)


def _exchange(name, comm):
    ns, nd = len(comm.srcs), len(comm.dsts)

    def body(*refs):
        descs = comm.descriptors(refs[:ns], refs[ns + nd:ns + 2 * nd], refs[ns + 2 * nd:], _coords())
        _comm_start(descs)
        _comm_mid(descs)
        _comm_wait(descs)

    res = pl.pallas_call(
        body, name=name, in_specs=[ANY] * (ns + nd), out_specs=[ANY] * nd, out_shape=[SDS(a.shape, a.dtype) for a in comm.dsts],
        input_output_aliases={ns + q: q for q in range(nd)}, scratch_shapes=comm.scratch(),
    )(*comm.srcs, *comm.dsts)
    comm.store(res)


def _allgather(name, a):
    env = {"g": lax.empty((N_DEV,) + a.shape, a.dtype)}
    cm = _Comm(env)
    cm.add(a, "g", lambda ref, me, pi, r: ref, lambda ref, sender, r: ref.at[sender])
    _exchange(name, cm)
    return env["g"]


def _sum8(x):
    n = x.shape[1]
    tn = _tile(n, 2048, 8)

    def body(x_ref, o_ref):
        acc = x_ref[0]
        for j in range(1, N_DEV):
            acc = acc + x_ref[j]
        o_ref[...] = acc
    return pl.pallas_call(body, name="sum8", grid=(n // tn,), in_specs=[pl.BlockSpec((N_DEV, tn, 128), lambda i: (0, i, 0))],
                          out_specs=pl.BlockSpec((tn, 128), lambda i: (i, 0)), out_shape=SDS((n, 128), f32),
                          compiler_params=_params(("parallel",)))(x)


def _cast_direct(w):
    ly, R, Cn = w.shape
    tc = _tile(Cn, 512, 128)
    spec = pl.BlockSpec((None, R, tc), lambda l, j: (l, 0, j))

    def body(w_ref, o_ref):
        o_ref[...] = w_ref[...].astype(bf16)
    return pl.pallas_call(body, name="cast", grid=(ly, Cn // tc), in_specs=[spec], out_specs=spec, out_shape=SDS(w.shape, bf16),
                          compiler_params=_params(("parallel", "parallel")))(w)


def _cast_transposed(w):
    four = w.ndim == 4
    ly, I, (K, Nl) = w.shape[0], (w.shape[1] if four else 1), w.shape[-2:]
    tk = _tile(K, 256, 128)
    mid = (lambda i: (i,)) if four else (lambda i: ())
    nn = (None,) * (w.ndim - 2)

    def body(w_ref, o_ref):
        o_ref[...] = w_ref[...].T.astype(bf16)
    return pl.pallas_call(
        body, name="cast_t", grid=(ly, I, K // tk), in_specs=[pl.BlockSpec(nn + (tk, Nl), lambda l, i, k: (l,) + mid(i) + (k, 0))],
        out_specs=pl.BlockSpec(nn + (Nl, tk), lambda l, i, k: (l,) + mid(i) + (0, k)), out_shape=SDS(w.shape[:-2] + (Nl, K), bf16),
        compiler_params=_params(("parallel", "parallel", "parallel")))(w)


def _adam(g, w, m, v):
    m = ADAM_B1 * m + (1.0 - ADAM_B1) * g
    v = ADAM_B2 * v + (1.0 - ADAM_B2) * (g * g)
    m_hat = m / (1.0 - ADAM_B1 ** ADAM_STEP)
    v_hat = v / (1.0 - ADAM_B2 ** ADAM_STEP)
    delta = -ADAM_LR * (m_hat / (jnp.sqrt(v_hat) + ADAM_EPS) + ADAM_WD * w)
    return delta, m, v


def _slot_sum(x):
    acc = x[0].astype(f32)
    for r in range(1, x.shape[0]):
        acc = acc + x[r].astype(f32)
    return acc


def _adamw_direct(g, w, m, v, slots):
    ly, R, Cn = w.shape
    tr = R if R <= 1024 else _tile(R, 512, 16)
    tc = _tile(Cn, max(128, (256 * 1024) // tr // 128 * 128), 128)
    spec = pl.BlockSpec((None, tr, tc), lambda l, i, j: (l, i, j))
    gspec = pl.BlockSpec((None, N_DEV, tr, tc), lambda l, i, j: (l, 0, i, j)) if slots else spec

    def body(g_ref, w_ref, m_ref, v_ref, go, do, mo, vo):
        gv = _slot_sum(g_ref[...]) if slots else g_ref[...]
        go[...] = gv
        do[...], mo[...], vo[...] = _adam(gv, w_ref[...], m_ref[...], v_ref[...])
    return pl.pallas_call(body, name="adamw", grid=(ly, R // tr, Cn // tc), in_specs=[gspec, spec, spec, spec], out_specs=[spec] * 4,
                          out_shape=[SDS(w.shape, f32)] * 4, compiler_params=_params(("parallel",) * 3))(g, w, m, v)


def _adamw_transposed(g, w, m, v):
    four = w.ndim == 4
    ly, I, (K, Nl) = w.shape[0], (w.shape[1] if four else 1), w.shape[-2:]
    tk = _tile(K, 128, 128)
    mid = (lambda i: (i,)) if four else (lambda i: ())
    nn = (None,) * (w.ndim - 3)
    spec = pl.BlockSpec((None,) + nn + (tk, Nl), lambda l, i, k: (l,) + mid(i) + (k, 0))
    gspec = pl.BlockSpec((None, N_DEV) + nn + (Nl, tk), lambda l, i, k: (l, 0) + mid(i) + (0, k))

    def body(g_ref, w_ref, m_ref, v_ref, go, do, mo, vo):
        gv = _slot_sum(g_ref[...]).T
        go[...] = gv
        do[...], mo[...], vo[...] = _adam(gv, w_ref[...], m_ref[...], v_ref[...])
    return pl.pallas_call(body, name="adamw_t", grid=(ly, I, K // tk), in_specs=[gspec, spec, spec, spec], out_specs=[spec] * 4,
                          out_shape=[SDS(w.shape, f32)] * 4, compiler_params=_params(("parallel",) * 3))(g, w, m, v)


def _ada_fwd(cvec, ada_w, ada_b_cols):
    ly, D, cols = ada_w.shape
    tn = _tile(cols, 768, 128)

    def body(c_ref, w_ref, b_ref, o_ref):
        o_ref[...] = _bdot(jax.nn.silu(c_ref[...]), w_ref[...]) + b_ref[...]
    return pl.pallas_call(
        body, name="ada", grid=(ly, cols // tn),
        in_specs=[pl.BlockSpec((16, D), lambda l, j: (0, 0)), pl.BlockSpec((None, D, tn), lambda l, j: (l, 0, j)),
                  pl.BlockSpec((None, 1, tn), lambda l, j: (l, 0, j))],
        out_specs=pl.BlockSpec((None, 16, tn), lambda l, j: (l, 0, j)), out_shape=SDS((ly, 16, cols), f32),
        compiler_params=_params(("parallel", "parallel")))(cvec, ada_w, ada_b_cols)


def _ada_bwd(cvec, ada_w, dm):
    ly, D, cols = ada_w.shape
    tn = _tile(cols, 768, 128)

    def body(c_ref, w_ref, d_ref, gw_ref, gc_ref):
        first = jnp.logical_and(pl.program_id(0) == 0, pl.program_id(1) == 0)

        @pl.when(first)
        def _():
            gc_ref[...] = jnp.zeros_like(gc_ref)

        def f(cv, w):
            return _bdot(jax.nn.silu(cv), w)
        _, vjp = jax.vjp(f, c_ref[...], w_ref[...])
        dc, dw = vjp(d_ref[...])
        gw_ref[...] = dw
        gc_ref[...] += dc
    return pl.pallas_call(
        body, name="ada_bwd", grid=(ly, cols // tn),
        in_specs=[pl.BlockSpec((16, D), lambda l, j: (0, 0)), pl.BlockSpec((None, D, tn), lambda l, j: (l, 0, j)),
                  pl.BlockSpec((None, 16, tn), lambda l, j: (l, 0, j))],
        out_specs=[pl.BlockSpec((None, D, tn), lambda l, j: (l, 0, j)), pl.BlockSpec((16, D), lambda l, j: (0, 0))],
        out_shape=[SDS(ada_w.shape, f32), SDS((16, D), f32)], compiler_params=_params(("arbitrary", "arbitrary")))(cvec, ada_w, dm)


def _flat_adamw(g, w, m, v):
    n = w.shape[0]
    tn = _tile(n, 1024, 8)
    spec = pl.BlockSpec((tn, 128), lambda i: (i, 0))

    def body(g_ref, w_ref, m_ref, v_ref, do, mo, vo):
        do[...], mo[...], vo[...] = _adam(g_ref[...], w_ref[...], m_ref[...], v_ref[...])
    return pl.pallas_call(body, name="adamw_small", grid=(n // tn,), in_specs=[spec] * 4, out_specs=[spec] * 3,
                          out_shape=[SDS(w.shape, f32)] * 3, compiler_params=_params(("parallel",)))(g, w, m, v)


def _pack(arrs):
    flat = jnp.concatenate([a.reshape(-1).astype(f32) for a in arrs])
    n = _round_up(flat.shape[0], 512 * 128)
    return jnp.pad(flat, (0, n - flat.shape[0])).reshape(n // 128, 128)


def _unpack(flat, shapes, lead=()):
    flat = flat.reshape(lead + (-1,))
    out, off = [], 0
    for s in shapes:
        sz = math.prod(s)
        out.append(flat[..., off:off + sz].reshape(lead + tuple(s)))
        off += sz
    return out


def _unshard_last(g):
    g = jnp.moveaxis(g, 0, -2)
    return g.reshape(g.shape[:-2] + (g.shape[-2] * g.shape[-1],))


class _Dims:
    pass


def _rope_tables(L, C, NH):
    rows = L // GRID_W
    row = jnp.repeat(jnp.arange(rows), GRID_W).astype(f32)
    col = jnp.tile(jnp.arange(GRID_W), rows).astype(f32)
    half = HEAD_DIM // 2
    inv = ROPE_BASE ** (-jnp.arange(0, half, 2, dtype=f32) / half)
    ar, ac = row[:, None] * inv, col[:, None] * inv
    cos = jnp.concatenate([jnp.cos(ar), jnp.cos(ar), jnp.cos(ac), jnp.cos(ac)], axis=-1)
    sin = jnp.concatenate([-jnp.sin(ar), jnp.sin(ar), -jnp.sin(ac), jnp.sin(ac)], axis=-1)
    cos = jnp.concatenate([jnp.ones((C, HEAD_DIM), f32), cos], axis=0)
    sin = jnp.concatenate([jnp.zeros((C, HEAD_DIM), f32), sin], axis=0)
    return dict(cos=cos, sin=sin)


def kernel(x, c, ctx, c_ctx, ada_w, ada_b, norm_g, ffn1_w13, ffn1_w2, w_in, b_merge, rnn_conv_w, rnn_conv_b, lru_w_a, lru_b_a, lru_w_x, lru_b_x, lru_lambda, sc_conv_w, attn_sink, w_branch, w_out, ffn2_w13, ffn2_w2, final_norm_g, loss_target, m_c_ctx, m_ada_w, m_ada_b, m_norm_g, m_ffn1_w13, m_ffn1_w2, m_w_in, m_b_merge, m_rnn_conv_w, m_rnn_conv_b, m_lru_w_a, m_lru_b_a, m_lru_w_x, m_lru_b_x, m_lru_lambda, m_sc_conv_w, m_attn_sink, m_w_branch, m_w_out, m_ffn2_w13, m_ffn2_w2, m_final_norm_g, v_c_ctx, v_ada_w, v_ada_b, v_norm_g, v_ffn1_w13, v_ffn1_w2, v_w_in, v_b_merge, v_rnn_conv_w, v_rnn_conv_b, v_lru_w_a, v_lru_b_a, v_lru_w_x, v_lru_b_x, v_lru_lambda, v_sc_conv_w, v_attn_sink, v_w_branch, v_w_out, v_ffn2_w13, v_ffn2_w2, v_final_norm_g):
    d = _Dims()
    L, D = x.shape[1], x.shape[2]
    C = ctx.shape[1]
    LY = ada_w.shape[0]
    d.D, d.C, d.T = D, C, C + L
    d.F = ffn1_w2.shape[1] * N_DEV
    d.Fp = _round_up(d.F, 512)
    d.IN = w_in.shape[2] * N_DEV
    d.BW = w_branch.shape[2]
    d.NH = attn_sink.shape[1]
    d.NA = d.IN - N_BRANCH * D
    d.NKV = (d.NA - 5 * d.BW - d.NH * HEAD_DIM) // (2 * HEAD_DIM)
    d.G = d.NH // d.NKV
    d.NB = lru_w_a.shape[2]
    d.tr = _tile(C, 256, 8)
    d.trw = _tile(C, 128, 8)
    assert d.NH * HEAD_DIM == d.BW and L % GRID_W == 0 and C % ATT_BLOCK == 0 and L % ATT_BLOCK == 0
    sh13, sh2, shin, shd = 2 * d.F // N_DEV, d.F // N_DEV, d.IN // N_DEV, D // N_DEV
    cols9 = N_MOD * D // N_DEV
    me3 = _coords()
    me = _index(me3)

    sharded = [norm_g, b_merge, rnn_conv_w, lru_b_a, lru_b_x, lru_lambda, sc_conv_w]
    shapes1 = [a.shape for a in sharded] + [(D,)]
    g1 = _allgather("gather_small", _pack(sharded + [c.reshape(-1)]))
    parts = _unpack(g1, shapes1, (N_DEV,))
    norm_g_f, b_merge_f, conv_w_f, lru_b_a_f, lru_b_x_f, lru_lam_f, sc_w_f = [_unshard_last(p) for p in parts[:-1]]
    cvec = jnp.concatenate([parts[-1], c_ctx[None], jnp.zeros((7, D), f32)], axis=0)
    ada_b_cols = lax.dynamic_slice_in_dim(ada_b, me * cols9, cols9, axis=1)[:, None, :]
    modcols = _ada_fwd(cvec, ada_w, ada_b_cols)
    g2 = _allgather("gather_mod", modcols)
    modall = jnp.moveaxis(g2, 0, 2).reshape(LY, 16, N_MOD, D)
    mod_lat = lax.dynamic_index_in_dim(modall, me, axis=1, keepdims=False)
    modv = jnp.stack([modall[:, N_DEV], mod_lat], axis=1)

    assert LY == 2
    off13 = lambda j: (j // 4) * d.Fp + (j % 4) * sh13
    tr_ = lambda a: jnp.swapaxes(a, 1, 2)
    cast_t = lambda w: _cast_direct(tr_(w))
    c13a, c13b = cast_t(ffn1_w13), cast_t(ffn2_w13)
    wt = {"w13a": (c13a, off13, sh13, 2 * d.Fp, (D,), False), "w13b": (c13b, off13, sh13, 2 * d.Fp, (D,), False),
          "w2a": (_cast_direct(ffn1_w2), lambda j: j * sh2, sh2, d.Fp, (D,), False),
          "w2b": (_cast_direct(ffn2_w2), lambda j: j * sh2, sh2, d.Fp, (D,), False),
          "wint": (cast_t(w_in), lambda j: j * shin, shin, d.IN, (D,), False),
          "wbt": (_cast_transposed(w_branch), lambda j: j * shd, shd, D, (d.BW,), True),
          "wout": (_cast_direct(w_out), lambda j: j * shd, shd, D, (D,), False)}
    env = {}
    for name, (cw, off, sh, tot, rest, three) in wt.items():
        for l in range(LY):
            env[(name, l)] = lax.empty(((N_BRANCH, tot) if three else (tot,)) + rest, bf16)
        env[("r", name)] = lax.empty((LY, N_DEV) + ((N_BRANCH, sh) if three else (sh,)) + rest, bf16)
    zpad = jnp.zeros((max(d.Fp - d.F, 16), D), bf16)

    def part(sh, h):
        return (0, sh) if h is None else (h * (sh // 2), sh // 2)

    def rows_of(ref, start, n, three):
        sl = pl.ds(start if isinstance(start, int) else pl.multiple_of(start, 16), n)
        return ref.at[:, sl] if three else ref.at[sl]

    def add_gather(cm, name, l, h=None):
        cw, off, sh, tot, rest, three = wt[name]
        r0, n = part(sh, h)
        cm.gather(cw, (name, l), lambda ref: rows_of(ref.at[l], r0, n, three),
                  lambda ref, origin: rows_of(ref, off(origin) + r0, n, three))
        if d.Fp > d.F and h in (None, 0) and name[:2] in ("w1", "w2"):
            for base in ((0, d.Fp) if name[:3] == "w13" else (0,)):
                cm.fill(zpad, (name, l), lambda ref, base=base: ref.at[pl.ds(base + d.F, d.Fp - d.F)])

    def add_rs(cm, name, l, h=None):
        cw, off, sh, tot, rest, three = wt[name]
        r0, n = part(sh, h)
        cm.add(env[("g", name, l)], ("r", name), lambda ref, me_, pi, r: rows_of(ref, off(pi) + r0, n, three),
               lambda ref, sender, r: rows_of(ref.at[l, r], r0, n, three))

    plan = {"F0b.up": [("ag", "w2b", 0), ("ag", "w13a", 1, 0)], "F0b.down": [("ag", "w13a", 1, 1)], "F1b.up": [("ag", "w2b", 1)],
            "B0m.in_x": [("lru",)]}
    for l in range(LY):
        plan.update({f"F{l}a.up": [("ag", "w2a", l), ("ag", "wint", l, 0)], f"F{l}a.down": [("ag", "wint", l, 1), ("ag", "wbt", l)],
                     f"F{l}m.in_a": [("ag", "wout", l), ("ag", "w13b", l, 0)], f"F{l}m.in_g": [("ag", "w13b", l, 1)]})
    for l in range(LY):
        plan.update({f"B{l}b.dw": [("rs", "w13b", l, 0)], f"B{l}b.ux": [("rs", "w13b", l, 1)],
                     f"B{l}m.in_w": [("rs", "wout", l), ("rs", "wbt", l), ("rs", "w2b", l)], f"B{l}a.dx": [("rs", "wint", l, 0)],
                     f"B{l}a.uw": [("rs", "wint", l, 1)], f"B{l}a.dw": [("rs", "w13a", l, 0)]})
    plan.update({"B1a.ux": [("rs", "w13a", 1, 1)], "B0b.dx": [("rs", "w2a", 1)], "B0a.ux": [("rs", "w13a", 0, 1), ("rs", "w2a", 0)]})

    def add_lru(cm):
        for nm in ("lru_w_a", "lru_w_x"):
            gs = jnp.stack([env[("gs", nm, l)] for l in range(LY)]).reshape(-1, 128)
            env[("all", nm)] = lax.empty((N_DEV,) + gs.shape, f32)
            cm.add(gs, ("all", nm), lambda ref, me_, pi, r: ref, lambda ref, sender, r: ref.at[sender])
    sc = _Sched(plan, env, {"ag": add_gather, "rs": add_rs, "lru": add_lru})
    first = _Comm(env)
    add_gather(first, "w13a", 0)
    _exchange("gather_first", first)

    tabs = _rope_tables(L, C, d.NH)
    h = jnp.concatenate([ctx[0], x[0]], axis=0)
    Ps, saves = [], []
    for l in range(LY):
        sinkrows = jnp.repeat(attn_sink[l].reshape(d.NKV, d.G), ATT_BLOCK, axis=1)[:, :, None]
        Ps.append(dict(norm_g=norm_g_f[l], b_merge=b_merge_f[l], rnn_conv_w=conv_w_f[l], rnn_conv_b=rnn_conv_b[l][None],
                       lru_w_a=lru_w_a[l], lru_b_a=lru_b_a_f[l], lru_w_x=lru_w_x[l], lru_b_x=lru_b_x_f[l], lru_lambda=lru_lam_f[l],
                       sc_conv_w=sc_w_f[l], sinkrows=sinkrows))
    for l in range(LY):
        P = Ps[l]
        h, s1 = _ffn_fwd(h, modv[l], P["norm_g"][0:1], ("w13a", l), ("w2a", l), 0, d, sc, f"F{l}a.")
        h, s2 = _mixer_fwd(h, modv[l], P, l, tabs, d, sc, f"F{l}m.")
        h, s3 = _ffn_fwd(h, modv[l], P["norm_g"][2:3], ("w13b", l), ("w2b", l), 6, d, sc, f"F{l}b.")
        saves.append((s1, s2, s3))

    fng = final_norm_g[None]

    def loss_fn(t):
        def f(hv, g):
            y = hv * lax.rsqrt(jnp.mean(hv * hv, axis=-1, keepdims=True) + EPS) * g
            e = y - t.R[1]
            return 0.5 * jnp.sum(jnp.mean(e * e, axis=-1))
        lat = (t.seg == 1).astype(f32)
        val, (dhv, dg) = jax.value_and_grad(f, argnums=(0, 1))(t.R[0], t.K[0])
        return [dhv * lat], [dg * lat, jnp.full((1, 128), val * lat, f32)], []
    dh, dfng, lossv = _rowwise("loss", loss_fn, d.T, C, d.tr, rows=[(h, D, 0, 0), (loss_target[0], D, 0, -(C // d.tr))],
                               consts=[fng], outs=[(D, f32)], accs=[(1, D), (1, 128)])
    loss = lax.psum(lossv[0, 0], AXES)

    gl = [None] * LY
    pre = None
    for l in reversed(range(LY)):
        P = Ps[l]
        s1, s2, s3 = saves[l]
        dh, dg2, dss2, dgt2, pre = _ffn_bwd(dh, s3, modv[l], P["norm_g"][2:3], ("w13b", l), ("w2b", l), 6, d, sc, f"B{l}b.",
                                            pre, (s2["y"], modv[l], 5, 1.0))
        dh, gm, pre = _mixer_bwd(dh, s2, modv[l], P, l, tabs, d, sc, f"B{l}m.", pre, (s1["o"], modv[l], 2, 0.5))
        below = (saves[l - 1][2]["o"], modv[l - 1], 8, 0.5) if l > 0 else None
        dh, dg0, dss0, dgt0, pre = _ffn_bwd(dh, s1, modv[l], P["norm_g"][0:1], ("w13a", l), ("w2a", l), 0, d, sc, f"B{l}a.",
                                            pre, below)
        gm.update(norm_g=jnp.concatenate([dg0, gm["norm_g1"], dg2], axis=0),
                  dmod=jnp.concatenate([dss0, dgt0, gm["dss"], gm["dgate"], dss2, dgt2], axis=1))
        gl[l] = gm
    grad_x = dh[C:][None]
    st = lambda k: jnp.stack([g[k] for g in gl])

    r13a, r2a, rin, rwb, rwo, r13b, r2b = [env[("r", k)] for k in ("w13a", "w2a", "wint", "wbt", "wout", "w13b", "w2b")]
    big = {}
    adamw_tr = lambda r, w, m, v: [tr_(o) for o in _adamw_direct(r, tr_(w), tr_(m), tr_(v), True)]
    big["ffn1_w13"] = adamw_tr(r13a, ffn1_w13, m_ffn1_w13, v_ffn1_w13)
    big["ffn2_w13"] = adamw_tr(r13b, ffn2_w13, m_ffn2_w13, v_ffn2_w13)
    big["w_in"] = adamw_tr(rin, w_in, m_w_in, v_w_in)
    big["w_branch"] = _adamw_transposed(rwb, w_branch, m_w_branch, v_w_branch)
    big["ffn1_w2"] = _adamw_direct(r2a, ffn1_w2, m_ffn1_w2, v_ffn1_w2, True)
    big["ffn2_w2"] = _adamw_direct(r2b, ffn2_w2, m_ffn2_w2, v_ffn2_w2, True)
    big["w_out"] = _adamw_direct(rwo, w_out, m_w_out, v_w_out, True)

    sink_g = jnp.stack([jnp.sum(g["sinkrows"].reshape(d.NKV, d.G, ATT_BLOCK), axis=-1).reshape(d.NH) for g in gl])
    small_full = dict(norm_g=st("norm_g"), b_merge=st("b_merge"), rnn_conv_w=st("rnn_conv_w"), rnn_conv_b=st("rnn_conv_b")[:, 0],
                      lru_b_a=st("lru_b_a"), lru_b_x=st("lru_b_x"), lru_lambda=st("lru_lambda"), sc_conv_w=st("sc_conv_w"), attn_sink=sink_g, final_norm_g=dfng[0])
    names_s = list(small_full)
    dmod = st("dmod")
    pk = _pack([small_full[k] for k in names_s] + [dmod])
    g4 = _allgather("gather_small_grads", pk)
    tot = _unpack(_sum8(g4), [small_full[k].shape for k in names_s] + [dmod.shape])
    sums = dict(zip(names_s, tot[:-1]))
    dmod_sum = tot[-1].reshape(LY, 2, N_MOD * D)
    dmod_all = _unpack(g4, [small_full[k].shape for k in names_s] + [dmod.shape], (N_DEV,))[-1].reshape(N_DEV, LY, 2, N_MOD * D)
    dm_rows = jnp.concatenate([jnp.moveaxis(dmod_all[:, :, 1], 0, 1), dmod_sum[:, 0:1], jnp.zeros((LY, 7, N_MOD * D), f32)], axis=1)
    dm_cols = lax.dynamic_slice_in_dim(dm_rows, me * cols9, cols9, axis=2)
    g_ada_w, dcv = _ada_bwd(cvec, ada_w, dm_cols)
    g5 = _allgather("gather_cctx", _pack([dcv[N_DEV]]))
    g_c_ctx = _sum8(g5).reshape(-1)[:D]
    g_ada_b = dmod_sum[:, 0] + dmod_sum[:, 1]
    ada_out = _adamw_direct(g_ada_w, ada_w, m_ada_w, v_ada_w, False)

    def shard_last(a, n):
        return lax.dynamic_slice_in_dim(a, me * n, n, axis=a.ndim - 1)
    local_g = dict(c_ctx=g_c_ctx, ada_b=g_ada_b, norm_g=shard_last(sums["norm_g"], shd), b_merge=shard_last(sums["b_merge"], shd),
                   rnn_conv_w=shard_last(sums["rnn_conv_w"], d.BW // N_DEV), rnn_conv_b=sums["rnn_conv_b"],
                   lru_b_a=shard_last(sums["lru_b_a"], d.BW // N_DEV), lru_b_x=shard_last(sums["lru_b_x"], d.BW // N_DEV),
                   lru_lambda=shard_last(sums["lru_lambda"], d.BW // N_DEV), sc_conv_w=shard_last(sums["sc_conv_w"], d.BW // N_DEV),
                   attn_sink=sums["attn_sink"], final_norm_g=sums["final_norm_g"])
    wmv = dict(c_ctx=(c_ctx, m_c_ctx, v_c_ctx), ada_b=(ada_b, m_ada_b, v_ada_b), norm_g=(norm_g, m_norm_g, v_norm_g),
               b_merge=(b_merge, m_b_merge, v_b_merge), rnn_conv_w=(rnn_conv_w, m_rnn_conv_w, v_rnn_conv_w),
               rnn_conv_b=(rnn_conv_b, m_rnn_conv_b, v_rnn_conv_b),
               lru_b_a=(lru_b_a, m_lru_b_a, v_lru_b_a), lru_b_x=(lru_b_x, m_lru_b_x, v_lru_b_x),
               lru_lambda=(lru_lambda, m_lru_lambda, v_lru_lambda), sc_conv_w=(sc_conv_w, m_sc_conv_w, v_sc_conv_w),
               attn_sink=(attn_sink, m_attn_sink, v_attn_sink), final_norm_g=(final_norm_g, m_final_norm_g, v_final_norm_g))
    names_l = list(local_g)
    shapes_l = [wmv[k][0].shape for k in names_l]
    gp = _pack([local_g[k].reshape(wmv[k][0].shape) for k in names_l])
    outs_s = _flat_adamw(gp, _pack([wmv[k][0] for k in names_l]), _pack([wmv[k][1] for k in names_l]), _pack([wmv[k][2] for k in names_l]))
    small = {k: [local_g[k].reshape(wmv[k][0].shape)] for k in names_l}
    for o in outs_s:
        for k, a in zip(names_l, _unpack(o, shapes_l)):
            small[k].append(a)
    for nm, (w_, m_, v_) in (("lru_w_a", (lru_w_a, m_lru_w_a, v_lru_w_a)), ("lru_w_x", (lru_w_x, m_lru_w_x, v_lru_w_x))):
        gsum = _sum8(env[("all", nm)])
        flat = lambda a: a.reshape(-1, 128)
        small[nm] = [a.reshape(w_.shape) for a in (gsum, *_flat_adamw(gsum, flat(w_), flat(m_), flat(v_)))]

    order = ["c_ctx", "ada_w", "ada_b", "norm_g", "ffn1_w13", "ffn1_w2", "w_in", "b_merge", "rnn_conv_w", "rnn_conv_b", "lru_w_a",
             "lru_b_a", "lru_w_x", "lru_b_x", "lru_lambda", "sc_conv_w", "attn_sink", "w_branch", "w_out", "ffn2_w13", "ffn2_w2",
             "final_norm_g"]
    allo = dict(small)
    allo.update(big)
    allo["ada_w"] = ada_out
    res = [loss, grad_x]
    for q in range(4):
        res += [allo[k][q] for k in order]
    return tuple(res)
```

```python
import functools
import math

import jax
import jax.numpy as jnp
from jax import lax
from jax.experimental import pallas as pl
from jax.experimental.pallas import tpu as pltpu

f32 = jnp.float32
bf16 = jnp.bfloat16
SDS = jax.ShapeDtypeStruct

N_DEV = 8
AXES = ("x", "y", "c")
HEAD_DIM = 128
GRID_W = 64
ATT_BLOCK = 128
HALO = 16
ROPE_BASE = 10000.0
LRU_C = 8.0
EPS = 1e-6
NEG_INF = -1e30
N_MOD = 9
N_BRANCH = 3
ADAM_LR, ADAM_B1, ADAM_B2, ADAM_EPS, ADAM_WD, ADAM_STEP = 0.001, 0.9, 0.999, 1e-08, 0.01, 10
VMEM_LIMIT = 56 * 1024 * 1024
ANY = pl.BlockSpec(memory_space=pl.ANY)


def _round_up(n, m):
    return (n + m - 1) // m * m


def _tile(n, target, align):
    best = None
    t = align
    while t <= min(n, target):
        if n % t == 0:
            best = t
        t += align
    return best if best is not None else n


def _params(sem):
    return pltpu.CompilerParams(dimension_semantics=sem, vmem_limit_bytes=VMEM_LIMIT)


_DIMS = {"nn": (((1,), (0,)), ((), ())), "nt": (((1,), (1,)), ((), ())), "tn": (((0,), (0,)), ((), ()))}


def _coords():
    return lax.axis_index("x"), lax.axis_index("y"), lax.axis_index("c")


def _peer(xyc, r):
    x, y, c = xyc
    return (1 - x if r & 4 else x, 1 - y if r & 2 else y, 1 - c if r & 1 else c)


def _index(xyc):
    return xyc[0] * 4 + xyc[1] * 2 + xyc[2]


class _Comm:
    def __init__(self, env):
        self.env, self.srcs, self.dsts, self.keys, self.items, self.fills, self.gathers = env, [], [], [], [], [], []

    def gather(self, src, key, src_view, dst_view):
        self.gathers.append((self._src(src), self._dst(key), src_view, dst_view))

    def _src(self, a):
        for q, b in enumerate(self.srcs):
            if b is a:
                return q
        self.srcs.append(a)
        return len(self.srcs) - 1

    def _dst(self, key):
        if key not in self.keys:
            self.keys.append(key)
            self.dsts.append(self.env[key])
        return self.keys.index(key)

    def add(self, src, key, src_view, dst_view):
        self.items.append((self._src(src), self._dst(key), src_view, dst_view))

    def fill(self, src, key, dst_view):
        self.fills.append((self._src(src), self._dst(key), dst_view))

    def scratch(self):
        n = len(self.items) + len(self.gathers)
        return [pltpu.SemaphoreType.DMA((7 * n,)), pltpu.SemaphoreType.DMA((7 * n,)), pltpu.SemaphoreType.DMA((n + len(self.fills),))]

    def store(self, arrays):
        for key, a in zip(self.keys, arrays):
            self.env[key] = a

    def descriptors(self, srcs, dsts, sems, me3):
        send, recv, lsem = sems
        me = _index(me3)
        local, sends, recvs = [], [], []
        for q, (si, di, sv, dv) in enumerate(self.items):
            local.append(pltpu.make_async_copy(sv(srcs[si], me, me, 0), dv(dsts[di], me, 0), lsem.at[q]))
            for r in range(1, N_DEV):
                p3 = _peer(me3, r)
                pi = _index(p3)
                kw = dict(send_sem=send.at[q * 7 + r - 1], recv_sem=recv.at[q * 7 + r - 1], device_id=p3,
                          device_id_type=pl.DeviceIdType.MESH)
                sends.append(pltpu.make_async_remote_copy(src_ref=sv(srcs[si], me, pi, r), dst_ref=dv(dsts[di], me, r), **kw))
                recvs.append(pltpu.make_async_remote_copy(src_ref=sv(srcs[si], me, pi, r), dst_ref=dv(dsts[di], pi, r), **kw))
        n1 = len(self.items)
        landed, passed = [], []
        sib = _peer(me3, 1)
        for q, (si, di, sv, dv) in enumerate(self.gathers):
            def rc(src_ref, dst_ref, k, dev, base=(n1 + q) * 7):
                return pltpu.make_async_remote_copy(src_ref=src_ref, dst_ref=dst_ref, send_sem=send.at[base + k - 1],
                                                    recv_sem=recv.at[base + k - 1], device_id=dev, device_id_type=pl.DeviceIdType.MESH)
            mine = sv(srcs[si])
            local.append(pltpu.make_async_copy(mine, dv(dsts[di], me), lsem.at[n1 + q]))
            for r in (1, 2, 4, 6):
                sends.append(rc(mine, dv(dsts[di], me), r, _peer(me3, r)))
            for r in (2, 4, 6):
                blk = dv(dsts[di], _index(_peer(me3, r)))
                landed.append(rc(mine, blk, r, _peer(me3, r)))
                passed.append(rc(blk, blk, r ^ 1, sib))
            for k in (1, 3, 5, 7):
                recvs.append(rc(mine, dv(dsts[di], _index(_peer(me3, k))), k, sib))
        for q, (si, di, dv) in enumerate(self.fills):
            local.append(pltpu.make_async_copy(srcs[si], dv(dsts[di]), lsem.at[n1 + len(self.gathers) + q]))
        return local, sends, recvs, landed, passed


def _comm_start(descs):
    for cp in descs[0] + descs[1]:
        cp.start()


def _comm_mid(descs):
    for cp in descs[3]:
        cp.wait_recv()
    for cp in descs[4]:
        cp.start()


def _comm_wait(descs):
    local, sends, recvs, _, passed = descs
    sends = sends + passed
    for cp in recvs:
        cp.wait_recv()
    for cp in sends:
        cp.wait_send()
    for cp in local:
        cp.wait()


def _carry_call(core, name, grid, in_specs, out_specs, out_shape, scratch, args, comm):
    ni, no, nsc, ng = len(in_specs), len(out_specs), len(scratch), len(grid)
    if comm is None:
        def plain(*refs):
            core(refs[:ni], refs[ni:ni + no], refs[ni + no:])
        return pl.pallas_call(plain, name=name, grid=grid, in_specs=in_specs, out_specs=out_specs, out_shape=out_shape,
                              scratch_shapes=scratch, compiler_params=_params(("parallel",) * (ng - 1) + ("arbitrary",)))(*args)
    ns, nd = len(comm.srcs), len(comm.dsts)
    o0 = ni + ns + nd

    def body(*refs):
        pid = [pl.program_id(q) for q in range(ng)]
        first, last = pid[0] == 0, pid[0] == grid[0] - 1
        for q in range(1, ng):
            first = jnp.logical_and(first, pid[q] == 0)
            last = jnp.logical_and(last, pid[q] == grid[q] - 1)
        me3 = _coords()
        mk = lambda: comm.descriptors(refs[ni:ni + ns], refs[o0 + no:o0 + no + nd], refs[o0 + no + nd + nsc:], me3)

        @pl.when(first)
        def _():
            _comm_start(mk())

        if comm.gathers:
            lin = pid[0]
            for q in range(1, ng):
                lin = lin * grid[q] + pid[q]

            @pl.when(lin == min(math.prod(grid) - 1, int(0.85 * math.prod(grid))))
            def _():
                _comm_mid(mk())

        core(refs[:ni], refs[o0:o0 + no], refs[o0 + no + nd:o0 + no + nd + nsc])

        @pl.when(last)
        def _():
            _comm_wait(mk())

    res = pl.pallas_call(
        body, name=name, grid=grid, in_specs=list(in_specs) + [ANY] * (ns + nd), out_specs=list(out_specs) + [ANY] * nd,
        out_shape=list(out_shape) + [SDS(x.shape, x.dtype) for x in comm.dsts], scratch_shapes=list(scratch) + comm.scratch(),
        input_output_aliases={ni + ns + q: no + q for q in range(nd)}, compiler_params=_params(("arbitrary",) * ng),
    )(*args, *comm.srcs, *comm.dsts)
    comm.store(res[no:])
    return res[:no]


def _mm(a, b, mode, out_dtype, name, *, M=None, N=None, K=None, a_off=0, b_off=0, b_lead=None,
        tm=768, tn=512, tk=2048, comm=None, a3=False, resid=None):
    b2 = b.shape[1:] if b_lead is not None else b.shape
    if mode == "tn":
        K = a.shape[-2] if K is None else K
        M = (a.shape[0] * a.shape[2] if a3 else a.shape[1]) if M is None else M
        N = b2[1] if N is None else N
    else:
        M = a.shape[-2] if M is None else M
        K = (a.shape[0] * a.shape[2] if a3 else a.shape[1]) if K is None else K
        N = (b2[1] if mode == "nn" else b2[0]) if N is None else N
    g = math.gcd
    if mode == "tn":
        tm = _tile(g(M, a_off) if a_off else (a.shape[2] if a3 else M), tm, 128)
        tk = _tile(K, tk, 16)
    else:
        tm = _tile(M, tm, 16)
        tk = _tile(g(K, a_off) if a_off else (a.shape[2] if a3 else K), tk, 128)
    tn = _tile(g(N, b_off) if b_off else N, tn, 128)
    nk = K // tk
    ao, bo = (a_off // (tm if mode == "tn" else tk)), b_off // tn
    lead = () if b_lead is None else (b_lead,)
    ld = () if b_lead is None else (None,)
    if mode == "nn":
        a_spec = pl.BlockSpec((tm, tk), lambda i, j, k: (i, ao + k))
        b_spec = pl.BlockSpec(ld + (tk, tn), lambda i, j, k: lead + (k, bo + j))
    elif mode == "nt":
        a_spec = pl.BlockSpec((tm, tk), lambda i, j, k: (i, ao + k))
        b_spec = pl.BlockSpec(ld + (tn, tk), lambda i, j, k: lead + (bo + j, k))
    else:
        a_spec = pl.BlockSpec((tk, tm), lambda i, j, k: (k, ao + i))
        b_spec = pl.BlockSpec(ld + (tk, tn), lambda i, j, k: lead + (k, bo + j))
    if a3:
        if mode == "nn":
            nkc = a.shape[2] // tk
            a_spec = pl.BlockSpec((None, tm, tk), lambda i, j, k: (k // nkc, i, k % nkc))
        else:
            nmc = a.shape[2] // tm
            a_spec = pl.BlockSpec((None, tk, tm), lambda i, j, k: (i // nmc, k, i % nmc))
    dims = _DIMS[mode]

    def finish(ins, outs, val):
        outs[0][...] = val.astype(out_dtype)
        if resid is not None:
            gi, coef, C = resid[2:]
            row = pl.program_id(0) * tm + lax.broadcasted_iota(jnp.int32, (tm, 1), 0)
            gate = jnp.where(row < C, ins[3][0, gi:gi + 1], ins[3][1, gi:gi + 1])
            outs[1][...] = ins[2][...] + coef * gate * val

    def core(ins, outs, scr):
        a_ref, b_ref = ins[0], ins[1]
        k = pl.program_id(2)
        if nk == 1:
            finish(ins, outs, lax.dot_general(a_ref[...], b_ref[...], dims, preferred_element_type=f32))
        else:
            acc = scr[0]

            @pl.when(k == 0)
            def _():
                acc[...] = jnp.zeros_like(acc)

            acc[...] += lax.dot_general(a_ref[...], b_ref[...], dims, preferred_element_type=f32)

            @pl.when(k == nk - 1)
            def _():
                finish(ins, outs, acc[...])

    o_spec = pl.BlockSpec((tm, tn), lambda i, j, k: (i, j))
    in_specs, out_specs, out_shape, args = [a_spec, b_spec], [o_spec], [SDS((M, N), out_dtype)], (a, b)
    if resid is not None:
        in_specs += [o_spec, pl.BlockSpec((2, N_MOD, tn), lambda i, j, k: (0, 0, j))]
        out_specs, out_shape, args = out_specs + [o_spec], out_shape + [SDS((M, N), f32)], args + tuple(resid[:2])
    res = _carry_call(core, name, (M // tm, N // tn, nk), in_specs, out_specs, out_shape,
                      [] if nk == 1 else [pltpu.VMEM((tm, tn), f32)], args, comm)
    return res[0] if resid is None else res


class _Tile:
    pass


def _rowwise(name, fn, T, C, tr, rows=(), halos=(), consts=(), mods=(), outs=(), accs=(), maccs=()):
    nT, nC = T // tr, C // tr
    assert T % tr == 0 and C % tr == 0 and nC >= 1 and tr % HALO == 0
    r8, n8 = tr // HALO, T // HALO
    seg_of = lambda i: jnp.where(i >= nC, 1, 0)
    in_specs, args = [], []
    for arr, w, cb, ro in rows:
        nt = arr.shape[0] // tr
        in_specs.append(pl.BlockSpec((tr, w), (lambda i, cb=cb, ro=ro, nt=nt: (jnp.clip(i + ro, 0, nt - 1), cb))))
        args.append(arr)
    for arr, w, cb in halos:
        in_specs.append(pl.BlockSpec((HALO, w), lambda i, cb=cb: (jnp.maximum(i * r8 - 1, 0), cb)))
        in_specs.append(pl.BlockSpec((tr, w), lambda i, cb=cb: (i, cb)))
        in_specs.append(pl.BlockSpec((HALO, w), lambda i, cb=cb: (jnp.minimum((i + 1) * r8, n8 - 1), cb)))
        args += [arr, arr, arr]
    for arr in consts:
        in_specs.append(pl.BlockSpec(arr.shape, lambda i, nd=arr.ndim: (0,) * nd))
        args.append(arr)
    for arr in mods:
        in_specs.append(pl.BlockSpec((None,) + arr.shape[1:], lambda i, nd=arr.ndim: (seg_of(i),) + (0,) * (nd - 1)))
        args.append(arr)
    out_specs, out_shape = [], []
    for w, dt in outs:
        out_specs.append(pl.BlockSpec((tr, w), lambda i: (i, 0)))
        out_shape.append(SDS((T, w), dt))
    for shp in accs:
        out_specs.append(pl.BlockSpec(shp, lambda i, nd=len(shp): (0,) * nd))
        out_shape.append(SDS(shp, f32))
    for shp in maccs:
        out_specs.append(pl.BlockSpec((None,) + shp, lambda i, nd=len(shp): (seg_of(i),) + (0,) * nd))
        out_shape.append(SDS((2,) + shp, f32))
    nr, nh, nk, nm, no, na, nma = len(rows), len(halos), len(consts), len(mods), len(outs), len(accs), len(maccs)

    def body(*refs):
        i = pl.program_id(0)
        t = _Tile()
        t.i, t.seg = i, seg_of(i)
        p = 0
        t.R = [refs[p + k][...].astype(f32) for k in range(nr)]
        p += nr
        pvalid = jnp.logical_and(i != 0, i != nC)
        nvalid = jnp.logical_and(i != nC - 1, i != nT - 1)
        t.H = []
        for k in range(nh):
            pr, cu, nx = refs[p][...].astype(f32), refs[p + 1][...].astype(f32), refs[p + 2][...].astype(f32)
            p += 3
            pr = jnp.where(pvalid, pr, jnp.zeros_like(pr))
            nx = jnp.where(nvalid, nx, jnp.zeros_like(nx))
            t.H.append(jnp.concatenate([pr, cu, nx], axis=0))
        t.K = [refs[p + k][...] for k in range(nk)]
        p += nk
        t.M = [refs[p + k][...] for k in range(nm)]
        p += nm
        o, a, ma = fn(t)
        for k in range(no):
            ref = refs[p + k]
            pieces = o[k] if isinstance(o[k], (list, tuple)) else [o[k]]
            c0 = 0
            for pc in pieces:
                ref[:, c0:c0 + pc.shape[1]] = pc.astype(ref.dtype)
                c0 += pc.shape[1]
        p += no
        for k in range(na):
            ref = refs[p + k]

            @pl.when(i == 0)
            def _(ref=ref):
                ref[...] = jnp.zeros_like(ref)

            ref[...] += a[k]
        p += na
        for k in range(nma):
            ref = refs[p + k]

            @pl.when(jnp.logical_or(i == 0, i == nC))
            def _(ref=ref):
                ref[...] = jnp.zeros_like(ref)

            ref[...] += ma[k]

    res = pl.pallas_call(
        body, name=name, grid=(nT,), in_specs=in_specs, out_specs=out_specs, out_shape=out_shape,
        compiler_params=_params(("arbitrary",)),
    )(*args)
    return res


@jax.custom_vjp
def _bdot(a, b):
    return jnp.dot(a.astype(bf16), b.astype(bf16), preferred_element_type=f32)


def _bdot_fwd(a, b):
    return _bdot(a, b), (a, b)


def _bdot_bwd(res, ct):
    a, b = res
    ctb = ct.astype(bf16)
    da = lax.dot_general(ctb, b.astype(bf16), _DIMS["nt"], preferred_element_type=f32)
    db = lax.dot_general(a.astype(bf16), ctb, _DIMS["tn"], preferred_element_type=f32)
    return da, db


_bdot.defvjp(_bdot_fwd, _bdot_bwd)


def _rms_mod(h, g, shift, scale):
    y = h * lax.rsqrt(jnp.mean(h * h, axis=-1, keepdims=True) + EPS) * g
    return y * (1.0 + scale) + shift


def _norm_mod_fwd(h, g, modv, s, d):
    def fn(t):
        m = t.M[0]
        return [_rms_mod(t.R[0], t.K[0], m[s:s + 1], m[s + 1:s + 2])], [], []
    return _rowwise("norm_mod", fn, d.T, d.C, d.tr, rows=[(h, d.D, 0, 0)], consts=[g], mods=[modv],
                    outs=[(d.D, bf16)])[0]


def _norm_mod_bwd(h, du, dh, g, modv, s, d, nxt=None):
    def fn(t):
        m = t.M[0]
        _, vjp = jax.vjp(_rms_mod, t.R[0], t.K[0], m[s:s + 1], m[s + 1:s + 2])
        dx, dg, dsh, dsc = vjp(t.R[1])
        dhn = t.R[2] + dx
        dss = jnp.concatenate([dsh, dsc], axis=0)
        if nxt is None:
            return [dhn], [dg], [dss]
        gi, coef = nxt[2:]
        return ([dhn, coef * t.M[1][gi:gi + 1] * dhn], [dg], [dss, jnp.sum(coef * dhn * t.R[3], axis=0, keepdims=True)])
    rows = [(h, d.D, 0, 0), (du, d.D, 0, 0), (dh, d.D, 0, 0)]
    if nxt is None:
        return _rowwise("norm_mod_bwd", fn, d.T, d.C, d.tr, rows=rows, consts=[g], mods=[modv], outs=[(d.D, f32)],
                        accs=[(1, d.D)], maccs=[(2, d.D)])
    dhn, do, dg, dss, dgate = _rowwise("norm_mod_bwd", fn, d.T, d.C, d.tr, rows=rows + [(nxt[0], d.D, 0, 0)], consts=[g],
                                       mods=[modv, nxt[1]], outs=[(d.D, f32), (d.D, bf16)], accs=[(1, d.D)], maccs=[(2, d.D), (1, d.D)])
    return dhn, dg, dss, (do, dgate)


def _swiglu(g, u):
    return jax.nn.silu(g) * u


def _resid_bwd(dh, o, modv, gi, coef, d):
    def fn(t):
        dhv = t.R[0]
        return [coef * t.M[0][gi:gi + 1] * dhv], [], [jnp.sum(coef * dhv * t.R[1], axis=0, keepdims=True)]
    return _rowwise("resid_bwd", fn, d.T, d.C, d.tr, rows=[(dh, d.D, 0, 0), (o, d.D, 0, 0)], mods=[modv],
                    outs=[(d.D, bf16)], maccs=[(1, d.D)])


class _Sched:
    def __init__(self, plan, env, builders):
        self.plan, self.env, self.builders = plan, env, builders

    def at(self, site):
        specs = self.plan.get(site)
        if not specs:
            return None
        cm = _Comm(self.env)
        for kind, *args in specs:
            self.builders[kind](cm, *args)
        return cm


def _ffn_up(u, w13t, d, comm):
    T, D, Fp = d.T, d.D, d.Fp
    tm, tn = _tile(T, 768, 16), _tile(Fp, 512, 128)

    def core(ins, outs, scr):
        uv, w_ref = ins[0][...], ins[1]
        gt = lax.dot_general(uv, w_ref[0], _DIMS["nt"], preferred_element_type=f32)
        up = lax.dot_general(uv, w_ref[1], _DIMS["nt"], preferred_element_type=f32)
        sig = jax.nn.sigmoid(gt)
        silu = gt * sig
        outs[0][0] = silu.astype(bf16)
        outs[0][1] = (up * (sig + silu * (1.0 - sig))).astype(bf16)
        outs[1][...] = (silu * up).astype(bf16)

    return _carry_call(core, "ffn_up", (T // tm, Fp // tn),
                       [pl.BlockSpec((tm, D), lambda i, j: (i, 0)), pl.BlockSpec((2, tn, D), lambda i, j: (0, j, 0))],
                       [pl.BlockSpec((2, tm, tn), lambda i, j: (0, i, j)), pl.BlockSpec((tm, tn), lambda i, j: (i, j))],
                       [SDS((2, T, Fp), bf16), SDS((T, Fp), bf16)], [], (u, w13t.reshape(2, Fp, D)), comm)


def _ffn_down_dx(do, w2, gu, d, comm):
    T, D, Fp = d.T, d.D, d.Fp
    tm, tn = _tile(T, 768, 16), _tile(Fp, 512, 128)

    def core(ins, outs, scr):
        dhm = lax.dot_general(ins[0][...], ins[1][...], _DIMS["nt"], preferred_element_type=f32)
        outs[0][0] = (dhm * ins[2][1].astype(f32)).astype(bf16)
        outs[0][1] = (dhm * ins[2][0].astype(f32)).astype(bf16)

    return _carry_call(core, "ffn_down_dx", (T // tm, Fp // tn),
                       [pl.BlockSpec((tm, D), lambda i, j: (i, 0)), pl.BlockSpec((tn, D), lambda i, j: (j, 0)),
                        pl.BlockSpec((2, tm, tn), lambda i, j: (0, i, j))],
                       [pl.BlockSpec((2, tm, tn), lambda i, j: (0, i, j))], [SDS((2, T, Fp), bf16)], [], (do, w2, gu), comm)[0]


def _ffn_fwd(h, modv, g, k13, k2, s, d, sc, site):
    u = _norm_mod_fwd(h, g, modv, s, d)
    gu, hmid = _ffn_up(u, sc.env[k13], d, sc.at(site + "up"))
    o, hn = _mm(hmid, sc.env[k2], "nn", bf16, "ffn_down", tn=512, tk=5632, comm=sc.at(site + "down"),
                resid=(h, modv, s + 2, 0.5, d.C))
    return hn, dict(h=h, u=u, gu=gu, hmid=hmid, o=o)


def _ffn_bwd(dh, sv, modv, g, k13, k2, s, d, sc, site, pre=None, nxt=None):
    w13t, w2 = sc.env[k13], sc.env[k2]
    do, dgate = pre if pre is not None else _resid_bwd(dh, sv["o"], modv, s + 2, 0.5, d)
    dgu = _ffn_down_dx(do, w2, sv["gu"], d, sc.at(site + "dx"))
    sc.env[("g",) + k13] = _mm(dgu, sv["u"], "tn", bf16, "ffn_up_dw", tm=2816, tn=1024, tk=768, comm=sc.at(site + "uw"), a3=True)
    sc.env[("g",) + k2] = _mm(sv["hmid"], do, "tn", bf16, "ffn_down_dw", tm=2816, tn=1024, tk=768, comm=sc.at(site + "dw"))
    du = _mm(dgu, w13t, "nn", f32, "ffn_up_dx", tn=1024, tk=2816, comm=sc.at(site + "ux"), a3=True)
    res = _norm_mod_bwd(sv["h"], du, dh, g, modv, s, d, nxt)
    return res[0], res[1], res[2], dgate, (res[3] if nxt is not None else None)


def _taps(ext, w, left, tr, sign):
    acc = None
    for k in range(w.shape[0]):
        o = HALO + sign * (k - left)
        term = w[k:k + 1] * ext[o:o + tr]
        acc = term if acc is None else acc + term
    return acc


def _gates(xa, wa, ba, wx, bx, lam, nb):
    bs = xa.shape[1] // nb
    out = []
    for dr in range(2):
        pa = jnp.concatenate([_bdot(xa[:, n * bs:(n + 1) * bs], wa[dr, n]) for n in range(nb)], axis=1)
        px = jnp.concatenate([_bdot(xa[:, n * bs:(n + 1) * bs], wx[dr, n]) for n in range(nb)], axis=1)
        rg = jax.nn.sigmoid(pa + ba[dr:dr + 1])
        ig = jax.nn.sigmoid(px + bx[dr:dr + 1])
        log_a = -LRU_C * rg * jax.nn.softplus(-lam[dr:dr + 1])
        a = jnp.exp(log_a)
        u = jnp.sqrt(1.0 - jnp.exp(2.0 * log_a)) * (ig * xa)
        out += [a, u]
    return out


def _combine_a(hf, hb, rg):
    return (hf + hb) * jax.nn.gelu(rg)


def _scan_order(kind, nT, nC):
    nL = nT - nC
    if kind == "F":
        return (lambda s: s), False
    if kind == "revF":
        return (lambda s: nT - 1 - s), True
    if kind == "B":
        return (lambda s: jnp.where(s < nC, nC - 1 - s, nT - 1 - (s - nC))), True
    return (lambda s: jnp.where(s < nL, nC + s, s - nL)), False


def _scan8(c, b, down):
    row = lax.broadcasted_iota(jnp.int32, c.shape, 0)
    for s in (1, 2, 4):
        sh = 8 - s if down else s
        cs, bs = pltpu.roll(c, sh, 0), pltpu.roll(b, sh, 0)
        ok = (row < 8 - s) if down else (row >= s)
        b = jnp.where(ok, c * bs + b, b)
        c = jnp.where(ok, c * cs, c)
    return c, b


def _prev8(x, x_in, down):
    row = lax.broadcasted_iota(jnp.int32, x.shape, 0)
    return jnp.where(row == (7 if down else 0), x_in, pltpu.roll(x, 7 if down else 1, 0))


def _scan_fwd(au, dr, d):
    R, tc = d.BW, d.tr
    tile_of, down = _scan_order("F" if dr == 0 else "B", d.T // tc, d.C // tc)

    def body(a_ref, u_ref, h_ref, hp_ref, st):
        @pl.when(pl.program_id(0) == 0)
        def _():
            st[...] = jnp.zeros_like(st)

        def grp(gi, h):
            r = pl.multiple_of((tc // 8 - 1 - gi if down else gi) * 8, 8)
            cc, bb = _scan8(a_ref[pl.ds(r, 8), :], u_ref[pl.ds(r, 8), :], down)
            hs = cc * h + bb
            h_ref[pl.ds(r, 8), :] = hs
            hp_ref[pl.ds(r, 8), :] = _prev8(hs, h, down)
            return hs[0:1] if down else hs[7:8]
        st[...] = lax.fori_loop(0, tc // 8, grp, st[...])

    return pl.pallas_call(
        body, name="lru_scan", grid=(d.T // tc,),
        in_specs=[pl.BlockSpec((tc, R), lambda s: (tile_of(s), 2 * dr)), pl.BlockSpec((tc, R), lambda s: (tile_of(s), 2 * dr + 1))],
        out_specs=[pl.BlockSpec((tc, R), lambda s: (tile_of(s), 0))] * 2, out_shape=[SDS((d.T, R), f32)] * 2,
        scratch_shapes=[pltpu.VMEM((1, R), f32)], compiler_params=_params(("arbitrary",)),
    )(au, au)


def _scan_bwd(dh, au, hp, dr, d):
    R, tc = d.BW, d.tr
    tile_of, down = _scan_order("revF" if dr == 0 else "revB", d.T // tc, d.C // tc)

    def body(dh_ref, a_ref, hp_ref, o_ref, st):
        @pl.when(pl.program_id(0) == 0)
        def _():
            st[...] = jnp.zeros_like(st)

        def grp(gi, carry):
            lam, an = carry
            r = pl.multiple_of((tc // 8 - 1 - gi if down else gi) * 8, 8)
            at, dt, ht = a_ref[pl.ds(r, 8), :], dh_ref[pl.ds(r, 8), :], hp_ref[pl.ds(r, 8), :]
            cc, bb = _scan8(_prev8(at, an, down), dt, down)
            lt = cc * lam + bb
            o_ref[pl.ds(r, 8), 0:R] = lt * ht
            o_ref[pl.ds(r, 8), R:2 * R] = lt
            return (lt[0:1], at[0:1]) if down else (lt[7:8], at[7:8])
        lam, an = lax.fori_loop(0, tc // 8, grp, (st[0:1], st[1:2]))
        st[0:1] = lam
        st[1:2] = an

    return pl.pallas_call(
        body, name="lru_scan_bwd", grid=(d.T // tc,),
        in_specs=[pl.BlockSpec((tc, R), lambda s: (tile_of(s), 0)), pl.BlockSpec((tc, R), lambda s: (tile_of(s), 2 * dr)),
                  pl.BlockSpec((tc, R), lambda s: (tile_of(s), 0))],
        out_specs=pl.BlockSpec((tc, 2 * R), lambda s: (tile_of(s), 0)), out_shape=SDS((d.T, 2 * R), f32),
        scratch_shapes=[pltpu.VMEM((2, R), f32)], compiler_params=_params(("arbitrary",)),
    )(dh, au, hp)


def _swap_pairs(x):
    w = x.shape[1]
    lane = lax.broadcasted_iota(jnp.int32, x.shape, 1)
    return jnp.where(lane % 64 < 32, pltpu.roll(x, w - 32, 1), pltpu.roll(x, 32, 1))


def _att_bias(G):
    rows, cols = G * ATT_BLOCK, 3 * ATT_BLOCK
    qi = lax.broadcasted_iota(jnp.int32, (rows, cols), 0) % ATT_BLOCK
    kj = lax.broadcasted_iota(jnp.int32, (rows, cols), 1)
    band = jnp.abs(kj - ATT_BLOCK - qi) <= ATT_BLOCK
    kb = kj // ATT_BLOCK
    out = [jnp.zeros((rows, cols), bool)]
    for p in (False, True):
        for n in (False, True):
            out.append(band & ((kb == 1) | ((kb == 0) & p) | ((kb == 2) & n)))
    return jnp.where(jnp.stack(out), 0.0, NEG_INF).astype(f32)


def _att_specs(d):
    G, nB, nCb = d.G, d.T // ATT_BLOCK, d.C // ATT_BLOCK
    KW, QW = d.NKV * HEAD_DIM, d.NH * HEAD_DIM
    blk = lambda f: pl.BlockSpec((ATT_BLOCK, KW), lambda b: (f(b), 0))
    three = [blk(lambda b: jnp.maximum(b - 1, 0)), blk(lambda b: b), blk(lambda b: jnp.minimum(b + 1, nB - 1))]
    ctxs = pl.BlockSpec((d.C, KW), lambda b: (0, 0))
    qs = pl.BlockSpec((ATT_BLOCK, QW), lambda b: (b, 0))
    sk = pl.BlockSpec((d.NKV, G * ATT_BLOCK, 1), lambda b: (0, 0, 0))
    variant = lambda b: jnp.where(b < nCb, 0, 1 + 2 * jnp.where(b - 1 >= nCb, 1, 0) + jnp.where(b + 1 <= nB - 1, 1, 0))
    bias = pl.BlockSpec((None, G * ATT_BLOCK, 3 * ATT_BLOCK), lambda b: (variant(b), 0, 0))
    return qs, three, ctxs, sk, bias


def _stack_heads(x, G):
    return jnp.concatenate([x[:, g * HEAD_DIM:(g + 1) * HEAD_DIM] for g in range(G)], axis=0)


def _unstack_heads(x, G):
    return jnp.concatenate([x[g * ATT_BLOCK:(g + 1) * ATT_BLOCK] for g in range(G)], axis=1)


def _att_probs(qg, kcat, kctx, sink, bias):
    scale = HEAD_DIM ** -0.5
    s = lax.dot_general(qg, kcat, _DIMS["nt"], preferred_element_type=f32) * scale + bias
    sc = lax.dot_general(qg, kctx, _DIMS["nt"], preferred_element_type=f32) * scale
    m = jnp.maximum(jnp.maximum(jnp.max(s, axis=1, keepdims=True), jnp.max(sc, axis=1, keepdims=True)), sink)
    e, ec, es = jnp.exp(s - m), jnp.exp(sc - m), jnp.exp(sink - m)
    inv = 1.0 / (jnp.sum(e, axis=1, keepdims=True) + jnp.sum(ec, axis=1, keepdims=True) + es)
    return e, ec, es, inv


def _attn_fwd(qr, kr, vb, sinkrows, d):
    G, nB, nCb = d.G, d.T // ATT_BLOCK, d.C // ATT_BLOCK
    qs, three, ctxs, sk, bs = _att_specs(d)

    def body(q_ref, k0, k1, k2, v0, v1, v2, kc_ref, vc_ref, s_ref, b_ref, o_ref):
        bias = b_ref[...]
        for kh in range(d.NKV):
            ks, qsl = slice(kh * HEAD_DIM, (kh + 1) * HEAD_DIM), slice(kh * G * HEAD_DIM, (kh + 1) * G * HEAD_DIM)
            qg = _stack_heads(q_ref[:, qsl], G)
            kcat = jnp.concatenate([k0[:, ks], k1[:, ks], k2[:, ks]], axis=0)
            vcat = jnp.concatenate([v0[:, ks], v1[:, ks], v2[:, ks]], axis=0)
            e, ec, _, inv = _att_probs(qg, kcat, kc_ref[:, ks], s_ref[kh], bias)
            p, pc = e * inv, ec * inv
            o = jnp.dot(p.astype(bf16), vcat, preferred_element_type=f32) + jnp.dot(pc.astype(bf16), vc_ref[:, ks], preferred_element_type=f32)
            o_ref[:, qsl] = _unstack_heads(o, G).astype(bf16)

    return pl.pallas_call(
        body, name="attn", grid=(nB,), in_specs=[qs] + three + three + [ctxs, ctxs, sk, bs],
        out_specs=qs, out_shape=SDS((d.T, d.NH * HEAD_DIM), bf16), compiler_params=_params(("arbitrary",)),
    )(qr, kr, kr, kr, vb, vb, vb, kr, vb, sinkrows, _att_bias(G))


def _attn_bwd(qr, kr, vb, sinkrows, dy, d):
    G, nB, nCb = d.G, d.T // ATT_BLOCK, d.C // ATT_BLOCK
    qs, three, ctxs, sk, bs = _att_specs(d)
    KW = d.NKV * HEAD_DIM
    part = pl.BlockSpec((ATT_BLOCK, KW), lambda b: (b, 0))

    def body(q_ref, k0, k1, k2, v0, v1, v2, kc_ref, vc_ref, s_ref, dy_ref, b_ref,
             dq_ref, dk0, dk1, dk2, dv0, dv1, dv2, dkc_ref, dvc_ref, ds_ref):
        scale = HEAD_DIM ** -0.5
        bias = b_ref[...]

        @pl.when(pl.program_id(0) == 0)
        def _():
            dkc_ref[...] = jnp.zeros_like(dkc_ref)
            dvc_ref[...] = jnp.zeros_like(dvc_ref)
            ds_ref[...] = jnp.zeros_like(ds_ref)

        for kh in range(d.NKV):
            ks, qsl = slice(kh * HEAD_DIM, (kh + 1) * HEAD_DIM), slice(kh * G * HEAD_DIM, (kh + 1) * G * HEAD_DIM)
            qg = _stack_heads(q_ref[:, qsl], G)
            kcat = jnp.concatenate([k0[:, ks], k1[:, ks], k2[:, ks]], axis=0)
            vcat = jnp.concatenate([v0[:, ks], v1[:, ks], v2[:, ks]], axis=0)
            kctx, vctx = kc_ref[:, ks], vc_ref[:, ks]
            e, ec, es, inv = _att_probs(qg, kcat, kctx, s_ref[kh], bias)
            p, pc, ps = e * inv, ec * inv, es * inv
            dog = _stack_heads(dy_ref[:, qsl], G).astype(bf16)
            dp = lax.dot_general(dog, vcat, _DIMS["nt"], preferred_element_type=f32)
            dpc = lax.dot_general(dog, vctx, _DIMS["nt"], preferred_element_type=f32)
            delta = jnp.sum(p * dp, axis=1, keepdims=True) + jnp.sum(pc * dpc, axis=1, keepdims=True)
            ds = (p * (dp - delta) * scale).astype(bf16)
            dsc = (pc * (dpc - delta) * scale).astype(bf16)
            dq = jnp.dot(ds, kcat, preferred_element_type=f32) + jnp.dot(dsc, kctx, preferred_element_type=f32)
            dq_ref[:, qsl] = _unstack_heads(dq, G)
            dk = lax.dot_general(ds, qg, _DIMS["tn"], preferred_element_type=f32)
            dv = lax.dot_general(p.astype(bf16), dog, _DIMS["tn"], preferred_element_type=f32)
            for j, (rk, rv) in enumerate(((dk0, dv0), (dk1, dv1), (dk2, dv2))):
                rk[:, ks] = dk[j * ATT_BLOCK:(j + 1) * ATT_BLOCK]
                rv[:, ks] = dv[j * ATT_BLOCK:(j + 1) * ATT_BLOCK]
            dkc_ref[:, ks] += lax.dot_general(dsc, qg, _DIMS["tn"], preferred_element_type=f32)
            dvc_ref[:, ks] += lax.dot_general(pc.astype(bf16), dog, _DIMS["tn"], preferred_element_type=f32)
            ds_ref[kh] += -ps * delta

    kv = SDS((d.T, KW), f32)
    return pl.pallas_call(
        body, name="attn_bwd", grid=(nB,), in_specs=[qs] + three + three + [ctxs, ctxs, sk, qs, bs],
        out_specs=[qs] + [part] * 6 + [ctxs, ctxs, sk],
        out_shape=[SDS((d.T, d.NH * HEAD_DIM), f32)] + [kv] * 6 + [SDS((d.C, KW), f32)] * 2 + [SDS((d.NKV, G * ATT_BLOCK, 1), f32)],
        compiler_params=_params(("arbitrary",)),
    )(qr, kr, kr, kr, vb, vb, vb, kr, vb, sinkrows, dy, _att_bias(G))


def _mixer_fwd(h, modv, P, l, tabs, d, sc, site):
    D, BW, T, C = d.D, d.BW, d.T, d.C
    u = _norm_mod_fwd(h, P["norm_g"][1:2], modv, 3, d)
    za = _mm(u, sc.env[("wint", l)], "nt", bf16, "in_proj_a", N=d.NA, tm=1408, comm=sc.at(site + "in_a"))
    gz = _mm(u, sc.env[("wint", l)], "nt", bf16, "in_proj_g", N=3 * D, tm=1408, b_off=d.NA, comm=sc.at(site + "in_g"))
    xa = _rowwise("lru_conv", lambda t: ([_taps(t.H[0], t.K[0], 2, d.tr, 1) + t.K[1]], [], []), T, C, d.tr,
                  halos=[(za, BW, 0)], consts=[P["rnn_conv_w"], P["rnn_conv_b"]], outs=[(BW, f32)])[0]
    lru = [P["lru_w_a"], P["lru_b_a"], P["lru_w_x"], P["lru_b_x"], P["lru_lambda"]]
    au = _rowwise("lru_gates", lambda t: ([_gates(t.R[0], *t.K, d.NB)], [], []), T, C, d.tr,
                  rows=[(xa, BW, 0, 0)], consts=lru, outs=[(4 * BW, f32)])[0]
    hf, hpf = _scan_fwd(au, 0, d)
    hb, hpb = _scan_fwd(au, 1, d)
    ya = _rowwise("lru_out", lambda t: ([_combine_a(*t.R)], [], []), T, C, d.tr,
                  rows=[(hf, BW, 0, 0), (hb, BW, 0, 0), (za, BW, 1, 0)], outs=[(BW, bf16)])[0]
    yb = _rowwise("sconv", lambda t: ([t.R[0] * _taps(t.H[0] * t.H[1], t.K[0], 1, d.tr, 1)], [], []), T, C, d.tr,
                  rows=[(za, BW, 2, 0)], halos=[(za, BW, 3), (za, BW, 4)], consts=[P["sc_conv_w"]], outs=[(BW, bf16)])[0]
    QW, KW = d.NH * HEAD_DIM, d.NKV * HEAD_DIM

    def rope(t):
        q, k, v, cs, sn = t.R
        cq, sq, ck, skn = jnp.tile(cs, (1, d.NH)), jnp.tile(sn, (1, d.NH)), jnp.tile(cs, (1, d.NKV)), jnp.tile(sn, (1, d.NKV))
        return [q * cq + _swap_pairs(q) * sq, k * ck + _swap_pairs(k) * skn, v], [], []
    kcb = (5 * BW + QW) // KW
    qr, kr, vb = _rowwise("rope", rope, T, C, d.tr,
                          rows=[(za, QW, (5 * BW) // QW, 0), (za, KW, kcb, 0), (za, KW, kcb + 1, 0),
                                (tabs["cos"], HEAD_DIM, 0, 0), (tabs["sin"], HEAD_DIM, 0, 0)],
                          outs=[(QW, bf16), (KW, bf16), (KW, bf16)])
    yatt = _attn_fwd(qr, kr, vb, P["sinkrows"], d)
    ys = (ya, yb, yatt)
    ps = [_mm(ys[i], sc.env[("wbt", l)], "nt", bf16, "lift", b_lead=i, tm=1408, tn=1024) for i in range(N_BRANCH)]

    def merge(t):
        gzv, bm = t.R[0], t.K[0]
        acc = None
        for i in range(N_BRANCH):
            term = jax.nn.sigmoid(gzv[:, i * D:(i + 1) * D] + bm[i:i + 1]) * t.R[1 + i]
            acc = term if acc is None else acc + term
        return [acc], [], []
    merged = _rowwise("merge", merge, T, C, d.trw, rows=[(gz, 3 * D, 0, 0)] + [(p, D, 0, 0) for p in ps],
                      consts=[P["b_merge"]], outs=[(D, bf16)])[0]
    y, hn = _mm(merged, sc.env[("wout", l)], "nn", bf16, "out_proj", tn=1024, comm=sc.at(site + "out"),
                resid=(h, modv, 5, 1.0, d.C))
    sv = dict(h=h, u=u, za=za, gz=gz, xa=xa, au=au, hf=hf, hpf=hpf, hb=hb, hpb=hpb, ys=ys, qr=qr, kr=kr, vb=vb,
              ps=ps, merged=merged, y=y)
    return hn, sv


def _mixer_bwd(dh, sv, modv, P, l, tabs, d, sc, site, pre=None, nxt=None):
    W = dict(wint=sc.env[("wint", l)], wbt=sc.env[("wbt", l)], wout=sc.env[("wout", l)])
    D, BW, T, C = d.D, d.BW, d.T, d.C
    QW, KW = d.NH * HEAD_DIM, d.NKV * HEAD_DIM
    za, gz = sv["za"], sv["gz"]
    dy, dgate = pre if pre is not None else _resid_bwd(dh, sv["y"], modv, 5, 1.0, d)
    dmerged = _mm(dy, W["wout"], "nt", f32, "out_proj_dx", tn=1024)
    sc.env[("g", "wout", l)] = _mm(sv["merged"], dy, "tn", bf16, "out_proj_dw", tm=1024, tn=2048, tk=768)

    def merge_bwd(t):
        gzv, bm, dm = t.R[0], t.K[0], t.R[4]
        dps, dgs, dbs = [], [], []
        for i in range(N_BRANCH):
            gate = jax.nn.sigmoid(gzv[:, i * D:(i + 1) * D] + bm[i:i + 1])
            dps.append(dm * gate)
            dgi = dm * t.R[1 + i] * gate * (1.0 - gate)
            dgs.append(dgi)
            dbs.append(jnp.sum(dgi, axis=0, keepdims=True))
        return [dps, dgs], [jnp.concatenate(dbs, axis=0)], []
    dp, dgz, dbm = _rowwise("merge_bwd", merge_bwd, T, C, d.trw,
                            rows=[(gz, 3 * D, 0, 0)] + [(p, D, 0, 0) for p in sv["ps"]] + [(dmerged, D, 0, 0)],
                            consts=[P["b_merge"]], outs=[(3 * D, bf16), (3 * D, bf16)], accs=[(N_BRANCH, D)])
    dys = [_mm(dp, W["wbt"], "nn", f32, "lift_dx", K=D, a_off=i * D, b_lead=i, tn=1024) for i in range(N_BRANCH)]
    sc.env[("g", "wbt", l)] = jnp.stack([_mm(dp, sv["ys"][i], "tn", bf16, "lift_dw", M=D, a_off=i * D, tm=1024, tn=1024, tk=768)
                                         for i in range(N_BRANCH)])
    def out_bwd(t):
        _, vjp = jax.vjp(_combine_a, t.R[1], t.R[2], t.R[3])
        dhf, _, drg = vjp(t.R[0])
        return [dhf, drg], [], []
    dhs, drg = _rowwise("lru_out_bwd", out_bwd, T, C, d.tr,
                        rows=[(dys[0], BW, 0, 0), (sv["hf"], BW, 0, 0), (sv["hb"], BW, 0, 0), (za, BW, 1, 0)],
                        outs=[(BW, f32), (BW, bf16)])
    dau0 = _scan_bwd(dhs, sv["au"], sv["hpf"], 0, d)
    dau1 = _scan_bwd(dhs, sv["au"], sv["hpb"], 1, d)
    lru = [P["lru_w_a"], P["lru_b_a"], P["lru_w_x"], P["lru_b_x"], P["lru_lambda"]]

    def gates_bwd(t):
        _, vjp = jax.vjp(lambda xa, *k: _gates(xa, *k, d.NB), t.R[0], *t.K)
        d0, d1 = t.R[1], t.R[2]
        g = vjp([d0[:, :BW], d0[:, BW:], d1[:, :BW], d1[:, BW:]])
        return [g[0]], list(g[1:]), []
    dxa, dwa, dba, dwx, dbx, dlam = _rowwise(
        "lru_gates_bwd", gates_bwd, T, C, d.tr, rows=[(sv["xa"], BW, 0, 0), (dau0, 2 * BW, 0, 0), (dau1, 2 * BW, 0, 0)],
        consts=lru, outs=[(BW, f32)], accs=[p.shape for p in lru])
    sc.env[("gs", "lru_w_a", l)], sc.env[("gs", "lru_w_x", l)] = dwa, dwx

    def conv_bwd(t):
        dxe, xe, w = t.H[0], t.H[1], t.K[0]
        cur = dxe[HALO:HALO + d.tr]
        dw = jnp.concatenate([jnp.sum(cur * xe[HALO + k - 2:HALO + k - 2 + d.tr], axis=0, keepdims=True) for k in range(w.shape[0])], axis=0)
        return [_taps(dxe, w, 2, d.tr, -1)], [dw, jnp.sum(cur, axis=0, keepdims=True)], []
    drx, dcw, dcb = _rowwise("lru_conv_bwd", conv_bwd, T, C, d.tr, halos=[(dxa, BW, 0), (za, BW, 0)],
                             consts=[P["rnn_conv_w"]], outs=[(BW, bf16)], accs=[P["rnn_conv_w"].shape, (1, BW)])
    def sconv_bwd(t):
        scg, sx, sb, dyb = t.H
        w, tr = t.K[0], d.tr
        me = scg * sx
        dsb = dyb[HALO:HALO + tr] * _taps(me, w, 1, tr, 1)
        dce = dyb * sb
        dm = _taps(dce, w, 1, tr, -1)
        cur = dce[HALO:HALO + tr]
        dw = jnp.concatenate([jnp.sum(cur * me[HALO + k - 1:HALO + k - 1 + tr], axis=0, keepdims=True) for k in range(w.shape[0])], axis=0)
        return [[dsb, dm * sx[HALO:HALO + tr], dm * scg[HALO:HALO + tr]]], [dw], []
    dsc, dscw = _rowwise("sconv_bwd", sconv_bwd, T, C, d.tr, halos=[(za, BW, 3), (za, BW, 4), (za, BW, 2), (dys[1], BW, 0)],
                         consts=[P["sc_conv_w"]], outs=[(3 * BW, bf16)], accs=[P["sc_conv_w"].shape])
    dqr, dk0, dk1, dk2, dv0, dv1, dv2, dkc, dvc, dsink = _attn_bwd(sv["qr"], sv["kr"], sv["vb"], P["sinkrows"], dys[2], d)
    nB, nCb = T // ATT_BLOCK, C // ATT_BLOCK

    def att_join(t):
        dq, a1, a0, a2, b1, b0, b2, kc, vc, cs, sn = t.R
        cq, sq, ck, skn = jnp.tile(cs, (1, d.NH)), jnp.tile(sn, (1, d.NH)), jnp.tile(cs, (1, d.NKV)), jnp.tile(sn, (1, d.NKV))
        up = t.i + 1 <= nB - 1
        dn = t.i >= 1
        isc = t.seg == 0
        dk = a1 + jnp.where(up, a0, 0.0) + jnp.where(dn, a2, 0.0) + jnp.where(isc, kc, 0.0)
        dv = b1 + jnp.where(up, b0, 0.0) + jnp.where(dn, b2, 0.0) + jnp.where(isc, vc, 0.0)
        return [[dq * cq + _swap_pairs(dq * sq), dk * ck + _swap_pairs(dk * skn), dv]], [], []
    dqkv = _rowwise("attn_join", att_join, T, C, ATT_BLOCK,
                    rows=[(dqr, QW, 0, 0), (dk1, KW, 0, 0), (dk0, KW, 0, 1), (dk2, KW, 0, -1), (dv1, KW, 0, 0), (dv0, KW, 0, 1),
                          (dv2, KW, 0, -1), (dkc, KW, 0, 0), (dvc, KW, 0, 0), (tabs["cos"], HEAD_DIM, 0, 0),
                          (tabs["sin"], HEAD_DIM, 0, 0)], outs=[(QW + 2 * KW, bf16)])[0]
    dz = jnp.concatenate([drx, drg, dsc, dqkv, dgz], axis=1)
    du = _mm(dz, W["wint"], "nn", f32, "in_proj_dx", tn=1024, tk=3200, comm=sc.at(site + "in_x"))
    sc.env[("g", "wint", l)] = _mm(dz, sv["u"], "tn", bf16, "in_proj_dw", tm=2560, tn=1024, tk=768, comm=sc.at(site + "in_w"))
    res = _norm_mod_bwd(sv["h"], du, dh, P["norm_g"][1:2], modv, 3, d, nxt)
    grads = dict(norm_g1=res[1], dss=res[2], dgate=dgate, b_merge=dbm, rnn_conv_w=dcw, rnn_conv_b=dcb,
                 lru_b_a=dba, lru_b_x=dbx, lru_lambda=dlam, sc_conv_w=dscw, sinkrows=dsink)
    return res[0], grads, (res[3] if nxt is not None else None)


def _exchange(name, comm):
    ns, nd = len(comm.srcs), len(comm.dsts)

    def body(*refs):
        descs = comm.descriptors(refs[:ns], refs[ns + nd:ns + 2 * nd], refs[ns + 2 * nd:], _coords())
        _comm_start(descs)
        _comm_mid(descs)
        _comm_wait(descs)

    res = pl.pallas_call(
        body, name=name, in_specs=[ANY] * (ns + nd), out_specs=[ANY] * nd, out_shape=[SDS(a.shape, a.dtype) for a in comm.dsts],
        input_output_aliases={ns + q: q for q in range(nd)}, scratch_shapes=comm.scratch(),
    )(*comm.srcs, *comm.dsts)
    comm.store(res)


def _allgather(name, a):
    env = {"g": lax.empty((N_DEV,) + a.shape, a.dtype)}
    cm = _Comm(env)
    cm.add(a, "g", lambda ref, me, pi, r: ref, lambda ref, sender, r: ref.at[sender])
    _exchange(name, cm)
    return env["g"]


def _sum8(x):
    n = x.shape[1]
    tn = _tile(n, 2048, 8)

    def body(x_ref, o_ref):
        acc = x_ref[0]
        for j in range(1, N_DEV):
            acc = acc + x_ref[j]
        o_ref[...] = acc
    return pl.pallas_call(body, name="sum8", grid=(n // tn,), in_specs=[pl.BlockSpec((N_DEV, tn, 128), lambda i: (0, i, 0))],
                          out_specs=pl.BlockSpec((tn, 128), lambda i: (i, 0)), out_shape=SDS((n, 128), f32),
                          compiler_params=_params(("parallel",)))(x)


def _cast_direct(w):
    ly, R, Cn = w.shape
    tc = _tile(Cn, 512, 128)
    spec = pl.BlockSpec((None, R, tc), lambda l, j: (l, 0, j))

    def body(w_ref, o_ref):
        o_ref[...] = w_ref[...].astype(bf16)
    return pl.pallas_call(body, name="cast", grid=(ly, Cn // tc), in_specs=[spec], out_specs=spec, out_shape=SDS(w.shape, bf16),
                          compiler_params=_params(("parallel", "parallel")))(w)


def _cast_transposed(w):
    four = w.ndim == 4
    ly, I, (K, Nl) = w.shape[0], (w.shape[1] if four else 1), w.shape[-2:]
    tk = _tile(K, 256, 128)
    mid = (lambda i: (i,)) if four else (lambda i: ())
    nn = (None,) * (w.ndim - 2)

    def body(w_ref, o_ref):
        o_ref[...] = w_ref[...].T.astype(bf16)
    return pl.pallas_call(
        body, name="cast_t", grid=(ly, I, K // tk), in_specs=[pl.BlockSpec(nn + (tk, Nl), lambda l, i, k: (l,) + mid(i) + (k, 0))],
        out_specs=pl.BlockSpec(nn + (Nl, tk), lambda l, i, k: (l,) + mid(i) + (0, k)), out_shape=SDS(w.shape[:-2] + (Nl, K), bf16),
        compiler_params=_params(("parallel", "parallel", "parallel")))(w)


def _adam(g, w, m, v):
    m = ADAM_B1 * m + (1.0 - ADAM_B1) * g
    v = ADAM_B2 * v + (1.0 - ADAM_B2) * (g * g)
    m_hat = m / (1.0 - ADAM_B1 ** ADAM_STEP)
    v_hat = v / (1.0 - ADAM_B2 ** ADAM_STEP)
    delta = -ADAM_LR * (m_hat / (jnp.sqrt(v_hat) + ADAM_EPS) + ADAM_WD * w)
    return delta, m, v


def _slot_sum(x):
    acc = x[0].astype(f32)
    for r in range(1, x.shape[0]):
        acc = acc + x[r].astype(f32)
    return acc


def _adamw_direct(g, w, m, v, slots):
    ly, R, Cn = w.shape
    tr = R if R <= 1024 else _tile(R, 512, 16)
    tc = _tile(Cn, max(128, (256 * 1024) // tr // 128 * 128), 128)
    spec = pl.BlockSpec((None, tr, tc), lambda l, i, j: (l, i, j))
    gspec = pl.BlockSpec((None, N_DEV, tr, tc), lambda l, i, j: (l, 0, i, j)) if slots else spec

    def body(g_ref, w_ref, m_ref, v_ref, go, do, mo, vo):
        gv = _slot_sum(g_ref[...]) if slots else g_ref[...]
        go[...] = gv
        do[...], mo[...], vo[...] = _adam(gv, w_ref[...], m_ref[...], v_ref[...])
    return pl.pallas_call(body, name="adamw", grid=(ly, R // tr, Cn // tc), in_specs=[gspec, spec, spec, spec], out_specs=[spec] * 4,
                          out_shape=[SDS(w.shape, f32)] * 4, compiler_params=_params(("parallel",) * 3))(g, w, m, v)


def _adamw_transposed(g, w, m, v):
    four = w.ndim == 4
    ly, I, (K, Nl) = w.shape[0], (w.shape[1] if four else 1), w.shape[-2:]
    tk = _tile(K, 128, 128)
    mid = (lambda i: (i,)) if four else (lambda i: ())
    nn = (None,) * (w.ndim - 3)
    spec = pl.BlockSpec((None,) + nn + (tk, Nl), lambda l, i, k: (l,) + mid(i) + (k, 0))
    gspec = pl.BlockSpec((None, N_DEV) + nn + (Nl, tk), lambda l, i, k: (l, 0) + mid(i) + (0, k))

    def body(g_ref, w_ref, m_ref, v_ref, go, do, mo, vo):
        gv = _slot_sum(g_ref[...]).T
        go[...] = gv
        do[...], mo[...], vo[...] = _adam(gv, w_ref[...], m_ref[...], v_ref[...])
    return pl.pallas_call(body, name="adamw_t", grid=(ly, I, K // tk), in_specs=[gspec, spec, spec, spec], out_specs=[spec] * 4,
                          out_shape=[SDS(w.shape, f32)] * 4, compiler_params=_params(("parallel",) * 3))(g, w, m, v)


def _ada_fwd(cvec, ada_w, ada_b_cols):
    ly, D, cols = ada_w.shape
    tn = _tile(cols, 768, 128)

    def body(c_ref, w_ref, b_ref, o_ref):
        o_ref[...] = _bdot(jax.nn.silu(c_ref[...]), w_ref[...]) + b_ref[...]
    return pl.pallas_call(
        body, name="ada", grid=(ly, cols // tn),
        in_specs=[pl.BlockSpec((16, D), lambda l, j: (0, 0)), pl.BlockSpec((None, D, tn), lambda l, j: (l, 0, j)),
                  pl.BlockSpec((None, 1, tn), lambda l, j: (l, 0, j))],
        out_specs=pl.BlockSpec((None, 16, tn), lambda l, j: (l, 0, j)), out_shape=SDS((ly, 16, cols), f32),
        compiler_params=_params(("parallel", "parallel")))(cvec, ada_w, ada_b_cols)


def _ada_bwd(cvec, ada_w, dm):
    ly, D, cols = ada_w.shape
    tn = _tile(cols, 768, 128)

    def body(c_ref, w_ref, d_ref, gw_ref, gc_ref):
        first = jnp.logical_and(pl.program_id(0) == 0, pl.program_id(1) == 0)

        @pl.when(first)
        def _():
            gc_ref[...] = jnp.zeros_like(gc_ref)

        def f(cv, w):
            return _bdot(jax.nn.silu(cv), w)
        _, vjp = jax.vjp(f, c_ref[...], w_ref[...])
        dc, dw = vjp(d_ref[...])
        gw_ref[...] = dw
        gc_ref[...] += dc
    return pl.pallas_call(
        body, name="ada_bwd", grid=(ly, cols // tn),
        in_specs=[pl.BlockSpec((16, D), lambda l, j: (0, 0)), pl.BlockSpec((None, D, tn), lambda l, j: (l, 0, j)),
                  pl.BlockSpec((None, 16, tn), lambda l, j: (l, 0, j))],
        out_specs=[pl.BlockSpec((None, D, tn), lambda l, j: (l, 0, j)), pl.BlockSpec((16, D), lambda l, j: (0, 0))],
        out_shape=[SDS(ada_w.shape, f32), SDS((16, D), f32)], compiler_params=_params(("arbitrary", "arbitrary")))(cvec, ada_w, dm)


def _flat_adamw(g, w, m, v):
    n = w.shape[0]
    tn = _tile(n, 1024, 8)
    spec = pl.BlockSpec((tn, 128), lambda i: (i, 0))

    def body(g_ref, w_ref, m_ref, v_ref, do, mo, vo):
        do[...], mo[...], vo[...] = _adam(g_ref[...], w_ref[...], m_ref[...], v_ref[...])
    return pl.pallas_call(body, name="adamw_small", grid=(n // tn,), in_specs=[spec] * 4, out_specs=[spec] * 3,
                          out_shape=[SDS(w.shape, f32)] * 3, compiler_params=_params(("parallel",)))(g, w, m, v)


def _pack(arrs):
    flat = jnp.concatenate([a.reshape(-1).astype(f32) for a in arrs])
    n = _round_up(flat.shape[0], 512 * 128)
    return jnp.pad(flat, (0, n - flat.shape[0])).reshape(n // 128, 128)


def _unpack(flat, shapes, lead=()):
    flat = flat.reshape(lead + (-1,))
    out, off = [], 0
    for s in shapes:
        sz = math.prod(s)
        out.append(flat[..., off:off + sz].reshape(lead + tuple(s)))
        off += sz
    return out


def _unshard_last(g):
    g = jnp.moveaxis(g, 0, -2)
    return g.reshape(g.shape[:-2] + (g.shape[-2] * g.shape[-1],))


class _Dims:
    pass


def _rope_tables(L, C, NH):
    rows = L // GRID_W
    row = jnp.repeat(jnp.arange(rows), GRID_W).astype(f32)
    col = jnp.tile(jnp.arange(GRID_W), rows).astype(f32)
    half = HEAD_DIM // 2
    inv = ROPE_BASE ** (-jnp.arange(0, half, 2, dtype=f32) / half)
    ar, ac = row[:, None] * inv, col[:, None] * inv
    cos = jnp.concatenate([jnp.cos(ar), jnp.cos(ar), jnp.cos(ac), jnp.cos(ac)], axis=-1)
    sin = jnp.concatenate([-jnp.sin(ar), jnp.sin(ar), -jnp.sin(ac), jnp.sin(ac)], axis=-1)
    cos = jnp.concatenate([jnp.ones((C, HEAD_DIM), f32), cos], axis=0)
    sin = jnp.concatenate([jnp.zeros((C, HEAD_DIM), f32), sin], axis=0)
    return dict(cos=cos, sin=sin)


def kernel(x, c, ctx, c_ctx, ada_w, ada_b, norm_g, ffn1_w13, ffn1_w2, w_in, b_merge, rnn_conv_w, rnn_conv_b, lru_w_a, lru_b_a, lru_w_x, lru_b_x, lru_lambda, sc_conv_w, attn_sink, w_branch, w_out, ffn2_w13, ffn2_w2, final_norm_g, loss_target, m_c_ctx, m_ada_w, m_ada_b, m_norm_g, m_ffn1_w13, m_ffn1_w2, m_w_in, m_b_merge, m_rnn_conv_w, m_rnn_conv_b, m_lru_w_a, m_lru_b_a, m_lru_w_x, m_lru_b_x, m_lru_lambda, m_sc_conv_w, m_attn_sink, m_w_branch, m_w_out, m_ffn2_w13, m_ffn2_w2, m_final_norm_g, v_c_ctx, v_ada_w, v_ada_b, v_norm_g, v_ffn1_w13, v_ffn1_w2, v_w_in, v_b_merge, v_rnn_conv_w, v_rnn_conv_b, v_lru_w_a, v_lru_b_a, v_lru_w_x, v_lru_b_x, v_lru_lambda, v_sc_conv_w, v_attn_sink, v_w_branch, v_w_out, v_ffn2_w13, v_ffn2_w2, v_final_norm_g):
    d = _Dims()
    L, D = x.shape[1], x.shape[2]
    C = ctx.shape[1]
    LY = ada_w.shape[0]
    d.D, d.C, d.T = D, C, C + L
    d.F = ffn1_w2.shape[1] * N_DEV
    d.Fp = _round_up(d.F, 512)
    d.IN = w_in.shape[2] * N_DEV
    d.BW = w_branch.shape[2]
    d.NH = attn_sink.shape[1]
    d.NA = d.IN - N_BRANCH * D
    d.NKV = (d.NA - 5 * d.BW - d.NH * HEAD_DIM) // (2 * HEAD_DIM)
    d.G = d.NH // d.NKV
    d.NB = lru_w_a.shape[2]
    d.tr = _tile(C, 256, 8)
    d.trw = _tile(C, 128, 8)
    assert d.NH * HEAD_DIM == d.BW and L % GRID_W == 0 and C % ATT_BLOCK == 0 and L % ATT_BLOCK == 0
    sh13, sh2, shin, shd = 2 * d.F // N_DEV, d.F // N_DEV, d.IN // N_DEV, D // N_DEV
    cols9 = N_MOD * D // N_DEV
    me3 = _coords()
    me = _index(me3)

    sharded = [norm_g, b_merge, rnn_conv_w, lru_b_a, lru_b_x, lru_lambda, sc_conv_w]
    shapes1 = [a.shape for a in sharded] + [(D,)]
    g1 = _allgather("gather_small", _pack(sharded + [c.reshape(-1)]))
    parts = _unpack(g1, shapes1, (N_DEV,))
    norm_g_f, b_merge_f, conv_w_f, lru_b_a_f, lru_b_x_f, lru_lam_f, sc_w_f = [_unshard_last(p) for p in parts[:-1]]
    cvec = jnp.concatenate([parts[-1], c_ctx[None], jnp.zeros((7, D), f32)], axis=0)
    ada_b_cols = lax.dynamic_slice_in_dim(ada_b, me * cols9, cols9, axis=1)[:, None, :]
    modcols = _ada_fwd(cvec, ada_w, ada_b_cols)
    g2 = _allgather("gather_mod", modcols)
    modall = jnp.moveaxis(g2, 0, 2).reshape(LY, 16, N_MOD, D)
    mod_lat = lax.dynamic_index_in_dim(modall, me, axis=1, keepdims=False)
    modv = jnp.stack([modall[:, N_DEV], mod_lat], axis=1)

    assert LY == 2
    off13 = lambda j: (j // 4) * d.Fp + (j % 4) * sh13
    tr_ = lambda a: jnp.swapaxes(a, 1, 2)
    cast_t = lambda w: _cast_direct(tr_(w))
    c13a, c13b = cast_t(ffn1_w13), cast_t(ffn2_w13)
    wt = {"w13a": (c13a, off13, sh13, 2 * d.Fp, (D,), False), "w13b": (c13b, off13, sh13, 2 * d.Fp, (D,), False),
          "w2a": (_cast_direct(ffn1_w2), lambda j: j * sh2, sh2, d.Fp, (D,), False),
          "w2b": (_cast_direct(ffn2_w2), lambda j: j * sh2, sh2, d.Fp, (D,), False),
          "wint": (cast_t(w_in), lambda j: j * shin, shin, d.IN, (D,), False),
          "wbt": (_cast_transposed(w_branch), lambda j: j * shd, shd, D, (d.BW,), True),
          "wout": (_cast_direct(w_out), lambda j: j * shd, shd, D, (D,), False)}
    env = {}
    for name, (cw, off, sh, tot, rest, three) in wt.items():
        for l in range(LY):
            env[(name, l)] = lax.empty(((N_BRANCH, tot) if three else (tot,)) + rest, bf16)
        env[("r", name)] = lax.empty((LY, N_DEV) + ((N_BRANCH, sh) if three else (sh,)) + rest, bf16)
    zpad = jnp.zeros((max(d.Fp - d.F, 16), D), bf16)

    def part(sh, h):
        return (0, sh) if h is None else (h * (sh // 2), sh // 2)

    def rows_of(ref, start, n, three):
        sl = pl.ds(start if isinstance(start, int) else pl.multiple_of(start, 16), n)
        return ref.at[:, sl] if three else ref.at[sl]

    def add_gather(cm, name, l, h=None):
        cw, off, sh, tot, rest, three = wt[name]
        r0, n = part(sh, h)
        cm.gather(cw, (name, l), lambda ref: rows_of(ref.at[l], r0, n, three),
                  lambda ref, origin: rows_of(ref, off(origin) + r0, n, three))
        if d.Fp > d.F and h in (None, 0) and name[:2] in ("w1", "w2"):
            for base in ((0, d.Fp) if name[:3] == "w13" else (0,)):
                cm.fill(zpad, (name, l), lambda ref, base=base: ref.at[pl.ds(base + d.F, d.Fp - d.F)])

    def add_rs(cm, name, l, h=None):
        cw, off, sh, tot, rest, three = wt[name]
        r0, n = part(sh, h)
        cm.add(env[("g", name, l)], ("r", name), lambda ref, me_, pi, r: rows_of(ref, off(pi) + r0, n, three),
               lambda ref, sender, r: rows_of(ref.at[l, r], r0, n, three))

    plan = {"F0b.up": [("ag", "w2b", 0), ("ag", "w13a", 1, 0)], "F0b.down": [("ag", "w13a", 1, 1)], "F1b.up": [("ag", "w2b", 1)],
            "B0m.in_x": [("lru",)]}
    for l in range(LY):
        plan.update({f"F{l}a.up": [("ag", "w2a", l), ("ag", "wint", l, 0)], f"F{l}a.down": [("ag", "wint", l, 1), ("ag", "wbt", l)],
                     f"F{l}m.in_a": [("ag", "wout", l), ("ag", "w13b", l, 0)], f"F{l}m.in_g": [("ag", "w13b", l, 1)]})
    for l in range(LY):
        plan.update({f"B{l}b.dw": [("rs", "w13b", l, 0)], f"B{l}b.ux": [("rs", "w13b", l, 1)],
                     f"B{l}m.in_w": [("rs", "wout", l), ("rs", "wbt", l), ("rs", "w2b", l)], f"B{l}a.dx": [("rs", "wint", l, 0)],
                     f"B{l}a.uw": [("rs", "wint", l, 1)], f"B{l}a.dw": [("rs", "w13a", l, 0)]})
    plan.update({"B1a.ux": [("rs", "w13a", 1, 1)], "B0b.dx": [("rs", "w2a", 1)], "B0a.ux": [("rs", "w13a", 0, 1), ("rs", "w2a", 0)]})

    def add_lru(cm):
        for nm in ("lru_w_a", "lru_w_x"):
            gs = jnp.stack([env[("gs", nm, l)] for l in range(LY)]).reshape(-1, 128)
            env[("all", nm)] = lax.empty((N_DEV,) + gs.shape, f32)
            cm.add(gs, ("all", nm), lambda ref, me_, pi, r: ref, lambda ref, sender, r: ref.at[sender])
    sc = _Sched(plan, env, {"ag": add_gather, "rs": add_rs, "lru": add_lru})
    first = _Comm(env)
    add_gather(first, "w13a", 0)
    _exchange("gather_first", first)

    tabs = _rope_tables(L, C, d.NH)
    h = jnp.concatenate([ctx[0], x[0]], axis=0)
    Ps, saves = [], []
    for l in range(LY):
        sinkrows = jnp.repeat(attn_sink[l].reshape(d.NKV, d.G), ATT_BLOCK, axis=1)[:, :, None]
        Ps.append(dict(norm_g=norm_g_f[l], b_merge=b_merge_f[l], rnn_conv_w=conv_w_f[l], rnn_conv_b=rnn_conv_b[l][None],
                       lru_w_a=lru_w_a[l], lru_b_a=lru_b_a_f[l], lru_w_x=lru_w_x[l], lru_b_x=lru_b_x_f[l], lru_lambda=lru_lam_f[l],
                       sc_conv_w=sc_w_f[l], sinkrows=sinkrows))
    for l in range(LY):
        P = Ps[l]
        h, s1 = _ffn_fwd(h, modv[l], P["norm_g"][0:1], ("w13a", l), ("w2a", l), 0, d, sc, f"F{l}a.")
        h, s2 = _mixer_fwd(h, modv[l], P, l, tabs, d, sc, f"F{l}m.")
        h, s3 = _ffn_fwd(h, modv[l], P["norm_g"][2:3], ("w13b", l), ("w2b", l), 6, d, sc, f"F{l}b.")
        saves.append((s1, s2, s3))

    fng = final_norm_g[None]

    def loss_fn(t):
        def f(hv, g):
            y = hv * lax.rsqrt(jnp.mean(hv * hv, axis=-1, keepdims=True) + EPS) * g
            e = y - t.R[1]
            return 0.5 * jnp.sum(jnp.mean(e * e, axis=-1))
        lat = (t.seg == 1).astype(f32)
        val, (dhv, dg) = jax.value_and_grad(f, argnums=(0, 1))(t.R[0], t.K[0])
        return [dhv * lat], [dg * lat, jnp.full((1, 128), val * lat, f32)], []
    dh, dfng, lossv = _rowwise("loss", loss_fn, d.T, C, d.tr, rows=[(h, D, 0, 0), (loss_target[0], D, 0, -(C // d.tr))],
                               consts=[fng], outs=[(D, f32)], accs=[(1, D), (1, 128)])
    loss = lax.psum(lossv[0, 0], AXES)

    gl = [None] * LY
    pre = None
    for l in reversed(range(LY)):
        P = Ps[l]
        s1, s2, s3 = saves[l]
        dh, dg2, dss2, dgt2, pre = _ffn_bwd(dh, s3, modv[l], P["norm_g"][2:3], ("w13b", l), ("w2b", l), 6, d, sc, f"B{l}b.",
                                            pre, (s2["y"], modv[l], 5, 1.0))
        dh, gm, pre = _mixer_bwd(dh, s2, modv[l], P, l, tabs, d, sc, f"B{l}m.", pre, (s1["o"], modv[l], 2, 0.5))
        below = (saves[l - 1][2]["o"], modv[l - 1], 8, 0.5) if l > 0 else None
        dh, dg0, dss0, dgt0, pre = _ffn_bwd(dh, s1, modv[l], P["norm_g"][0:1], ("w13a", l), ("w2a", l), 0, d, sc, f"B{l}a.",
                                            pre, below)
        gm.update(norm_g=jnp.concatenate([dg0, gm["norm_g1"], dg2], axis=0),
                  dmod=jnp.concatenate([dss0, dgt0, gm["dss"], gm["dgate"], dss2, dgt2], axis=1))
        gl[l] = gm
    grad_x = dh[C:][None]
    st = lambda k: jnp.stack([g[k] for g in gl])

    r13a, r2a, rin, rwb, rwo, r13b, r2b = [env[("r", k)] for k in ("w13a", "w2a", "wint", "wbt", "wout", "w13b", "w2b")]
    big = {}
    adamw_tr = lambda r, w, m, v: [tr_(o) for o in _adamw_direct(r, tr_(w), tr_(m), tr_(v), True)]
    big["ffn1_w13"] = adamw_tr(r13a, ffn1_w13, m_ffn1_w13, v_ffn1_w13)
    big["ffn2_w13"] = adamw_tr(r13b, ffn2_w13, m_ffn2_w13, v_ffn2_w13)
    big["w_in"] = adamw_tr(rin, w_in, m_w_in, v_w_in)
    big["w_branch"] = _adamw_transposed(rwb, w_branch, m_w_branch, v_w_branch)
    big["ffn1_w2"] = _adamw_direct(r2a, ffn1_w2, m_ffn1_w2, v_ffn1_w2, True)
    big["ffn2_w2"] = _adamw_direct(r2b, ffn2_w2, m_ffn2_w2, v_ffn2_w2, True)
    big["w_out"] = _adamw_direct(rwo, w_out, m_w_out, v_w_out, True)

    sink_g = jnp.stack([jnp.sum(g["sinkrows"].reshape(d.NKV, d.G, ATT_BLOCK), axis=-1).reshape(d.NH) for g in gl])
    small_full = dict(norm_g=st("norm_g"), b_merge=st("b_merge"), rnn_conv_w=st("rnn_conv_w"), rnn_conv_b=st("rnn_conv_b")[:, 0],
                      lru_b_a=st("lru_b_a"), lru_b_x=st("lru_b_x"), lru_lambda=st("lru_lambda"), sc_conv_w=st("sc_conv_w"), attn_sink=sink_g, final_norm_g=dfng[0])
    names_s = list(small_full)
    dmod = st("dmod")
    pk = _pack([small_full[k] for k in names_s] + [dmod])
    g4 = _allgather("gather_small_grads", pk)
    tot = _unpack(_sum8(g4), [small_full[k].shape for k in names_s] + [dmod.shape])
    sums = dict(zip(names_s, tot[:-1]))
    dmod_sum = tot[-1].reshape(LY, 2, N_MOD * D)
    dmod_all = _unpack(g4, [small_full[k].shape for k in names_s] + [dmod.shape], (N_DEV,))[-1].reshape(N_DEV, LY, 2, N_MOD * D)
    dm_rows = jnp.concatenate([jnp.moveaxis(dmod_all[:, :, 1], 0, 1), dmod_sum[:, 0:1], jnp.zeros((LY, 7, N_MOD * D), f32)], axis=1)
    dm_cols = lax.dynamic_slice_in_dim(dm_rows, me * cols9, cols9, axis=2)
    g_ada_w, dcv = _ada_bwd(cvec, ada_w, dm_cols)
    g5 = _allgather("gather_cctx", _pack([dcv[N_DEV]]))
    g_c_ctx = _sum8(g5).reshape(-1)[:D]
    g_ada_b = dmod_sum[:, 0] + dmod_sum[:, 1]
    ada_out = _adamw_direct(g_ada_w, ada_w, m_ada_w, v_ada_w, False)

    def shard_last(a, n):
        return lax.dynamic_slice_in_dim(a, me * n, n, axis=a.ndim - 1)
    local_g = dict(c_ctx=g_c_ctx, ada_b=g_ada_b, norm_g=shard_last(sums["norm_g"], shd), b_merge=shard_last(sums["b_merge"], shd),
                   rnn_conv_w=shard_last(sums["rnn_conv_w"], d.BW // N_DEV), rnn_conv_b=sums["rnn_conv_b"],
                   lru_b_a=shard_last(sums["lru_b_a"], d.BW // N_DEV), lru_b_x=shard_last(sums["lru_b_x"], d.BW // N_DEV),
                   lru_lambda=shard_last(sums["lru_lambda"], d.BW // N_DEV), sc_conv_w=shard_last(sums["sc_conv_w"], d.BW // N_DEV),
                   attn_sink=sums["attn_sink"], final_norm_g=sums["final_norm_g"])
    wmv = dict(c_ctx=(c_ctx, m_c_ctx, v_c_ctx), ada_b=(ada_b, m_ada_b, v_ada_b), norm_g=(norm_g, m_norm_g, v_norm_g),
               b_merge=(b_merge, m_b_merge, v_b_merge), rnn_conv_w=(rnn_conv_w, m_rnn_conv_w, v_rnn_conv_w),
               rnn_conv_b=(rnn_conv_b, m_rnn_conv_b, v_rnn_conv_b),
               lru_b_a=(lru_b_a, m_lru_b_a, v_lru_b_a), lru_b_x=(lru_b_x, m_lru_b_x, v_lru_b_x),
               lru_lambda=(lru_lambda, m_lru_lambda, v_lru_lambda), sc_conv_w=(sc_conv_w, m_sc_conv_w, v_sc_conv_w),
               attn_sink=(attn_sink, m_attn_sink, v_attn_sink), final_norm_g=(final_norm_g, m_final_norm_g, v_final_norm_g))
    names_l = list(local_g)
    shapes_l = [wmv[k][0].shape for k in names_l]
    gp = _pack([local_g[k].reshape(wmv[k][0].shape) for k in names_l])
    outs_s = _flat_adamw(gp, _pack([wmv[k][0] for k in names_l]), _pack([wmv[k][1] for k in names_l]), _pack([wmv[k][2] for k in names_l]))
    small = {k: [local_g[k].reshape(wmv[k][0].shape)] for k in names_l}
    for o in outs_s:
        for k, a in zip(names_l, _unpack(o, shapes_l)):
            small[k].append(a)
    for nm, (w_, m_, v_) in (("lru_w_a", (lru_w_a, m_lru_w_a, v_lru_w_a)), ("lru_w_x", (lru_w_x, m_lru_w_x, v_lru_w_x))):
        gsum = _sum8(env[("all", nm)])
        flat = lambda a: a.reshape(-1, 128)
        small[nm] = [a.reshape(w_.shape) for a in (gsum, *_flat_adamw(gsum, flat(w_), flat(m_), flat(v_)))]

    order = ["c_ctx", "ada_w", "ada_b", "norm_g", "ffn1_w13", "ffn1_w2", "w_in", "b_merge", "rnn_conv_w", "rnn_conv_b", "lru_w_a",
             "lru_b_a", "lru_w_x", "lru_b_x", "lru_lambda", "sc_conv_w", "attn_sink", "w_branch", "w_out", "ffn2_w13", "ffn2_w2",
             "final_norm_g"]
    allo = dict(small)
    allo.update(big)
    allo["ada_w"] = ada_out
    res = [loss, grad_x]
    for q in range(4):
        res += [allo[k][q] for k in order]
    return tuple(res)
```

```python
import functools
import math

import jax
import jax.numpy as jnp
from jax import lax
from jax.experimental import pallas as pl
from jax.experimental.pallas import tpu as pltpu

f32 = jnp.float32
bf16 = jnp.bfloat16
SDS = jax.ShapeDtypeStruct

N_DEV = 8
AXES = ("x", "y", "c")
HEAD_DIM = 128
GRID_W = 64
ATT_BLOCK = 128
HALO = 16
ROPE_BASE = 10000.0
LRU_C = 8.0
EPS = 1e-6
NEG_INF = -1e30
N_MOD = 9
N_BRANCH = 3
ADAM_LR, ADAM_B1, ADAM_B2, ADAM_EPS, ADAM_WD, ADAM_STEP = 0.001, 0.9, 0.999, 1e-08, 0.01, 10
VMEM_LIMIT = 56 * 1024 * 1024
ANY = pl.BlockSpec(memory_space=pl.ANY)


def _round_up(n, m):
    return (n + m - 1) // m * m


def _tile(n, target, align):
    best = None
    t = align
    while t <= min(n, target):
        if n % t == 0:
            best = t
        t += align
    return best if best is not None else n


def _params(sem):
    return pltpu.CompilerParams(dimension_semantics=sem, vmem_limit_bytes=VMEM_LIMIT)


_DIMS = {"nn": (((1,), (0,)), ((), ())), "nt": (((1,), (1,)), ((), ())), "tn": (((0,), (0,)), ((), ()))}


def _coords():
    return lax.axis_index("x"), lax.axis_index("y"), lax.axis_index("c")


def _peer(xyc, r):
    x, y, c = xyc
    return (1 - x if r & 4 else x, 1 - y if r & 2 else y, 1 - c if r & 1 else c)


def _index(xyc):
    return xyc[0] * 4 + xyc[1] * 2 + xyc[2]


class _Comm:
    def __init__(self, env):
        self.env, self.srcs, self.dsts, self.keys, self.items, self.fills, self.gathers = env, [], [], [], [], [], []

    def gather(self, src, key, src_view, dst_view):
        self.gathers.append((self._src(src), self._dst(key), src_view, dst_view))

    def _src(self, a):
        for q, b in enumerate(self.srcs):
            if b is a:
                return q
        self.srcs.append(a)
        return len(self.srcs) - 1

    def _dst(self, key):
        if key not in self.keys:
            self.keys.append(key)
            self.dsts.append(self.env[key])
        return self.keys.index(key)

    def add(self, src, key, src_view, dst_view):
        self.items.append((self._src(src), self._dst(key), src_view, dst_view))

    def fill(self, src, key, dst_view):
        self.fills.append((self._src(src), self._dst(key), dst_view))

    def scratch(self):
        n = len(self.items) + len(self.gathers)
        return [pltpu.SemaphoreType.DMA((7 * n,)), pltpu.SemaphoreType.DMA((7 * n,)), pltpu.SemaphoreType.DMA((n + len(self.fills),))]

    def store(self, arrays):
        for key, a in zip(self.keys, arrays):
            self.env[key] = a

    def descriptors(self, srcs, dsts, sems, me3):
        send, recv, lsem = sems
        me = _index(me3)
        local, sends, recvs = [], [], []
        for q, (si, di, sv, dv) in enumerate(self.items):
            local.append(pltpu.make_async_copy(sv(srcs[si], me, me, 0), dv(dsts[di], me, 0), lsem.at[q]))
            for r in range(1, N_DEV):
                p3 = _peer(me3, r)
                pi = _index(p3)
                kw = dict(send_sem=send.at[q * 7 + r - 1], recv_sem=recv.at[q * 7 + r - 1], device_id=p3,
                          device_id_type=pl.DeviceIdType.MESH)
                sends.append(pltpu.make_async_remote_copy(src_ref=sv(srcs[si], me, pi, r), dst_ref=dv(dsts[di], me, r), **kw))
                recvs.append(pltpu.make_async_remote_copy(src_ref=sv(srcs[si], me, pi, r), dst_ref=dv(dsts[di], pi, r), **kw))
        n1 = len(self.items)
        landed, passed = [], []
        sib = _peer(me3, 1)
        for q, (si, di, sv, dv) in enumerate(self.gathers):
            def rc(src_ref, dst_ref, k, dev, base=(n1 + q) * 7):
                return pltpu.make_async_remote_copy(src_ref=src_ref, dst_ref=dst_ref, send_sem=send.at[base + k - 1],
                                                    recv_sem=recv.at[base + k - 1], device_id=dev, device_id_type=pl.DeviceIdType.MESH)
            mine = sv(srcs[si])
            local.append(pltpu.make_async_copy(mine, dv(dsts[di], me), lsem.at[n1 + q]))
            for r in (1, 2, 4, 6):
                sends.append(rc(mine, dv(dsts[di], me), r, _peer(me3, r)))
            for r in (2, 4, 6):
                blk = dv(dsts[di], _index(_peer(me3, r)))
                landed.append(rc(mine, blk, r, _peer(me3, r)))
                passed.append(rc(blk, blk, r ^ 1, sib))
            for k in (1, 3, 5, 7):
                recvs.append(rc(mine, dv(dsts[di], _index(_peer(me3, k))), k, sib))
        for q, (si, di, dv) in enumerate(self.fills):
            local.append(pltpu.make_async_copy(srcs[si], dv(dsts[di]), lsem.at[n1 + len(self.gathers) + q]))
        return local, sends, recvs, landed, passed


def _comm_start(descs):
    for cp in descs[0] + descs[1]:
        cp.start()


def _comm_mid(descs):
    for cp in descs[3]:
        cp.wait_recv()
    for cp in descs[4]:
        cp.start()


def _comm_wait(descs):
    local, sends, recvs, _, passed = descs
    sends = sends + passed
    for cp in recvs:
        cp.wait_recv()
    for cp in sends:
        cp.wait_send()
    for cp in local:
        cp.wait()


def _carry_call(core, name, grid, in_specs, out_specs, out_shape, scratch, args, comm):
    ni, no, nsc, ng = len(in_specs), len(out_specs), len(scratch), len(grid)
    if comm is None:
        def plain(*refs):
            core(refs[:ni], refs[ni:ni + no], refs[ni + no:])
        return pl.pallas_call(plain, name=name, grid=grid, in_specs=in_specs, out_specs=out_specs, out_shape=out_shape,
                              scratch_shapes=scratch, compiler_params=_params(("parallel",) * (ng - 1) + ("arbitrary",)))(*args)
    ns, nd = len(comm.srcs), len(comm.dsts)
    o0 = ni + ns + nd

    def body(*refs):
        pid = [pl.program_id(q) for q in range(ng)]
        first, last = pid[0] == 0, pid[0] == grid[0] - 1
        for q in range(1, ng):
            first = jnp.logical_and(first, pid[q] == 0)
            last = jnp.logical_and(last, pid[q] == grid[q] - 1)
        me3 = _coords()
        mk = lambda: comm.descriptors(refs[ni:ni + ns], refs[o0 + no:o0 + no + nd], refs[o0 + no + nd + nsc:], me3)

        @pl.when(first)
        def _():
            _comm_start(mk())

        if comm.gathers:
            lin = pid[0]
            for q in range(1, ng):
                lin = lin * grid[q] + pid[q]

            @pl.when(lin == min(math.prod(grid) - 1, int(0.85 * math.prod(grid))))
            def _():
                _comm_mid(mk())

        core(refs[:ni], refs[o0:o0 + no], refs[o0 + no + nd:o0 + no + nd + nsc])

        @pl.when(last)
        def _():
            _comm_wait(mk())

    res = pl.pallas_call(
        body, name=name, grid=grid, in_specs=list(in_specs) + [ANY] * (ns + nd), out_specs=list(out_specs) + [ANY] * nd,
        out_shape=list(out_shape) + [SDS(x.shape, x.dtype) for x in comm.dsts], scratch_shapes=list(scratch) + comm.scratch(),
        input_output_aliases={ni + ns + q: no + q for q in range(nd)}, compiler_params=_params(("arbitrary",) * ng),
    )(*args, *comm.srcs, *comm.dsts)
    comm.store(res[no:])
    return res[:no]


def _mm(a, b, mode, out_dtype, name, *, M=None, N=None, K=None, a_off=0, b_off=0, b_lead=None,
        tm=768, tn=512, tk=2048, comm=None, a3=False, resid=None):
    b2 = b.shape[1:] if b_lead is not None else b.shape
    if mode == "tn":
        K = a.shape[-2] if K is None else K
        M = (a.shape[0] * a.shape[2] if a3 else a.shape[1]) if M is None else M
        N = b2[1] if N is None else N
    else:
        M = a.shape[-2] if M is None else M
        K = (a.shape[0] * a.shape[2] if a3 else a.shape[1]) if K is None else K
        N = (b2[1] if mode == "nn" else b2[0]) if N is None else N
    g = math.gcd
    if mode == "tn":
        tm = _tile(g(M, a_off) if a_off else (a.shape[2] if a3 else M), tm, 128)
        tk = _tile(K, tk, 16)
    else:
        tm = _tile(M, tm, 16)
        tk = _tile(g(K, a_off) if a_off else (a.shape[2] if a3 else K), tk, 128)
    tn = _tile(g(N, b_off) if b_off else N, tn, 128)
    nk = K // tk
    ao, bo = (a_off // (tm if mode == "tn" else tk)), b_off // tn
    lead = () if b_lead is None else (b_lead,)
    ld = () if b_lead is None else (None,)
    if mode == "nn":
        a_spec = pl.BlockSpec((tm, tk), lambda i, j, k: (i, ao + k))
        b_spec = pl.BlockSpec(ld + (tk, tn), lambda i, j, k: lead + (k, bo + j))
    elif mode == "nt":
        a_spec = pl.BlockSpec((tm, tk), lambda i, j, k: (i, ao + k))
        b_spec = pl.BlockSpec(ld + (tn, tk), lambda i, j, k: lead + (bo + j, k))
    else:
        a_spec = pl.BlockSpec((tk, tm), lambda i, j, k: (k, ao + i))
        b_spec = pl.BlockSpec(ld + (tk, tn), lambda i, j, k: lead + (k, bo + j))
    if a3:
        if mode == "nn":
            nkc = a.shape[2] // tk
            a_spec = pl.BlockSpec((None, tm, tk), lambda i, j, k: (k // nkc, i, k % nkc))
        else:
            nmc = a.shape[2] // tm
            a_spec = pl.BlockSpec((None, tk, tm), lambda i, j, k: (i // nmc, k, i % nmc))
    dims = _DIMS[mode]

    def finish(ins, outs, val):
        outs[0][...] = val.astype(out_dtype)
        if resid is not None:
            gi, coef, C = resid[2:]
            row = pl.program_id(0) * tm + lax.broadcasted_iota(jnp.int32, (tm, 1), 0)
            gate = jnp.where(row < C, ins[3][0, gi:gi + 1], ins[3][1, gi:gi + 1])
            outs[1][...] = ins[2][...] + coef * gate * val

    def core(ins, outs, scr):
        a_ref, b_ref = ins[0], ins[1]
        k = pl.program_id(2)
        if nk == 1:
            finish(ins, outs, lax.dot_general(a_ref[...], b_ref[...], dims, preferred_element_type=f32))
        else:
            acc = scr[0]

            @pl.when(k == 0)
            def _():
                acc[...] = jnp.zeros_like(acc)

            acc[...] += lax.dot_general(a_ref[...], b_ref[...], dims, preferred_element_type=f32)

            @pl.when(k == nk - 1)
            def _():
                finish(ins, outs, acc[...])

    o_spec = pl.BlockSpec((tm, tn), lambda i, j, k: (i, j))
    in_specs, out_specs, out_shape, args = [a_spec, b_spec], [o_spec], [SDS((M, N), out_dtype)], (a, b)
    if resid is not None:
        in_specs += [o_spec, pl.BlockSpec((2, N_MOD, tn), lambda i, j, k: (0, 0, j))]
        out_specs, out_shape, args = out_specs + [o_spec], out_shape + [SDS((M, N), f32)], args + tuple(resid[:2])
    res = _carry_call(core, name, (M // tm, N // tn, nk), in_specs, out_specs, out_shape,
                      [] if nk == 1 else [pltpu.VMEM((tm, tn), f32)], args, comm)
    return res[0] if resid is None else res


class _Tile:
    pass


def _rowwise(name, fn, T, C, tr, rows=(), halos=(), consts=(), mods=(), outs=(), accs=(), maccs=()):
    nT, nC = T // tr, C // tr
    assert T % tr == 0 and C % tr == 0 and nC >= 1 and tr % HALO == 0
    r8, n8 = tr // HALO, T // HALO
    seg_of = lambda i: jnp.where(i >= nC, 1, 0)
    in_specs, args = [], []
    for arr, w, cb, ro in rows:
        nt = arr.shape[0] // tr
        in_specs.append(pl.BlockSpec((tr, w), (lambda i, cb=cb, ro=ro, nt=nt: (jnp.clip(i + ro, 0, nt - 1), cb))))
        args.append(arr)
    for arr, w, cb in halos:
        in_specs.append(pl.BlockSpec((HALO, w), lambda i, cb=cb: (jnp.maximum(i * r8 - 1, 0), cb)))
        in_specs.append(pl.BlockSpec((tr, w), lambda i, cb=cb: (i, cb)))
        in_specs.append(pl.BlockSpec((HALO, w), lambda i, cb=cb: (jnp.minimum((i + 1) * r8, n8 - 1), cb)))
        args += [arr, arr, arr]
    for arr in consts:
        in_specs.append(pl.BlockSpec(arr.shape, lambda i, nd=arr.ndim: (0,) * nd))
        args.append(arr)
    for arr in mods:
        in_specs.append(pl.BlockSpec((None,) + arr.shape[1:], lambda i, nd=arr.ndim: (seg_of(i),) + (0,) * (nd - 1)))
        args.append(arr)
    out_specs, out_shape = [], []
    for w, dt in outs:
        out_specs.append(pl.BlockSpec((tr, w), lambda i: (i, 0)))
        out_shape.append(SDS((T, w), dt))
    for shp in accs:
        out_specs.append(pl.BlockSpec(shp, lambda i, nd=len(shp): (0,) * nd))
        out_shape.append(SDS(shp, f32))
    for shp in maccs:
        out_specs.append(pl.BlockSpec((None,) + shp, lambda i, nd=len(shp): (seg_of(i),) + (0,) * nd))
        out_shape.append(SDS((2,) + shp, f32))
    nr, nh, nk, nm, no, na, nma = len(rows), len(halos), len(consts), len(mods), len(outs), len(accs), len(maccs)

    def body(*refs):
        i = pl.program_id(0)
        t = _Tile()
        t.i, t.seg = i, seg_of(i)
        p = 0
        t.R = [refs[p + k][...].astype(f32) for k in range(nr)]
        p += nr
        pvalid = jnp.logical_and(i != 0, i != nC)
        nvalid = jnp.logical_and(i != nC - 1, i != nT - 1)
        t.H = []
        for k in range(nh):
            pr, cu, nx = refs[p][...].astype(f32), refs[p + 1][...].astype(f32), refs[p + 2][...].astype(f32)
            p += 3
            pr = jnp.where(pvalid, pr, jnp.zeros_like(pr))
            nx = jnp.where(nvalid, nx, jnp.zeros_like(nx))
            t.H.append(jnp.concatenate([pr, cu, nx], axis=0))
        t.K = [refs[p + k][...] for k in range(nk)]
        p += nk
        t.M = [refs[p + k][...] for k in range(nm)]
        p += nm
        o, a, ma = fn(t)
        for k in range(no):
            ref = refs[p + k]
            pieces = o[k] if isinstance(o[k], (list, tuple)) else [o[k]]
            c0 = 0
            for pc in pieces:
                ref[:, c0:c0 + pc.shape[1]] = pc.astype(ref.dtype)
                c0 += pc.shape[1]
        p += no
        for k in range(na):
            ref = refs[p + k]

            @pl.when(i == 0)
            def _(ref=ref):
                ref[...] = jnp.zeros_like(ref)

            ref[...] += a[k]
        p += na
        for k in range(nma):
            ref = refs[p + k]

            @pl.when(jnp.logical_or(i == 0, i == nC))
            def _(ref=ref):
                ref[...] = jnp.zeros_like(ref)

            ref[...] += ma[k]

    res = pl.pallas_call(
        body, name=name, grid=(nT,), in_specs=in_specs, out_specs=out_specs, out_shape=out_shape,
        compiler_params=_params(("arbitrary",)),
    )(*args)
    return res


@jax.custom_vjp
def _bdot(a, b):
    return jnp.dot(a.astype(bf16), b.astype(bf16), preferred_element_type=f32)


def _bdot_fwd(a, b):
    return _bdot(a, b), (a, b)


def _bdot_bwd(res, ct):
    a, b = res
    ctb = ct.astype(bf16)
    da = lax.dot_general(ctb, b.astype(bf16), _DIMS["nt"], preferred_element_type=f32)
    db = lax.dot_general(a.astype(bf16), ctb, _DIMS["tn"], preferred_element_type=f32)
    return da, db


_bdot.defvjp(_bdot_fwd, _bdot_bwd)


def _rms_mod(h, g, shift, scale):
    y = h * lax.rsqrt(jnp.mean(h * h, axis=-1, keepdims=True) + EPS) * g
    return y * (1.0 + scale) + shift


def _norm_mod_fwd(h, g, modv, s, d):
    def fn(t):
        m = t.M[0]
        return [_rms_mod(t.R[0], t.K[0], m[s:s + 1], m[s + 1:s + 2])], [], []
    return _rowwise("norm_mod", fn, d.T, d.C, d.tr, rows=[(h, d.D, 0, 0)], consts=[g], mods=[modv],
                    outs=[(d.D, bf16)])[0]


def _norm_mod_bwd(h, du, dh, g, modv, s, d, nxt=None):
    def fn(t):
        m = t.M[0]
        _, vjp = jax.vjp(_rms_mod, t.R[0], t.K[0], m[s:s + 1], m[s + 1:s + 2])
        dx, dg, dsh, dsc = vjp(t.R[1])
        dhn = t.R[2] + dx
        dss = jnp.concatenate([dsh, dsc], axis=0)
        if nxt is None:
            return [dhn], [dg], [dss]
        gi, coef = nxt[2:]
        return ([dhn, coef * t.M[1][gi:gi + 1] * dhn], [dg], [dss, jnp.sum(coef * dhn * t.R[3], axis=0, keepdims=True)])
    rows = [(h, d.D, 0, 0), (du, d.D, 0, 0), (dh, d.D, 0, 0)]
    if nxt is None:
        return _rowwise("norm_mod_bwd", fn, d.T, d.C, d.tr, rows=rows, consts=[g], mods=[modv], outs=[(d.D, f32)],
                        accs=[(1, d.D)], maccs=[(2, d.D)])
    dhn, do, dg, dss, dgate = _rowwise("norm_mod_bwd", fn, d.T, d.C, d.tr, rows=rows + [(nxt[0], d.D, 0, 0)], consts=[g],
                                       mods=[modv, nxt[1]], outs=[(d.D, f32), (d.D, bf16)], accs=[(1, d.D)], maccs=[(2, d.D), (1, d.D)])
    return dhn, dg, dss, (do, dgate)


def _swiglu(g, u):
    return jax.nn.silu(g) * u


def _resid_bwd(dh, o, modv, gi, coef, d):
    def fn(t):
        dhv = t.R[0]
        return [coef * t.M[0][gi:gi + 1] * dhv], [], [jnp.sum(coef * dhv * t.R[1], axis=0, keepdims=True)]
    return _rowwise("resid_bwd", fn, d.T, d.C, d.tr, rows=[(dh, d.D, 0, 0), (o, d.D, 0, 0)], mods=[modv],
                    outs=[(d.D, bf16)], maccs=[(1, d.D)])


class _Sched:
    def __init__(self, plan, env, builders):
        self.plan, self.env, self.builders = plan, env, builders

    def at(self, site):
        specs = self.plan.get(site)
        if not specs:
            return None
        cm = _Comm(self.env)
        for kind, *args in specs:
            self.builders[kind](cm, *args)
        return cm


def _ffn_up(u, w13t, d, comm):
    T, D, Fp = d.T, d.D, d.Fp
    tm, tn = _tile(T, 768, 16), _tile(Fp, 512, 128)

    def core(ins, outs, scr):
        uv, w_ref = ins[0][...], ins[1]
        gt = lax.dot_general(uv, w_ref[0], _DIMS["nt"], preferred_element_type=f32)
        up = lax.dot_general(uv, w_ref[1], _DIMS["nt"], preferred_element_type=f32)
        sig = jax.nn.sigmoid(gt)
        silu = gt * sig
        outs[0][0] = silu.astype(bf16)
        outs[0][1] = (up * (sig + silu * (1.0 - sig))).astype(bf16)
        outs[1][...] = (silu * up).astype(bf16)

    return _carry_call(core, "ffn_up", (T // tm, Fp // tn),
                       [pl.BlockSpec((tm, D), lambda i, j: (i, 0)), pl.BlockSpec((2, tn, D), lambda i, j: (0, j, 0))],
                       [pl.BlockSpec((2, tm, tn), lambda i, j: (0, i, j)), pl.BlockSpec((tm, tn), lambda i, j: (i, j))],
                       [SDS((2, T, Fp), bf16), SDS((T, Fp), bf16)], [], (u, w13t.reshape(2, Fp, D)), comm)


def _ffn_down_dx(do, w2, gu, d, comm):
    T, D, Fp = d.T, d.D, d.Fp
    tm, tn = _tile(T, 768, 16), _tile(Fp, 512, 128)

    def core(ins, outs, scr):
        w = ins[1][pl.ds(pl.multiple_of(pl.program_id(1) * tn, tn), tn), :]
        dhm = lax.dot_general(ins[0][...], w, _DIMS["nt"], preferred_element_type=f32)
        outs[0][0] = (dhm * ins[2][1].astype(f32)).astype(bf16)
        outs[0][1] = (dhm * ins[2][0].astype(f32)).astype(bf16)

    return _carry_call(core, "ffn_down_dx", (T // tm, Fp // tn),
                       [pl.BlockSpec((tm, D), lambda i, j: (i, 0)),
                        pl.BlockSpec((Fp, D), lambda i, j: (0, 0), pipeline_mode=pl.Buffered(1)),
                        pl.BlockSpec((2, tm, tn), lambda i, j: (0, i, j))],
                       [pl.BlockSpec((2, tm, tn), lambda i, j: (0, i, j))], [SDS((2, T, Fp), bf16)], [], (do, w2, gu), comm)[0]


def _ffn_fwd(h, modv, g, k13, k2, s, d, sc, site):
    u = _norm_mod_fwd(h, g, modv, s, d)
    gu, hmid = _ffn_up(u, sc.env[k13], d, sc.at(site + "up"))
    o, hn = _mm(hmid, sc.env[k2], "nn", bf16, "ffn_down", tn=512, tk=5632, comm=sc.at(site + "down"),
                resid=(h, modv, s + 2, 0.5, d.C))
    return hn, dict(h=h, u=u, gu=gu, hmid=hmid, o=o)


def _ffn_bwd(dh, sv, modv, g, k13, k2, s, d, sc, site, pre=None, nxt=None):
    w13t, w2 = sc.env[k13], sc.env[k2]
    do, dgate = pre if pre is not None else _resid_bwd(dh, sv["o"], modv, s + 2, 0.5, d)
    dgu = _ffn_down_dx(do, w2, sv["gu"], d, sc.at(site + "dx"))
    sc.env[("g",) + k13] = _mm(dgu, sv["u"], "tn", bf16, "ffn_up_dw", tm=2816, tn=1024, tk=768, comm=sc.at(site + "uw"), a3=True)
    sc.env[("g",) + k2] = _mm(sv["hmid"], do, "tn", bf16, "ffn_down_dw", tm=2816, tn=1024, tk=768, comm=sc.at(site + "dw"))
    du = _mm(dgu, w13t, "nn", f32, "ffn_up_dx", tn=1024, tk=2816, comm=sc.at(site + "ux"), a3=True)
    res = _norm_mod_bwd(sv["h"], du, dh, g, modv, s, d, nxt)
    return res[0], res[1], res[2], dgate, (res[3] if nxt is not None else None)


def _taps(ext, w, left, tr, sign):
    acc = None
    for k in range(w.shape[0]):
        o = HALO + sign * (k - left)
        term = w[k:k + 1] * ext[o:o + tr]
        acc = term if acc is None else acc + term
    return acc


def _gates(xa, wa, ba, wx, bx, lam, nb):
    bs = xa.shape[1] // nb
    out = []
    for dr in range(2):
        pa = jnp.concatenate([_bdot(xa[:, n * bs:(n + 1) * bs], wa[dr, n]) for n in range(nb)], axis=1)
        px = jnp.concatenate([_bdot(xa[:, n * bs:(n + 1) * bs], wx[dr, n]) for n in range(nb)], axis=1)
        rg = jax.nn.sigmoid(pa + ba[dr:dr + 1])
        ig = jax.nn.sigmoid(px + bx[dr:dr + 1])
        log_a = -LRU_C * rg * jax.nn.softplus(-lam[dr:dr + 1])
        a = jnp.exp(log_a)
        u = jnp.sqrt(1.0 - jnp.exp(2.0 * log_a)) * (ig * xa)
        out += [a, u]
    return out


def _combine_a(hf, hb, rg):
    return (hf + hb) * jax.nn.gelu(rg)


def _scan_order(kind, nT, nC):
    nL = nT - nC
    if kind == "F":
        return (lambda s: s), False
    if kind == "revF":
        return (lambda s: nT - 1 - s), True
    if kind == "B":
        return (lambda s: jnp.where(s < nC, nC - 1 - s, nT - 1 - (s - nC))), True
    return (lambda s: jnp.where(s < nL, nC + s, s - nL)), False


def _scan8(c, b, down):
    row = lax.broadcasted_iota(jnp.int32, c.shape, 0)
    for s in (1, 2, 4):
        sh = 8 - s if down else s
        cs, bs = pltpu.roll(c, sh, 0), pltpu.roll(b, sh, 0)
        ok = (row < 8 - s) if down else (row >= s)
        b = jnp.where(ok, c * bs + b, b)
        c = jnp.where(ok, c * cs, c)
    return c, b


def _prev8(x, x_in, down):
    row = lax.broadcasted_iota(jnp.int32, x.shape, 0)
    return jnp.where(row == (7 if down else 0), x_in, pltpu.roll(x, 7 if down else 1, 0))


def _scan_fwd(au, dr, d):
    R, tc = d.BW, d.tr
    tile_of, down = _scan_order("F" if dr == 0 else "B", d.T // tc, d.C // tc)

    def body(a_ref, u_ref, h_ref, hp_ref, st):
        @pl.when(pl.program_id(0) == 0)
        def _():
            st[...] = jnp.zeros_like(st)

        def grp(gi, h):
            r = pl.multiple_of((tc // 8 - 1 - gi if down else gi) * 8, 8)
            cc, bb = _scan8(a_ref[pl.ds(r, 8), :], u_ref[pl.ds(r, 8), :], down)
            hs = cc * h + bb
            h_ref[pl.ds(r, 8), :] = hs
            hp_ref[pl.ds(r, 8), :] = _prev8(hs, h, down)
            return hs[0:1] if down else hs[7:8]
        st[...] = lax.fori_loop(0, tc // 8, grp, st[...])

    return pl.pallas_call(
        body, name="lru_scan", grid=(d.T // tc,),
        in_specs=[pl.BlockSpec((tc, R), lambda s: (tile_of(s), 2 * dr)), pl.BlockSpec((tc, R), lambda s: (tile_of(s), 2 * dr + 1))],
        out_specs=[pl.BlockSpec((tc, R), lambda s: (tile_of(s), 0))] * 2, out_shape=[SDS((d.T, R), f32)] * 2,
        scratch_shapes=[pltpu.VMEM((1, R), f32)], compiler_params=_params(("arbitrary",)),
    )(au, au)


def _scan_bwd(dh, au, hp, dr, d):
    R, tc = d.BW, d.tr
    tile_of, down = _scan_order("revF" if dr == 0 else "revB", d.T // tc, d.C // tc)

    def body(dh_ref, a_ref, hp_ref, o_ref, st):
        @pl.when(pl.program_id(0) == 0)
        def _():
            st[...] = jnp.zeros_like(st)

        def grp(gi, carry):
            lam, an = carry
            r = pl.multiple_of((tc // 8 - 1 - gi if down else gi) * 8, 8)
            at, dt, ht = a_ref[pl.ds(r, 8), :], dh_ref[pl.ds(r, 8), :], hp_ref[pl.ds(r, 8), :]
            cc, bb = _scan8(_prev8(at, an, down), dt, down)
            lt = cc * lam + bb
            o_ref[pl.ds(r, 8), 0:R] = lt * ht
            o_ref[pl.ds(r, 8), R:2 * R] = lt
            return (lt[0:1], at[0:1]) if down else (lt[7:8], at[7:8])
        lam, an = lax.fori_loop(0, tc // 8, grp, (st[0:1], st[1:2]))
        st[0:1] = lam
        st[1:2] = an

    return pl.pallas_call(
        body, name="lru_scan_bwd", grid=(d.T // tc,),
        in_specs=[pl.BlockSpec((tc, R), lambda s: (tile_of(s), 0)), pl.BlockSpec((tc, R), lambda s: (tile_of(s), 2 * dr)),
                  pl.BlockSpec((tc, R), lambda s: (tile_of(s), 0))],
        out_specs=pl.BlockSpec((tc, 2 * R), lambda s: (tile_of(s), 0)), out_shape=SDS((d.T, 2 * R), f32),
        scratch_shapes=[pltpu.VMEM((2, R), f32)], compiler_params=_params(("arbitrary",)),
    )(dh, au, hp)


def _swap_pairs(x):
    w = x.shape[1]
    lane = lax.broadcasted_iota(jnp.int32, x.shape, 1)
    return jnp.where(lane % 64 < 32, pltpu.roll(x, w - 32, 1), pltpu.roll(x, 32, 1))


def _att_bias(G):
    rows, cols = G * ATT_BLOCK, 3 * ATT_BLOCK
    qi = lax.broadcasted_iota(jnp.int32, (rows, cols), 0) % ATT_BLOCK
    kj = lax.broadcasted_iota(jnp.int32, (rows, cols), 1)
    band = jnp.abs(kj - ATT_BLOCK - qi) <= ATT_BLOCK
    kb = kj // ATT_BLOCK
    out = [jnp.zeros((rows, cols), bool)]
    for p in (False, True):
        for n in (False, True):
            out.append(band & ((kb == 1) | ((kb == 0) & p) | ((kb == 2) & n)))
    return jnp.where(jnp.stack(out), 0.0, NEG_INF).astype(f32)


def _att_specs(d):
    G, nB, nCb = d.G, d.T // ATT_BLOCK, d.C // ATT_BLOCK
    KW, QW = d.NKV * HEAD_DIM, d.NH * HEAD_DIM
    blk = lambda f: pl.BlockSpec((ATT_BLOCK, KW), lambda b: (f(b), 0))
    three = [blk(lambda b: jnp.maximum(b - 1, 0)), blk(lambda b: b), blk(lambda b: jnp.minimum(b + 1, nB - 1))]
    ctxs = pl.BlockSpec((d.C, KW), lambda b: (0, 0))
    qs = pl.BlockSpec((ATT_BLOCK, QW), lambda b: (b, 0))
    sk = pl.BlockSpec((d.NKV, G * ATT_BLOCK, 1), lambda b: (0, 0, 0))
    variant = lambda b: jnp.where(b < nCb, 0, 1 + 2 * jnp.where(b - 1 >= nCb, 1, 0) + jnp.where(b + 1 <= nB - 1, 1, 0))
    bias = pl.BlockSpec((None, G * ATT_BLOCK, 3 * ATT_BLOCK), lambda b: (variant(b), 0, 0))
    return qs, three, ctxs, sk, bias


def _stack_heads(x, G):
    return jnp.concatenate([x[:, g * HEAD_DIM:(g + 1) * HEAD_DIM] for g in range(G)], axis=0)


def _unstack_heads(x, G):
    return jnp.concatenate([x[g * ATT_BLOCK:(g + 1) * ATT_BLOCK] for g in range(G)], axis=1)


def _att_probs(qg, kcat, kctx, sink, bias):
    scale = HEAD_DIM ** -0.5
    s = lax.dot_general(qg, kcat, _DIMS["nt"], preferred_element_type=f32) * scale + bias
    sc = lax.dot_general(qg, kctx, _DIMS["nt"], preferred_element_type=f32) * scale
    m = jnp.maximum(jnp.maximum(jnp.max(s, axis=1, keepdims=True), jnp.max(sc, axis=1, keepdims=True)), sink)
    e, ec, es = jnp.exp(s - m), jnp.exp(sc - m), jnp.exp(sink - m)
    inv = 1.0 / (jnp.sum(e, axis=1, keepdims=True) + jnp.sum(ec, axis=1, keepdims=True) + es)
    return e, ec, es, inv


def _attn_fwd(qr, kr, vb, sinkrows, d):
    G, nB, nCb = d.G, d.T // ATT_BLOCK, d.C // ATT_BLOCK
    qs, three, ctxs, sk, bs = _att_specs(d)

    def body(q_ref, k0, k1, k2, v0, v1, v2, kc_ref, vc_ref, s_ref, b_ref, o_ref):
        bias = b_ref[...]
        for kh in range(d.NKV):
            ks, qsl = slice(kh * HEAD_DIM, (kh + 1) * HEAD_DIM), slice(kh * G * HEAD_DIM, (kh + 1) * G * HEAD_DIM)
            qg = _stack_heads(q_ref[:, qsl], G)
            kcat = jnp.concatenate([k0[:, ks], k1[:, ks], k2[:, ks]], axis=0)
            vcat = jnp.concatenate([v0[:, ks], v1[:, ks], v2[:, ks]], axis=0)
            e, ec, _, inv = _att_probs(qg, kcat, kc_ref[:, ks], s_ref[kh], bias)
            p, pc = e * inv, ec * inv
            o = jnp.dot(p.astype(bf16), vcat, preferred_element_type=f32) + jnp.dot(pc.astype(bf16), vc_ref[:, ks], preferred_element_type=f32)
            o_ref[:, qsl] = _unstack_heads(o, G).astype(bf16)

    return pl.pallas_call(
        body, name="attn", grid=(nB,), in_specs=[qs] + three + three + [ctxs, ctxs, sk, bs],
        out_specs=qs, out_shape=SDS((d.T, d.NH * HEAD_DIM), bf16), compiler_params=_params(("arbitrary",)),
    )(qr, kr, kr, kr, vb, vb, vb, kr, vb, sinkrows, _att_bias(G))


def _attn_bwd(qr, kr, vb, sinkrows, dy, d):
    G, nB, nCb = d.G, d.T // ATT_BLOCK, d.C // ATT_BLOCK
    qs, three, ctxs, sk, bs = _att_specs(d)
    KW = d.NKV * HEAD_DIM
    part = pl.BlockSpec((ATT_BLOCK, KW), lambda b: (b, 0))

    def body(q_ref, k0, k1, k2, v0, v1, v2, kc_ref, vc_ref, s_ref, dy_ref, b_ref,
             dq_ref, dk0, dk1, dk2, dv0, dv1, dv2, dkc_ref, dvc_ref, ds_ref):
        scale = HEAD_DIM ** -0.5
        bias = b_ref[...]

        @pl.when(pl.program_id(0) == 0)
        def _():
            dkc_ref[...] = jnp.zeros_like(dkc_ref)
            dvc_ref[...] = jnp.zeros_like(dvc_ref)
            ds_ref[...] = jnp.zeros_like(ds_ref)

        for kh in range(d.NKV):
            ks, qsl = slice(kh * HEAD_DIM, (kh + 1) * HEAD_DIM), slice(kh * G * HEAD_DIM, (kh + 1) * G * HEAD_DIM)
            qg = _stack_heads(q_ref[:, qsl], G)
            kcat = jnp.concatenate([k0[:, ks], k1[:, ks], k2[:, ks]], axis=0)
            vcat = jnp.concatenate([v0[:, ks], v1[:, ks], v2[:, ks]], axis=0)
            kctx, vctx = kc_ref[:, ks], vc_ref[:, ks]
            e, ec, es, inv = _att_probs(qg, kcat, kctx, s_ref[kh], bias)
            p, pc, ps = e * inv, ec * inv, es * inv
            dog = _stack_heads(dy_ref[:, qsl], G).astype(bf16)
            dp = lax.dot_general(dog, vcat, _DIMS["nt"], preferred_element_type=f32)
            dpc = lax.dot_general(dog, vctx, _DIMS["nt"], preferred_element_type=f32)
            delta = jnp.sum(p * dp, axis=1, keepdims=True) + jnp.sum(pc * dpc, axis=1, keepdims=True)
            ds = (p * (dp - delta) * scale).astype(bf16)
            dsc = (pc * (dpc - delta) * scale).astype(bf16)
            dq = jnp.dot(ds, kcat, preferred_element_type=f32) + jnp.dot(dsc, kctx, preferred_element_type=f32)
            dq_ref[:, qsl] = _unstack_heads(dq, G)
            dk = lax.dot_general(ds, qg, _DIMS["tn"], preferred_element_type=f32)
            dv = lax.dot_general(p.astype(bf16), dog, _DIMS["tn"], preferred_element_type=f32)
            for j, (rk, rv) in enumerate(((dk0, dv0), (dk1, dv1), (dk2, dv2))):
                rk[:, ks] = dk[j * ATT_BLOCK:(j + 1) * ATT_BLOCK]
                rv[:, ks] = dv[j * ATT_BLOCK:(j + 1) * ATT_BLOCK]
            dkc_ref[:, ks] += lax.dot_general(dsc, qg, _DIMS["tn"], preferred_element_type=f32)
            dvc_ref[:, ks] += lax.dot_general(pc.astype(bf16), dog, _DIMS["tn"], preferred_element_type=f32)
            ds_ref[kh] += -ps * delta

    kv = SDS((d.T, KW), f32)
    return pl.pallas_call(
        body, name="attn_bwd", grid=(nB,), in_specs=[qs] + three + three + [ctxs, ctxs, sk, qs, bs],
        out_specs=[qs] + [part] * 6 + [ctxs, ctxs, sk],
        out_shape=[SDS((d.T, d.NH * HEAD_DIM), f32)] + [kv] * 6 + [SDS((d.C, KW), f32)] * 2 + [SDS((d.NKV, G * ATT_BLOCK, 1), f32)],
        compiler_params=_params(("arbitrary",)),
    )(qr, kr, kr, kr, vb, vb, vb, kr, vb, sinkrows, dy, _att_bias(G))


def _mixer_fwd(h, modv, P, l, tabs, d, sc, site):
    D, BW, T, C = d.D, d.BW, d.T, d.C
    u = _norm_mod_fwd(h, P["norm_g"][1:2], modv, 3, d)
    za = _mm(u, sc.env[("wint", l)], "nt", bf16, "in_proj_a", N=d.NA, tm=1408, comm=sc.at(site + "in_a"))
    gz = _mm(u, sc.env[("wint", l)], "nt", bf16, "in_proj_g", N=3 * D, tm=1408, b_off=d.NA, comm=sc.at(site + "in_g"))
    xa = _rowwise("lru_conv", lambda t: ([_taps(t.H[0], t.K[0], 2, d.tr, 1) + t.K[1]], [], []), T, C, d.tr,
                  halos=[(za, BW, 0)], consts=[P["rnn_conv_w"], P["rnn_conv_b"]], outs=[(BW, f32)])[0]
    lru = [P["lru_w_a"], P["lru_b_a"], P["lru_w_x"], P["lru_b_x"], P["lru_lambda"]]
    au = _rowwise("lru_gates", lambda t: ([_gates(t.R[0], *t.K, d.NB)], [], []), T, C, d.tr,
                  rows=[(xa, BW, 0, 0)], consts=lru, outs=[(4 * BW, f32)])[0]
    hf, hpf = _scan_fwd(au, 0, d)
    hb, hpb = _scan_fwd(au, 1, d)
    ya = _rowwise("lru_out", lambda t: ([_combine_a(*t.R)], [], []), T, C, d.tr,
                  rows=[(hf, BW, 0, 0), (hb, BW, 0, 0), (za, BW, 1, 0)], outs=[(BW, bf16)])[0]
    yb = _rowwise("sconv", lambda t: ([t.R[0] * _taps(t.H[0] * t.H[1], t.K[0], 1, d.tr, 1)], [], []), T, C, d.tr,
                  rows=[(za, BW, 2, 0)], halos=[(za, BW, 3), (za, BW, 4)], consts=[P["sc_conv_w"]], outs=[(BW, bf16)])[0]
    QW, KW = d.NH * HEAD_DIM, d.NKV * HEAD_DIM

    def rope(t):
        q, k, v, cs, sn = t.R
        cq, sq, ck, skn = jnp.tile(cs, (1, d.NH)), jnp.tile(sn, (1, d.NH)), jnp.tile(cs, (1, d.NKV)), jnp.tile(sn, (1, d.NKV))
        return [q * cq + _swap_pairs(q) * sq, k * ck + _swap_pairs(k) * skn, v], [], []
    kcb = (5 * BW + QW) // KW
    qr, kr, vb = _rowwise("rope", rope, T, C, d.tr,
                          rows=[(za, QW, (5 * BW) // QW, 0), (za, KW, kcb, 0), (za, KW, kcb + 1, 0),
                                (tabs["cos"], HEAD_DIM, 0, 0), (tabs["sin"], HEAD_DIM, 0, 0)],
                          outs=[(QW, bf16), (KW, bf16), (KW, bf16)])
    yatt = _attn_fwd(qr, kr, vb, P["sinkrows"], d)
    ys = (ya, yb, yatt)
    ps = [_mm(ys[i], sc.env[("wbt", l)], "nt", bf16, "lift", b_lead=i, tm=1408, tn=1024) for i in range(N_BRANCH)]

    def merge(t):
        gzv, bm = t.R[0], t.K[0]
        acc = None
        for i in range(N_BRANCH):
            term = jax.nn.sigmoid(gzv[:, i * D:(i + 1) * D] + bm[i:i + 1]) * t.R[1 + i]
            acc = term if acc is None else acc + term
        return [acc], [], []
    merged = _rowwise("merge", merge, T, C, d.trw, rows=[(gz, 3 * D, 0, 0)] + [(p, D, 0, 0) for p in ps],
                      consts=[P["b_merge"]], outs=[(D, bf16)])[0]
    y, hn = _mm(merged, sc.env[("wout", l)], "nn", bf16, "out_proj", tn=1024, comm=sc.at(site + "out"),
                resid=(h, modv, 5, 1.0, d.C))
    sv = dict(h=h, u=u, za=za, gz=gz, xa=xa, au=au, hf=hf, hpf=hpf, hb=hb, hpb=hpb, ys=ys, qr=qr, kr=kr, vb=vb,
              ps=ps, merged=merged, y=y)
    return hn, sv


def _mixer_bwd(dh, sv, modv, P, l, tabs, d, sc, site, pre=None, nxt=None):
    W = dict(wint=sc.env[("wint", l)], wbt=sc.env[("wbt", l)], wout=sc.env[("wout", l)])
    D, BW, T, C = d.D, d.BW, d.T, d.C
    QW, KW = d.NH * HEAD_DIM, d.NKV * HEAD_DIM
    za, gz = sv["za"], sv["gz"]
    dy, dgate = pre if pre is not None else _resid_bwd(dh, sv["y"], modv, 5, 1.0, d)
    dmerged = _mm(dy, W["wout"], "nt", f32, "out_proj_dx", tn=1024)
    sc.env[("g", "wout", l)] = _mm(sv["merged"], dy, "tn", bf16, "out_proj_dw", tm=1024, tn=2048, tk=768)

    def merge_bwd(t):
        gzv, bm, dm = t.R[0], t.K[0], t.R[4]
        dps, dgs, dbs = [], [], []
        for i in range(N_BRANCH):
            gate = jax.nn.sigmoid(gzv[:, i * D:(i + 1) * D] + bm[i:i + 1])
            dps.append(dm * gate)
            dgi = dm * t.R[1 + i] * gate * (1.0 - gate)
            dgs.append(dgi)
            dbs.append(jnp.sum(dgi, axis=0, keepdims=True))
        return [dps, dgs], [jnp.concatenate(dbs, axis=0)], []
    dp, dgz, dbm = _rowwise("merge_bwd", merge_bwd, T, C, d.trw,
                            rows=[(gz, 3 * D, 0, 0)] + [(p, D, 0, 0) for p in sv["ps"]] + [(dmerged, D, 0, 0)],
                            consts=[P["b_merge"]], outs=[(3 * D, bf16), (3 * D, bf16)], accs=[(N_BRANCH, D)])
    dys = [_mm(dp, W["wbt"], "nn", f32, "lift_dx", K=D, a_off=i * D, b_lead=i, tn=1024) for i in range(N_BRANCH)]
    sc.env[("g", "wbt", l)] = jnp.stack([_mm(dp, sv["ys"][i], "tn", bf16, "lift_dw", M=D, a_off=i * D, tm=1024, tn=1024, tk=768)
                                         for i in range(N_BRANCH)])
    def out_bwd(t):
        _, vjp = jax.vjp(_combine_a, t.R[1], t.R[2], t.R[3])
        dhf, _, drg = vjp(t.R[0])
        return [dhf, drg], [], []
    dhs, drg = _rowwise("lru_out_bwd", out_bwd, T, C, d.tr,
                        rows=[(dys[0], BW, 0, 0), (sv["hf"], BW, 0, 0), (sv["hb"], BW, 0, 0), (za, BW, 1, 0)],
                        outs=[(BW, f32), (BW, bf16)])
    dau0 = _scan_bwd(dhs, sv["au"], sv["hpf"], 0, d)
    dau1 = _scan_bwd(dhs, sv["au"], sv["hpb"], 1, d)
    lru = [P["lru_w_a"], P["lru_b_a"], P["lru_w_x"], P["lru_b_x"], P["lru_lambda"]]

    def gates_bwd(t):
        _, vjp = jax.vjp(lambda xa, *k: _gates(xa, *k, d.NB), t.R[0], *t.K)
        d0, d1 = t.R[1], t.R[2]
        g = vjp([d0[:, :BW], d0[:, BW:], d1[:, :BW], d1[:, BW:]])
        return [g[0]], list(g[1:]), []
    dxa, dwa, dba, dwx, dbx, dlam = _rowwise(
        "lru_gates_bwd", gates_bwd, T, C, d.tr, rows=[(sv["xa"], BW, 0, 0), (dau0, 2 * BW, 0, 0), (dau1, 2 * BW, 0, 0)],
        consts=lru, outs=[(BW, f32)], accs=[p.shape for p in lru])
    sc.env[("gs", "lru_w_a", l)], sc.env[("gs", "lru_w_x", l)] = dwa, dwx

    def conv_bwd(t):
        dxe, xe, w = t.H[0], t.H[1], t.K[0]
        cur = dxe[HALO:HALO + d.tr]
        dw = jnp.concatenate([jnp.sum(cur * xe[HALO + k - 2:HALO + k - 2 + d.tr], axis=0, keepdims=True) for k in range(w.shape[0])], axis=0)
        return [_taps(dxe, w, 2, d.tr, -1)], [dw, jnp.sum(cur, axis=0, keepdims=True)], []
    drx, dcw, dcb = _rowwise("lru_conv_bwd", conv_bwd, T, C, d.tr, halos=[(dxa, BW, 0), (za, BW, 0)],
                             consts=[P["rnn_conv_w"]], outs=[(BW, bf16)], accs=[P["rnn_conv_w"].shape, (1, BW)])
    def sconv_bwd(t):
        scg, sx, sb, dyb = t.H
        w, tr = t.K[0], d.tr
        me = scg * sx
        dsb = dyb[HALO:HALO + tr] * _taps(me, w, 1, tr, 1)
        dce = dyb * sb
        dm = _taps(dce, w, 1, tr, -1)
        cur = dce[HALO:HALO + tr]
        dw = jnp.concatenate([jnp.sum(cur * me[HALO + k - 1:HALO + k - 1 + tr], axis=0, keepdims=True) for k in range(w.shape[0])], axis=0)
        return [[dsb, dm * sx[HALO:HALO + tr], dm * scg[HALO:HALO + tr]]], [dw], []
    dsc, dscw = _rowwise("sconv_bwd", sconv_bwd, T, C, d.tr, halos=[(za, BW, 3), (za, BW, 4), (za, BW, 2), (dys[1], BW, 0)],
                         consts=[P["sc_conv_w"]], outs=[(3 * BW, bf16)], accs=[P["sc_conv_w"].shape])
    dqr, dk0, dk1, dk2, dv0, dv1, dv2, dkc, dvc, dsink = _attn_bwd(sv["qr"], sv["kr"], sv["vb"], P["sinkrows"], dys[2], d)
    nB, nCb = T // ATT_BLOCK, C // ATT_BLOCK

    def att_join(t):
        dq, a1, a0, a2, b1, b0, b2, kc, vc, cs, sn = t.R
        cq, sq, ck, skn = jnp.tile(cs, (1, d.NH)), jnp.tile(sn, (1, d.NH)), jnp.tile(cs, (1, d.NKV)), jnp.tile(sn, (1, d.NKV))
        up = t.i + 1 <= nB - 1
        dn = t.i >= 1
        isc = t.seg == 0
        dk = a1 + jnp.where(up, a0, 0.0) + jnp.where(dn, a2, 0.0) + jnp.where(isc, kc, 0.0)
        dv = b1 + jnp.where(up, b0, 0.0) + jnp.where(dn, b2, 0.0) + jnp.where(isc, vc, 0.0)
        return [[dq * cq + _swap_pairs(dq * sq), dk * ck + _swap_pairs(dk * skn), dv]], [], []
    dqkv = _rowwise("attn_join", att_join, T, C, ATT_BLOCK,
                    rows=[(dqr, QW, 0, 0), (dk1, KW, 0, 0), (dk0, KW, 0, 1), (dk2, KW, 0, -1), (dv1, KW, 0, 0), (dv0, KW, 0, 1),
                          (dv2, KW, 0, -1), (dkc, KW, 0, 0), (dvc, KW, 0, 0), (tabs["cos"], HEAD_DIM, 0, 0),
                          (tabs["sin"], HEAD_DIM, 0, 0)], outs=[(QW + 2 * KW, bf16)])[0]
    dz = jnp.concatenate([drx, drg, dsc, dqkv, dgz], axis=1)
    du = _mm(dz, W["wint"], "nn", f32, "in_proj_dx", tn=1024, tk=3200, comm=sc.at(site + "in_x"))
    sc.env[("g", "wint", l)] = _mm(dz, sv["u"], "tn", bf16, "in_proj_dw", tm=2560, tn=1024, tk=768, comm=sc.at(site + "in_w"))
    res = _norm_mod_bwd(sv["h"], du, dh, P["norm_g"][1:2], modv, 3, d, nxt)
    grads = dict(norm_g1=res[1], dss=res[2], dgate=dgate, b_merge=dbm, rnn_conv_w=dcw, rnn_conv_b=dcb,
                 lru_b_a=dba, lru_b_x=dbx, lru_lambda=dlam, sc_conv_w=dscw, sinkrows=dsink)
    return res[0], grads, (res[3] if nxt is not None else None)


def _exchange(name, comm):
    ns, nd = len(comm.srcs), len(comm.dsts)

    def body(*refs):
        descs = comm.descriptors(refs[:ns], refs[ns + nd:ns + 2 * nd], refs[ns + 2 * nd:], _coords())
        _comm_start(descs)
        _comm_mid(descs)
        _comm_wait(descs)

    res = pl.pallas_call(
        body, name=name, in_specs=[ANY] * (ns + nd), out_specs=[ANY] * nd, out_shape=[SDS(a.shape, a.dtype) for a in comm.dsts],
        input_output_aliases={ns + q: q for q in range(nd)}, scratch_shapes=comm.scratch(),
    )(*comm.srcs, *comm.dsts)
    comm.store(res)


def _allgather(name, a):
    env = {"g": lax.empty((N_DEV,) + a.shape, a.dtype)}
    cm = _Comm(env)
    cm.add(a, "g", lambda ref, me, pi, r: ref, lambda ref, sender, r: ref.at[sender])
    _exchange(name, cm)
    return env["g"]


def _sum8(x):
    n = x.shape[1]
    tn = _tile(n, 2048, 8)

    def body(x_ref, o_ref):
        acc = x_ref[0]
        for j in range(1, N_DEV):
            acc = acc + x_ref[j]
        o_ref[...] = acc
    return pl.pallas_call(body, name="sum8", grid=(n // tn,), in_specs=[pl.BlockSpec((N_DEV, tn, 128), lambda i: (0, i, 0))],
                          out_specs=pl.BlockSpec((tn, 128), lambda i: (i, 0)), out_shape=SDS((n, 128), f32),
                          compiler_params=_params(("parallel",)))(x)


def _cast_direct(w):
    ly, R, Cn = w.shape
    tc = _tile(Cn, 512, 128)
    spec = pl.BlockSpec((None, R, tc), lambda l, j: (l, 0, j))

    def body(w_ref, o_ref):
        o_ref[...] = w_ref[...].astype(bf16)
    return pl.pallas_call(body, name="cast", grid=(ly, Cn // tc), in_specs=[spec], out_specs=spec, out_shape=SDS(w.shape, bf16),
                          compiler_params=_params(("parallel", "parallel")))(w)


def _cast_transposed(w):
    four = w.ndim == 4
    ly, I, (K, Nl) = w.shape[0], (w.shape[1] if four else 1), w.shape[-2:]
    tk = _tile(K, 256, 128)
    mid = (lambda i: (i,)) if four else (lambda i: ())
    nn = (None,) * (w.ndim - 2)

    def body(w_ref, o_ref):
        o_ref[...] = w_ref[...].T.astype(bf16)
    return pl.pallas_call(
        body, name="cast_t", grid=(ly, I, K // tk), in_specs=[pl.BlockSpec(nn + (tk, Nl), lambda l, i, k: (l,) + mid(i) + (k, 0))],
        out_specs=pl.BlockSpec(nn + (Nl, tk), lambda l, i, k: (l,) + mid(i) + (0, k)), out_shape=SDS(w.shape[:-2] + (Nl, K), bf16),
        compiler_params=_params(("parallel", "parallel", "parallel")))(w)


def _adam(g, w, m, v):
    m = ADAM_B1 * m + (1.0 - ADAM_B1) * g
    v = ADAM_B2 * v + (1.0 - ADAM_B2) * (g * g)
    m_hat = m / (1.0 - ADAM_B1 ** ADAM_STEP)
    v_hat = v / (1.0 - ADAM_B2 ** ADAM_STEP)
    delta = -ADAM_LR * (m_hat / (jnp.sqrt(v_hat) + ADAM_EPS) + ADAM_WD * w)
    return delta, m, v


def _slot_sum(x):
    acc = x[0].astype(f32)
    for r in range(1, x.shape[0]):
        acc = acc + x[r].astype(f32)
    return acc


def _adamw_direct(g, w, m, v, slots):
    ly, R, Cn = w.shape
    tr = R if R <= 1024 else _tile(R, 512, 16)
    tc = _tile(Cn, max(128, (256 * 1024) // tr // 128 * 128), 128)
    spec = pl.BlockSpec((None, tr, tc), lambda l, i, j: (l, i, j))
    gspec = pl.BlockSpec((None, N_DEV, tr, tc), lambda l, i, j: (l, 0, i, j)) if slots else spec

    def body(g_ref, w_ref, m_ref, v_ref, go, do, mo, vo):
        gv = _slot_sum(g_ref[...]) if slots else g_ref[...]
        go[...] = gv
        do[...], mo[...], vo[...] = _adam(gv, w_ref[...], m_ref[...], v_ref[...])
    return pl.pallas_call(body, name="adamw", grid=(ly, R // tr, Cn // tc), in_specs=[gspec, spec, spec, spec], out_specs=[spec] * 4,
                          out_shape=[SDS(w.shape, f32)] * 4, compiler_params=_params(("parallel",) * 3))(g, w, m, v)


def _adamw_transposed(g, w, m, v):
    four = w.ndim == 4
    ly, I, (K, Nl) = w.shape[0], (w.shape[1] if four else 1), w.shape[-2:]
    tk = _tile(K, 128, 128)
    mid = (lambda i: (i,)) if four else (lambda i: ())
    nn = (None,) * (w.ndim - 3)
    spec = pl.BlockSpec((None,) + nn + (tk, Nl), lambda l, i, k: (l,) + mid(i) + (k, 0))
    gspec = pl.BlockSpec((None, N_DEV) + nn + (Nl, tk), lambda l, i, k: (l, 0) + mid(i) + (0, k))

    def body(g_ref, w_ref, m_ref, v_ref, go, do, mo, vo):
        gv = _slot_sum(g_ref[...]).T
        go[...] = gv
        do[...], mo[...], vo[...] = _adam(gv, w_ref[...], m_ref[...], v_ref[...])
    return pl.pallas_call(body, name="adamw_t", grid=(ly, I, K // tk), in_specs=[gspec, spec, spec, spec], out_specs=[spec] * 4,
                          out_shape=[SDS(w.shape, f32)] * 4, compiler_params=_params(("parallel",) * 3))(g, w, m, v)


def _ada_fwd(cvec, ada_w, ada_b_cols):
    ly, D, cols = ada_w.shape
    tn = _tile(cols, 768, 128)

    def body(c_ref, w_ref, b_ref, o_ref):
        o_ref[...] = _bdot(jax.nn.silu(c_ref[...]), w_ref[...]) + b_ref[...]
    return pl.pallas_call(
        body, name="ada", grid=(ly, cols // tn),
        in_specs=[pl.BlockSpec((16, D), lambda l, j: (0, 0)), pl.BlockSpec((None, D, tn), lambda l, j: (l, 0, j)),
                  pl.BlockSpec((None, 1, tn), lambda l, j: (l, 0, j))],
        out_specs=pl.BlockSpec((None, 16, tn), lambda l, j: (l, 0, j)), out_shape=SDS((ly, 16, cols), f32),
        compiler_params=_params(("parallel", "parallel")))(cvec, ada_w, ada_b_cols)


def _ada_bwd(cvec, ada_w, dm):
    ly, D, cols = ada_w.shape
    tn = _tile(cols, 768, 128)

    def body(c_ref, w_ref, d_ref, gw_ref, gc_ref):
        first = jnp.logical_and(pl.program_id(0) == 0, pl.program_id(1) == 0)

        @pl.when(first)
        def _():
            gc_ref[...] = jnp.zeros_like(gc_ref)

        def f(cv, w):
            return _bdot(jax.nn.silu(cv), w)
        _, vjp = jax.vjp(f, c_ref[...], w_ref[...])
        dc, dw = vjp(d_ref[...])
        gw_ref[...] = dw
        gc_ref[...] += dc
    return pl.pallas_call(
        body, name="ada_bwd", grid=(ly, cols // tn),
        in_specs=[pl.BlockSpec((16, D), lambda l, j: (0, 0)), pl.BlockSpec((None, D, tn), lambda l, j: (l, 0, j)),
                  pl.BlockSpec((None, 16, tn), lambda l, j: (l, 0, j))],
        out_specs=[pl.BlockSpec((None, D, tn), lambda l, j: (l, 0, j)), pl.BlockSpec((16, D), lambda l, j: (0, 0))],
        out_shape=[SDS(ada_w.shape, f32), SDS((16, D), f32)], compiler_params=_params(("arbitrary", "arbitrary")))(cvec, ada_w, dm)


def _flat_adamw(g, w, m, v):
    n = w.shape[0]
    tn = _tile(n, 1024, 8)
    spec = pl.BlockSpec((tn, 128), lambda i: (i, 0))

    def body(g_ref, w_ref, m_ref, v_ref, do, mo, vo):
        do[...], mo[...], vo[...] = _adam(g_ref[...], w_ref[...], m_ref[...], v_ref[...])
    return pl.pallas_call(body, name="adamw_small", grid=(n // tn,), in_specs=[spec] * 4, out_specs=[spec] * 3,
                          out_shape=[SDS(w.shape, f32)] * 3, compiler_params=_params(("parallel",)))(g, w, m, v)


def _pack(arrs):
    flat = jnp.concatenate([a.reshape(-1).astype(f32) for a in arrs])
    n = _round_up(flat.shape[0], 512 * 128)
    return jnp.pad(flat, (0, n - flat.shape[0])).reshape(n // 128, 128)


def _unpack(flat, shapes, lead=()):
    flat = flat.reshape(lead + (-1,))
    out, off = [], 0
    for s in shapes:
        sz = math.prod(s)
        out.append(flat[..., off:off + sz].reshape(lead + tuple(s)))
        off += sz
    return out


def _unshard_last(g):
    g = jnp.moveaxis(g, 0, -2)
    return g.reshape(g.shape[:-2] + (g.shape[-2] * g.shape[-1],))


class _Dims:
    pass


def _rope_tables(L, C, NH):
    rows = L // GRID_W
    row = jnp.repeat(jnp.arange(rows), GRID_W).astype(f32)
    col = jnp.tile(jnp.arange(GRID_W), rows).astype(f32)
    half = HEAD_DIM // 2
    inv = ROPE_BASE ** (-jnp.arange(0, half, 2, dtype=f32) / half)
    ar, ac = row[:, None] * inv, col[:, None] * inv
    cos = jnp.concatenate([jnp.cos(ar), jnp.cos(ar), jnp.cos(ac), jnp.cos(ac)], axis=-1)
    sin = jnp.concatenate([-jnp.sin(ar), jnp.sin(ar), -jnp.sin(ac), jnp.sin(ac)], axis=-1)
    cos = jnp.concatenate([jnp.ones((C, HEAD_DIM), f32), cos], axis=0)
    sin = jnp.concatenate([jnp.zeros((C, HEAD_DIM), f32), sin], axis=0)
    return dict(cos=cos, sin=sin)


def kernel(x, c, ctx, c_ctx, ada_w, ada_b, norm_g, ffn1_w13, ffn1_w2, w_in, b_merge, rnn_conv_w, rnn_conv_b, lru_w_a, lru_b_a, lru_w_x, lru_b_x, lru_lambda, sc_conv_w, attn_sink, w_branch, w_out, ffn2_w13, ffn2_w2, final_norm_g, loss_target, m_c_ctx, m_ada_w, m_ada_b, m_norm_g, m_ffn1_w13, m_ffn1_w2, m_w_in, m_b_merge, m_rnn_conv_w, m_rnn_conv_b, m_lru_w_a, m_lru_b_a, m_lru_w_x, m_lru_b_x, m_lru_lambda, m_sc_conv_w, m_attn_sink, m_w_branch, m_w_out, m_ffn2_w13, m_ffn2_w2, m_final_norm_g, v_c_ctx, v_ada_w, v_ada_b, v_norm_g, v_ffn1_w13, v_ffn1_w2, v_w_in, v_b_merge, v_rnn_conv_w, v_rnn_conv_b, v_lru_w_a, v_lru_b_a, v_lru_w_x, v_lru_b_x, v_lru_lambda, v_sc_conv_w, v_attn_sink, v_w_branch, v_w_out, v_ffn2_w13, v_ffn2_w2, v_final_norm_g):
    d = _Dims()
    L, D = x.shape[1], x.shape[2]
    C = ctx.shape[1]
    LY = ada_w.shape[0]
    d.D, d.C, d.T = D, C, C + L
    d.F = ffn1_w2.shape[1] * N_DEV
    d.Fp = _round_up(d.F, 512)
    d.IN = w_in.shape[2] * N_DEV
    d.BW = w_branch.shape[2]
    d.NH = attn_sink.shape[1]
    d.NA = d.IN - N_BRANCH * D
    d.NKV = (d.NA - 5 * d.BW - d.NH * HEAD_DIM) // (2 * HEAD_DIM)
    d.G = d.NH // d.NKV
    d.NB = lru_w_a.shape[2]
    d.tr = _tile(C, 256, 8)
    d.trw = _tile(C, 128, 8)
    assert d.NH * HEAD_DIM == d.BW and L % GRID_W == 0 and C % ATT_BLOCK == 0 and L % ATT_BLOCK == 0
    sh13, sh2, shin, shd = 2 * d.F // N_DEV, d.F // N_DEV, d.IN // N_DEV, D // N_DEV
    cols9 = N_MOD * D // N_DEV
    me3 = _coords()
    me = _index(me3)

    sharded = [norm_g, b_merge, rnn_conv_w, lru_b_a, lru_b_x, lru_lambda, sc_conv_w]
    shapes1 = [a.shape for a in sharded] + [(D,)]
    g1 = _allgather("gather_small", _pack(sharded + [c.reshape(-1)]))
    parts = _unpack(g1, shapes1, (N_DEV,))
    norm_g_f, b_merge_f, conv_w_f, lru_b_a_f, lru_b_x_f, lru_lam_f, sc_w_f = [_unshard_last(p) for p in parts[:-1]]
    cvec = jnp.concatenate([parts[-1], c_ctx[None], jnp.zeros((7, D), f32)], axis=0)
    ada_b_cols = lax.dynamic_slice_in_dim(ada_b, me * cols9, cols9, axis=1)[:, None, :]
    modcols = _ada_fwd(cvec, ada_w, ada_b_cols)
    g2 = _allgather("gather_mod", modcols)
    modall = jnp.moveaxis(g2, 0, 2).reshape(LY, 16, N_MOD, D)
    mod_lat = lax.dynamic_index_in_dim(modall, me, axis=1, keepdims=False)
    modv = jnp.stack([modall[:, N_DEV], mod_lat], axis=1)

    assert LY == 2
    off13 = lambda j: (j // 4) * d.Fp + (j % 4) * sh13
    tr_ = lambda a: jnp.swapaxes(a, 1, 2)
    cast_t = lambda w: _cast_direct(tr_(w))
    c13a, c13b = cast_t(ffn1_w13), cast_t(ffn2_w13)
    wt = {"w13a": (c13a, off13, sh13, 2 * d.Fp, (D,), False), "w13b": (c13b, off13, sh13, 2 * d.Fp, (D,), False),
          "w2a": (_cast_direct(ffn1_w2), lambda j: j * sh2, sh2, d.Fp, (D,), False),
          "w2b": (_cast_direct(ffn2_w2), lambda j: j * sh2, sh2, d.Fp, (D,), False),
          "wint": (cast_t(w_in), lambda j: j * shin, shin, d.IN, (D,), False),
          "wbt": (_cast_transposed(w_branch), lambda j: j * shd, shd, D, (d.BW,), True),
          "wout": (_cast_direct(w_out), lambda j: j * shd, shd, D, (D,), False)}
    env = {}
    for name, (cw, off, sh, tot, rest, three) in wt.items():
        for l in range(LY):
            env[(name, l)] = lax.empty(((N_BRANCH, tot) if three else (tot,)) + rest, bf16)
        env[("r", name)] = lax.empty((LY, N_DEV) + ((N_BRANCH, sh) if three else (sh,)) + rest, bf16)
    zpad = jnp.zeros((max(d.Fp - d.F, 16), D), bf16)

    def part(sh, h):
        return (0, sh) if h is None else (h * (sh // 2), sh // 2)

    def rows_of(ref, start, n, three):
        sl = pl.ds(start if isinstance(start, int) else pl.multiple_of(start, 16), n)
        return ref.at[:, sl] if three else ref.at[sl]

    def add_gather(cm, name, l, h=None):
        cw, off, sh, tot, rest, three = wt[name]
        r0, n = part(sh, h)
        cm.gather(cw, (name, l), lambda ref: rows_of(ref.at[l], r0, n, three),
                  lambda ref, origin: rows_of(ref, off(origin) + r0, n, three))
        if d.Fp > d.F and h in (None, 0) and name[:2] in ("w1", "w2"):
            for base in ((0, d.Fp) if name[:3] == "w13" else (0,)):
                cm.fill(zpad, (name, l), lambda ref, base=base: ref.at[pl.ds(base + d.F, d.Fp - d.F)])

    def add_rs(cm, name, l, h=None):
        cw, off, sh, tot, rest, three = wt[name]
        r0, n = part(sh, h)
        cm.add(env[("g", name, l)], ("r", name), lambda ref, me_, pi, r: rows_of(ref, off(pi) + r0, n, three),
               lambda ref, sender, r: rows_of(ref.at[l, r], r0, n, three))

    plan = {"F0b.up": [("ag", "w2b", 0), ("ag", "w13a", 1, 0)], "F0b.down": [("ag", "w13a", 1, 1)], "F1b.up": [("ag", "w2b", 1)],
            "B0m.in_x": [("lru",)]}
    for l in range(LY):
        plan.update({f"F{l}a.up": [("ag", "w2a", l), ("ag", "wint", l, 0)], f"F{l}a.down": [("ag", "wint", l, 1), ("ag", "wbt", l)],
                     f"F{l}m.in_a": [("ag", "wout", l), ("ag", "w13b", l, 0)], f"F{l}m.in_g": [("ag", "w13b", l, 1)]})
    for l in range(LY):
        plan.update({f"B{l}b.dw": [("rs", "w13b", l, 0)], f"B{l}b.ux": [("rs", "w13b", l, 1)],
                     f"B{l}m.in_w": [("rs", "wout", l), ("rs", "wbt", l), ("rs", "w2b", l)], f"B{l}a.dx": [("rs", "wint", l, 0)],
                     f"B{l}a.uw": [("rs", "wint", l, 1)], f"B{l}a.dw": [("rs", "w13a", l, 0)]})
    plan.update({"B1a.ux": [("rs", "w13a", 1, 1)], "B0b.dx": [("rs", "w2a", 1)], "B0a.ux": [("rs", "w13a", 0, 1), ("rs", "w2a", 0)]})

    def add_lru(cm):
        for nm in ("lru_w_a", "lru_w_x"):
            gs = jnp.stack([env[("gs", nm, l)] for l in range(LY)]).reshape(-1, 128)
            env[("all", nm)] = lax.empty((N_DEV,) + gs.shape, f32)
            cm.add(gs, ("all", nm), lambda ref, me_, pi, r: ref, lambda ref, sender, r: ref.at[sender])
    sc = _Sched(plan, env, {"ag": add_gather, "rs": add_rs, "lru": add_lru})
    first = _Comm(env)
    add_gather(first, "w13a", 0)
    _exchange("gather_first", first)

    tabs = _rope_tables(L, C, d.NH)
    h = jnp.concatenate([ctx[0], x[0]], axis=0)
    Ps, saves = [], []
    for l in range(LY):
        sinkrows = jnp.repeat(attn_sink[l].reshape(d.NKV, d.G), ATT_BLOCK, axis=1)[:, :, None]
        Ps.append(dict(norm_g=norm_g_f[l], b_merge=b_merge_f[l], rnn_conv_w=conv_w_f[l], rnn_conv_b=rnn_conv_b[l][None],
                       lru_w_a=lru_w_a[l], lru_b_a=lru_b_a_f[l], lru_w_x=lru_w_x[l], lru_b_x=lru_b_x_f[l], lru_lambda=lru_lam_f[l],
                       sc_conv_w=sc_w_f[l], sinkrows=sinkrows))
    for l in range(LY):
        P = Ps[l]
        h, s1 = _ffn_fwd(h, modv[l], P["norm_g"][0:1], ("w13a", l), ("w2a", l), 0, d, sc, f"F{l}a.")
        h, s2 = _mixer_fwd(h, modv[l], P, l, tabs, d, sc, f"F{l}m.")
        h, s3 = _ffn_fwd(h, modv[l], P["norm_g"][2:3], ("w13b", l), ("w2b", l), 6, d, sc, f"F{l}b.")
        saves.append((s1, s2, s3))

    fng = final_norm_g[None]

    def loss_fn(t):
        def f(hv, g):
            y = hv * lax.rsqrt(jnp.mean(hv * hv, axis=-1, keepdims=True) + EPS) * g
            e = y - t.R[1]
            return 0.5 * jnp.sum(jnp.mean(e * e, axis=-1))
        lat = (t.seg == 1).astype(f32)
        val, (dhv, dg) = jax.value_and_grad(f, argnums=(0, 1))(t.R[0], t.K[0])
        return [dhv * lat], [dg * lat, jnp.full((1, 128), val * lat, f32)], []
    dh, dfng, lossv = _rowwise("loss", loss_fn, d.T, C, d.tr, rows=[(h, D, 0, 0), (loss_target[0], D, 0, -(C // d.tr))],
                               consts=[fng], outs=[(D, f32)], accs=[(1, D), (1, 128)])
    loss = lax.psum(lossv[0, 0], AXES)

    gl = [None] * LY
    pre = None
    for l in reversed(range(LY)):
        P = Ps[l]
        s1, s2, s3 = saves[l]
        dh, dg2, dss2, dgt2, pre = _ffn_bwd(dh, s3, modv[l], P["norm_g"][2:3], ("w13b", l), ("w2b", l), 6, d, sc, f"B{l}b.",
                                            pre, (s2["y"], modv[l], 5, 1.0))
        dh, gm, pre = _mixer_bwd(dh, s2, modv[l], P, l, tabs, d, sc, f"B{l}m.", pre, (s1["o"], modv[l], 2, 0.5))
        below = (saves[l - 1][2]["o"], modv[l - 1], 8, 0.5) if l > 0 else None
        dh, dg0, dss0, dgt0, pre = _ffn_bwd(dh, s1, modv[l], P["norm_g"][0:1], ("w13a", l), ("w2a", l), 0, d, sc, f"B{l}a.",
                                            pre, below)
        gm.update(norm_g=jnp.concatenate([dg0, gm["norm_g1"], dg2], axis=0),
                  dmod=jnp.concatenate([dss0, dgt0, gm["dss"], gm["dgate"], dss2, dgt2], axis=1))
        gl[l] = gm
    grad_x = dh[C:][None]
    st = lambda k: jnp.stack([g[k] for g in gl])

    r13a, r2a, rin, rwb, rwo, r13b, r2b = [env[("r", k)] for k in ("w13a", "w2a", "wint", "wbt", "wout", "w13b", "w2b")]
    big = {}
    adamw_tr = lambda r, w, m, v: [tr_(o) for o in _adamw_direct(r, tr_(w), tr_(m), tr_(v), True)]
    big["ffn1_w13"] = adamw_tr(r13a, ffn1_w13, m_ffn1_w13, v_ffn1_w13)
    big["ffn2_w13"] = adamw_tr(r13b, ffn2_w13, m_ffn2_w13, v_ffn2_w13)
    big["w_in"] = adamw_tr(rin, w_in, m_w_in, v_w_in)
    big["w_branch"] = _adamw_transposed(rwb, w_branch, m_w_branch, v_w_branch)
    big["ffn1_w2"] = _adamw_direct(r2a, ffn1_w2, m_ffn1_w2, v_ffn1_w2, True)
    big["ffn2_w2"] = _adamw_direct(r2b, ffn2_w2, m_ffn2_w2, v_ffn2_w2, True)
    big["w_out"] = _adamw_direct(rwo, w_out, m_w_out, v_w_out, True)

    sink_g = jnp.stack([jnp.sum(g["sinkrows"].reshape(d.NKV, d.G, ATT_BLOCK), axis=-1).reshape(d.NH) for g in gl])
    small_full = dict(norm_g=st("norm_g"), b_merge=st("b_merge"), rnn_conv_w=st("rnn_conv_w"), rnn_conv_b=st("rnn_conv_b")[:, 0],
                      lru_b_a=st("lru_b_a"), lru_b_x=st("lru_b_x"), lru_lambda=st("lru_lambda"), sc_conv_w=st("sc_conv_w"), attn_sink=sink_g, final_norm_g=dfng[0])
    names_s = list(small_full)
    dmod = st("dmod")
    pk = _pack([small_full[k] for k in names_s] + [dmod])
    g4 = _allgather("gather_small_grads", pk)
    tot = _unpack(_sum8(g4), [small_full[k].shape for k in names_s] + [dmod.shape])
    sums = dict(zip(names_s, tot[:-1]))
    dmod_sum = tot[-1].reshape(LY, 2, N_MOD * D)
    dmod_all = _unpack(g4, [small_full[k].shape for k in names_s] + [dmod.shape], (N_DEV,))[-1].reshape(N_DEV, LY, 2, N_MOD * D)
    dm_rows = jnp.concatenate([jnp.moveaxis(dmod_all[:, :, 1], 0, 1), dmod_sum[:, 0:1], jnp.zeros((LY, 7, N_MOD * D), f32)], axis=1)
    dm_cols = lax.dynamic_slice_in_dim(dm_rows, me * cols9, cols9, axis=2)
    g_ada_w, dcv = _ada_bwd(cvec, ada_w, dm_cols)
    g5 = _allgather("gather_cctx", _pack([dcv[N_DEV]]))
    g_c_ctx = _sum8(g5).reshape(-1)[:D]
    g_ada_b = dmod_sum[:, 0] + dmod_sum[:, 1]
    ada_out = _adamw_direct(g_ada_w, ada_w, m_ada_w, v_ada_w, False)

    def shard_last(a, n):
        return lax.dynamic_slice_in_dim(a, me * n, n, axis=a.ndim - 1)
    local_g = dict(c_ctx=g_c_ctx, ada_b=g_ada_b, norm_g=shard_last(sums["norm_g"], shd), b_merge=shard_last(sums["b_merge"], shd),
                   rnn_conv_w=shard_last(sums["rnn_conv_w"], d.BW // N_DEV), rnn_conv_b=sums["rnn_conv_b"],
                   lru_b_a=shard_last(sums["lru_b_a"], d.BW // N_DEV), lru_b_x=shard_last(sums["lru_b_x"], d.BW // N_DEV),
                   lru_lambda=shard_last(sums["lru_lambda"], d.BW // N_DEV), sc_conv_w=shard_last(sums["sc_conv_w"], d.BW // N_DEV),
                   attn_sink=sums["attn_sink"], final_norm_g=sums["final_norm_g"])
    wmv = dict(c_ctx=(c_ctx, m_c_ctx, v_c_ctx), ada_b=(ada_b, m_ada_b, v_ada_b), norm_g=(norm_g, m_norm_g, v_norm_g),
               b_merge=(b_merge, m_b_merge, v_b_merge), rnn_conv_w=(rnn_conv_w, m_rnn_conv_w, v_rnn_conv_w),
               rnn_conv_b=(rnn_conv_b, m_rnn_conv_b, v_rnn_conv_b),
               lru_b_a=(lru_b_a, m_lru_b_a, v_lru_b_a), lru_b_x=(lru_b_x, m_lru_b_x, v_lru_b_x),
               lru_lambda=(lru_lambda, m_lru_lambda, v_lru_lambda), sc_conv_w=(sc_conv_w, m_sc_conv_w, v_sc_conv_w),
               attn_sink=(attn_sink, m_attn_sink, v_attn_sink), final_norm_g=(final_norm_g, m_final_norm_g, v_final_norm_g))
    names_l = list(local_g)
    shapes_l = [wmv[k][0].shape for k in names_l]
    gp = _pack([local_g[k].reshape(wmv[k][0].shape) for k in names_l])
    outs_s = _flat_adamw(gp, _pack([wmv[k][0] for k in names_l]), _pack([wmv[k][1] for k in names_l]), _pack([wmv[k][2] for k in names_l]))
    small = {k: [local_g[k].reshape(wmv[k][0].shape)] for k in names_l}
    for o in outs_s:
        for k, a in zip(names_l, _unpack(o, shapes_l)):
            small[k].append(a)
    for nm, (w_, m_, v_) in (("lru_w_a", (lru_w_a, m_lru_w_a, v_lru_w_a)), ("lru_w_x", (lru_w_x, m_lru_w_x, v_lru_w_x))):
        gsum = _sum8(env[("all", nm)])
        flat = lambda a: a.reshape(-1, 128)
        small[nm] = [a.reshape(w_.shape) for a in (gsum, *_flat_adamw(gsum, flat(w_), flat(m_), flat(v_)))]

    order = ["c_ctx", "ada_w", "ada_b", "norm_g", "ffn1_w13", "ffn1_w2", "w_in", "b_merge", "rnn_conv_w", "rnn_conv_b", "lru_w_a",
             "lru_b_a", "lru_w_x", "lru_b_x", "lru_lambda", "sc_conv_w", "attn_sink", "w_branch", "w_out", "ffn2_w13", "ffn2_w2",
             "final_norm_g"]
    allo = dict(small)
    allo.update(big)
    allo["ada_w"] = ada_out
    res = [loss, grad_x]
    for q in range(4):
        res += [allo[k][q] for k in order]
    return tuple(res)
```
